```python
import jax
import jax.numpy as jnp
from jax import lax
import numpy as np

D_MODEL = 1024
BATCH = 2
SEQ = 8192
DEPTH = 2
DEC_BATCH = 128
DEC_SEQ = 8
PAST_LEN = 16384
PAGE_SIZE = 128

N_EVEN = (DEPTH + 1) // 2
N_ODD = DEPTH // 2
MIX_WIDTH = D_MODEL
EPS = 1e-6
NEG = -1e30

A_GROUPS = 4
A_CH = MIX_WIDTH // 2 // A_GROUPS
A_WIDTH = A_GROUPS * A_CH
CHUNK_A = 128
B_HEADS = 4
B_DV = MIX_WIDTH // 2 // B_HEADS
B_DK = B_DV // 2
B_GATE_RANK = 16
B_GATE_TAU = 16.0
CHUNK_B = 16
C_HEADS = 4
C_DH = MIX_WIDTH // 2 // C_HEADS
CHUNK_C = 64
D_HEADS = 4
D_V = MIX_WIDTH // 2 // D_HEADS
D_NOPE = 128
D_ROPE = 64
D_Q_LORA = 256
D_KV_LORA = 128
ROPE_BASE = 10000.0
MLA_SCALE = (D_NOPE + D_ROPE) ** -0.5
Q_BLOCK = 128
N_EXPERTS = 64
N_EXPERT_GROUPS = 8
TOPK_GROUPS = 4
TOP_K = 8
D_EXPERT = 256
ROUTED_SCALE = 2.5
MOE_BLOCK = 64

EVEN_SPLITS = (A_WIDTH, A_WIDTH, B_HEADS * B_DK, B_HEADS * B_DK, B_HEADS * B_DV, B_HEADS * B_DV, B_GATE_RANK)
ODD_SPLITS = (C_HEADS * C_DH, C_HEADS * C_DH, C_HEADS * C_DH, C_HEADS, C_HEADS, C_HEADS * C_DH, D_Q_LORA, D_KV_LORA, D_ROPE)
EVEN_IN = sum(EVEN_SPLITS)
ODD_IN = sum(ODD_SPLITS)

kernel_name = "hybrid_gmlp_gla_mlstm_mla_moe_step"


def rmsnorm(x, g):
    xf = x.astype(jnp.float32)
    y = xf * lax.rsqrt(jnp.mean(xf * xf, axis=-1, keepdims=True) + EPS)
    return (y * g.astype(jnp.float32)).astype(x.dtype)


def split_cols(z, sizes):
    return jnp.split(z, [int(s) for s in np.cumsum(sizes)[:-1]], axis=-1)


def rotary(x, pos):
    half = D_ROPE // 2
    inv = ROPE_BASE ** (-jnp.arange(half, dtype=jnp.float32) / half)
    ang = pos.astype(jnp.float32)[:, None] * inv
    bshape = (ang.shape[0],) + (1,) * (x.ndim - 3) + (half,)
    cos = jnp.cos(ang).reshape(bshape)
    sin = jnp.sin(ang).reshape(bshape)
    xf = x.astype(jnp.float32)
    x1, x2 = xf[..., :half], xf[..., half:]
    return jnp.concatenate([x1 * cos - x2 * sin, x1 * sin + x2 * cos], axis=-1).astype(x.dtype)


def pad_time(a, mult, value=0.0):
    pad = (-a.shape[1]) % mult
    return jnp.pad(a, ((0, 0), (0, pad)) + ((0, 0),) * (a.ndim - 2), constant_values=value)


def to_chunks(a, L):
    return a.reshape((a.shape[0], a.shape[1] // L, L) + a.shape[2:])


def swiglu(x, wg, wu, wd):
    return (jax.nn.silu(x @ wg) * (x @ wu)) @ wd


def chunk_gating(u, v, w_s, b_s):
    N, T = u.shape[:2]
    vc = to_chunks(pad_time(v, CHUNK_A), CHUNK_A)
    causal = jnp.tril(jnp.ones((CHUNK_A, CHUNK_A), dtype=bool))
    w = jnp.where(causal, w_s, jnp.zeros_like(w_s))
    mixed = jnp.einsum('gts,ncsgk->nctgk', w, vc) + b_s.T[:, :, None]
    mixed = mixed.reshape((N, -1) + mixed.shape[3:])[:, :T]
    return u * mixed


def gla(q, k, v, lg, S0):
    N, T, H, DK = q.shape
    L = CHUNK_B
    q, k, v, lg = (to_chunks(pad_time(a, L), L) for a in (q, k, v, lg))
    b = jnp.cumsum(lg, axis=2)
    causal = jnp.tril(jnp.ones((L, L), dtype=bool))[None, None, :, :, None, None]
    decay = jnp.exp(jnp.where(causal, b[:, :, :, None] - b[:, :, None], -jnp.inf))
    att = jnp.einsum('nclhd,ncshd,nclshd->nchls', q, k, decay)
    o_intra = jnp.einsum('nchls,ncshv->nclhv', att, v)
    bL = b[:, :, -1]
    k_end = k * jnp.exp(bL[:, :, None] - b)

    def step(S, xs):
        bL_c, k_c, v_c = xs
        S_next = jnp.exp(bL_c)[..., None] * S + jnp.einsum('nlhd,nlhv->nhdv', k_c, v_c)
        return S_next, S

    S_last, S_start = lax.scan(step, S0, (jnp.moveaxis(bL, 1, 0), jnp.moveaxis(k_end, 1, 0), jnp.moveaxis(v, 1, 0)))
    S_start = jnp.moveaxis(S_start, 0, 1)
    o_inter = jnp.einsum('nclhd,nchdv->nclhv', q * jnp.exp(b), S_start)
    o = (o_intra + o_inter).reshape(N, -1, H, v.shape[-1])[:, :T]
    return o, S_last


def mlstm(q, k, v, ig, lf, C0, n0, m0):
    N, T, H, DH = q.shape
    L = CHUNK_C
    q, k, v = (to_chunks(pad_time(a, L), L) for a in (q, k, v))
    ig = to_chunks(pad_time(ig, L, NEG), L)
    lf = to_chunks(pad_time(lf, L), L)
    F = jnp.cumsum(lf, axis=2)
    FL = F[:, :, -1]
    causal = jnp.tril(jnp.ones((L, L), dtype=bool))[None, None, :, :, None]
    logD = jnp.where(causal, F[:, :, :, None] - F[:, :, None] + ig[:, :, None], NEG)
    a_end = FL[:, :, None] - F + ig

    def step(carry, xs):
        C, n, m = carry
        FL_c, a_c, k_c, v_c = xs
        m_new = jnp.maximum(FL_c + m, jnp.max(a_c, axis=1))
        w = jnp.exp(a_c - m_new[:, None])
        dec = jnp.exp(FL_c + m - m_new)
        C_new = dec[..., None, None] * C + jnp.einsum('nlh,nlhv,nlhk->nhvk', w, v_c, k_c)
        n_new = dec[..., None] * n + jnp.einsum('nlh,nlhk->nhk', w, k_c)
        return (C_new, n_new, m_new), (C, n, m)

    (C_f, n_f, m_f), (Cs, ns, ms) = lax.scan(
        step, (C0, n0, m0), tuple(jnp.moveaxis(a, 1, 0) for a in (FL, a_end, k, v)))
    Cs, ns, ms = (jnp.moveaxis(a, 0, 1) for a in (Cs, ns, ms))
    inter_log = F + ms[:, :, None]
    m_t = jnp.maximum(inter_log, jnp.max(logD, axis=3))
    w_inter = jnp.exp(inter_log - m_t)
    qk = jnp.einsum('nclhd,ncshd->nclsh', q, k) * jnp.exp(logD - m_t[:, :, :, None])
    num = jnp.einsum('nclsh,ncshv->nclhv', qk, v) + w_inter[..., None] * jnp.einsum('nchvd,nclhd->nclhv', Cs, q)
    den = jnp.sum(qk, axis=3) + w_inter * jnp.einsum('nchd,nclhd->nclh', ns, q)
    h = num / jnp.maximum(jnp.abs(den), jnp.exp(-m_t))[..., None]
    return h.reshape(N, -1, H, DH)[:, :T], C_f, n_f, m_f


def attend_prompt(q_nope, q_rope, ckv, kr, w_uk, w_uv, g_kn):
    N, T, H, _ = q_nope.shape
    k_nope = rmsnorm(jnp.einsum('nsc,chd->nshd', ckv, w_uk), g_kn)
    v = jnp.einsum('nsc,chv->nshv', ckv, w_uv)
    pad = (-T) % Q_BLOCK
    nb = (T + pad) // Q_BLOCK

    def blocks(a):
        a = jnp.pad(a, ((0, 0), (0, pad), (0, 0), (0, 0)))
        return jnp.moveaxis(a.reshape((N, nb, Q_BLOCK) + a.shape[2:]), 1, 0)

    key_pos = jnp.arange(T)

    def one_block(args):
        qn, qr, bi = args
        q_pos = bi * Q_BLOCK + jnp.arange(Q_BLOCK)
        s = jnp.einsum('nqhd,nkhd->nhqk', qn, k_nope) + jnp.einsum('nqhd,nkd->nhqk', qr, kr)
        s = jnp.where(key_pos[None, :] <= q_pos[:, None], s.astype(jnp.float32) * MLA_SCALE, -jnp.inf)
        pr = jax.nn.softmax(s, axis=-1).astype(v.dtype)
        return jnp.einsum('nhqk,nkhv->nqhv', pr, v)

    out = lax.map(one_block, (blocks(q_nope), blocks(q_rope), jnp.arange(nb)))
    return jnp.moveaxis(out, 0, 1).reshape(N, nb * Q_BLOCK, H, D_V)[:, :T]


def attend_sample(q_nope, q_rope, ckv, kr, pool_ckv, pool_kr, page_table, li, w_uk, w_uv, g_kn):
    N, T, H, _ = q_nope.shape
    past = page_table.shape[1] * PAGE_SIZE
    key_pos = jnp.arange(past + T)
    q_pos = past + jnp.arange(T)
    mask = key_pos[None, :] <= q_pos[:, None]

    def one_seq(args):
        pages, qn, qr, ckv_new, kr_new = args
        ckv_all = jnp.concatenate([pool_ckv[pages, li].reshape(past, D_KV_LORA).astype(ckv_new.dtype), ckv_new], 0)
        kr_all = jnp.concatenate([pool_kr[pages, li].reshape(past, D_ROPE).astype(kr_new.dtype), kr_new], 0)
        k_nope = rmsnorm(jnp.einsum('sc,chd->shd', ckv_all, w_uk), g_kn)
        s = jnp.einsum('qhd,shd->hqs', qn, k_nope) + jnp.einsum('qhd,sd->hqs', qr, kr_all)
        s = jnp.where(mask, s.astype(jnp.float32) * MLA_SCALE, -jnp.inf)
        pr = jax.nn.softmax(s, axis=-1).astype(ckv_all.dtype)
        lat = jnp.einsum('hqs,sc->qhc', pr, ckv_all)
        return jnp.einsum('qhc,chv->qhv', lat, w_uv)

    return lax.map(one_seq, (page_table, q_nope, q_rope, ckv, kr))


def even_mixer(h, li, S0, p):
    N, T, _ = h.shape
    f32 = jnp.float32
    z = h @ p['w_in_even'][li]
    a_u, a_v, b_q, b_k, b_v, b_r, b_g = split_cols(z, EVEN_SPLITS)
    u = jax.nn.gelu(a_u).reshape(N, T, A_GROUPS, A_CH)
    v = rmsnorm(jax.nn.gelu(a_v).reshape(N, T, A_GROUPS, A_CH), p['a_norm_g'][li])
    out_a = chunk_gating(u, v, p['a_ws'][li], p['a_bs'][li]).reshape(N, T, A_WIDTH)
    q = b_q.reshape(N, T, B_HEADS, B_DK).astype(f32) * (B_DK ** -0.5)
    k = b_k.reshape(N, T, B_HEADS, B_DK).astype(f32)
    vb = b_v.reshape(N, T, B_HEADS, B_DV).astype(f32)
    lg = jax.nn.log_sigmoid((b_g @ p['b_w_gate2'][li] + p['b_gate_bias'][li]).astype(f32)) / B_GATE_TAU
    o, S_new = gla(q, k, vb, lg.reshape(N, T, B_HEADS, B_DK), S0.astype(f32))
    o = rmsnorm(o, p['b_norm_g'][li]).astype(h.dtype) * jax.nn.silu(b_r).reshape(N, T, B_HEADS, B_DV)
    out_b = o.reshape(N, T, B_HEADS * B_DV)
    return jnp.concatenate([out_a, out_b], axis=-1), S_new, v.reshape(N, T, A_WIDTH)


def odd_mixer(h, li, pos, C0, n0, m0, attend, p):
    N, T, _ = h.shape
    f32 = jnp.float32
    z = h @ p['w_in_odd'][li]
    c_q, c_k, c_v, c_i, c_f, c_o, d_qa, d_kva, d_kr = split_cols(z, ODD_SPLITS)
    q = c_q.reshape(N, T, C_HEADS, C_DH).astype(f32)
    k = c_k.reshape(N, T, C_HEADS, C_DH).astype(f32) * (C_DH ** -0.5)
    v = c_v.reshape(N, T, C_HEADS, C_DH).astype(f32)
    ig = (c_i + p['c_ig_bias'][li]).astype(f32)
    lf = jax.nn.log_sigmoid((c_f + p['c_fg_bias'][li]).astype(f32))
    hc, C_new, n_new, m_new = mlstm(q, k, v, ig, lf, C0.astype(f32), n0.astype(f32), m0.astype(f32))
    hc = rmsnorm(hc, p['c_norm_g'][li]).astype(h.dtype) * jax.nn.sigmoid(c_o).reshape(N, T, C_HEADS, C_DH)
    out_c = hc.reshape(N, T, C_HEADS * C_DH)
    qd = (rmsnorm(d_qa, p['d_g_qa'][li]) @ p['d_w_qb'][li]).reshape(N, T, D_HEADS, D_NOPE + D_ROPE)
    q_nope = rmsnorm(qd[..., :D_NOPE], p['d_g_qn'][li])
    q_rope = rotary(rmsnorm(qd[..., D_NOPE:], p['d_g_qr'][li]), pos)
    ckv = rmsnorm(d_kva, p['d_g_kva'][li])
    kr = rotary(rmsnorm(d_kr, p['d_g_kr'][li]), pos)
    out_d = attend(q_nope, q_rope, ckv, kr, li).reshape(N, T, D_HEADS * D_V)
    return jnp.concatenate([out_c, out_d.astype(out_c.dtype)], axis=-1), (C_new, n_new, m_new), (ckv, kr)


def routed_experts(h, idx, w, wg, wu, wd):
    M, D = h.shape
    A = M * TOP_K
    flat_e = idx.reshape(A)
    flat_tok = jnp.repeat(jnp.arange(M, dtype=jnp.int32), TOP_K)
    flat_w = w.reshape(A)
    order = jnp.argsort(flat_e)
    e_sorted = flat_e[order]
    counts = jnp.bincount(flat_e, length=N_EXPERTS)
    padded = (counts + MOE_BLOCK - 1) // MOE_BLOCK * MOE_BLOCK
    start = jnp.cumsum(counts) - counts
    pend = jnp.cumsum(padded)
    pstart = pend - padded
    dest = pstart[e_sorted] + jnp.arange(A) - start[e_sorted]
    n_blocks = -(-A // MOE_BLOCK) + N_EXPERTS
    R = n_blocks * MOE_BLOCK
    row_tok = jnp.full((R,), M, jnp.int32).at[dest].set(flat_tok[order])
    row_w = jnp.zeros((R,), h.dtype).at[dest].set(flat_w[order].astype(h.dtype))
    block_e = jnp.minimum(jnp.searchsorted(pend, jnp.arange(n_blocks) * MOE_BLOCK, side='right'), N_EXPERTS - 1)
    xs = jnp.concatenate([h, jnp.zeros((1, D), h.dtype)], 0)[row_tok].reshape(n_blocks, MOE_BLOCK, D)

    def one_block(args):
        xb, e = args
        return swiglu(xb, wg[e], wu[e], wd[e])

    yb = lax.map(one_block, (xs, block_e)).reshape(R, D)
    return jax.ops.segment_sum(yb * row_w[:, None], row_tok, num_segments=M + 1)[:M]


def moe(h, layer, p):
    M = h.shape[0]
    scores = jax.nn.sigmoid((h @ p['w_router'][layer]).astype(jnp.float32))
    biased = scores + p['b_router'][layer].astype(jnp.float32)
    per_group = N_EXPERTS // N_EXPERT_GROUPS
    grp_score = jnp.sum(lax.top_k(biased.reshape(M, N_EXPERT_GROUPS, per_group), 2)[0], axis=-1)
    _, grp_idx = lax.top_k(grp_score, TOPK_GROUPS)
    grp_mask = jnp.any(grp_idx[..., None] == jnp.arange(N_EXPERT_GROUPS), axis=1)
    masked = jnp.where(jnp.repeat(grp_mask, per_group, axis=1), biased, -jnp.inf)
    _, idx = lax.top_k(masked, TOP_K)
    w = jnp.take_along_axis(scores, idx, axis=1)
    w = w / jnp.sum(w, axis=-1, keepdims=True) * ROUTED_SCALE
    routed = routed_experts(h, idx, w, p['w_gate_e'][layer], p['w_up_e'][layer], p['w_down_e'][layer])
    shared = swiglu(h, p['w_gate_s'][layer], p['w_up_s'][layer], p['w_down_s'][layer])
    return routed + shared


def trunk(x, c, pos, gla0, mc0, mn0, mm0, attend, p):
    N, T, _ = x.shape
    cs = jax.nn.silu(c)
    gla_new, v_new, mc_new, mn_new, mm_new, ckv_new, kr_new = [], [], [], [], [], [], []
    for layer in range(DEPTH):
        li = layer // 2
        mod = (cs @ p['w_ada'][layer] + p['b_ada'][layer]).reshape(N, 6, 1, D_MODEL).astype(x.dtype)
        sh1, sc1, g1, sh2, sc2, g2 = (mod[:, i] for i in range(6))
        h = rmsnorm(x, p['norm_mix_g'][layer]) * (1 + sc1) + sh1
        if layer % 2 == 0:
            mix, S_new, v_rows = even_mixer(h, li, gla0[li], p)
            gla_new.append(S_new)
            v_new.append(v_rows)
        else:
            mix, (C_new, n_new, m_new), (ckv, kr) = odd_mixer(h, li, pos, mc0[li], mn0[li], mm0[li], attend, p)
            mc_new.append(C_new)
            mn_new.append(n_new)
            mm_new.append(m_new)
            ckv_new.append(ckv)
            kr_new.append(kr)
        x = x + g1 * (mix.astype(x.dtype) @ p['w_out'][layer])
        h = rmsnorm(x, p['norm_ffn_g'][layer]) * (1 + sc2) + sh2
        x = x + g2 * moe(h.reshape(N * T, D_MODEL), layer, p).reshape(N, T, D_MODEL).astype(x.dtype)
    return (x, jnp.stack(gla_new), jnp.stack(v_new), jnp.stack(mc_new), jnp.stack(mn_new),
            jnp.stack(mm_new), jnp.stack(ckv_new, axis=1), jnp.stack(kr_new, axis=1))


def setup_inputs(seed: int = 0) -> dict:
    key = jax.random.key(seed)
    ks = iter(jax.random.split(key, 64))

    def nrm(shape, scale=1.0):
        return jax.random.normal(next(ks), shape, jnp.float32) * scale

    def gain(shape):
        return 1.0 + nrm(shape, 0.02)

    n_pages = PAST_LEN // PAGE_SIZE
    n_used = DEC_BATCH * n_pages
    n_pool = n_used + n_used // 4
    page_table = jax.random.permutation(next(ks), n_pool)[:n_used].reshape(DEC_BATCH, n_pages).astype(jnp.int32)
    D = D_MODEL
    return {
        'x_prompt': nrm((BATCH, SEQ, D)),
        'x_sample': nrm((DEC_BATCH, DEC_SEQ, D)),
        'state_gla': nrm((N_EVEN, DEC_BATCH, B_HEADS, B_DK, B_DV)),
        'state_mlstm_c': nrm((N_ODD, DEC_BATCH, C_HEADS, C_DH, C_DH)),
        'state_mlstm_n': nrm((N_ODD, DEC_BATCH, C_HEADS, C_DH)),
        'state_mlstm_m': nrm((N_ODD, DEC_BATCH, C_HEADS)),
        'cache_ckv': nrm((n_pool, N_ODD, PAGE_SIZE, D_KV_LORA)),
        'cache_krope': nrm((n_pool, N_ODD, PAGE_SIZE, D_ROPE)),
        'page_table': page_table,
        'c_prompt': nrm((BATCH, D)),
        'c_sample': nrm((DEC_BATCH, D)),
        'norm_mix_g': gain((DEPTH, D)),
        'norm_ffn_g': gain((DEPTH, D)),
        'w_ada': nrm((DEPTH, D, 6 * D), 0.5 * D ** -0.5),
        'b_ada': nrm((DEPTH, 6 * D), 0.02),
        'w_out': nrm((DEPTH, MIX_WIDTH, D), MIX_WIDTH ** -0.5),
        'w_in_even': nrm((N_EVEN, D, EVEN_IN), D ** -0.5),
        'a_norm_g': gain((N_EVEN, A_GROUPS, A_CH)),
        'a_ws': nrm((N_EVEN, A_GROUPS, CHUNK_A, CHUNK_A), CHUNK_A ** -0.5),
        'a_bs': 1.0 + nrm((N_EVEN, A_GROUPS, CHUNK_A), 0.1),
        'b_w_gate2': nrm((N_EVEN, B_GATE_RANK, B_HEADS * B_DK), B_GATE_RANK ** -0.5),
        'b_gate_bias': nrm((N_EVEN, B_HEADS * B_DK), 0.02),
        'b_norm_g': gain((N_EVEN, B_DV)),
        'w_in_odd': nrm((N_ODD, D, ODD_IN), D ** -0.5),
        'c_ig_bias': nrm((N_ODD, C_HEADS), 0.1),
        'c_fg_bias': 3.0 + nrm((N_ODD, C_HEADS), 0.5),
        'c_norm_g': gain((N_ODD, C_DH)),
        'd_g_qa': gain((N_ODD, D_Q_LORA)),
        'd_w_qb': nrm((N_ODD, D_Q_LORA, D_HEADS * (D_NOPE + D_ROPE)), D_Q_LORA ** -0.5),
        'd_g_kva': gain((N_ODD, D_KV_LORA)),
        'd_g_qn': gain((N_ODD, D_NOPE)),
        'd_g_qr': gain((N_ODD, D_ROPE)),
        'd_g_kr': gain((N_ODD, D_ROPE)),
        'd_g_kn': gain((N_ODD, D_NOPE)),
        'd_w_uk': nrm((N_ODD, D_KV_LORA, D_HEADS, D_NOPE), D_KV_LORA ** -0.5),
        'd_w_uv': nrm((N_ODD, D_KV_LORA, D_HEADS, D_V), D_KV_LORA ** -0.5),
        'w_router': nrm((DEPTH, D, N_EXPERTS), D ** -0.5),
        'b_router': nrm((DEPTH, N_EXPERTS), 0.01),
        'w_gate_e': nrm((DEPTH, N_EXPERTS, D, D_EXPERT), D ** -0.5),
        'w_up_e': nrm((DEPTH, N_EXPERTS, D, D_EXPERT), D ** -0.5),
        'w_down_e': nrm((DEPTH, N_EXPERTS, D_EXPERT, D), D_EXPERT ** -0.5),
        'w_gate_s': nrm((DEPTH, D, D_EXPERT), D ** -0.5),
        'w_up_s': nrm((DEPTH, D, D_EXPERT), D ** -0.5),
        'w_down_s': nrm((DEPTH, D_EXPERT, D), D_EXPERT ** -0.5),
    }


def reference(x_prompt, x_sample, state_gla, state_mlstm_c, state_mlstm_n, state_mlstm_m,
              cache_ckv, cache_krope, page_table, c_prompt, c_sample,
              norm_mix_g, norm_ffn_g, w_ada, b_ada, w_out,
              w_in_even, a_norm_g, a_ws, a_bs, b_w_gate2, b_gate_bias, b_norm_g,
              w_in_odd, c_ig_bias, c_fg_bias, c_norm_g,
              d_g_qa, d_w_qb, d_g_kva, d_g_qn, d_g_qr, d_g_kr, d_g_kn, d_w_uk, d_w_uv,
              w_router, b_router, w_gate_e, w_up_e, w_down_e, w_gate_s, w_up_s, w_down_s):
    p = dict(norm_mix_g=norm_mix_g, norm_ffn_g=norm_ffn_g, w_ada=w_ada, b_ada=b_ada, w_out=w_out,
             w_in_even=w_in_even, a_norm_g=a_norm_g, a_ws=a_ws, a_bs=a_bs, b_w_gate2=b_w_gate2,
             b_gate_bias=b_gate_bias, b_norm_g=b_norm_g, w_in_odd=w_in_odd, c_ig_bias=c_ig_bias,
             c_fg_bias=c_fg_bias, c_norm_g=c_norm_g, d_g_qa=d_g_qa, d_w_qb=d_w_qb, d_g_kva=d_g_kva,
             d_g_qn=d_g_qn, d_g_qr=d_g_qr, d_g_kr=d_g_kr, d_g_kn=d_g_kn, d_w_uk=d_w_uk, d_w_uv=d_w_uv,
             w_router=w_router, b_router=b_router, w_gate_e=w_gate_e, w_up_e=w_up_e, w_down_e=w_down_e,
             w_gate_s=w_gate_s, w_up_s=w_up_s, w_down_s=w_down_s)
    f32 = jnp.float32
    n_p = x_prompt.shape[0]
    gla0_p = jnp.zeros((N_EVEN, n_p, B_HEADS, B_DK, B_DV), f32)
    mc0_p = jnp.zeros((N_ODD, n_p, C_HEADS, C_DH, C_DH), f32)
    mn0_p = jnp.zeros((N_ODD, n_p, C_HEADS, C_DH), f32)
    mm0_p = jnp.full((N_ODD, n_p, C_HEADS), NEG, f32)
    pos_p = jnp.arange(x_prompt.shape[1])
    pos_s = PAST_LEN + jnp.arange(x_sample.shape[1])

    def attend_p(qn, qr, ckv, kr, li):
        return attend_prompt(qn, qr, ckv, kr, d_w_uk[li], d_w_uv[li], d_g_kn[li])

    def attend_s(qn, qr, ckv, kr, li):
        return attend_sample(qn, qr, ckv, kr, cache_ckv, cache_krope, page_table, li,
                             d_w_uk[li], d_w_uv[li], d_g_kn[li])

    y_p, gla_p, _, mc_p, mn_p, mm_p, ckv_p, kr_p = trunk(
        x_prompt, c_prompt, pos_p, gla0_p, mc0_p, mn0_p, mm0_p, attend_p, p)
    y_s, gla_s, v_s, mc_s, mn_s, mm_s, ckv_s, kr_s = trunk(
        x_sample, c_sample, pos_s, state_gla, state_mlstm_c, state_mlstm_n, state_mlstm_m, attend_s, p)
    return (y_p, y_s, gla_p, gla_s, v_s, mc_p, mc_s, mn_p, mn_s, mm_p, mm_s, ckv_p, ckv_s, kr_p, kr_s)
```

```python
import functools

import numpy as np
import jax
import jax.numpy as jnp
from jax import lax
from jax.experimental import pallas as pl
from jax.experimental.pallas import tpu as pltpu

F32 = jnp.float32
BF16 = jnp.bfloat16

EPS = 1e-6
NEG = -1e30

A_GROUPS = 4
CHUNK_A = 128
B_HEADS = 4
B_GATE_RANK = 16
B_GATE_TAU = 16.0
CHUNK_B = 16
C_HEADS = 4
CHUNK_C = 128
D_HEADS = 4
D_NOPE = 128
D_ROPE = 64
ROPE_BASE = 10000.0
MLA_SCALE = (D_NOPE + D_ROPE) ** -0.5
PAGE_SIZE = 128
N_EXPERT_GROUPS = 8
TOPK_GROUPS = 4
TOP_K = 8
ROUTED_SCALE = 2.5

ROW_TILE = 256
SAMPLE_SEQS = 32
FLASH_BLOCK = 512
PAGES_PER_STEP = 16
VMEM_LIMIT = 56 * 1024 * 1024


def _dot(a, b):
    return jnp.dot(a, b, preferred_element_type=F32)


def _dot_nt(a, b):
    return lax.dot_general(a, b, (((1,), (1,)), ((), ())), preferred_element_type=F32)


def _split2(x):
    hi = x.astype(BF16)
    lo = (x - hi.astype(F32)).astype(BF16)
    return hi, lo


def _split3(x):
    hi = x.astype(BF16)
    r = x - hi.astype(F32)
    mid = r.astype(BF16)
    lo = (r - mid.astype(F32)).astype(BF16)
    return hi, mid, lo


def _dot3(a, b):
    ah, al = _split2(a)
    bh, bl = _split2(b)
    return _dot(ah, bh) + _dot(ah, bl) + _dot(al, bh)


def _dot3_nt(a, b):
    ah, al = _split2(a)
    bh, bl = _split2(b)
    return _dot_nt(ah, bh) + _dot_nt(ah, bl) + _dot_nt(al, bh)


def _dot_sel(x, m01, parts=2):
    ps = _split2(x) if parts == 2 else _split3(x)
    acc = _dot(ps[0], m01)
    for p in ps[1:]:
        acc = acc + _dot(p, m01)
    return acc


def _sigmoid(x):
    return 1.0 / (1.0 + jnp.exp(-x))


def _silu(x):
    return x * _sigmoid(x)


def _log_sigmoid(x):
    return jnp.minimum(x, 0.0) - jnp.log(1.0 + jnp.exp(-jnp.abs(x)))


def _gelu(x):
    return 0.5 * x * (1.0 + jnp.tanh(0.7978845608028654 * (x + 0.044715 * x * x * x)))


def _rms(x, eps=EPS):
    return x * lax.rsqrt(jnp.mean(x * x, axis=-1, keepdims=True) + eps)


def _params(*sem):
    return pltpu.CompilerParams(dimension_semantics=sem, vmem_limit_bytes=VMEM_LIMIT)


def _const_spec(shape):
    nd = len(shape)
    return pl.BlockSpec(shape, lambda *_: (0,) * nd)


class _Rows:
    def __init__(self, n, t):
        self.n, self.t = n, t
        if t % ROW_TILE == 0:
            self.s, self.r = 1, ROW_TILE
            self.tpb = t // ROW_TILE
        else:
            assert t == 8 and n % SAMPLE_SEQS == 0, (n, t)
            self.s, self.r = SAMPLE_SEQS, t
            self.tpb = 1
        self.rows = self.s * self.r
        self.steps = n * t // self.rows

    def seq_spec(self, mid, d):
        s, tpb = self.s, self.tpb
        if mid == self.t:
            return pl.BlockSpec((s, self.r, d), lambda i: (i // tpb, i % tpb, 0))
        return pl.BlockSpec((s, mid, d), lambda i: (i // tpb, 0, 0))

    def row_spec(self, c):
        return pl.BlockSpec((self.rows, c), lambda i: (i, 0))

    def pos_spec(self, c):
        tpb = self.tpb
        return pl.BlockSpec((self.rows, c), lambda i: (i % tpb, 0))


def _ada_kernel(c_ref, w_ref, b_ref, o_ref):
    o_ref[0] = _dot3(_silu(c_ref[...]), w_ref[0]) + b_ref[0]


def _ada(c_all, w_ada, b_ada):
    depth, d, d6 = w_ada.shape
    nc = c_all.shape[0]
    tn = 1536
    return pl.pallas_call(
        _ada_kernel,
        grid=(depth, d6 // tn),
        in_specs=[pl.BlockSpec((nc, d), lambda l, j: (0, 0)),
                  pl.BlockSpec((1, d, tn), lambda l, j: (l, 0, j)),
                  pl.BlockSpec((1, 1, tn), lambda l, j: (l, 0, j))],
        out_specs=pl.BlockSpec((1, nc, tn), lambda l, j: (l, 0, j)),
        out_shape=jax.ShapeDtypeStruct((depth, nc, d6), F32),
        compiler_params=_params("arbitrary", "arbitrary"),
        name="ada_modulation",
    )(c_all, w_ada, b_ada.reshape(depth, 1, d6))


def _norm_mod(x_ref, mod_ref, g_ref, shift_row, scale_row):
    x = x_ref[...]
    h = _rms(x) * g_ref[...]
    h = h * (1.0 + mod_ref[:, scale_row:scale_row + 1, :]) + mod_ref[:, shift_row:shift_row + 1, :]
    s, r, d = x.shape
    return h.reshape(s * r, d)


def _even_in_kernel(x_ref, mod_ref, g_ref, w_ref, an_ref, ws_ref, bs_ref,
                    oa_ref, v_ref, q_ref, k_ref, bv_ref, r_ref, bg_ref):
    h = _norm_mod(x_ref, mod_ref, g_ref, 0, 1).astype(BF16)
    z = _dot(h, w_ref[...])
    aw = A_GROUPS * CHUNK_A
    rows = z.shape[0]
    u = _gelu(z[:, 0:aw])
    va = _gelu(z[:, aw:2 * aw])
    vn = jnp.concatenate(
        [_rms(va[:, g * CHUNK_A:(g + 1) * CHUNK_A]) for g in range(A_GROUPS)], axis=1) * an_ref[...]
    v_ref[...] = vn
    vb = vn.astype(BF16)
    for c in range(rows // CHUNK_A):
        r0 = c * CHUNK_A
        mixed = jnp.concatenate(
            [_dot(ws_ref[g], vb[r0:r0 + CHUNK_A, g * CHUNK_A:(g + 1) * CHUNK_A]) for g in range(A_GROUPS)],
            axis=1) + bs_ref[...]
        oa_ref[r0:r0 + CHUNK_A, :] = u[r0:r0 + CHUNK_A, :] * mixed
    o = 2 * aw
    q_ref[...] = z[:, o:o + 256]
    k_ref[...] = z[:, o + 256:o + 512]
    bv_ref[...] = z[:, o + 512:o + 1024]
    r_ref[...] = z[:, o + 1024:o + 1536]
    bg_ref[...] = z[:, o + 1536:o + 1664]


def _even_in(x, mod, g, w, an, ws, bs):
    n, t, d = x.shape
    rt = _Rows(n, t)
    m = n * t
    widths = (512, 512, 256, 256, 512, 512, 128)
    return pl.pallas_call(
        _even_in_kernel,
        grid=(rt.steps,),
        in_specs=[rt.seq_spec(t, d), rt.seq_spec(6, d), _const_spec((1, d)), _const_spec(w.shape),
                  _const_spec(an.shape), _const_spec(ws.shape), _const_spec(bs.shape)],
        out_specs=[rt.row_spec(c) for c in widths],
        out_shape=[jax.ShapeDtypeStruct((m, c), F32) for c in widths],
        compiler_params=_params("arbitrary"),
        name="even_in_chunk_gate",
    )(x, mod, g, w, an, ws, bs)


def _gla_kernel(q_ref, k_ref, v_ref, r_ref, g_ref, wg_ref, gb_ref, ng_ref, s0_ref, seg_ref, eye_ref,
                o_ref, so_ref, s_scr, lg_scr, o_scr, *, chunk, n_chunks):
    hk = q_ref.shape[1]
    dk = hk // B_HEADS
    dv = v_ref.shape[1] // B_HEADS
    t = pl.program_id(1)

    @pl.when(t == 0)
    def _():
        s_scr[...] = s0_ref[0].reshape(hk, dv)

    lg_scr[...] = _log_sigmoid(_dot3(g_ref[...], wg_ref[...]) + gb_ref[...]) * (1.0 / B_GATE_TAU)
    row = lax.broadcasted_iota(jnp.int32, (chunk, hk), 0)
    lane = lax.broadcasted_iota(jnp.int32, (chunk, hk), 1)
    pad = 16 - chunk if chunk < 16 else 0

    def pad_rows(a):
        if pad:
            return jnp.concatenate([a, jnp.zeros((pad, a.shape[1]), a.dtype)], axis=0)
        return a

    def body(c, carry):
        r0 = pl.multiple_of(c * chunk, chunk)
        lg = lg_scr[pl.ds(r0, chunk), :]
        q = q_ref[pl.ds(r0, chunk), :] * (dk ** -0.5)
        k = k_ref[pl.ds(r0, chunk), :]
        v = v_ref[pl.ds(r0, chunk), :]
        b = jnp.zeros_like(lg)
        for s in range(chunk):
            b = b + jnp.where(row >= s, lg[s:s + 1, :], 0.0)
        prods = []
        for l in range(chunk):
            e = jnp.exp(jnp.where(row <= l, b[l:l + 1, :] - b, NEG))
            prods.append(q[l:l + 1, :] * k * e)
        att = _dot_sel(jnp.concatenate(prods, axis=0), seg_ref[...])
        o_intra = jnp.concatenate(
            [jnp.sum(att[l * chunk:(l + 1) * chunk, :] * v, axis=0, keepdims=True) for l in range(chunk)], axis=0)
        s_old = s_scr[...]
        s_b = s_old.astype(BF16)
        qe = q * jnp.exp(b)
        o_inter = jnp.concatenate(
            [_dot(jnp.where(lane // dk == h, qe, 0.0).astype(BF16), s_b) for h in range(B_HEADS)], axis=1)
        o_scr[pl.ds(r0, chunk), :] = o_intra + o_inter
        b_last = b[chunk - 1:chunk, :]
        k_end = pad_rows((k * jnp.exp(b_last - b)).astype(BF16))
        k_t = _dot_nt(eye_ref[...], k_end).astype(BF16)
        dh, dm, dl = _split3(jnp.exp(b_last))
        dec3 = _dot_nt(eye_ref[...], jnp.concatenate([dh, dm, dl, jnp.zeros((13, hk), BF16)], axis=0))
        dec = dec3[:, 0:1] + dec3[:, 1:2] + dec3[:, 2:3]
        v_b = pad_rows(v.astype(BF16))
        kv = jnp.concatenate(
            [_dot(k_t[h * dk:(h + 1) * dk, :], v_b[:, h * dv:(h + 1) * dv]) for h in range(B_HEADS)], axis=0)
        s_scr[...] = dec * s_old + kv
        return carry

    lax.fori_loop(0, n_chunks, body, 0)

    o = o_scr[...]
    on = jnp.concatenate([_rms(o[:, h * dv:(h + 1) * dv]) for h in range(B_HEADS)], axis=1) * ng_ref[...]
    o_ref[...] = on * _silu(r_ref[...])

    @pl.when(t == pl.num_programs(1) - 1)
    def _():
        so_ref[0] = s_scr[...].reshape(B_HEADS, dk, dv)


def _gla(n, t, q, k, v, r, g, wg, gb, ng, s0, seg, eye):
    tt = ROW_TILE if t % ROW_TILE == 0 else t
    chunk = CHUNK_B if tt % CHUNK_B == 0 else tt
    nt = t // tt
    hk, hv = q.shape[1], v.shape[1]
    dk, dv = hk // B_HEADS, hv // B_HEADS

    def rows(c):
        return pl.BlockSpec((tt, c), lambda i, j: (i * nt + j, 0))

    st = pl.BlockSpec((1, B_HEADS, dk, dv), lambda i, j: (i, 0, 0, 0))
    return pl.pallas_call(
        functools.partial(_gla_kernel, chunk=chunk, n_chunks=tt // chunk),
        grid=(n, nt),
        in_specs=[rows(hk), rows(hk), rows(hv), rows(hv), rows(g.shape[1]),
                  pl.BlockSpec(wg.shape, lambda i, j: (0, 0)), pl.BlockSpec(gb.shape, lambda i, j: (0, 0)),
                  pl.BlockSpec(ng.shape, lambda i, j: (0, 0)), st,
                  pl.BlockSpec(seg.shape, lambda i, j: (0, 0)), pl.BlockSpec(eye.shape, lambda i, j: (0, 0))],
        out_specs=[rows(hv), st],
        out_shape=[jax.ShapeDtypeStruct((n * t, hv), F32), jax.ShapeDtypeStruct(s0.shape, F32)],
        scratch_shapes=[pltpu.VMEM((hk, dv), F32), pltpu.VMEM((tt, hk), F32), pltpu.VMEM((tt, hv), F32)],
        compiler_params=_params("arbitrary", "arbitrary"),
        name="gla_scan",
    )(q, k, v, r, g, wg, gb, ng, s0, seg, eye)


def _route(logits_t, bias_col, scores_out=None):
    n_e, r = logits_t.shape
    per = n_e // N_EXPERT_GROUPS
    scores = _sigmoid(logits_t)
    biased = scores + bias_col
    sub = lax.broadcasted_iota(jnp.int32, (per, r), 0).astype(F32)
    ninf = -jnp.inf
    gs = []
    for g in range(N_EXPERT_GROUPS):
        blk = biased[g * per:(g + 1) * per, :]
        m1 = jnp.max(blk, axis=0, keepdims=True)
        i1 = jnp.min(jnp.where(blk == m1, sub, float(per)), axis=0, keepdims=True)
        m2 = jnp.max(jnp.where(sub == i1, ninf, blk), axis=0, keepdims=True)
        gs.append(m1 + m2)
    cur = jnp.concatenate(gs, axis=0)
    gsub = lax.broadcasted_iota(jnp.int32, (N_EXPERT_GROUPS, r), 0).astype(F32)
    gsel = jnp.zeros((N_EXPERT_GROUPS, r), F32)
    for _ in range(TOPK_GROUPS):
        m = jnp.max(cur, axis=0, keepdims=True)
        i = jnp.min(jnp.where(cur == m, gsub, float(N_EXPERT_GROUPS)), axis=0, keepdims=True)
        hit = gsub == i
        gsel = jnp.where(hit, 1.0, gsel)
        cur = jnp.where(hit, ninf, cur)
    cur = jnp.concatenate(
        [jnp.where(gsel[g:g + 1, :] > 0.5, biased[g * per:(g + 1) * per, :], ninf) for g in range(N_EXPERT_GROUPS)],
        axis=0)
    esub = lax.broadcasted_iota(jnp.int32, (n_e, r), 0).astype(F32)
    idx, wts = [], []
    for _ in range(TOP_K):
        m = jnp.max(cur, axis=0, keepdims=True)
        i = jnp.min(jnp.where(cur == m, esub, float(n_e)), axis=0, keepdims=True)
        hit = esub == i
        idx.append(i)
        wts.append(jnp.sum(jnp.where(hit, scores, 0.0), axis=0, keepdims=True))
        cur = jnp.where(hit, ninf, cur)
    w = jnp.concatenate(wts, axis=0)
    w = w / jnp.sum(w, axis=0, keepdims=True) * ROUTED_SCALE
    return jnp.concatenate(idx, axis=0), w


def _out_kernel(x_ref, mod_ref, ma_ref, mb_ref, wo_ref, g_ref, wr_ref, br_ref, eye_ref,
                x1_ref, h2_ref, idx_ref, w3_ref):
    half = ma_ref.shape[1]
    y = _dot(ma_ref[...].astype(BF16), wo_ref[0:half, :]) + _dot(mb_ref[...].astype(BF16), wo_ref[half:, :])
    x = x_ref[...]
    s, r, d = x.shape
    x1 = x + mod_ref[:, 2:3, :] * y.reshape(s, r, d)
    x1_ref[...] = x1
    h2 = (_rms(x1) * g_ref[...] * (1.0 + mod_ref[:, 4:5, :]) + mod_ref[:, 3:4, :]).reshape(s * r, d)
    h2_ref[...] = h2.astype(BF16)
    idx, w = _route(_dot3_nt(wr_ref[...], h2), br_ref[...])
    idx_ref[...] = idx.astype(jnp.int32)
    wh, wm, wl = _split3(w)
    stack = jnp.concatenate([wh, wm, wl, jnp.zeros((128 - 3 * TOP_K, s * r), BF16)], axis=0)
    w3_ref[...] = _dot_nt(eye_ref[...], stack)


def _out_route(x, mod, mix_a, mix_b, wo, g, wr_t, br, eye):
    n, t, d = x.shape
    rt = _Rows(n, t)
    m = n * t
    return pl.pallas_call(
        _out_kernel,
        grid=(rt.steps,),
        in_specs=[rt.seq_spec(t, d), rt.seq_spec(6, d), rt.row_spec(mix_a.shape[1]), rt.row_spec(mix_b.shape[1]),
                  _const_spec(wo.shape), _const_spec((1, d)), _const_spec(wr_t.shape), _const_spec(br.shape),
                  _const_spec(eye.shape)],
        out_specs=[rt.seq_spec(t, d), rt.row_spec(d),
                   pl.BlockSpec((TOP_K, rt.rows), lambda i: (0, i)), rt.row_spec(128)],
        out_shape=[jax.ShapeDtypeStruct((n, t, d), F32), jax.ShapeDtypeStruct((m, d), BF16),
                   jax.ShapeDtypeStruct((TOP_K, m), jnp.int32), jax.ShapeDtypeStruct((m, 128), F32)],
        compiler_params=_params("arbitrary"),
        name="out_proj_route",
    )(x, mod, mix_a, mix_b, wo, g, wr_t, br, eye)


def _gmm_kernel(be_ref, nb_ref, x_ref, wg_ref, wu_ref, wd_ref, y_ref):
    i = pl.program_id(0)

    @pl.when(i < nb_ref[0])
    def _():
        x = x_ref[...]
        a = _silu(_dot(x, wg_ref[...].astype(BF16))) * _dot(x, wu_ref[...].astype(BF16))
        y_ref[...] = _dot(a.astype(BF16), wd_ref[...].astype(BF16)).astype(BF16)

    @pl.when(i >= nb_ref[0])
    def _():
        y_ref[...] = jnp.zeros_like(y_ref)


def _gmm(layer, block_e, n_used, xs, wg, wu, wd, tm):
    r, d = xs.shape
    de = wg.shape[-1]
    nb = r // tm
    grid_spec = pltpu.PrefetchScalarGridSpec(
        num_scalar_prefetch=2,
        grid=(nb,),
        in_specs=[pl.BlockSpec((tm, d), lambda i, be, nu: (i, 0)),
                  pl.BlockSpec((None, None, d, de), lambda i, be, nu: (layer, be[i], 0, 0)),
                  pl.BlockSpec((None, None, d, de), lambda i, be, nu: (layer, be[i], 0, 0)),
                  pl.BlockSpec((None, None, de, d), lambda i, be, nu: (layer, be[i], 0, 0))],
        out_specs=pl.BlockSpec((tm, d), lambda i, be, nu: (i, 0)),
    )
    return pl.pallas_call(
        _gmm_kernel,
        grid_spec=grid_spec,
        out_shape=jax.ShapeDtypeStruct((r, d), BF16),
        compiler_params=_params("arbitrary"),
        name="moe_grouped_swiglu",
    )(block_e, n_used, xs, wg, wu, wd)


def _combine_kernel(x_ref, mod_ref, h_ref, yg_ref, w3_ref, wg_ref, wu_ref, wd_ref, o_ref):
    h = h_ref[...]
    a = _silu(_dot(h, wg_ref[...])) * _dot(h, wu_ref[...])
    acc = _dot(a.astype(BF16), wd_ref[...])
    w3 = w3_ref[...]
    for k in range(TOP_K):
        wk = w3[:, k:k + 1] + w3[:, TOP_K + k:TOP_K + k + 1] + w3[:, 2 * TOP_K + k:2 * TOP_K + k + 1]
        acc = acc + wk * yg_ref[k].astype(F32)
    x = x_ref[...]
    o_ref[...] = x + mod_ref[:, 5:6, :] * acc.reshape(x.shape)


def _combine(x1, mod, h2, yg, w3, wg, wu, wd):
    n, t, d = x1.shape
    rt = _Rows(n, t)
    return pl.pallas_call(
        _combine_kernel,
        grid=(rt.steps,),
        in_specs=[rt.seq_spec(t, d), rt.seq_spec(6, d), rt.row_spec(d),
                  pl.BlockSpec((TOP_K, rt.rows, d), lambda i: (0, i, 0)), rt.row_spec(128),
                  _const_spec(wg.shape), _const_spec(wu.shape), _const_spec(wd.shape)],
        out_specs=rt.seq_spec(t, d),
        out_shape=jax.ShapeDtypeStruct((n, t, d), F32),
        compiler_params=_params("arbitrary"),
        name="moe_combine_shared",
    )(x1, mod, h2, yg, w3, wg, wu, wd)


def _moe(layer, x1, mod, h2, idx_t, w3, p):
    n, t, d = x1.shape
    m = n * t
    n_e = p['w_gate_e'].shape[1]
    a = m * TOP_K
    tm = 256 if a // n_e >= 512 else 128
    flat_e = idx_t.reshape(a)
    flat_tok = jnp.tile(jnp.arange(m, dtype=jnp.int32), TOP_K)
    order = jnp.argsort(flat_e)
    e_sorted = flat_e[order]
    counts = jnp.zeros((n_e,), jnp.int32).at[flat_e].add(1)
    padded = (counts + tm - 1) // tm * tm
    start = jnp.cumsum(counts) - counts
    pend = jnp.cumsum(padded)
    pstart = pend - padded
    dest = pstart[e_sorted] + jnp.arange(a, dtype=jnp.int32) - start[e_sorted]
    n_blocks = a // tm + n_e
    r = n_blocks * tm
    row_tok = jnp.full((r,), m, jnp.int32).at[dest].set(flat_tok[order])
    dest_flat = jnp.zeros((a,), jnp.int32).at[order].set(dest)
    block_e = jnp.minimum(jnp.searchsorted(pend, jnp.arange(n_blocks, dtype=jnp.int32) * tm, side='right'),
                          n_e - 1).astype(jnp.int32)
    n_used = (pend[-1] // tm).astype(jnp.int32).reshape(1)
    xs = jnp.concatenate([h2, jnp.zeros((1, d), h2.dtype)], axis=0)[row_tok]
    ys = _gmm(layer, block_e, n_used, xs, p['w_gate_e'], p['w_up_e'], p['w_down_e'], tm)
    yg = ys[dest_flat].reshape(TOP_K, m, d)
    return _combine(x1, mod, h2, yg, w3, p['w_gate_s'][layer].astype(BF16), p['w_up_s'][layer].astype(BF16),
                    p['w_down_s'][layer].astype(BF16))


def _odd_in_kernel(x_ref, mod_ref, g_ref, w_ref, gqa_ref, wqb_ref, gqn_ref, gqr_ref, gqrs_ref, gkva_ref,
                   gkr_ref, gkrs_ref, gkn_ref, cos_ref, sin_ref, seg_ref, wuk_ref, wuv_ref,
                   cq_ref, ck_ref, cv_ref, co_ref, gt_ref, ckv_ref, kr_ref, *outs, prompt):
    h = _norm_mod(x_ref, mod_ref, g_ref, 0, 1).astype(BF16)
    z = _dot(h, w_ref[...])
    cq_ref[...] = z[:, 0:512]
    ck_ref[...] = z[:, 512:1024]
    cv_ref[...] = z[:, 1024:1536]
    co_ref[...] = z[:, 1536:2048]
    gt_ref[...] = z[:, 2560:2688]
    cos = cos_ref[...]
    sin = sin_ref[...]
    ckv = _rms(z[:, 2304:2432]) * gkva_ref[...]
    ckv_ref[...] = ckv
    xr = z[:, 2432:2496]
    xr_rot = z[:, 2496:2560]
    rr = lax.rsqrt(jnp.mean(xr * xr, axis=-1, keepdims=True) + EPS)
    kr = rr * (xr * gkr_ref[...] * cos[:, 0:D_ROPE] + xr_rot * gkrs_ref[...] * sin[:, 0:D_ROPE])
    kr_ref[...] = kr
    qa = (_rms(z[:, 2048:2304]) * gqa_ref[...]).astype(BF16)
    qd = _dot(qa, wqb_ref[...])
    nw = D_HEADS * D_NOPE
    rw = D_HEADS * D_ROPE
    qn = [_rms(qd[:, hh * D_NOPE:(hh + 1) * D_NOPE]) * gqn_ref[...] * MLA_SCALE for hh in range(D_HEADS)]
    xq = qd[:, nw:nw + rw]
    xq_rot = qd[:, nw + rw:nw + 2 * rw]
    rq = lax.rsqrt(_dot_sel(xq * xq, seg_ref[...]) * (1.0 / D_ROPE) + EPS)
    qr = rq * (xq * gqr_ref[...] * cos + xq_rot * gqrs_ref[...] * sin) * MLA_SCALE
    ckv_b = ckv.astype(BF16)
    if prompt:
        qcat_ref, kcat_ref, v_ref = outs
        kn = _dot(ckv_b, wuk_ref[...])
        zpad = jnp.zeros((z.shape[0], 256 - D_NOPE - D_ROPE), F32)
        qcat_ref[...] = jnp.concatenate(
            [piece for hh in range(D_HEADS) for piece in (qn[hh], qr[:, hh * D_ROPE:(hh + 1) * D_ROPE], zpad)],
            axis=1).astype(BF16)
        kcat_ref[...] = jnp.concatenate(
            [piece for hh in range(D_HEADS)
             for piece in (_rms(kn[:, hh * D_NOPE:(hh + 1) * D_NOPE]) * gkn_ref[...], kr, zpad)],
            axis=1).astype(BF16)
        v_ref[...] = _dot(ckv_b, wuv_ref[...]).astype(BF16)
    else:
        u_ref, qr_ref = outs
        u_ref[...] = jnp.concatenate(
            [_dot_nt((qn[hh] * gkn_ref[...]).astype(BF16), wuk_ref[:, hh * D_NOPE:(hh + 1) * D_NOPE])
             for hh in range(D_HEADS)], axis=1).astype(BF16)
        qr_ref[...] = qr.astype(BF16)


def _odd_in(x, mod, g, w, consts, cos, sin, prompt):
    n, t, d = x.shape
    rt = _Rows(n, t)
    m = n * t
    widths = [(512, F32)] * 4 + [(128, F32), (128, F32), (D_ROPE, F32)]
    if prompt:
        widths += [(1024, BF16), (1024, BF16), (512, BF16)]
    else:
        widths += [(512, BF16), (256, BF16)]
    return pl.pallas_call(
        functools.partial(_odd_in_kernel, prompt=prompt),
        grid=(rt.steps,),
        in_specs=[rt.seq_spec(t, d), rt.seq_spec(6, d), _const_spec((1, d)), _const_spec(w.shape)]
                 + [_const_spec(c.shape) for c in consts[:9]]
                 + [rt.pos_spec(cos.shape[1]), rt.pos_spec(sin.shape[1])]
                 + [_const_spec(c.shape) for c in consts[9:]],
        out_specs=[rt.row_spec(c) for c, _ in widths],
        out_shape=[jax.ShapeDtypeStruct((m, c), dt) for c, dt in widths],
        compiler_params=_params("arbitrary"),
        name="odd_in_latent_prep",
    )(x, mod, g, w, *consts[:9], cos, sin, *consts[9:])


def _mlstm_kernel(q_ref, k_ref, v_ref, og_ref, gt_ref, gb_ref, ng_ref, c0_ref, n0_ref, m0_ref, tri_ref, eye_ref,
                  o_ref, co_ref, no_ref, mo_ref, c_scr, n_scr, m_scr, *, chunk):
    dh = q_ref.shape[1] // C_HEADS
    t = pl.program_id(1)

    @pl.when(t == 0)
    def _():
        c_scr[...] = c0_ref[0]
        n_scr[...] = n0_ref[0]
        m_scr[...] = m0_ref[0]

    pre = gt_ref[...] + gb_ref[...]
    lf = _log_sigmoid(pre)
    if chunk >= 16:
        l_hi, l_mid, l_lo = _split3(lf)
        f_cum = _dot(tri_ref[...], l_hi) + _dot(tri_ref[...], l_mid) + _dot(tri_ref[...], l_lo)
    else:
        rowg = lax.broadcasted_iota(jnp.int32, lf.shape, 0)
        f_cum = jnp.zeros_like(lf)
        for s in range(chunk):
            f_cum = f_cum + jnp.where(rowg >= s, lf[s:s + 1, :], 0.0)
    ri = lax.broadcasted_iota(jnp.int32, (chunk, chunk), 0)
    ci = lax.broadcasted_iota(jnp.int32, (chunk, chunk), 1)
    pad = 16 - chunk if chunk < 16 else 0

    def pad_rows(a):
        if pad:
            return jnp.concatenate([a, jnp.zeros((pad, a.shape[1]), a.dtype)], axis=0)
        return a

    for h in range(C_HEADS):
        sl = slice(h * dh, (h + 1) * dh)
        q = q_ref[:, sl]
        k = k_ref[:, sl] * (dh ** -0.5)
        v = v_ref[:, sl]
        qb, kb, vb = q.astype(BF16), k.astype(BF16), v.astype(BF16)
        f_col = f_cum[:, C_HEADS + h:C_HEADS + h + 1]
        i_col = pre[:, h:h + 1]
        m_prev = m_scr[h:h + 1, 0:1]
        a_row = jnp.sum(jnp.where(ri == ci, i_col - f_col, 0.0), axis=0, keepdims=True)
        log_d = jnp.where(ci <= ri, f_col + a_row, NEG)
        inter = f_col + m_prev
        m_t = jnp.maximum(inter, jnp.max(log_d, axis=-1, keepdims=True))
        w_inter = jnp.exp(inter - m_t)
        qk = _dot_nt(qb, kb) * jnp.exp(log_d - m_t)
        c_old = c_scr[h]
        n_old = n_scr[h:h + 1, :]
        num = _dot(qk.astype(BF16), vb) + w_inter * _dot_nt(qb, c_old.astype(BF16))
        den = jnp.sum(qk, axis=-1, keepdims=True) + w_inter * jnp.sum(q * n_old, axis=-1, keepdims=True)
        hh = num / jnp.maximum(jnp.abs(den), jnp.exp(-m_t))
        o_ref[:, sl] = _rms(hh) * ng_ref[...] * _sigmoid(og_ref[:, sl])
        f_last = f_col[chunk - 1:chunk, :]
        a_end = f_last - f_col + i_col
        m_new = jnp.maximum(f_last + m_prev, jnp.max(a_end, axis=0, keepdims=True))
        w = jnp.exp(a_end - m_new)
        dec = jnp.exp(f_last + m_prev - m_new)
        wv_t = _dot_nt(eye_ref[...], pad_rows((w * v).astype(BF16))).astype(BF16)
        c_scr[h] = dec * c_old + _dot(wv_t, pad_rows(kb))
        n_scr[h:h + 1, :] = dec * n_old + jnp.sum(w * k, axis=0, keepdims=True)
        m_scr[h:h + 1, :] = jnp.broadcast_to(m_new, (1, m_scr.shape[1]))

    @pl.when(t == pl.num_programs(1) - 1)
    def _():
        co_ref[0] = c_scr[...]
        no_ref[0] = n_scr[...]
        mo_ref[0] = m_scr[...]


def _mlstm(n, t, q, k, v, og, gt, gb, ng, c0, n0, m0, tri, eye):
    chunk = CHUNK_C if t % CHUNK_C == 0 else t
    nt = t // chunk
    w = q.shape[1]
    dh = w // C_HEADS

    def rows(c):
        return pl.BlockSpec((chunk, c), lambda i, j: (i * nt + j, 0))

    cst = pl.BlockSpec((1, C_HEADS, dh, dh), lambda i, j: (i, 0, 0, 0))
    nst = pl.BlockSpec((1, C_HEADS, dh), lambda i, j: (i, 0, 0))
    return pl.pallas_call(
        functools.partial(_mlstm_kernel, chunk=chunk),
        grid=(n, nt),
        in_specs=[rows(w), rows(w), rows(w), rows(w), rows(128),
                  pl.BlockSpec(gb.shape, lambda i, j: (0, 0)), pl.BlockSpec(ng.shape, lambda i, j: (0, 0)),
                  cst, nst, nst,
                  pl.BlockSpec(tri.shape, lambda i, j: (0, 0)), pl.BlockSpec(eye.shape, lambda i, j: (0, 0))],
        out_specs=[rows(w), cst, nst, nst],
        out_shape=[jax.ShapeDtypeStruct((n * t, w), F32), jax.ShapeDtypeStruct(c0.shape, F32),
                   jax.ShapeDtypeStruct(n0.shape, F32), jax.ShapeDtypeStruct(m0.shape, F32)],
        scratch_shapes=[pltpu.VMEM((C_HEADS, dh, dh), F32), pltpu.VMEM((C_HEADS, dh), F32),
                        pltpu.VMEM((C_HEADS, dh), F32)],
        compiler_params=_params("arbitrary", "arbitrary"),
        name="mlstm_scan",
    )(q, k, v, og, gt, gb, ng, c0, n0, m0, tri, eye)


def _flash_kernel(qi_ref, kj_ref, q_ref, k_ref, v_ref, o_ref, m_scr, l_scr, acc_scr):
    p = pl.program_id(2)
    qi = qi_ref[p]
    kj = kj_ref[p]

    @pl.when(kj == 0)
    def _():
        m_scr[...] = jnp.full_like(m_scr, NEG)
        l_scr[...] = jnp.zeros_like(l_scr)
        acc_scr[...] = jnp.zeros_like(acc_scr)

    s = _dot_nt(q_ref[...], k_ref[...])
    bq, bk = s.shape
    ri = lax.broadcasted_iota(jnp.int32, (bq, bk), 0)
    ci = lax.broadcasted_iota(jnp.int32, (bq, bk), 1)
    s = jnp.where(jnp.logical_or(kj < qi, ci <= ri), s, NEG)
    m_old = m_scr[...]
    m_new = jnp.maximum(m_old, jnp.max(s, axis=-1, keepdims=True))
    alpha = jnp.exp(m_old - m_new)
    pr = jnp.exp(s - m_new)
    l_scr[...] = alpha * l_scr[...] + jnp.sum(pr, axis=-1, keepdims=True)
    acc_scr[...] = alpha * acc_scr[...] + _dot(pr.astype(BF16), v_ref[...])
    m_scr[...] = m_new

    @pl.when(kj == qi)
    def _():
        o_ref[...] = acc_scr[...] / l_scr[...]


def _flash(n, t, qcat, kcat, v):
    blk = FLASH_BLOCK if t % FLASH_BLOCK == 0 else t
    nq = t // blk
    pairs = [(i, j) for i in range(nq) for j in range(i + 1)]
    qi = jnp.asarray([a for a, _ in pairs], jnp.int32)
    kj = jnp.asarray([b for _, b in pairs], jnp.int32)
    dv = v.shape[1] // D_HEADS
    grid_spec = pltpu.PrefetchScalarGridSpec(
        num_scalar_prefetch=2,
        grid=(n, D_HEADS, len(pairs)),
        in_specs=[pl.BlockSpec((blk, 256), lambda b, h, p, qi, kj: (b * nq + qi[p], h)),
                  pl.BlockSpec((blk, 256), lambda b, h, p, qi, kj: (b * nq + kj[p], h)),
                  pl.BlockSpec((blk, dv), lambda b, h, p, qi, kj: (b * nq + kj[p], h))],
        out_specs=pl.BlockSpec((blk, dv), lambda b, h, p, qi, kj: (b * nq + qi[p], h)),
        scratch_shapes=[pltpu.VMEM((blk, 1), F32), pltpu.VMEM((blk, 1), F32), pltpu.VMEM((blk, dv), F32)],
    )
    return pl.pallas_call(
        _flash_kernel,
        grid_spec=grid_spec,
        out_shape=jax.ShapeDtypeStruct((n * t, v.shape[1]), F32),
        compiler_params=_params("arbitrary", "arbitrary", "arbitrary"),
        name="mla_prompt_flash",
    )(qi, kj, qcat, kcat, v)


def _paged_kernel(pt_ref, u_ref, qr_ref, cn_ref, kn_ref, wukt_ref, wuv_ref, *rest, pages):
    ck_refs = rest[:pages]
    kr_refs = rest[pages:2 * pages]
    o_ref, m_scr, l_scr, acc_scr = rest[2 * pages:]
    g = pl.program_id(1)
    hq = u_ref.shape[1]
    tq = hq // D_HEADS

    @pl.when(g == 0)
    def _():
        m_scr[...] = jnp.full_like(m_scr, NEG)
        l_scr[...] = jnp.zeros_like(l_scr)
        acc_scr[...] = jnp.zeros_like(acc_scr)

    lhs = jnp.concatenate([wukt_ref[...], u_ref[0]], axis=0)
    qr = qr_ref[0]
    nk = D_HEADS * D_NOPE

    def scores(ck_b, kr_b):
        big = _dot_nt(lhs, ck_b)
        rows = []
        for h in range(D_HEADS):
            kn_t = big[h * D_NOPE:(h + 1) * D_NOPE, :]
            rinv = lax.rsqrt(jnp.sum(kn_t * kn_t, axis=0, keepdims=True) * (1.0 / D_NOPE) + EPS)
            rows.append(big[nk + h * tq:nk + (h + 1) * tq, :] * rinv)
        return jnp.concatenate(rows, axis=0) + _dot_nt(qr, kr_b)

    def update(s, ck_b):
        m_old = m_scr[...]
        m_new = jnp.maximum(m_old, jnp.max(s, axis=-1, keepdims=True))
        alpha = jnp.exp(m_old - m_new)
        pr = jnp.exp(s - m_new)
        l_scr[...] = alpha * l_scr[...] + jnp.sum(pr, axis=-1, keepdims=True)
        acc_scr[...] = alpha * acc_scr[...] + _dot(pr.astype(BF16), ck_b)
        m_scr[...] = m_new

    cks = [r[...].astype(BF16) for r in ck_refs]
    krs = [r[...].astype(BF16) for r in kr_refs]
    s_all = jnp.concatenate(
        [scores(jnp.concatenate(cks[i:i + 2], axis=0), jnp.concatenate(krs[i:i + 2], axis=0))
         for i in range(0, pages, 2)], axis=1)
    update(s_all, jnp.concatenate(cks, axis=0))

    @pl.when(g == pl.num_programs(1) - 1)
    def _():
        fill = PAGE_SIZE - tq
        ck_b = jnp.concatenate([cn_ref[...], jnp.zeros((fill, cn_ref.shape[1]), F32)], axis=0).astype(BF16)
        kr_b = jnp.concatenate([kn_ref[...], jnp.zeros((fill, kn_ref.shape[1]), F32)], axis=0).astype(BF16)
        s = scores(ck_b, kr_b)
        ri = lax.broadcasted_iota(jnp.int32, s.shape, 0)
        ci = lax.broadcasted_iota(jnp.int32, s.shape, 1)
        update(jnp.where(ci <= ri % tq, s, NEG), ck_b)
        lat = (acc_scr[...] / l_scr[...]).astype(BF16)
        full = _dot(lat, wuv_ref[...])
        dv = wuv_ref.shape[1] // D_HEADS
        o_ref[...] = jnp.concatenate(
            [full[h * tq:(h + 1) * tq, h * dv:(h + 1) * dv] for h in range(D_HEADS)], axis=1)


def _paged(page_table, li, u3, qr3, ckv_new, kr_new, wuk_t, wuv, cache_ckv, cache_kr):
    n, hq, lat = u3.shape
    tq = hq // D_HEADS
    n_pages = page_table.shape[1]
    pages = min(PAGES_PER_STEP, n_pages)
    assert n_pages % pages == 0 and pages % 2 == 0
    ng = n_pages // pages

    def page_spec(i, width):
        return pl.BlockSpec((None, None, PAGE_SIZE, width), lambda b, g, pt: (pt[b, g * pages + i], li, 0, 0))

    grid_spec = pltpu.PrefetchScalarGridSpec(
        num_scalar_prefetch=1,
        grid=(n, ng),
        in_specs=[pl.BlockSpec((1, hq, lat), lambda b, g, pt: (b, 0, 0)),
                  pl.BlockSpec((1, hq, D_ROPE), lambda b, g, pt: (b, 0, 0)),
                  pl.BlockSpec((tq, lat), lambda b, g, pt: (b, 0)),
                  pl.BlockSpec((tq, D_ROPE), lambda b, g, pt: (b, 0)),
                  pl.BlockSpec(wuk_t.shape, lambda b, g, pt: (0, 0)),
                  pl.BlockSpec(wuv.shape, lambda b, g, pt: (0, 0))]
                 + [page_spec(i, lat) for i in range(pages)]
                 + [page_spec(i, D_ROPE) for i in range(pages)],
        out_specs=pl.BlockSpec((tq, wuv.shape[1]), lambda b, g, pt: (b, 0)),
        scratch_shapes=[pltpu.VMEM((hq, 1), F32), pltpu.VMEM((hq, 1), F32), pltpu.VMEM((hq, lat), F32)],
    )
    return pl.pallas_call(
        functools.partial(_paged_kernel, pages=pages),
        grid_spec=grid_spec,
        out_shape=jax.ShapeDtypeStruct((n * tq, wuv.shape[1]), F32),
        compiler_params=_params("arbitrary", "arbitrary"),
        name="mla_sample_paged",
    )(page_table, u3, qr3, ckv_new, kr_new, wuk_t, wuv, *([cache_ckv] * pages), *([cache_kr] * pages))


def _np_seg(n_seg, seg_in, seg_out):
    mat = np.zeros((n_seg * seg_in, n_seg * seg_out), np.float32)
    for s in range(n_seg):
        mat[s * seg_in:(s + 1) * seg_in, s * seg_out:(s + 1) * seg_out] = 1.0
    return mat


def _rot_cols(w):
    half = w.shape[-1] // 2
    return jnp.concatenate([-w[..., half:], w[..., :half]], axis=-1)


def _swap_halves(g):
    half = g.shape[-1] // 2
    return jnp.concatenate([g[..., half:], g[..., :half]], axis=-1)


def _rope_tables(pos, reps):
    half = D_ROPE // 2
    inv = ROPE_BASE ** (-jnp.arange(half, dtype=F32) / half)
    ang = pos.astype(F32)[:, None] * inv
    cos = jnp.concatenate([jnp.cos(ang), jnp.cos(ang)], axis=-1)
    sin = jnp.concatenate([jnp.sin(ang), jnp.sin(ang)], axis=-1)
    return jnp.tile(cos, (1, reps)), jnp.tile(sin, (1, reps))


def _trunk(x, c_mod, pos0, gla0, mc0, mn0, mm0, p, sample_ctx):
    n, t, d = x.shape
    m = n * t
    rt = _Rows(n, t)
    eye_r = jnp.eye(rt.rows, dtype=BF16)
    dh = mc0.shape[3]
    eye_hk = jnp.eye(B_HEADS * gla0.shape[3], dtype=BF16)
    eye_dh = jnp.eye(dh, dtype=BF16)
    results = {}

    layer, li = 0, 0
    mod = c_mod[layer]
    w_in = p['w_in_even'][li]
    w_in = jnp.concatenate([w_in, jnp.zeros((d, 128 - B_GATE_RANK), F32)], axis=1).astype(BF16)
    a_ws = jnp.tril(p['a_ws'][li])
    a_bs = p['a_bs'][li]
    if t % CHUNK_A == 0:
        ws = a_ws
        bs = jnp.repeat(a_bs.T, CHUNK_A, axis=1)
    else:
        ws = jnp.stack([jnp.kron(jnp.eye(CHUNK_A // t, dtype=F32), a_ws[g, :t, :t]) for g in range(A_GROUPS)])
        bs = jnp.repeat(jnp.tile(a_bs[:, :t], (1, CHUNK_A // t)).T, CHUNK_A, axis=1)
    out_a, v_rows, bq, bk, bv, br, bg = _even_in(
        x, mod, p['norm_mix_g'][layer].reshape(1, d), w_in, p['a_norm_g'][li].reshape(1, -1),
        ws.astype(BF16), bs)
    hk = bq.shape[1]
    dv = bv.shape[1] // B_HEADS
    wg2 = jnp.concatenate([p['b_w_gate2'][li], jnp.zeros((128 - B_GATE_RANK, hk), F32)], axis=0)
    seg = jnp.asarray(_np_seg(B_HEADS, hk // B_HEADS, dv), BF16)
    out_b, s_new = _gla(n, t, bq, bk, bv, br, bg, wg2, p['b_gate_bias'][li].reshape(1, hk),
                        jnp.tile(p['b_norm_g'][li], B_HEADS).reshape(1, -1), gla0[li], seg, eye_hk)
    results['gla'] = s_new
    results['v_rows'] = v_rows
    x1, h2, idx_t, w3 = _out_route(
        x, mod, out_a, out_b, p['w_out'][layer].astype(BF16), p['norm_ffn_g'][layer].reshape(1, d),
        p['w_router'][layer].T, p['b_router'][layer].reshape(-1, 1), eye_r)
    x = _moe(layer, x1, mod, h2, idx_t, w3, p)

    layer, li = 1, 0
    mod = c_mod[layer]
    w = p['w_in_odd'][li]
    hw = C_HEADS * dh
    o_g = 3 * hw
    o_o = o_g + 2 * C_HEADS
    o_qa = o_o + hw
    o_kva = o_qa + p['d_g_qa'].shape[1]
    o_kr = o_kva + p['d_g_kva'].shape[1]
    w_kr = w[:, o_kr:o_kr + D_ROPE]
    w_odd = jnp.concatenate(
        [w[:, :o_g], w[:, o_o:o_qa], w[:, o_qa:o_kva], w[:, o_kva:o_kr], w_kr, _rot_cols(w_kr),
         w[:, o_g:o_o], jnp.zeros((d, 128 - 2 * C_HEADS), F32)], axis=1).astype(BF16)
    wqb = p['d_w_qb'][li].reshape(-1, D_HEADS, D_NOPE + D_ROPE)
    wqb_r = wqb[:, :, D_NOPE:]
    wqb2 = jnp.concatenate([wqb[:, :, :D_NOPE].reshape(-1, D_HEADS * D_NOPE),
                            wqb_r.reshape(-1, D_HEADS * D_ROPE),
                            _rot_cols(wqb_r).reshape(-1, D_HEADS * D_ROPE)], axis=1).astype(BF16)
    w_uk = p['d_w_uk'][li]
    lat = w_uk.shape[0]
    g_qr = p['d_g_qr'][li]
    g_kr = p['d_g_kr'][li]
    cos, sin = _rope_tables(pos0 + jnp.arange(t), D_HEADS)
    prompt = sample_ctx is None
    if not prompt:
        cos, sin = jnp.tile(cos, (rt.s, 1)), jnp.tile(sin, (rt.s, 1))
    consts = [p['d_g_qa'][li].reshape(1, -1), wqb2, p['d_g_qn'][li].reshape(1, -1),
              jnp.tile(g_qr, D_HEADS).reshape(1, -1), jnp.tile(_swap_halves(g_qr), D_HEADS).reshape(1, -1),
              p['d_g_kva'][li].reshape(1, -1), g_kr.reshape(1, -1), _swap_halves(g_kr).reshape(1, -1),
              p['d_g_kn'][li].reshape(1, -1),
              jnp.asarray(_np_seg(D_HEADS, D_ROPE, D_ROPE), BF16),
              w_uk.reshape(lat, -1).astype(BF16), p['d_w_uv'][li].reshape(lat, -1).astype(BF16)]
    outs = _odd_in(x, mod, p['norm_mix_g'][layer].reshape(1, d), w_odd, consts, cos, sin, prompt)
    cq, ck, cv, co, gates, ckv, kr = outs[:7]
    gb = jnp.concatenate([p['c_ig_bias'][li], p['c_fg_bias'][li],
                          jnp.zeros((128 - 2 * C_HEADS,), F32)]).reshape(1, 128)
    chunk = CHUNK_C if t % CHUNK_C == 0 else t
    tri = jnp.asarray(np.tril(np.ones((chunk, chunk), np.float32)), BF16)
    m0b =jnp.broadcast_to(mm0[li][:, :, None], (n, C_HEADS, dh))
    out_c, c_new, n_new, m_new = _mlstm(n, t, cq, ck, cv, co, gates, gb,
                                        p['c_norm_g'][li].reshape(1, -1), mc0[li], mn0[li], m0b, tri, eye_dh)
    results['mlstm'] = (c_new, n_new, m_new[:, :, 0])
    results['ckv'] = ckv.reshape(n, t, -1)
    results['kr'] = kr.reshape(n, t, -1)
    if prompt:
        qcat, kcat, vv = outs[7:]
        out_d = _flash(n, t, qcat, kcat, vv)
    else:
        u, qr = outs[7:]
        cache_ckv, cache_kr, page_table = sample_ctx
        u3 = u.reshape(n, t, D_HEADS, lat).transpose(0, 2, 1, 3).reshape(n, D_HEADS * t, lat)
        qr3 = qr.reshape(n, t, D_HEADS, D_ROPE).transpose(0, 2, 1, 3).reshape(n, D_HEADS * t, D_ROPE)
        wuk_t = w_uk.transpose(1, 2, 0).reshape(-1, lat).astype(BF16)
        out_d = _paged(page_table, li, u3, qr3, ckv, kr, wuk_t, consts[11], cache_ckv, cache_kr)
    x1, h2, idx_t, w3 = _out_route(
        x, mod, out_c, out_d, p['w_out'][layer].astype(BF16), p['norm_ffn_g'][layer].reshape(1, d),
        p['w_router'][layer].T, p['b_router'][layer].reshape(-1, 1), eye_r)
    x = _moe(layer, x1, mod, h2, idx_t, w3, p)
    return x, results


def kernel(x_prompt, x_sample, state_gla, state_mlstm_c, state_mlstm_n, state_mlstm_m,
           cache_ckv, cache_krope, page_table, c_prompt, c_sample,
           norm_mix_g, norm_ffn_g, w_ada, b_ada, w_out,
           w_in_even, a_norm_g, a_ws, a_bs, b_w_gate2, b_gate_bias, b_norm_g,
           w_in_odd, c_ig_bias, c_fg_bias, c_norm_g,
           d_g_qa, d_w_qb, d_g_kva, d_g_qn, d_g_qr, d_g_kr, d_g_kn, d_w_uk, d_w_uv,
           w_router, b_router, w_gate_e, w_up_e, w_down_e, w_gate_s, w_up_s, w_down_s):
    p = dict(norm_mix_g=norm_mix_g, norm_ffn_g=norm_ffn_g, w_out=w_out,
             w_in_even=w_in_even, a_norm_g=a_norm_g, a_ws=a_ws, a_bs=a_bs, b_w_gate2=b_w_gate2,
             b_gate_bias=b_gate_bias, b_norm_g=b_norm_g, w_in_odd=w_in_odd, c_ig_bias=c_ig_bias,
             c_fg_bias=c_fg_bias, c_norm_g=c_norm_g, d_g_qa=d_g_qa, d_w_qb=d_w_qb, d_g_kva=d_g_kva,
             d_g_qn=d_g_qn, d_g_qr=d_g_qr, d_g_kr=d_g_kr, d_g_kn=d_g_kn, d_w_uk=d_w_uk, d_w_uv=d_w_uv,
             w_router=w_router, b_router=b_router, w_gate_e=w_gate_e, w_up_e=w_up_e, w_down_e=w_down_e,
             w_gate_s=w_gate_s, w_up_s=w_up_s, w_down_s=w_down_s)
    n_p, t_p, d = x_prompt.shape
    n_s, t_s, _ = x_sample.shape
    depth = w_ada.shape[0]
    pad_p = (-n_p) % 8
    c_all = jnp.concatenate([c_prompt, jnp.zeros((pad_p, d), F32), c_sample], axis=0)
    mod_all = _ada(c_all, w_ada, b_ada).reshape(depth, c_all.shape[0], 6, d)
    mod_p = mod_all[:, :n_p]
    mod_s = mod_all[:, n_p + pad_p:]

    n_even, _, bh, bdk, bdv = state_gla.shape
    n_odd, _, chh, cdh, _ = state_mlstm_c.shape
    gla0_p = jnp.zeros((n_even, n_p, bh, bdk, bdv), F32)
    mc0_p = jnp.zeros((n_odd, n_p, chh, cdh, cdh), F32)
    mn0_p = jnp.zeros((n_odd, n_p, chh, cdh), F32)
    mm0_p = jnp.full((n_odd, n_p, chh), NEG, F32)
    past_len = page_table.shape[1] * PAGE_SIZE

    y_p, rp = _trunk(x_prompt, mod_p, 0, gla0_p, mc0_p, mn0_p, mm0_p, p, None)
    y_s, rs = _trunk(x_sample, mod_s, past_len, state_gla, state_mlstm_c, state_mlstm_n, state_mlstm_m, p,
                     (cache_ckv, cache_krope, page_table))
    aw = rs['v_rows'].shape[1]
    return (y_p, y_s, rp['gla'][None], rs['gla'][None], rs['v_rows'].reshape(1, n_s, t_s, aw),
            rp['mlstm'][0][None], rs['mlstm'][0][None], rp['mlstm'][1][None], rs['mlstm'][1][None],
            rp['mlstm'][2][None], rs['mlstm'][2][None],
            rp['ckv'][:, None], rs['ckv'][:, None], rp['kr'][:, None], rs['kr'][:, None])
```

```python
import functools

import numpy as np
import jax
import jax.numpy as jnp
from jax import lax
from jax.experimental import pallas as pl
from jax.experimental.pallas import tpu as pltpu

F32 = jnp.float32
BF16 = jnp.bfloat16

EPS = 1e-6
NEG = -1e30

A_GROUPS = 4
CHUNK_A = 128
B_HEADS = 4
B_GATE_RANK = 16
B_GATE_TAU = 16.0
CHUNK_B = 16
C_HEADS = 4
CHUNK_C = 128
D_HEADS = 4
D_NOPE = 128
D_ROPE = 64
ROPE_BASE = 10000.0
MLA_SCALE = (D_NOPE + D_ROPE) ** -0.5
PAGE_SIZE = 128
N_EXPERT_GROUPS = 8
TOPK_GROUPS = 4
TOP_K = 8
ROUTED_SCALE = 2.5

ROW_TILE = 256
SAMPLE_SEQS = 32
FLASH_BLOCK = 512
PAGES_PER_STEP = 16
VMEM_LIMIT = 56 * 1024 * 1024


def _dot(a, b):
    return jnp.dot(a, b, preferred_element_type=F32)


def _dot_nt(a, b):
    return lax.dot_general(a, b, (((1,), (1,)), ((), ())), preferred_element_type=F32)


def _split2(x):
    hi = x.astype(BF16)
    lo = (x - hi.astype(F32)).astype(BF16)
    return hi, lo


def _split3(x):
    hi = x.astype(BF16)
    r = x - hi.astype(F32)
    mid = r.astype(BF16)
    lo = (r - mid.astype(F32)).astype(BF16)
    return hi, mid, lo


def _dot3(a, b):
    ah, al = _split2(a)
    bh, bl = _split2(b)
    return _dot(ah, bh) + _dot(ah, bl) + _dot(al, bh)


def _dot3_nt(a, b):
    ah, al = _split2(a)
    bh, bl = _split2(b)
    return _dot_nt(ah, bh) + _dot_nt(ah, bl) + _dot_nt(al, bh)


def _dot_sel(x, m01, parts=2):
    ps = _split2(x) if parts == 2 else _split3(x)
    acc = _dot(ps[0], m01)
    for p in ps[1:]:
        acc = acc + _dot(p, m01)
    return acc


def _sigmoid(x):
    return 1.0 / (1.0 + jnp.exp(-x))


def _silu(x):
    return x * _sigmoid(x)


def _log_sigmoid(x):
    return jnp.minimum(x, 0.0) - jnp.log(1.0 + jnp.exp(-jnp.abs(x)))


def _gelu(x):
    return 0.5 * x * (1.0 + jnp.tanh(0.7978845608028654 * (x + 0.044715 * x * x * x)))


def _rms(x, eps=EPS):
    return x * lax.rsqrt(jnp.mean(x * x, axis=-1, keepdims=True) + eps)


def _params(*sem):
    return pltpu.CompilerParams(dimension_semantics=sem, vmem_limit_bytes=VMEM_LIMIT)


def _const_spec(shape):
    nd = len(shape)
    return pl.BlockSpec(shape, lambda *_: (0,) * nd)


class _Rows:
    def __init__(self, n, t):
        self.n, self.t = n, t
        if t % ROW_TILE == 0:
            self.s, self.r = 1, ROW_TILE
            self.tpb = t // ROW_TILE
        else:
            assert t == 8 and n % SAMPLE_SEQS == 0, (n, t)
            self.s, self.r = SAMPLE_SEQS, t
            self.tpb = 1
        self.rows = self.s * self.r
        self.steps = n * t // self.rows

    def seq_spec(self, mid, d):
        s, tpb = self.s, self.tpb
        if mid == self.t:
            return pl.BlockSpec((s, self.r, d), lambda i: (i // tpb, i % tpb, 0))
        return pl.BlockSpec((s, mid, d), lambda i: (i // tpb, 0, 0))

    def row_spec(self, c):
        return pl.BlockSpec((self.rows, c), lambda i: (i, 0))

    def pos_spec(self, c):
        tpb = self.tpb
        return pl.BlockSpec((self.rows, c), lambda i: (i % tpb, 0))


def _ada_kernel(c_ref, w_ref, b_ref, o_ref):
    o_ref[0] = _dot3(_silu(c_ref[...]), w_ref[0]) + b_ref[0]


def _ada(c_all, w_ada, b_ada):
    depth, d, d6 = w_ada.shape
    nc = c_all.shape[0]
    tn = 1536
    return pl.pallas_call(
        _ada_kernel,
        grid=(depth, d6 // tn),
        in_specs=[pl.BlockSpec((nc, d), lambda l, j: (0, 0)),
                  pl.BlockSpec((1, d, tn), lambda l, j: (l, 0, j)),
                  pl.BlockSpec((1, 1, tn), lambda l, j: (l, 0, j))],
        out_specs=pl.BlockSpec((1, nc, tn), lambda l, j: (l, 0, j)),
        out_shape=jax.ShapeDtypeStruct((depth, nc, d6), F32),
        compiler_params=_params("arbitrary", "arbitrary"),
        name="ada_modulation",
    )(c_all, w_ada, b_ada.reshape(depth, 1, d6))


def _norm_mod(x_ref, mod_ref, g_ref, shift_row, scale_row):
    x = x_ref[...]
    h = _rms(x) * g_ref[...]
    h = h * (1.0 + mod_ref[:, scale_row:scale_row + 1, :]) + mod_ref[:, shift_row:shift_row + 1, :]
    s, r, d = x.shape
    return h.reshape(s * r, d)


def _even_in_kernel(x_ref, mod_ref, g_ref, w_ref, an_ref, ws_ref, bs_ref,
                    oa_ref, v_ref, q_ref, k_ref, bv_ref, r_ref, bg_ref):
    h = _norm_mod(x_ref, mod_ref, g_ref, 0, 1).astype(BF16)
    z = _dot(h, w_ref[...])
    aw = A_GROUPS * CHUNK_A
    rows = z.shape[0]
    u = _gelu(z[:, 0:aw])
    va = _gelu(z[:, aw:2 * aw])
    vn = jnp.concatenate(
        [_rms(va[:, g * CHUNK_A:(g + 1) * CHUNK_A]) for g in range(A_GROUPS)], axis=1) * an_ref[...]
    v_ref[...] = vn
    vb = vn.astype(BF16)
    for c in range(rows // CHUNK_A):
        r0 = c * CHUNK_A
        mixed = jnp.concatenate(
            [_dot(ws_ref[g], vb[r0:r0 + CHUNK_A, g * CHUNK_A:(g + 1) * CHUNK_A]) for g in range(A_GROUPS)],
            axis=1) + bs_ref[...]
        oa_ref[r0:r0 + CHUNK_A, :] = u[r0:r0 + CHUNK_A, :] * mixed
    o = 2 * aw
    q_ref[...] = z[:, o:o + 256]
    k_ref[...] = z[:, o + 256:o + 512]
    bv_ref[...] = z[:, o + 512:o + 1024]
    r_ref[...] = z[:, o + 1024:o + 1536]
    bg_ref[...] = z[:, o + 1536:o + 1664]


def _even_in(x, mod, g, w, an, ws, bs):
    n, t, d = x.shape
    rt = _Rows(n, t)
    m = n * t
    widths = (512, 512, 256, 256, 512, 512, 128)
    return pl.pallas_call(
        _even_in_kernel,
        grid=(rt.steps,),
        in_specs=[rt.seq_spec(t, d), rt.seq_spec(6, d), _const_spec((1, d)), _const_spec(w.shape),
                  _const_spec(an.shape), _const_spec(ws.shape), _const_spec(bs.shape)],
        out_specs=[rt.row_spec(c) for c in widths],
        out_shape=[jax.ShapeDtypeStruct((m, c), F32) for c in widths],
        compiler_params=_params("arbitrary"),
        name="even_in_chunk_gate",
    )(x, mod, g, w, an, ws, bs)


def _gla_kernel(q_ref, k_ref, v_ref, r_ref, g_ref, wg_ref, gb_ref, ng_ref, s0_ref, seg_ref, eye_ref,
                o_ref, so_ref, s_scr, lg_scr, o_scr, *, chunk, n_chunks):
    hk = q_ref.shape[1]
    dk = hk // B_HEADS
    dv = v_ref.shape[1] // B_HEADS
    t = pl.program_id(1)

    @pl.when(t == 0)
    def _():
        s_scr[...] = s0_ref[0].reshape(hk, dv)

    lg_scr[...] = _log_sigmoid(_dot3(g_ref[...], wg_ref[...]) + gb_ref[...]) * (1.0 / B_GATE_TAU)
    row = lax.broadcasted_iota(jnp.int32, (chunk, hk), 0)
    lane = lax.broadcasted_iota(jnp.int32, (chunk, hk), 1)
    pad = 16 - chunk if chunk < 16 else 0

    def pad_rows(a):
        if pad:
            return jnp.concatenate([a, jnp.zeros((pad, a.shape[1]), a.dtype)], axis=0)
        return a

    def body(c, carry):
        r0 = pl.multiple_of(c * chunk, chunk)
        lg = lg_scr[pl.ds(r0, chunk), :]
        q = q_ref[pl.ds(r0, chunk), :] * (dk ** -0.5)
        k = k_ref[pl.ds(r0, chunk), :]
        v = v_ref[pl.ds(r0, chunk), :]
        b = jnp.zeros_like(lg)
        for s in range(chunk):
            b = b + jnp.where(row >= s, lg[s:s + 1, :], 0.0)
        prods = []
        for l in range(chunk):
            e = jnp.exp(jnp.where(row <= l, b[l:l + 1, :] - b, NEG))
            prods.append(q[l:l + 1, :] * k * e)
        att = _dot_sel(jnp.concatenate(prods, axis=0), seg_ref[...])
        o_intra = jnp.concatenate(
            [jnp.sum(att[l * chunk:(l + 1) * chunk, :] * v, axis=0, keepdims=True) for l in range(chunk)], axis=0)
        s_old = s_scr[...]
        s_b = s_old.astype(BF16)
        qe = q * jnp.exp(b)
        o_inter = jnp.concatenate(
            [_dot(jnp.where(lane // dk == h, qe, 0.0).astype(BF16), s_b) for h in range(B_HEADS)], axis=1)
        o_scr[pl.ds(r0, chunk), :] = o_intra + o_inter
        b_last = b[chunk - 1:chunk, :]
        k_end = pad_rows((k * jnp.exp(b_last - b)).astype(BF16))
        k_t = _dot_nt(eye_ref[...], k_end).astype(BF16)
        dh, dm, dl = _split3(jnp.exp(b_last))
        dec3 = _dot_nt(eye_ref[...], jnp.concatenate([dh, dm, dl, jnp.zeros((13, hk), BF16)], axis=0))
        dec = dec3[:, 0:1] + dec3[:, 1:2] + dec3[:, 2:3]
        v_b = pad_rows(v.astype(BF16))
        kv = jnp.concatenate(
            [_dot(k_t[h * dk:(h + 1) * dk, :], v_b[:, h * dv:(h + 1) * dv]) for h in range(B_HEADS)], axis=0)
        s_scr[...] = dec * s_old + kv
        return carry

    lax.fori_loop(0, n_chunks, body, 0)

    o = o_scr[...]
    on = jnp.concatenate([_rms(o[:, h * dv:(h + 1) * dv]) for h in range(B_HEADS)], axis=1) * ng_ref[...]
    o_ref[...] = on * _silu(r_ref[...])

    @pl.when(t == pl.num_programs(1) - 1)
    def _():
        so_ref[0] = s_scr[...].reshape(B_HEADS, dk, dv)


def _gla(n, t, q, k, v, r, g, wg, gb, ng, s0, seg, eye):
    tt = ROW_TILE if t % ROW_TILE == 0 else t
    chunk = CHUNK_B if tt % CHUNK_B == 0 else tt
    nt = t // tt
    hk, hv = q.shape[1], v.shape[1]
    dk, dv = hk // B_HEADS, hv // B_HEADS

    def rows(c):
        return pl.BlockSpec((tt, c), lambda i, j: (i * nt + j, 0))

    st = pl.BlockSpec((1, B_HEADS, dk, dv), lambda i, j: (i, 0, 0, 0))
    return pl.pallas_call(
        functools.partial(_gla_kernel, chunk=chunk, n_chunks=tt // chunk),
        grid=(n, nt),
        in_specs=[rows(hk), rows(hk), rows(hv), rows(hv), rows(g.shape[1]),
                  pl.BlockSpec(wg.shape, lambda i, j: (0, 0)), pl.BlockSpec(gb.shape, lambda i, j: (0, 0)),
                  pl.BlockSpec(ng.shape, lambda i, j: (0, 0)), st,
                  pl.BlockSpec(seg.shape, lambda i, j: (0, 0)), pl.BlockSpec(eye.shape, lambda i, j: (0, 0))],
        out_specs=[rows(hv), st],
        out_shape=[jax.ShapeDtypeStruct((n * t, hv), F32), jax.ShapeDtypeStruct(s0.shape, F32)],
        scratch_shapes=[pltpu.VMEM((hk, dv), F32), pltpu.VMEM((tt, hk), F32), pltpu.VMEM((tt, hv), F32)],
        compiler_params=_params("arbitrary", "arbitrary"),
        name="gla_scan",
    )(q, k, v, r, g, wg, gb, ng, s0, seg, eye)


def _route(logits_t, bias_col):
    n_e, r = logits_t.shape
    per = n_e // N_EXPERT_GROUPS
    scores = _sigmoid(logits_t)
    biased = scores + bias_col
    sub = lax.broadcasted_iota(jnp.int32, (per, r), 0).astype(F32)
    ninf = -jnp.inf
    gs = []
    for g in range(N_EXPERT_GROUPS):
        blk = biased[g * per:(g + 1) * per, :]
        m1 = jnp.max(blk, axis=0, keepdims=True)
        i1 = jnp.min(jnp.where(blk == m1, sub, float(per)), axis=0, keepdims=True)
        m2 = jnp.max(jnp.where(sub == i1, ninf, blk), axis=0, keepdims=True)
        gs.append(m1 + m2)
    cur = jnp.concatenate(gs, axis=0)
    gsub = lax.broadcasted_iota(jnp.int32, (N_EXPERT_GROUPS, r), 0).astype(F32)
    gsel = jnp.zeros((N_EXPERT_GROUPS, r), F32)
    for _ in range(TOPK_GROUPS):
        m = jnp.max(cur, axis=0, keepdims=True)
        i = jnp.min(jnp.where(cur == m, gsub, float(N_EXPERT_GROUPS)), axis=0, keepdims=True)
        hit = gsub == i
        gsel = jnp.where(hit, 1.0, gsel)
        cur = jnp.where(hit, ninf, cur)
    cur = jnp.concatenate(
        [jnp.where(gsel[g:g + 1, :] > 0.5, biased[g * per:(g + 1) * per, :], ninf) for g in range(N_EXPERT_GROUPS)],
        axis=0)
    esub = lax.broadcasted_iota(jnp.int32, (n_e, r), 0).astype(F32)
    idx, wts, hits = [], [], []
    for _ in range(TOP_K):
        m = jnp.max(cur, axis=0, keepdims=True)
        i = jnp.min(jnp.where(cur == m, esub, float(n_e)), axis=0, keepdims=True)
        hit = esub == i
        idx.append(i)
        hits.append(hit)
        wts.append(jnp.sum(jnp.where(hit, scores, 0.0), axis=0, keepdims=True))
        cur = jnp.where(hit, ninf, cur)
    w = jnp.concatenate(wts, axis=0)
    w = w / jnp.sum(w, axis=0, keepdims=True) * ROUTED_SCALE
    return jnp.concatenate(idx, axis=0), w, hits


def _pack_pairs(x):
    w = x.shape[1] // 2
    hi = lax.bitcast_convert_type(x[:, :w].astype(BF16).astype(F32), jnp.uint32)
    lo = lax.bitcast_convert_type(x[:, w:].astype(BF16).astype(F32), jnp.uint32)
    return hi | (lo >> 16)


def _unpack_pairs(pk):
    a = lax.bitcast_convert_type(pk & jnp.uint32(0xFFFF0000), F32)
    b = lax.bitcast_convert_type(pk << 16, F32)
    return jnp.concatenate([a, b], axis=1)


def _out_kernel(x_ref, mod_ref, ma_ref, mb_ref, wo_ref, g_ref, wr_ref, br_ref, eye_ref, triu_ref,
                x1_ref, hp_ref, idx_ref, rank_ref, w3_ref, cnt_ref, run_scr):
    step = pl.program_id(0)

    @pl.when(step == 0)
    def _():
        run_scr[...] = jnp.zeros_like(run_scr)

    half = ma_ref.shape[1]
    y = _dot(ma_ref[...].astype(BF16), wo_ref[0:half, :]) + _dot(mb_ref[...].astype(BF16), wo_ref[half:, :])
    x = x_ref[...]
    s, r, d = x.shape
    x1 = x + mod_ref[:, 2:3, :] * y.reshape(s, r, d)
    x1_ref[...] = x1
    h2 = (_rms(x1) * g_ref[...] * (1.0 + mod_ref[:, 4:5, :]) + mod_ref[:, 3:4, :]).reshape(s * r, d)
    hp_ref[...] = _pack_pairs(h2)
    idx, w, hits = _route(_dot3_nt(wr_ref[...], h2), br_ref[...])
    idx_ref[0] = idx.astype(jnp.int32)
    sel = jnp.zeros(hits[0].shape, F32)
    for hit in hits:
        sel = jnp.where(hit, 1.0, sel)
    before = run_scr[:, 0:1] + _dot(sel.astype(BF16), triu_ref[...]) - sel
    rank_ref[0] = jnp.concatenate(
        [jnp.sum(jnp.where(hit, before, 0.0), axis=0, keepdims=True) for hit in hits], axis=0).astype(jnp.int32)
    run_scr[...] = run_scr[...] + jnp.sum(sel, axis=1, keepdims=True)
    cnt_ref[...] = run_scr[...]
    wh, wm, wl = _split3(w)
    stack = jnp.concatenate([wh, wm, wl, jnp.zeros((128 - 3 * TOP_K, s * r), BF16)], axis=0)
    w3_ref[...] = _dot_nt(eye_ref[...], stack)


def _out_route(x, mod, mix_a, mix_b, wo, g, wr_t, br, eye, triu):
    n, t, d = x.shape
    rt = _Rows(n, t)
    m = n * t
    n_e = wr_t.shape[0]
    slot = pl.BlockSpec((1, TOP_K, rt.rows), lambda i: (i, 0, 0))
    return pl.pallas_call(
        _out_kernel,
        grid=(rt.steps,),
        in_specs=[rt.seq_spec(t, d), rt.seq_spec(6, d), rt.row_spec(mix_a.shape[1]), rt.row_spec(mix_b.shape[1]),
                  _const_spec(wo.shape), _const_spec((1, d)), _const_spec(wr_t.shape), _const_spec(br.shape),
                  _const_spec(eye.shape), _const_spec(triu.shape)],
        out_specs=[rt.seq_spec(t, d), rt.row_spec(d // 2), slot, slot, rt.row_spec(128), _const_spec((n_e, 128))],
        out_shape=[jax.ShapeDtypeStruct((n, t, d), F32), jax.ShapeDtypeStruct((m, d // 2), jnp.uint32),
                   jax.ShapeDtypeStruct((rt.steps, TOP_K, rt.rows), jnp.int32),
                   jax.ShapeDtypeStruct((rt.steps, TOP_K, rt.rows), jnp.int32),
                   jax.ShapeDtypeStruct((m, 128), F32), jax.ShapeDtypeStruct((n_e, 128), F32)],
        scratch_shapes=[pltpu.VMEM((n_e, 128), F32)],
        compiler_params=_params("arbitrary"),
        name="out_proj_route",
    )(x, mod, mix_a, mix_b, wo, g, wr_t, br, eye, triu)


def _row_copy(src_ref, src_row, dst_ref, dst_row, sem):
    return pltpu.make_async_copy(src_ref.at[pl.ds(src_row, 1), :], dst_ref.at[pl.ds(dst_row, 1), :], sem)


def _dispatch_kernel(dest_ref, hp_ref, xs_ref, sem):
    rows = hp_ref.shape[0]
    shift = rows.bit_length() - 1

    def issue(j, carry):
        k = lax.shift_right_logical(j, shift)
        tok = jnp.bitwise_and(j, rows - 1)
        _row_copy(hp_ref, tok, xs_ref, dest_ref[0, k, tok], sem).start()
        return carry

    lax.fori_loop(0, TOP_K * rows, issue, 0, unroll=8)

    def drain(j, carry):
        _row_copy(hp_ref, 0, xs_ref, 0, sem).wait()
        return carry

    lax.fori_loop(0, TOP_K * rows, drain, 0, unroll=8)


def _dispatch(dest3, hp, n_rows):
    steps, _, rows = dest3.shape
    assert rows & (rows - 1) == 0
    return pl.pallas_call(
        _dispatch_kernel,
        grid=(steps,),
        in_specs=[pl.BlockSpec((1, TOP_K, rows), lambda i: (i, 0, 0), memory_space=pltpu.SMEM),
                  pl.BlockSpec((rows, hp.shape[1]), lambda i: (i, 0))],
        out_specs=pl.BlockSpec(memory_space=pl.ANY),
        out_shape=jax.ShapeDtypeStruct((n_rows, hp.shape[1]), jnp.uint32),
        scratch_shapes=[pltpu.SemaphoreType.DMA(())],
        compiler_params=_params("arbitrary"),
        name="moe_dispatch_rows",
    )(dest3, hp)


def _gmm_kernel(be_ref, va_ref, nu_ref, x_ref, wg_ref, wu_ref, wd_ref, y_ref):
    i = pl.program_id(0)

    @pl.when(i < nu_ref[0])
    def _():
        x = _unpack_pairs(x_ref[...])
        live = lax.broadcasted_iota(jnp.int32, x.shape, 0) < va_ref[i]
        x = jnp.where(live, x, 0.0).astype(BF16)
        a = _silu(_dot(x, wg_ref[...].astype(BF16))) * _dot(x, wu_ref[...].astype(BF16))
        y_ref[...] = _pack_pairs(_dot(a.astype(BF16), wd_ref[...].astype(BF16)))

    @pl.when(i >= nu_ref[0])
    def _():
        y_ref[...] = jnp.zeros_like(y_ref)


def _gmm(layer, block_e, valid, n_used, xs, wg, wu, wd, tm):
    r, dp = xs.shape
    d, de = wg.shape[-2:]
    nb = r // tm
    grid_spec = pltpu.PrefetchScalarGridSpec(
        num_scalar_prefetch=3,
        grid=(nb,),
        in_specs=[pl.BlockSpec((tm, dp), lambda i, be, va, nu: (i, 0)),
                  pl.BlockSpec((None, None, d, de), lambda i, be, va, nu: (layer, be[i], 0, 0)),
                  pl.BlockSpec((None, None, d, de), lambda i, be, va, nu: (layer, be[i], 0, 0)),
                  pl.BlockSpec((None, None, de, d), lambda i, be, va, nu: (layer, be[i], 0, 0))],
        out_specs=pl.BlockSpec((tm, dp), lambda i, be, va, nu: (i, 0)),
    )
    return pl.pallas_call(
        _gmm_kernel,
        grid_spec=grid_spec,
        out_shape=jax.ShapeDtypeStruct((r, dp), jnp.uint32),
        compiler_params=_params("arbitrary"),
        name="moe_grouped_swiglu",
    )(block_e, valid, n_used, xs, wg, wu, wd)


def _combine_kernel(dest_ref, x_ref, mod_ref, hp_ref, w3_ref, wg_ref, wu_ref, wd_ref, ys_ref, o_ref, yg_scr, sem):
    rows = hp_ref.shape[0]
    shift = rows.bit_length() - 1

    def issue(j, carry):
        k = lax.shift_right_logical(j, shift)
        tok = jnp.bitwise_and(j, rows - 1)
        _row_copy(ys_ref, dest_ref[0, k, tok], yg_scr.at[k], tok, sem).start()
        return carry

    lax.fori_loop(0, TOP_K * rows, issue, 0, unroll=8)
    h = _unpack_pairs(hp_ref[...]).astype(BF16)
    a = _silu(_dot(h, wg_ref[...])) * _dot(h, wu_ref[...])
    acc = _dot(a.astype(BF16), wd_ref[...])

    def drain(j, carry):
        _row_copy(ys_ref, 0, yg_scr.at[0], 0, sem).wait()
        return carry

    lax.fori_loop(0, TOP_K * rows, drain, 0, unroll=8)
    w3 = w3_ref[...]
    for k in range(TOP_K):
        wk = w3[:, k:k + 1] + w3[:, TOP_K + k:TOP_K + k + 1] + w3[:, 2 * TOP_K + k:2 * TOP_K + k + 1]
        acc = acc + wk * _unpack_pairs(yg_scr[k])
    x = x_ref[...]
    o_ref[...] = x + mod_ref[:, 5:6, :] * acc.reshape(x.shape)


def _combine(dest3, x1, mod, hp, w3, wg, wu, wd, ys):
    n, t, d = x1.shape
    rt = _Rows(n, t)
    assert rt.rows & (rt.rows - 1) == 0
    return pl.pallas_call(
        _combine_kernel,
        grid=(rt.steps,),
        in_specs=[pl.BlockSpec((1, TOP_K, rt.rows), lambda i: (i, 0, 0), memory_space=pltpu.SMEM),
                  rt.seq_spec(t, d), rt.seq_spec(6, d), rt.row_spec(hp.shape[1]), rt.row_spec(128),
                  _const_spec(wg.shape), _const_spec(wu.shape), _const_spec(wd.shape),
                  pl.BlockSpec(memory_space=pl.ANY)],
        out_specs=rt.seq_spec(t, d),
        out_shape=jax.ShapeDtypeStruct((n, t, d), F32),
        scratch_shapes=[pltpu.VMEM((TOP_K, rt.rows, hp.shape[1]), jnp.uint32), pltpu.SemaphoreType.DMA(())],
        compiler_params=_params("arbitrary"),
        name="moe_combine_shared",
    )(dest3, x1, mod, hp, w3, wg, wu, wd, ys)


def _moe(layer, x1, mod, hp, idx3, rank3, w3, counts, p):
    n, t, d = x1.shape
    m = n * t
    n_e = p['w_gate_e'].shape[1]
    a = m * TOP_K
    tm = 256 if a // n_e >= 512 else 128
    counts = counts[:, 0].astype(jnp.int32)
    padded = (counts + tm - 1) // tm * tm
    pend = jnp.cumsum(padded)
    pstart = pend - padded
    onehot = idx3[None] == jnp.arange(n_e, dtype=jnp.int32)[:, None, None, None]
    dest3 = rank3 + jnp.sum(jnp.where(onehot, pstart[:, None, None, None], 0), axis=0)
    n_blocks = a // tm + n_e
    first = jnp.arange(n_blocks, dtype=jnp.int32) * tm
    block_e = jnp.minimum(jnp.searchsorted(pend, first, side='right'), n_e - 1).astype(jnp.int32)
    valid = jnp.clip(counts[block_e] - (first - pstart[block_e]), 0, tm).astype(jnp.int32)
    n_used = (pend[-1] // tm).astype(jnp.int32).reshape(1)
    xs = _dispatch(dest3, hp, n_blocks * tm)
    ys = _gmm(layer, block_e, valid, n_used, xs, p['w_gate_e'], p['w_up_e'], p['w_down_e'], tm)
    return _combine(dest3, x1, mod, hp, w3, p['w_gate_s'][layer].astype(BF16), p['w_up_s'][layer].astype(BF16),
                    p['w_down_s'][layer].astype(BF16), ys)


def _odd_in_kernel(x_ref, mod_ref, g_ref, w_ref, gqa_ref, wqb_ref, gqn_ref, gqr_ref, gqrs_ref, gkva_ref,
                   gkr_ref, gkrs_ref, gkn_ref, cos_ref, sin_ref, seg_ref, wuk_ref, wuv_ref,
                   cq_ref, ck_ref, cv_ref, co_ref, gt_ref, ckv_ref, kr_ref, *outs, prompt):
    h = _norm_mod(x_ref, mod_ref, g_ref, 0, 1).astype(BF16)
    z = _dot(h, w_ref[...])
    cq_ref[...] = z[:, 0:512]
    ck_ref[...] = z[:, 512:1024]
    cv_ref[...] = z[:, 1024:1536]
    co_ref[...] = z[:, 1536:2048]
    gt_ref[...] = z[:, 2560:2688]
    cos = cos_ref[...]
    sin = sin_ref[...]
    ckv = _rms(z[:, 2304:2432]) * gkva_ref[...]
    ckv_ref[...] = ckv
    xr = z[:, 2432:2496]
    xr_rot = z[:, 2496:2560]
    rr = lax.rsqrt(jnp.mean(xr * xr, axis=-1, keepdims=True) + EPS)
    kr = rr * (xr * gkr_ref[...] * cos[:, 0:D_ROPE] + xr_rot * gkrs_ref[...] * sin[:, 0:D_ROPE])
    kr_ref[...] = kr
    qa = (_rms(z[:, 2048:2304]) * gqa_ref[...]).astype(BF16)
    qd = _dot(qa, wqb_ref[...])
    nw = D_HEADS * D_NOPE
    rw = D_HEADS * D_ROPE
    qn = [_rms(qd[:, hh * D_NOPE:(hh + 1) * D_NOPE]) * gqn_ref[...] * MLA_SCALE for hh in range(D_HEADS)]
    xq = qd[:, nw:nw + rw]
    xq_rot = qd[:, nw + rw:nw + 2 * rw]
    rq = lax.rsqrt(_dot_sel(xq * xq, seg_ref[...]) * (1.0 / D_ROPE) + EPS)
    qr = rq * (xq * gqr_ref[...] * cos + xq_rot * gqrs_ref[...] * sin) * MLA_SCALE
    ckv_b = ckv.astype(BF16)
    if prompt:
        qcat_ref, kcat_ref, v_ref = outs
        kn = _dot(ckv_b, wuk_ref[...])
        zpad = jnp.zeros((z.shape[0], 256 - D_NOPE - D_ROPE), F32)
        qcat_ref[...] = jnp.concatenate(
            [piece for hh in range(D_HEADS) for piece in (qn[hh], qr[:, hh * D_ROPE:(hh + 1) * D_ROPE], zpad)],
            axis=1).astype(BF16)
        kcat_ref[...] = jnp.concatenate(
            [piece for hh in range(D_HEADS)
             for piece in (_rms(kn[:, hh * D_NOPE:(hh + 1) * D_NOPE]) * gkn_ref[...], kr, zpad)],
            axis=1).astype(BF16)
        v_ref[...] = _dot(ckv_b, wuv_ref[...]).astype(BF16)
    else:
        u_ref, qr_ref = outs
        u_ref[...] = jnp.concatenate(
            [_dot_nt((qn[hh] * gkn_ref[...]).astype(BF16), wuk_ref[:, hh * D_NOPE:(hh + 1) * D_NOPE])
             for hh in range(D_HEADS)], axis=1).astype(BF16)
        qr_ref[...] = qr.astype(BF16)


def _odd_in(x, mod, g, w, consts, cos, sin, prompt):
    n, t, d = x.shape
    rt = _Rows(n, t)
    m = n * t
    widths = [(512, F32)] * 4 + [(128, F32), (128, F32), (D_ROPE, F32)]
    if prompt:
        widths += [(1024, BF16), (1024, BF16), (512, BF16)]
    else:
        widths += [(512, BF16), (256, BF16)]
    return pl.pallas_call(
        functools.partial(_odd_in_kernel, prompt=prompt),
        grid=(rt.steps,),
        in_specs=[rt.seq_spec(t, d), rt.seq_spec(6, d), _const_spec((1, d)), _const_spec(w.shape)]
                 + [_const_spec(c.shape) for c in consts[:9]]
                 + [rt.pos_spec(cos.shape[1]), rt.pos_spec(sin.shape[1])]
                 + [_const_spec(c.shape) for c in consts[9:]],
        out_specs=[rt.row_spec(c) for c, _ in widths],
        out_shape=[jax.ShapeDtypeStruct((m, c), dt) for c, dt in widths],
        compiler_params=_params("arbitrary"),
        name="odd_in_latent_prep",
    )(x, mod, g, w, *consts[:9], cos, sin, *consts[9:])


def _mlstm_kernel(q_ref, k_ref, v_ref, og_ref, gt_ref, gb_ref, ng_ref, c0_ref, n0_ref, m0_ref, tri_ref, eye_ref,
                  o_ref, co_ref, no_ref, mo_ref, c_scr, n_scr, m_scr, *, chunk):
    dh = q_ref.shape[1] // C_HEADS
    t = pl.program_id(1)

    @pl.when(t == 0)
    def _():
        c_scr[...] = c0_ref[0]
        n_scr[...] = n0_ref[0]
        m_scr[...] = m0_ref[0]

    pre = gt_ref[...] + gb_ref[...]
    lf = _log_sigmoid(pre)
    if chunk >= 16:
        l_hi, l_mid, l_lo = _split3(lf)
        f_cum = _dot(tri_ref[...], l_hi) + _dot(tri_ref[...], l_mid) + _dot(tri_ref[...], l_lo)
    else:
        rowg = lax.broadcasted_iota(jnp.int32, lf.shape, 0)
        f_cum = jnp.zeros_like(lf)
        for s in range(chunk):
            f_cum = f_cum + jnp.where(rowg >= s, lf[s:s + 1, :], 0.0)
    ri = lax.broadcasted_iota(jnp.int32, (chunk, chunk), 0)
    ci = lax.broadcasted_iota(jnp.int32, (chunk, chunk), 1)
    pad = 16 - chunk if chunk < 16 else 0

    def pad_rows(a):
        if pad:
            return jnp.concatenate([a, jnp.zeros((pad, a.shape[1]), a.dtype)], axis=0)
        return a

    for h in range(C_HEADS):
        sl = slice(h * dh, (h + 1) * dh)
        q = q_ref[:, sl]
        k = k_ref[:, sl] * (dh ** -0.5)
        v = v_ref[:, sl]
        qb, kb, vb = q.astype(BF16), k.astype(BF16), v.astype(BF16)
        f_col = f_cum[:, C_HEADS + h:C_HEADS + h + 1]
        i_col = pre[:, h:h + 1]
        m_prev = m_scr[h:h + 1, 0:1]
        a_row = jnp.sum(jnp.where(ri == ci, i_col - f_col, 0.0), axis=0, keepdims=True)
        log_d = jnp.where(ci <= ri, f_col + a_row, NEG)
        inter = f_col + m_prev
        m_t = jnp.maximum(inter, jnp.max(log_d, axis=-1, keepdims=True))
        w_inter = jnp.exp(inter - m_t)
        qk = _dot_nt(qb, kb) * jnp.exp(log_d - m_t)
        c_old = c_scr[h]
        n_old = n_scr[h:h + 1, :]
        num = _dot(qk.astype(BF16), vb) + w_inter * _dot_nt(qb, c_old.astype(BF16))
        den = jnp.sum(qk, axis=-1, keepdims=True) + w_inter * jnp.sum(q * n_old, axis=-1, keepdims=True)
        hh = num / jnp.maximum(jnp.abs(den), jnp.exp(-m_t))
        o_ref[:, sl] = _rms(hh) * ng_ref[...] * _sigmoid(og_ref[:, sl])
        f_last = f_col[chunk - 1:chunk, :]
        a_end = f_last - f_col + i_col
        m_new = jnp.maximum(f_last + m_prev, jnp.max(a_end, axis=0, keepdims=True))
        w = jnp.exp(a_end - m_new)
        dec = jnp.exp(f_last + m_prev - m_new)
        wv_t = _dot_nt(eye_ref[...], pad_rows((w * v).astype(BF16))).astype(BF16)
        c_scr[h] = dec * c_old + _dot(wv_t, pad_rows(kb))
        n_scr[h:h + 1, :] = dec * n_old + jnp.sum(w * k, axis=0, keepdims=True)
        m_scr[h:h + 1, :] = jnp.broadcast_to(m_new, (1, m_scr.shape[1]))

    @pl.when(t == pl.num_programs(1) - 1)
    def _():
        co_ref[0] = c_scr[...]
        no_ref[0] = n_scr[...]
        mo_ref[0] = m_scr[...]


def _mlstm(n, t, q, k, v, og, gt, gb, ng, c0, n0, m0, tri, eye):
    chunk = CHUNK_C if t % CHUNK_C == 0 else t
    nt = t // chunk
    w = q.shape[1]
    dh = w // C_HEADS

    def rows(c):
        return pl.BlockSpec((chunk, c), lambda i, j: (i * nt + j, 0))

    cst = pl.BlockSpec((1, C_HEADS, dh, dh), lambda i, j: (i, 0, 0, 0))
    nst = pl.BlockSpec((1, C_HEADS, dh), lambda i, j: (i, 0, 0))
    return pl.pallas_call(
        functools.partial(_mlstm_kernel, chunk=chunk),
        grid=(n, nt),
        in_specs=[rows(w), rows(w), rows(w), rows(w), rows(128),
                  pl.BlockSpec(gb.shape, lambda i, j: (0, 0)), pl.BlockSpec(ng.shape, lambda i, j: (0, 0)),
                  cst, nst, nst,
                  pl.BlockSpec(tri.shape, lambda i, j: (0, 0)), pl.BlockSpec(eye.shape, lambda i, j: (0, 0))],
        out_specs=[rows(w), cst, nst, nst],
        out_shape=[jax.ShapeDtypeStruct((n * t, w), F32), jax.ShapeDtypeStruct(c0.shape, F32),
                   jax.ShapeDtypeStruct(n0.shape, F32), jax.ShapeDtypeStruct(m0.shape, F32)],
        scratch_shapes=[pltpu.VMEM((C_HEADS, dh, dh), F32), pltpu.VMEM((C_HEADS, dh), F32),
                        pltpu.VMEM((C_HEADS, dh), F32)],
        compiler_params=_params("arbitrary", "arbitrary"),
        name="mlstm_scan",
    )(q, k, v, og, gt, gb, ng, c0, n0, m0, tri, eye)


def _flash_kernel(qi_ref, kj_ref, q_ref, k_ref, v_ref, o_ref, m_scr, l_scr, acc_scr):
    p = pl.program_id(2)
    qi = qi_ref[p]
    kj = kj_ref[p]

    @pl.when(kj == 0)
    def _():
        m_scr[...] = jnp.full_like(m_scr, NEG)
        l_scr[...] = jnp.zeros_like(l_scr)
        acc_scr[...] = jnp.zeros_like(acc_scr)

    def step(masked):
        s = _dot_nt(q_ref[...], k_ref[...])
        bq, bk = s.shape
        if masked:
            ri = lax.broadcasted_iota(jnp.int32, (bq, bk), 0)
            ci = lax.broadcasted_iota(jnp.int32, (bq, bk), 1)
            s = jnp.where(ci <= ri, s, NEG)
        lanes = m_scr.shape[1]
        m_old = m_scr[...]
        m_new = jnp.maximum(m_old, jnp.max(s, axis=-1, keepdims=True))
        alpha = jnp.exp(m_old - m_new)
        pr = jnp.exp(s - jnp.concatenate([m_new] * (bk // lanes), axis=1))
        l_scr[...] = alpha * l_scr[...] + jnp.sum(pr, axis=-1, keepdims=True)
        acc_scr[...] = alpha * acc_scr[...] + _dot(pr.astype(BF16), v_ref[...])
        m_scr[...] = m_new

    @pl.when(kj < qi)
    def _():
        step(False)

    @pl.when(kj == qi)
    def _():
        step(True)
        o_ref[...] = acc_scr[...] / l_scr[...]


def _flash(n, t, qcat, kcat, v):
    blk = FLASH_BLOCK if t % FLASH_BLOCK == 0 else t
    nq = t // blk
    pairs = [(i, j) for i in range(nq) for j in range(i + 1)]
    qi = jnp.asarray([a for a, _ in pairs], jnp.int32)
    kj = jnp.asarray([b for _, b in pairs], jnp.int32)
    dv = v.shape[1] // D_HEADS
    grid_spec = pltpu.PrefetchScalarGridSpec(
        num_scalar_prefetch=2,
        grid=(n, D_HEADS, len(pairs)),
        in_specs=[pl.BlockSpec((blk, 256), lambda b, h, p, qi, kj: (b * nq + qi[p], h)),
                  pl.BlockSpec((blk, 256), lambda b, h, p, qi, kj: (b * nq + kj[p], h)),
                  pl.BlockSpec((blk, dv), lambda b, h, p, qi, kj: (b * nq + kj[p], h))],
        out_specs=pl.BlockSpec((blk, dv), lambda b, h, p, qi, kj: (b * nq + qi[p], h)),
        scratch_shapes=[pltpu.VMEM((blk, dv), F32), pltpu.VMEM((blk, dv), F32), pltpu.VMEM((blk, dv), F32)],
    )
    assert blk % dv == 0
    return pl.pallas_call(
        _flash_kernel,
        grid_spec=grid_spec,
        out_shape=jax.ShapeDtypeStruct((n * t, v.shape[1]), F32),
        compiler_params=_params("arbitrary", "arbitrary", "arbitrary"),
        name="mla_prompt_flash",
    )(qi, kj, qcat, kcat, v)


def _paged_kernel(pt_ref, u_ref, qr_ref, cn_ref, kn_ref, wukt_ref, wuv_ref, *rest, pages):
    ck_refs = rest[:pages]
    kr_refs = rest[pages:2 * pages]
    o_ref, m_scr, l_scr, acc_scr = rest[2 * pages:]
    g = pl.program_id(1)
    hq = u_ref.shape[1]
    tq = hq // D_HEADS

    @pl.when(g == 0)
    def _():
        m_scr[...] = jnp.full_like(m_scr, NEG)
        l_scr[...] = jnp.zeros_like(l_scr)
        acc_scr[...] = jnp.zeros_like(acc_scr)

    lhs = jnp.concatenate([wukt_ref[...], u_ref[0]], axis=0)
    qr = qr_ref[0]
    nk = D_HEADS * D_NOPE

    def scores(ck_b, kr_b):
        big = _dot_nt(lhs, ck_b)
        rows = []
        for h in range(D_HEADS):
            kn_t = big[h * D_NOPE:(h + 1) * D_NOPE, :]
            rinv = lax.rsqrt(jnp.sum(kn_t * kn_t, axis=0, keepdims=True) * (1.0 / D_NOPE) + EPS)
            rows.append(big[nk + h * tq:nk + (h + 1) * tq, :] * rinv)
        return jnp.concatenate(rows, axis=0) + _dot_nt(qr, kr_b)

    def update(s, ck_b):
        m_old = m_scr[...]
        m_new = jnp.maximum(m_old, jnp.max(s, axis=-1, keepdims=True))
        alpha = jnp.exp(m_old - m_new)
        pr = jnp.exp(s - m_new)
        l_scr[...] = alpha * l_scr[...] + jnp.sum(pr, axis=-1, keepdims=True)
        acc_scr[...] = alpha * acc_scr[...] + _dot(pr.astype(BF16), ck_b)
        m_scr[...] = m_new

    cks = [r[...].astype(BF16) for r in ck_refs]
    krs = [r[...].astype(BF16) for r in kr_refs]
    s_all = jnp.concatenate(
        [scores(jnp.concatenate(cks[i:i + 2], axis=0), jnp.concatenate(krs[i:i + 2], axis=0))
         for i in range(0, pages, 2)], axis=1)
    update(s_all, jnp.concatenate(cks, axis=0))

    @pl.when(g == pl.num_programs(1) - 1)
    def _():
        fill = PAGE_SIZE - tq
        ck_b = jnp.concatenate([cn_ref[...], jnp.zeros((fill, cn_ref.shape[1]), F32)], axis=0).astype(BF16)
        kr_b = jnp.concatenate([kn_ref[...], jnp.zeros((fill, kn_ref.shape[1]), F32)], axis=0).astype(BF16)
        s = scores(ck_b, kr_b)
        ri = lax.broadcasted_iota(jnp.int32, s.shape, 0)
        ci = lax.broadcasted_iota(jnp.int32, s.shape, 1)
        update(jnp.where(ci <= ri % tq, s, NEG), ck_b)
        lat = (acc_scr[...] / l_scr[...]).astype(BF16)
        full = _dot(lat, wuv_ref[...])
        dv = wuv_ref.shape[1] // D_HEADS
        o_ref[...] = jnp.concatenate(
            [full[h * tq:(h + 1) * tq, h * dv:(h + 1) * dv] for h in range(D_HEADS)], axis=1)


def _paged(page_table, li, u3, qr3, ckv_new, kr_new, wuk_t, wuv, cache_ckv, cache_kr):
    n, hq, lat = u3.shape
    tq = hq // D_HEADS
    n_pages = page_table.shape[1]
    pages = min(PAGES_PER_STEP, n_pages)
    assert n_pages % pages == 0 and pages % 2 == 0
    ng = n_pages // pages

    def page_spec(i, width):
        return pl.BlockSpec((None, None, PAGE_SIZE, width), lambda b, g, pt: (pt[b, g * pages + i], li, 0, 0))

    grid_spec = pltpu.PrefetchScalarGridSpec(
        num_scalar_prefetch=1,
        grid=(n, ng),
        in_specs=[pl.BlockSpec((1, hq, lat), lambda b, g, pt: (b, 0, 0)),
                  pl.BlockSpec((1, hq, D_ROPE), lambda b, g, pt: (b, 0, 0)),
                  pl.BlockSpec((tq, lat), lambda b, g, pt: (b, 0)),
                  pl.BlockSpec((tq, D_ROPE), lambda b, g, pt: (b, 0)),
                  pl.BlockSpec(wuk_t.shape, lambda b, g, pt: (0, 0)),
                  pl.BlockSpec(wuv.shape, lambda b, g, pt: (0, 0))]
                 + [page_spec(i, lat) for i in range(pages)]
                 + [page_spec(i, D_ROPE) for i in range(pages)],
        out_specs=pl.BlockSpec((tq, wuv.shape[1]), lambda b, g, pt: (b, 0)),
        scratch_shapes=[pltpu.VMEM((hq, 1), F32), pltpu.VMEM((hq, 1), F32), pltpu.VMEM((hq, lat), F32)],
    )
    return pl.pallas_call(
        functools.partial(_paged_kernel, pages=pages),
        grid_spec=grid_spec,
        out_shape=jax.ShapeDtypeStruct((n * tq, wuv.shape[1]), F32),
        compiler_params=_params("arbitrary", "arbitrary"),
        name="mla_sample_paged",
    )(page_table, u3, qr3, ckv_new, kr_new, wuk_t, wuv, *([cache_ckv] * pages), *([cache_kr] * pages))


def _np_seg(n_seg, seg_in, seg_out):
    mat = np.zeros((n_seg * seg_in, n_seg * seg_out), np.float32)
    for s in range(n_seg):
        mat[s * seg_in:(s + 1) * seg_in, s * seg_out:(s + 1) * seg_out] = 1.0
    return mat


def _rot_cols(w):
    half = w.shape[-1] // 2
    return jnp.concatenate([-w[..., half:], w[..., :half]], axis=-1)


def _swap_halves(g):
    half = g.shape[-1] // 2
    return jnp.concatenate([g[..., half:], g[..., :half]], axis=-1)


def _rope_tables(pos, reps):
    half = D_ROPE // 2
    inv = ROPE_BASE ** (-jnp.arange(half, dtype=F32) / half)
    ang = pos.astype(F32)[:, None] * inv
    cos = jnp.concatenate([jnp.cos(ang), jnp.cos(ang)], axis=-1)
    sin = jnp.concatenate([jnp.sin(ang), jnp.sin(ang)], axis=-1)
    return jnp.tile(cos, (1, reps)), jnp.tile(sin, (1, reps))


def _trunk(x, c_mod, pos0, gla0, mc0, mn0, mm0, p, sample_ctx):
    n, t, d = x.shape
    m = n * t
    rt = _Rows(n, t)
    eye_r = jnp.eye(rt.rows, dtype=BF16)
    triu_r = jnp.asarray(np.triu(np.ones((rt.rows, rt.rows), np.float32)), BF16)
    dh = mc0.shape[3]
    eye_hk = jnp.eye(B_HEADS * gla0.shape[3], dtype=BF16)
    eye_dh = jnp.eye(dh, dtype=BF16)
    results = {}

    layer, li = 0, 0
    mod = c_mod[layer]
    w_in = p['w_in_even'][li]
    w_in = jnp.concatenate([w_in, jnp.zeros((d, 128 - B_GATE_RANK), F32)], axis=1).astype(BF16)
    a_ws = jnp.tril(p['a_ws'][li])
    a_bs = p['a_bs'][li]
    if t % CHUNK_A == 0:
        ws = a_ws
        bs = jnp.repeat(a_bs.T, CHUNK_A, axis=1)
    else:
        ws = jnp.stack([jnp.kron(jnp.eye(CHUNK_A // t, dtype=F32), a_ws[g, :t, :t]) for g in range(A_GROUPS)])
        bs = jnp.repeat(jnp.tile(a_bs[:, :t], (1, CHUNK_A // t)).T, CHUNK_A, axis=1)
    out_a, v_rows, bq, bk, bv, br, bg = _even_in(
        x, mod, p['norm_mix_g'][layer].reshape(1, d), w_in, p['a_norm_g'][li].reshape(1, -1),
        ws.astype(BF16), bs)
    hk = bq.shape[1]
    dv = bv.shape[1] // B_HEADS
    wg2 = jnp.concatenate([p['b_w_gate2'][li], jnp.zeros((128 - B_GATE_RANK, hk), F32)], axis=0)
    seg = jnp.asarray(_np_seg(B_HEADS, hk // B_HEADS, dv), BF16)
    out_b, s_new = _gla(n, t, bq, bk, bv, br, bg, wg2, p['b_gate_bias'][li].reshape(1, hk),
                        jnp.tile(p['b_norm_g'][li], B_HEADS).reshape(1, -1), gla0[li], seg, eye_hk)
    results['gla'] = s_new
    results['v_rows'] = v_rows
    routed = _out_route(
        x, mod, out_a, out_b, p['w_out'][layer].astype(BF16), p['norm_ffn_g'][layer].reshape(1, d),
        p['w_router'][layer].T, p['b_router'][layer].reshape(-1, 1), eye_r, triu_r)
    x = _moe(layer, routed[0], mod, *routed[1:], p)

    layer, li = 1, 0
    mod = c_mod[layer]
    w = p['w_in_odd'][li]
    hw = C_HEADS * dh
    o_g = 3 * hw
    o_o = o_g + 2 * C_HEADS
    o_qa = o_o + hw
    o_kva = o_qa + p['d_g_qa'].shape[1]
    o_kr = o_kva + p['d_g_kva'].shape[1]
    w_kr = w[:, o_kr:o_kr + D_ROPE]
    w_odd = jnp.concatenate(
        [w[:, :o_g], w[:, o_o:o_qa], w[:, o_qa:o_kva], w[:, o_kva:o_kr], w_kr, _rot_cols(w_kr),
         w[:, o_g:o_o], jnp.zeros((d, 128 - 2 * C_HEADS), F32)], axis=1).astype(BF16)
    wqb = p['d_w_qb'][li].reshape(-1, D_HEADS, D_NOPE + D_ROPE)
    wqb_r = wqb[:, :, D_NOPE:]
    wqb2 = jnp.concatenate([wqb[:, :, :D_NOPE].reshape(-1, D_HEADS * D_NOPE),
                            wqb_r.reshape(-1, D_HEADS * D_ROPE),
                            _rot_cols(wqb_r).reshape(-1, D_HEADS * D_ROPE)], axis=1).astype(BF16)
    w_uk = p['d_w_uk'][li]
    lat = w_uk.shape[0]
    g_qr = p['d_g_qr'][li]
    g_kr = p['d_g_kr'][li]
    cos, sin = _rope_tables(pos0 + jnp.arange(t), D_HEADS)
    prompt = sample_ctx is None
    if not prompt:
        cos, sin = jnp.tile(cos, (rt.s, 1)), jnp.tile(sin, (rt.s, 1))
    consts = [p['d_g_qa'][li].reshape(1, -1), wqb2, p['d_g_qn'][li].reshape(1, -1),
              jnp.tile(g_qr, D_HEADS).reshape(1, -1), jnp.tile(_swap_halves(g_qr), D_HEADS).reshape(1, -1),
              p['d_g_kva'][li].reshape(1, -1), g_kr.reshape(1, -1), _swap_halves(g_kr).reshape(1, -1),
              p['d_g_kn'][li].reshape(1, -1),
              jnp.asarray(_np_seg(D_HEADS, D_ROPE, D_ROPE), BF16),
              w_uk.reshape(lat, -1).astype(BF16), p['d_w_uv'][li].reshape(lat, -1).astype(BF16)]
    outs = _odd_in(x, mod, p['norm_mix_g'][layer].reshape(1, d), w_odd, consts, cos, sin, prompt)
    cq, ck, cv, co, gates, ckv, kr = outs[:7]
    gb = jnp.concatenate([p['c_ig_bias'][li], p['c_fg_bias'][li],
                          jnp.zeros((128 - 2 * C_HEADS,), F32)]).reshape(1, 128)
    chunk = CHUNK_C if t % CHUNK_C == 0 else t
    tri = jnp.asarray(np.tril(np.ones((chunk, chunk), np.float32)), BF16)
    m0b =jnp.broadcast_to(mm0[li][:, :, None], (n, C_HEADS, dh))
    out_c, c_new, n_new, m_new = _mlstm(n, t, cq, ck, cv, co, gates, gb,
                                        p['c_norm_g'][li].reshape(1, -1), mc0[li], mn0[li], m0b, tri, eye_dh)
    results['mlstm'] = (c_new, n_new, m_new[:, :, 0])
    results['ckv'] = ckv.reshape(n, t, -1)
    results['kr'] = kr.reshape(n, t, -1)
    if prompt:
        qcat, kcat, vv = outs[7:]
        out_d = _flash(n, t, qcat, kcat, vv)
    else:
        u, qr = outs[7:]
        cache_ckv, cache_kr, page_table = sample_ctx
        u3 = u.reshape(n, t, D_HEADS, lat).transpose(0, 2, 1, 3).reshape(n, D_HEADS * t, lat)
        qr3 = qr.reshape(n, t, D_HEADS, D_ROPE).transpose(0, 2, 1, 3).reshape(n, D_HEADS * t, D_ROPE)
        wuk_t = w_uk.transpose(1, 2, 0).reshape(-1, lat).astype(BF16)
        out_d = _paged(page_table, li, u3, qr3, ckv, kr, wuk_t, consts[11], cache_ckv, cache_kr)
    routed = _out_route(
        x, mod, out_c, out_d, p['w_out'][layer].astype(BF16), p['norm_ffn_g'][layer].reshape(1, d),
        p['w_router'][layer].T, p['b_router'][layer].reshape(-1, 1), eye_r, triu_r)
    x = _moe(layer, routed[0], mod, *routed[1:], p)
    return x, results


def kernel(x_prompt, x_sample, state_gla, state_mlstm_c, state_mlstm_n, state_mlstm_m,
           cache_ckv, cache_krope, page_table, c_prompt, c_sample,
           norm_mix_g, norm_ffn_g, w_ada, b_ada, w_out,
           w_in_even, a_norm_g, a_ws, a_bs, b_w_gate2, b_gate_bias, b_norm_g,
           w_in_odd, c_ig_bias, c_fg_bias, c_norm_g,
           d_g_qa, d_w_qb, d_g_kva, d_g_qn, d_g_qr, d_g_kr, d_g_kn, d_w_uk, d_w_uv,
           w_router, b_router, w_gate_e, w_up_e, w_down_e, w_gate_s, w_up_s, w_down_s):
    p = dict(norm_mix_g=norm_mix_g, norm_ffn_g=norm_ffn_g, w_out=w_out,
             w_in_even=w_in_even, a_norm_g=a_norm_g, a_ws=a_ws, a_bs=a_bs, b_w_gate2=b_w_gate2,
             b_gate_bias=b_gate_bias, b_norm_g=b_norm_g, w_in_odd=w_in_odd, c_ig_bias=c_ig_bias,
             c_fg_bias=c_fg_bias, c_norm_g=c_norm_g, d_g_qa=d_g_qa, d_w_qb=d_w_qb, d_g_kva=d_g_kva,
             d_g_qn=d_g_qn, d_g_qr=d_g_qr, d_g_kr=d_g_kr, d_g_kn=d_g_kn, d_w_uk=d_w_uk, d_w_uv=d_w_uv,
             w_router=w_router, b_router=b_router, w_gate_e=w_gate_e, w_up_e=w_up_e, w_down_e=w_down_e,
             w_gate_s=w_gate_s, w_up_s=w_up_s, w_down_s=w_down_s)
    n_p, t_p, d = x_prompt.shape
    n_s, t_s, _ = x_sample.shape
    depth = w_ada.shape[0]
    pad_p = (-n_p) % 8
    c_all = jnp.concatenate([c_prompt, jnp.zeros((pad_p, d), F32), c_sample], axis=0)
    mod_all = _ada(c_all, w_ada, b_ada).reshape(depth, c_all.shape[0], 6, d)
    mod_p = mod_all[:, :n_p]
    mod_s = mod_all[:, n_p + pad_p:]

    n_even, _, bh, bdk, bdv = state_gla.shape
    n_odd, _, chh, cdh, _ = state_mlstm_c.shape
    gla0_p = jnp.zeros((n_even, n_p, bh, bdk, bdv), F32)
    mc0_p = jnp.zeros((n_odd, n_p, chh, cdh, cdh), F32)
    mn0_p = jnp.zeros((n_odd, n_p, chh, cdh), F32)
    mm0_p = jnp.full((n_odd, n_p, chh), NEG, F32)
    past_len = page_table.shape[1] * PAGE_SIZE

    y_p, rp = _trunk(x_prompt, mod_p, 0, gla0_p, mc0_p, mn0_p, mm0_p, p, None)
    y_s, rs = _trunk(x_sample, mod_s, past_len, state_gla, state_mlstm_c, state_mlstm_n, state_mlstm_m, p,
                     (cache_ckv, cache_krope, page_table))
    aw = rs['v_rows'].shape[1]
    return (y_p, y_s, rp['gla'][None], rs['gla'][None], rs['v_rows'].reshape(1, n_s, t_s, aw),
            rp['mlstm'][0][None], rs['mlstm'][0][None], rp['mlstm'][1][None], rs['mlstm'][1][None],
            rp['mlstm'][2][None], rs['mlstm'][2][None],
            rp['ckv'][:, None], rs['ckv'][:, None], rp['kr'][:, None], rs['kr'][:, None])
```

```python
import functools

import numpy as np
import jax
import jax.numpy as jnp
from jax import lax
from jax.experimental import pallas as pl
from jax.experimental.pallas import tpu as pltpu

F32 = jnp.float32
BF16 = jnp.bfloat16

EPS = 1e-6
NEG = -1e30

A_GROUPS = 4
CHUNK_A = 128
B_HEADS = 4
B_GATE_RANK = 16
B_GATE_TAU = 16.0
CHUNK_B = 16
C_HEADS = 4
CHUNK_C = 128
D_HEADS = 4
D_NOPE = 128
D_ROPE = 64
ROPE_BASE = 10000.0
MLA_SCALE = (D_NOPE + D_ROPE) ** -0.5
PAGE_SIZE = 128
N_EXPERT_GROUPS = 8
TOPK_GROUPS = 4
TOP_K = 8
ROUTED_SCALE = 2.5

ROW_TILE = 256
SAMPLE_SEQS = 32
FLASH_BLOCK = 512
PAGES_PER_STEP = 16
VMEM_LIMIT = 56 * 1024 * 1024


def _dot(a, b):
    return jnp.dot(a, b, preferred_element_type=F32)


def _dot_nt(a, b):
    return lax.dot_general(a, b, (((1,), (1,)), ((), ())), preferred_element_type=F32)


def _split2(x):
    hi = x.astype(BF16)
    lo = (x - hi.astype(F32)).astype(BF16)
    return hi, lo


def _split3(x):
    hi = x.astype(BF16)
    r = x - hi.astype(F32)
    mid = r.astype(BF16)
    lo = (r - mid.astype(F32)).astype(BF16)
    return hi, mid, lo


def _dot3(a, b):
    ah, al = _split2(a)
    bh, bl = _split2(b)
    return _dot(ah, bh) + _dot(ah, bl) + _dot(al, bh)


def _dot3_nt(a, b):
    ah, al = _split2(a)
    bh, bl = _split2(b)
    return _dot_nt(ah, bh) + _dot_nt(ah, bl) + _dot_nt(al, bh)


def _dot_sel(x, m01, parts=2):
    ps = _split2(x) if parts == 2 else _split3(x)
    acc = _dot(ps[0], m01)
    for p in ps[1:]:
        acc = acc + _dot(p, m01)
    return acc


def _sigmoid(x):
    return 1.0 / (1.0 + jnp.exp(-x))


def _silu(x):
    return x * _sigmoid(x)


def _log_sigmoid(x):
    return jnp.minimum(x, 0.0) - jnp.log(1.0 + jnp.exp(-jnp.abs(x)))


def _gelu(x):
    return 0.5 * x * (1.0 + jnp.tanh(0.7978845608028654 * (x + 0.044715 * x * x * x)))


def _rms(x, eps=EPS):
    return x * lax.rsqrt(jnp.mean(x * x, axis=-1, keepdims=True) + eps)


def _params(*sem):
    return pltpu.CompilerParams(dimension_semantics=sem, vmem_limit_bytes=VMEM_LIMIT)


def _const_spec(shape):
    nd = len(shape)
    return pl.BlockSpec(shape, lambda *_: (0,) * nd)


class _Rows:
    def __init__(self, n, t):
        self.n, self.t = n, t
        if t % ROW_TILE == 0:
            self.s, self.r = 1, ROW_TILE
            self.tpb = t // ROW_TILE
        else:
            assert t == 8 and n % SAMPLE_SEQS == 0, (n, t)
            self.s, self.r = SAMPLE_SEQS, t
            self.tpb = 1
        self.rows = self.s * self.r
        self.steps = n * t // self.rows

    def seq_spec(self, mid, d):
        s, tpb = self.s, self.tpb
        if mid == self.t:
            return pl.BlockSpec((s, self.r, d), lambda i: (i // tpb, i % tpb, 0))
        return pl.BlockSpec((s, mid, d), lambda i: (i // tpb, 0, 0))

    def row_spec(self, c):
        return pl.BlockSpec((self.rows, c), lambda i: (i, 0))

    def tile_spec(self, c):
        return pl.BlockSpec((self.rows, c, 128), lambda i: (i, 0, 0))

    def pos_spec(self, c):
        tpb = self.tpb
        return pl.BlockSpec((self.rows, c), lambda i: (i % tpb, 0))


def _ada_kernel(c_ref, w_ref, b_ref, o_ref):
    o_ref[0] = _dot3(_silu(c_ref[...]), w_ref[0]) + b_ref[0]


def _ada(c_all, w_ada, b_ada):
    depth, d, d6 = w_ada.shape
    nc = c_all.shape[0]
    tn = 1536
    return pl.pallas_call(
        _ada_kernel,
        grid=(depth, d6 // tn),
        in_specs=[pl.BlockSpec((nc, d), lambda l, j: (0, 0)),
                  pl.BlockSpec((1, d, tn), lambda l, j: (l, 0, j)),
                  pl.BlockSpec((1, 1, tn), lambda l, j: (l, 0, j))],
        out_specs=pl.BlockSpec((1, nc, tn), lambda l, j: (l, 0, j)),
        out_shape=jax.ShapeDtypeStruct((depth, nc, d6), F32),
        compiler_params=_params("arbitrary", "arbitrary"),
        name="ada_modulation",
    )(c_all, w_ada, b_ada.reshape(depth, 1, d6))


def _norm_mod(x_ref, mod_ref, g_ref, shift_row, scale_row):
    x = x_ref[...]
    h = _rms(x) * g_ref[...]
    h = h * (1.0 + mod_ref[:, scale_row:scale_row + 1, :]) + mod_ref[:, shift_row:shift_row + 1, :]
    s, r, d = x.shape
    return h.reshape(s * r, d)


def _even_in_kernel(x_ref, mod_ref, g_ref, w_ref, an_ref, ws_ref, bs_ref,
                    oa_ref, v_ref, q_ref, k_ref, bv_ref, r_ref, bg_ref):
    h = _norm_mod(x_ref, mod_ref, g_ref, 0, 1).astype(BF16)
    z = _dot(h, w_ref[...])
    aw = A_GROUPS * CHUNK_A
    rows = z.shape[0]
    u = _gelu(z[:, 0:aw])
    va = _gelu(z[:, aw:2 * aw])
    vn = jnp.concatenate(
        [_rms(va[:, g * CHUNK_A:(g + 1) * CHUNK_A]) for g in range(A_GROUPS)], axis=1) * an_ref[...]
    v_ref[...] = vn
    vb = vn.astype(BF16)
    for c in range(rows // CHUNK_A):
        r0 = c * CHUNK_A
        mixed = jnp.concatenate(
            [_dot(ws_ref[g], vb[r0:r0 + CHUNK_A, g * CHUNK_A:(g + 1) * CHUNK_A]) for g in range(A_GROUPS)],
            axis=1) + bs_ref[...]
        oa_ref[r0:r0 + CHUNK_A, :] = u[r0:r0 + CHUNK_A, :] * mixed
    o = 2 * aw
    q_ref[...] = z[:, o:o + 256]
    k_ref[...] = z[:, o + 256:o + 512]
    bv_ref[...] = z[:, o + 512:o + 1024]
    r_ref[...] = z[:, o + 1024:o + 1536]
    bg_ref[...] = z[:, o + 1536:o + 1664]


def _even_in(x, mod, g, w, an, ws, bs):
    n, t, d = x.shape
    rt = _Rows(n, t)
    m = n * t
    widths = (512, 512, 256, 256, 512, 512, 128)
    return pl.pallas_call(
        _even_in_kernel,
        grid=(rt.steps,),
        in_specs=[rt.seq_spec(t, d), rt.seq_spec(6, d), _const_spec((1, d)), _const_spec(w.shape),
                  _const_spec(an.shape), _const_spec(ws.shape), _const_spec(bs.shape)],
        out_specs=[rt.row_spec(c) for c in widths],
        out_shape=[jax.ShapeDtypeStruct((m, c), F32) for c in widths],
        compiler_params=_params("arbitrary"),
        name="even_in_chunk_gate",
    )(x, mod, g, w, an, ws, bs)


def _gla_kernel(q_ref, k_ref, v_ref, r_ref, g_ref, wg_ref, gb_ref, ng_ref, s0_ref, seg_ref, eye_ref,
                o_ref, so_ref, s_scr, lg_scr, o_scr, *, chunk, n_chunks):
    hk = q_ref.shape[1]
    dk = hk // B_HEADS
    dv = v_ref.shape[1] // B_HEADS
    t = pl.program_id(1)

    @pl.when(t == 0)
    def _():
        s_scr[...] = s0_ref[0].reshape(hk, dv)

    lg_scr[...] = _log_sigmoid(_dot3(g_ref[...], wg_ref[...]) + gb_ref[...]) * (1.0 / B_GATE_TAU)
    row = lax.broadcasted_iota(jnp.int32, (chunk, hk), 0)
    lane = lax.broadcasted_iota(jnp.int32, (chunk, hk), 1)
    pad = 16 - chunk if chunk < 16 else 0

    def pad_rows(a):
        if pad:
            return jnp.concatenate([a, jnp.zeros((pad, a.shape[1]), a.dtype)], axis=0)
        return a

    def body(c, carry):
        r0 = pl.multiple_of(c * chunk, chunk)
        lg = lg_scr[pl.ds(r0, chunk), :]
        q = q_ref[pl.ds(r0, chunk), :] * (dk ** -0.5)
        k = k_ref[pl.ds(r0, chunk), :]
        v = v_ref[pl.ds(r0, chunk), :]
        b = jnp.zeros_like(lg)
        for s in range(chunk):
            b = b + jnp.where(row >= s, lg[s:s + 1, :], 0.0)
        prods = []
        for l in range(chunk):
            e = jnp.exp(jnp.where(row <= l, b[l:l + 1, :] - b, NEG))
            prods.append(q[l:l + 1, :] * k * e)
        att = _dot_sel(jnp.concatenate(prods, axis=0), seg_ref[...])
        o_intra = jnp.concatenate(
            [jnp.sum(att[l * chunk:(l + 1) * chunk, :] * v, axis=0, keepdims=True) for l in range(chunk)], axis=0)
        s_old = s_scr[...]
        s_b = s_old.astype(BF16)
        qe = q * jnp.exp(b)
        o_inter = jnp.concatenate(
            [_dot(jnp.where(lane // dk == h, qe, 0.0).astype(BF16), s_b) for h in range(B_HEADS)], axis=1)
        o_scr[pl.ds(r0, chunk), :] = o_intra + o_inter
        b_last = b[chunk - 1:chunk, :]
        k_end = pad_rows((k * jnp.exp(b_last - b)).astype(BF16))
        k_t = _dot_nt(eye_ref[...], k_end).astype(BF16)
        dh, dm, dl = _split3(jnp.exp(b_last))
        dec3 = _dot_nt(eye_ref[...], jnp.concatenate([dh, dm, dl, jnp.zeros((13, hk), BF16)], axis=0))
        dec = dec3[:, 0:1] + dec3[:, 1:2] + dec3[:, 2:3]
        v_b = pad_rows(v.astype(BF16))
        kv = jnp.concatenate(
            [_dot(k_t[h * dk:(h + 1) * dk, :], v_b[:, h * dv:(h + 1) * dv]) for h in range(B_HEADS)], axis=0)
        s_scr[...] = dec * s_old + kv
        return carry

    lax.fori_loop(0, n_chunks, body, 0)

    o = o_scr[...]
    on = jnp.concatenate([_rms(o[:, h * dv:(h + 1) * dv]) for h in range(B_HEADS)], axis=1) * ng_ref[...]
    o_ref[...] = on * _silu(r_ref[...])

    @pl.when(t == pl.num_programs(1) - 1)
    def _():
        so_ref[0] = s_scr[...].reshape(B_HEADS, dk, dv)


def _gla(n, t, q, k, v, r, g, wg, gb, ng, s0, seg, eye):
    tt = ROW_TILE if t % ROW_TILE == 0 else t
    chunk = CHUNK_B if tt % CHUNK_B == 0 else tt
    nt = t // tt
    hk, hv = q.shape[1], v.shape[1]
    dk, dv = hk // B_HEADS, hv // B_HEADS

    def rows(c):
        return pl.BlockSpec((tt, c), lambda i, j: (i * nt + j, 0))

    st = pl.BlockSpec((1, B_HEADS, dk, dv), lambda i, j: (i, 0, 0, 0))
    return pl.pallas_call(
        functools.partial(_gla_kernel, chunk=chunk, n_chunks=tt // chunk),
        grid=(n, nt),
        in_specs=[rows(hk), rows(hk), rows(hv), rows(hv), rows(g.shape[1]),
                  pl.BlockSpec(wg.shape, lambda i, j: (0, 0)), pl.BlockSpec(gb.shape, lambda i, j: (0, 0)),
                  pl.BlockSpec(ng.shape, lambda i, j: (0, 0)), st,
                  pl.BlockSpec(seg.shape, lambda i, j: (0, 0)), pl.BlockSpec(eye.shape, lambda i, j: (0, 0))],
        out_specs=[rows(hv), st],
        out_shape=[jax.ShapeDtypeStruct((n * t, hv), F32), jax.ShapeDtypeStruct(s0.shape, F32)],
        scratch_shapes=[pltpu.VMEM((hk, dv), F32), pltpu.VMEM((tt, hk), F32), pltpu.VMEM((tt, hv), F32)],
        compiler_params=_params("arbitrary", "arbitrary"),
        name="gla_scan",
    )(q, k, v, r, g, wg, gb, ng, s0, seg, eye)


def _route(logits_t, bias_col):
    n_e, r = logits_t.shape
    per = n_e // N_EXPERT_GROUPS
    scores = _sigmoid(logits_t)
    biased = scores + bias_col
    sub = lax.broadcasted_iota(jnp.int32, (per, r), 0).astype(F32)
    ninf = -jnp.inf
    gs = []
    for g in range(N_EXPERT_GROUPS):
        blk = biased[g * per:(g + 1) * per, :]
        m1 = jnp.max(blk, axis=0, keepdims=True)
        i1 = jnp.min(jnp.where(blk == m1, sub, float(per)), axis=0, keepdims=True)
        m2 = jnp.max(jnp.where(sub == i1, ninf, blk), axis=0, keepdims=True)
        gs.append(m1 + m2)
    cur = jnp.concatenate(gs, axis=0)
    gsub = lax.broadcasted_iota(jnp.int32, (N_EXPERT_GROUPS, r), 0).astype(F32)
    gsel = jnp.zeros((N_EXPERT_GROUPS, r), F32)
    for _ in range(TOPK_GROUPS):
        m = jnp.max(cur, axis=0, keepdims=True)
        i = jnp.min(jnp.where(cur == m, gsub, float(N_EXPERT_GROUPS)), axis=0, keepdims=True)
        hit = gsub == i
        gsel = jnp.where(hit, 1.0, gsel)
        cur = jnp.where(hit, ninf, cur)
    cur = jnp.concatenate(
        [jnp.where(gsel[g:g + 1, :] > 0.5, biased[g * per:(g + 1) * per, :], ninf) for g in range(N_EXPERT_GROUPS)],
        axis=0)
    esub = lax.broadcasted_iota(jnp.int32, (n_e, r), 0).astype(F32)
    idx, wts, hits = [], [], []
    for _ in range(TOP_K):
        m = jnp.max(cur, axis=0, keepdims=True)
        i = jnp.min(jnp.where(cur == m, esub, float(n_e)), axis=0, keepdims=True)
        hit = esub == i
        idx.append(i)
        hits.append(hit)
        wts.append(jnp.sum(jnp.where(hit, scores, 0.0), axis=0, keepdims=True))
        cur = jnp.where(hit, ninf, cur)
    w = jnp.concatenate(wts, axis=0)
    w = w / jnp.sum(w, axis=0, keepdims=True) * ROUTED_SCALE
    return jnp.concatenate(idx, axis=0), w, hits


def _store_row_tiles(ref, x):
    xr = x.astype(BF16).astype(F32)
    for c in range(ref.shape[1]):
        ref[:, c, :] = xr[:, c * 128:(c + 1) * 128]


def _load_row_tiles(ref):
    return jnp.concatenate([ref[:, c, :] for c in range(ref.shape[1])], axis=1)


def _out_kernel(x_ref, mod_ref, ma_ref, mb_ref, wo_ref, g_ref, wr_ref, br_ref, eye_ref, triu_ref,
                x1_ref, hp_ref, idx_ref, rank_ref, w3_ref, cnt_ref, run_scr):
    step = pl.program_id(0)

    @pl.when(step == 0)
    def _():
        run_scr[...] = jnp.zeros_like(run_scr)

    half = ma_ref.shape[1]
    y = _dot(ma_ref[...].astype(BF16), wo_ref[0:half, :]) + _dot(mb_ref[...].astype(BF16), wo_ref[half:, :])
    x = x_ref[...]
    s, r, d = x.shape
    x1 = x + mod_ref[:, 2:3, :] * y.reshape(s, r, d)
    x1_ref[...] = x1
    h2 = (_rms(x1) * g_ref[...] * (1.0 + mod_ref[:, 4:5, :]) + mod_ref[:, 3:4, :]).reshape(s * r, d)
    _store_row_tiles(hp_ref, h2)
    idx, w, hits = _route(_dot3_nt(wr_ref[...], h2), br_ref[...])
    idx_ref[0] = idx.astype(jnp.int32)
    sel = jnp.zeros(hits[0].shape, F32)
    for hit in hits:
        sel = jnp.where(hit, 1.0, sel)
    before = run_scr[:, 0:1] + _dot(sel.astype(BF16), triu_ref[...]) - sel
    rank_ref[0] = jnp.concatenate(
        [jnp.sum(jnp.where(hit, before, 0.0), axis=0, keepdims=True) for hit in hits], axis=0).astype(jnp.int32)
    run_scr[...] = run_scr[...] + jnp.sum(sel, axis=1, keepdims=True)
    cnt_ref[...] = run_scr[...]
    wh, wm, wl = _split3(w)
    stack = jnp.concatenate([wh, wm, wl, jnp.zeros((128 - 3 * TOP_K, s * r), BF16)], axis=0)
    w3_ref[...] = _dot_nt(eye_ref[...], stack)


def _out_route(x, mod, mix_a, mix_b, wo, g, wr_t, br, eye, triu):
    n, t, d = x.shape
    rt = _Rows(n, t)
    m = n * t
    n_e = wr_t.shape[0]
    slot = pl.BlockSpec((1, TOP_K, rt.rows), lambda i: (i, 0, 0))
    return pl.pallas_call(
        _out_kernel,
        grid=(rt.steps,),
        in_specs=[rt.seq_spec(t, d), rt.seq_spec(6, d), rt.row_spec(mix_a.shape[1]), rt.row_spec(mix_b.shape[1]),
                  _const_spec(wo.shape), _const_spec((1, d)), _const_spec(wr_t.shape), _const_spec(br.shape),
                  _const_spec(eye.shape), _const_spec(triu.shape)],
        out_specs=[rt.seq_spec(t, d), rt.tile_spec(d // 128), slot, slot, rt.row_spec(128), _const_spec((n_e, 128))],
        out_shape=[jax.ShapeDtypeStruct((n, t, d), F32), jax.ShapeDtypeStruct((m, d // 128, 128), F32),
                   jax.ShapeDtypeStruct((rt.steps, TOP_K, rt.rows), jnp.int32),
                   jax.ShapeDtypeStruct((rt.steps, TOP_K, rt.rows), jnp.int32),
                   jax.ShapeDtypeStruct((m, 128), F32), jax.ShapeDtypeStruct((n_e, 128), F32)],
        scratch_shapes=[pltpu.VMEM((n_e, 128), F32)],
        compiler_params=_params("arbitrary"),
        name="out_proj_route",
    )(x, mod, mix_a, mix_b, wo, g, wr_t, br, eye, triu)


def _row_copy(src_ref, src_row, dst_ref, dst_row, sem):
    return pltpu.make_async_copy(src_ref.at[pl.ds(src_row, 1)], dst_ref.at[pl.ds(dst_row, 1)], sem)


def _dispatch_kernel(dest_ref, hp_ref, xs_ref, sem):
    rows = hp_ref.shape[0]

    def issue(tok, carry):
        for k in range(TOP_K):
            _row_copy(hp_ref, tok, xs_ref, dest_ref[0, k, tok], sem).start(priority=k % 2)
        return carry

    lax.fori_loop(0, rows, issue, 0, unroll=2)

    def drain(j, carry):
        _row_copy(hp_ref, 0, xs_ref, 0, sem).wait()
        return carry

    lax.fori_loop(0, TOP_K * rows, drain, 0, unroll=16)


def _dispatch(dest3, hp, n_rows):
    steps, _, rows = dest3.shape
    assert rows & (rows - 1) == 0
    return pl.pallas_call(
        _dispatch_kernel,
        grid=(steps,),
        in_specs=[pl.BlockSpec((1, TOP_K, rows), lambda i: (i, 0, 0), memory_space=pltpu.SMEM),
                  pl.BlockSpec((rows,) + hp.shape[1:], lambda i: (i, 0, 0))],
        out_specs=pl.BlockSpec(memory_space=pl.ANY),
        out_shape=jax.ShapeDtypeStruct((n_rows,) + hp.shape[1:], F32),
        scratch_shapes=[pltpu.SemaphoreType.DMA(())],
        compiler_params=_params("arbitrary"),
        name="moe_dispatch_rows",
    )(dest3, hp)


def _gmm_kernel(be_ref, va_ref, nu_ref, x_ref, wg_ref, wu_ref, wd_ref, y_ref):
    i = pl.program_id(0)

    @pl.when(i < nu_ref[0])
    def _():
        x = _load_row_tiles(x_ref)
        live = lax.broadcasted_iota(jnp.int32, x.shape, 0) < va_ref[i]
        x = jnp.where(live, x, 0.0).astype(BF16)
        a = _silu(_dot(x, wg_ref[...].astype(BF16))) * _dot(x, wu_ref[...].astype(BF16))
        _store_row_tiles(y_ref, _dot(a.astype(BF16), wd_ref[...].astype(BF16)))

    @pl.when(i >= nu_ref[0])
    def _():
        y_ref[...] = jnp.zeros_like(y_ref)


def _gmm(layer, block_e, valid, n_used, xs, wg, wu, wd, tm):
    r = xs.shape[0]
    tile = xs.shape[1:]
    d, de = wg.shape[-2:]
    nb = r // tm
    grid_spec = pltpu.PrefetchScalarGridSpec(
        num_scalar_prefetch=3,
        grid=(nb,),
        in_specs=[pl.BlockSpec((tm,) + tile, lambda i, be, va, nu: (i, 0, 0)),
                  pl.BlockSpec((None, None, d, de), lambda i, be, va, nu: (layer, be[i], 0, 0)),
                  pl.BlockSpec((None, None, d, de), lambda i, be, va, nu: (layer, be[i], 0, 0)),
                  pl.BlockSpec((None, None, de, d), lambda i, be, va, nu: (layer, be[i], 0, 0))],
        out_specs=pl.BlockSpec((tm,) + tile, lambda i, be, va, nu: (i, 0, 0)),
    )
    return pl.pallas_call(
        _gmm_kernel,
        grid_spec=grid_spec,
        out_shape=jax.ShapeDtypeStruct(xs.shape, F32),
        compiler_params=_params("arbitrary"),
        name="moe_grouped_swiglu",
    )(block_e, valid, n_used, xs, wg, wu, wd)


def _combine_kernel(dest_ref, x_ref, mod_ref, hp_ref, w3_ref, wg_ref, wu_ref, wd_ref, ys_ref, o_ref, yg_scr, sem):
    rows = hp_ref.shape[0]

    def issue(tok, carry):
        for k in range(TOP_K):
            _row_copy(ys_ref, dest_ref[0, k, tok], yg_scr.at[k], tok, sem).start(priority=k % 2)
        return carry

    lax.fori_loop(0, rows, issue, 0, unroll=2)
    h = _load_row_tiles(hp_ref).astype(BF16)
    a = _silu(_dot(h, wg_ref[...])) * _dot(h, wu_ref[...])
    acc = _dot(a.astype(BF16), wd_ref[...])

    def drain(j, carry):
        _row_copy(ys_ref, 0, yg_scr.at[0], 0, sem).wait()
        return carry

    lax.fori_loop(0, TOP_K * rows, drain, 0, unroll=16)
    w3 = w3_ref[...]
    for k in range(TOP_K):
        wk = w3[:, k:k + 1] + w3[:, TOP_K + k:TOP_K + k + 1] + w3[:, 2 * TOP_K + k:2 * TOP_K + k + 1]
        acc = acc + wk * _load_row_tiles(yg_scr.at[k])
    x = x_ref[...]
    o_ref[...] = x + mod_ref[:, 5:6, :] * acc.reshape(x.shape)


def _combine(dest3, x1, mod, hp, w3, wg, wu, wd, ys):
    n, t, d = x1.shape
    rt = _Rows(n, t)
    assert rt.rows & (rt.rows - 1) == 0
    return pl.pallas_call(
        _combine_kernel,
        grid=(rt.steps,),
        in_specs=[pl.BlockSpec((1, TOP_K, rt.rows), lambda i: (i, 0, 0), memory_space=pltpu.SMEM),
                  rt.seq_spec(t, d), rt.seq_spec(6, d), rt.tile_spec(hp.shape[1]), rt.row_spec(128),
                  _const_spec(wg.shape), _const_spec(wu.shape), _const_spec(wd.shape),
                  pl.BlockSpec(memory_space=pl.ANY)],
        out_specs=rt.seq_spec(t, d),
        out_shape=jax.ShapeDtypeStruct((n, t, d), F32),
        scratch_shapes=[pltpu.VMEM((TOP_K, rt.rows) + hp.shape[1:], F32), pltpu.SemaphoreType.DMA(())],
        compiler_params=_params("arbitrary"),
        name="moe_combine_shared",
    )(dest3, x1, mod, hp, w3, wg, wu, wd, ys)


def _moe(layer, x1, mod, hp, idx3, rank3, w3, counts, p):
    n, t, d = x1.shape
    m = n * t
    n_e = p['w_gate_e'].shape[1]
    a = m * TOP_K
    tm = 256 if a // n_e >= 512 else 128
    counts = counts[:, 0].astype(jnp.int32)
    padded = (counts + tm - 1) // tm * tm
    pend = jnp.cumsum(padded)
    pstart = pend - padded
    onehot = idx3[None] == jnp.arange(n_e, dtype=jnp.int32)[:, None, None, None]
    dest3 = rank3 + jnp.sum(jnp.where(onehot, pstart[:, None, None, None], 0), axis=0)
    n_blocks = a // tm + n_e
    first = jnp.arange(n_blocks, dtype=jnp.int32) * tm
    block_e = jnp.minimum(jnp.sum((pend[None, :] <= first[:, None]).astype(jnp.int32), axis=1), n_e - 1)
    mine = block_e[:, None] == jnp.arange(n_e, dtype=jnp.int32)[None, :]
    last = jnp.sum(jnp.where(mine, (pstart + counts)[None, :], 0), axis=1)
    valid = jnp.clip(last - first, 0, tm).astype(jnp.int32)
    n_used = (pend[-1] // tm).astype(jnp.int32).reshape(1)
    xs = _dispatch(dest3, hp, n_blocks * tm)
    ys = _gmm(layer, block_e, valid, n_used, xs, p['w_gate_e'], p['w_up_e'], p['w_down_e'], tm)
    return _combine(dest3, x1, mod, hp, w3, p['w_gate_s'][layer].astype(BF16), p['w_up_s'][layer].astype(BF16),
                    p['w_down_s'][layer].astype(BF16), ys)


def _odd_in_kernel(x_ref, mod_ref, g_ref, w_ref, gqa_ref, wqb_ref, gqn_ref, gqr_ref, gqrs_ref, gkva_ref,
                   gkr_ref, gkrs_ref, gkn_ref, cos_ref, sin_ref, seg_ref, wuk_ref, wuv_ref,
                   cq_ref, ck_ref, cv_ref, co_ref, gt_ref, ckv_ref, kr_ref, *outs, prompt):
    h = _norm_mod(x_ref, mod_ref, g_ref, 0, 1).astype(BF16)
    z = _dot(h, w_ref[...])
    cq_ref[...] = z[:, 0:512]
    ck_ref[...] = z[:, 512:1024]
    cv_ref[...] = z[:, 1024:1536]
    co_ref[...] = z[:, 1536:2048]
    gt_ref[...] = z[:, 2560:2688]
    cos = cos_ref[...]
    sin = sin_ref[...]
    ckv = _rms(z[:, 2304:2432]) * gkva_ref[...]
    ckv_ref[...] = ckv
    xr = z[:, 2432:2496]
    xr_rot = z[:, 2496:2560]
    rr = lax.rsqrt(jnp.mean(xr * xr, axis=-1, keepdims=True) + EPS)
    kr = rr * (xr * gkr_ref[...] * cos[:, 0:D_ROPE] + xr_rot * gkrs_ref[...] * sin[:, 0:D_ROPE])
    kr_ref[...] = kr
    qa = (_rms(z[:, 2048:2304]) * gqa_ref[...]).astype(BF16)
    qd = _dot(qa, wqb_ref[...])
    nw = D_HEADS * D_NOPE
    rw = D_HEADS * D_ROPE
    qn = [_rms(qd[:, hh * D_NOPE:(hh + 1) * D_NOPE]) * gqn_ref[...] * MLA_SCALE for hh in range(D_HEADS)]
    xq = qd[:, nw:nw + rw]
    xq_rot = qd[:, nw + rw:nw + 2 * rw]
    rq = lax.rsqrt(_dot_sel(xq * xq, seg_ref[...]) * (1.0 / D_ROPE) + EPS)
    qr = rq * (xq * gqr_ref[...] * cos + xq_rot * gqrs_ref[...] * sin) * MLA_SCALE
    ckv_b = ckv.astype(BF16)
    if prompt:
        qcat_ref, kcat_ref, v_ref = outs
        kn = _dot(ckv_b, wuk_ref[...])
        zpad = jnp.zeros((z.shape[0], 256 - D_NOPE - D_ROPE), F32)
        qcat_ref[...] = jnp.concatenate(
            [piece for hh in range(D_HEADS) for piece in (qn[hh], qr[:, hh * D_ROPE:(hh + 1) * D_ROPE], zpad)],
            axis=1).astype(BF16)
        kcat_ref[...] = jnp.concatenate(
            [piece for hh in range(D_HEADS)
             for piece in (_rms(kn[:, hh * D_NOPE:(hh + 1) * D_NOPE]) * gkn_ref[...], kr, zpad)],
            axis=1).astype(BF16)
        v_ref[...] = _dot(ckv_b, wuv_ref[...]).astype(BF16)
    else:
        u_ref, qr_ref = outs
        u_ref[...] = jnp.concatenate(
            [_dot_nt((qn[hh] * gkn_ref[...]).astype(BF16), wuk_ref[:, hh * D_NOPE:(hh + 1) * D_NOPE])
             for hh in range(D_HEADS)], axis=1).astype(BF16)
        qr_ref[...] = qr.astype(BF16)


def _odd_in(x, mod, g, w, consts, cos, sin, prompt):
    n, t, d = x.shape
    rt = _Rows(n, t)
    m = n * t
    widths = [(512, F32)] * 4 + [(128, F32), (128, F32), (D_ROPE, F32)]
    if prompt:
        widths += [(1024, BF16), (1024, BF16), (512, BF16)]
    else:
        widths += [(512, BF16), (256, BF16)]
    return pl.pallas_call(
        functools.partial(_odd_in_kernel, prompt=prompt),
        grid=(rt.steps,),
        in_specs=[rt.seq_spec(t, d), rt.seq_spec(6, d), _const_spec((1, d)), _const_spec(w.shape)]
                 + [_const_spec(c.shape) for c in consts[:9]]
                 + [rt.pos_spec(cos.shape[1]), rt.pos_spec(sin.shape[1])]
                 + [_const_spec(c.shape) for c in consts[9:]],
        out_specs=[rt.row_spec(c) for c, _ in widths],
        out_shape=[jax.ShapeDtypeStruct((m, c), dt) for c, dt in widths],
        compiler_params=_params("arbitrary"),
        name="odd_in_latent_prep",
    )(x, mod, g, w, *consts[:9], cos, sin, *consts[9:])


def _mlstm_kernel(q_ref, k_ref, v_ref, og_ref, gt_ref, gb_ref, ng_ref, c0_ref, n0_ref, m0_ref, tri_ref, eye_ref,
                  o_ref, co_ref, no_ref, mo_ref, c_scr, n_scr, m_scr, *, chunk):
    dh = q_ref.shape[1] // C_HEADS
    t = pl.program_id(1)

    @pl.when(t == 0)
    def _():
        c_scr[...] = c0_ref[0]
        n_scr[...] = n0_ref[0]
        m_scr[...] = m0_ref[0]

    pre = gt_ref[...] + gb_ref[...]
    lf = _log_sigmoid(pre)
    if chunk >= 16:
        l_hi, l_mid, l_lo = _split3(lf)
        f_cum = _dot(tri_ref[...], l_hi) + _dot(tri_ref[...], l_mid) + _dot(tri_ref[...], l_lo)
    else:
        rowg = lax.broadcasted_iota(jnp.int32, lf.shape, 0)
        f_cum = jnp.zeros_like(lf)
        for s in range(chunk):
            f_cum = f_cum + jnp.where(rowg >= s, lf[s:s + 1, :], 0.0)
    ri = lax.broadcasted_iota(jnp.int32, (chunk, chunk), 0)
    ci = lax.broadcasted_iota(jnp.int32, (chunk, chunk), 1)
    pad = 16 - chunk if chunk < 16 else 0

    def pad_rows(a):
        if pad:
            return jnp.concatenate([a, jnp.zeros((pad, a.shape[1]), a.dtype)], axis=0)
        return a

    for h in range(C_HEADS):
        sl = slice(h * dh, (h + 1) * dh)
        q = q_ref[:, sl]
        k = k_ref[:, sl] * (dh ** -0.5)
        v = v_ref[:, sl]
        qb, kb, vb = q.astype(BF16), k.astype(BF16), v.astype(BF16)
        f_col = f_cum[:, C_HEADS + h:C_HEADS + h + 1]
        i_col = pre[:, h:h + 1]
        m_prev = m_scr[h:h + 1, 0:1]
        a_row = jnp.sum(jnp.where(ri == ci, i_col - f_col, 0.0), axis=0, keepdims=True)
        log_d = jnp.where(ci <= ri, f_col + a_row, NEG)
        inter = f_col + m_prev
        m_t = jnp.maximum(inter, jnp.max(log_d, axis=-1, keepdims=True))
        w_inter = jnp.exp(inter - m_t)
        qk = _dot_nt(qb, kb) * jnp.exp(log_d - m_t)
        c_old = c_scr[h]
        n_old = n_scr[h:h + 1, :]
        num = _dot(qk.astype(BF16), vb) + w_inter * _dot_nt(qb, c_old.astype(BF16))
        den = jnp.sum(qk, axis=-1, keepdims=True) + w_inter * jnp.sum(q * n_old, axis=-1, keepdims=True)
        hh = num / jnp.maximum(jnp.abs(den), jnp.exp(-m_t))
        o_ref[:, sl] = _rms(hh) * ng_ref[...] * _sigmoid(og_ref[:, sl])
        f_last = f_col[chunk - 1:chunk, :]
        a_end = f_last - f_col + i_col
        m_new = jnp.maximum(f_last + m_prev, jnp.max(a_end, axis=0, keepdims=True))
        w = jnp.exp(a_end - m_new)
        dec = jnp.exp(f_last + m_prev - m_new)
        wv_t = _dot_nt(eye_ref[...], pad_rows((w * v).astype(BF16))).astype(BF16)
        c_scr[h] = dec * c_old + _dot(wv_t, pad_rows(kb))
        n_scr[h:h + 1, :] = dec * n_old + jnp.sum(w * k, axis=0, keepdims=True)
        m_scr[h:h + 1, :] = jnp.broadcast_to(m_new, (1, m_scr.shape[1]))

    @pl.when(t == pl.num_programs(1) - 1)
    def _():
        co_ref[0] = c_scr[...]
        no_ref[0] = n_scr[...]
        mo_ref[0] = m_scr[...]


def _mlstm(n, t, q, k, v, og, gt, gb, ng, c0, n0, m0, tri, eye):
    chunk = CHUNK_C if t % CHUNK_C == 0 else t
    nt = t // chunk
    w = q.shape[1]
    dh = w // C_HEADS

    def rows(c):
        return pl.BlockSpec((chunk, c), lambda i, j: (i * nt + j, 0))

    cst = pl.BlockSpec((1, C_HEADS, dh, dh), lambda i, j: (i, 0, 0, 0))
    nst = pl.BlockSpec((1, C_HEADS, dh), lambda i, j: (i, 0, 0))
    return pl.pallas_call(
        functools.partial(_mlstm_kernel, chunk=chunk),
        grid=(n, nt),
        in_specs=[rows(w), rows(w), rows(w), rows(w), rows(128),
                  pl.BlockSpec(gb.shape, lambda i, j: (0, 0)), pl.BlockSpec(ng.shape, lambda i, j: (0, 0)),
                  cst, nst, nst,
                  pl.BlockSpec(tri.shape, lambda i, j: (0, 0)), pl.BlockSpec(eye.shape, lambda i, j: (0, 0))],
        out_specs=[rows(w), cst, nst, nst],
        out_shape=[jax.ShapeDtypeStruct((n * t, w), F32), jax.ShapeDtypeStruct(c0.shape, F32),
                   jax.ShapeDtypeStruct(n0.shape, F32), jax.ShapeDtypeStruct(m0.shape, F32)],
        scratch_shapes=[pltpu.VMEM((C_HEADS, dh, dh), F32), pltpu.VMEM((C_HEADS, dh), F32),
                        pltpu.VMEM((C_HEADS, dh), F32)],
        compiler_params=_params("arbitrary", "arbitrary"),
        name="mlstm_scan",
    )(q, k, v, og, gt, gb, ng, c0, n0, m0, tri, eye)


def _flash_kernel(qi_ref, kj_ref, q_ref, k_ref, v_ref, o_ref, m_scr, l_scr, acc_scr):
    p = pl.program_id(2)
    qi = qi_ref[p]
    kj = kj_ref[p]

    @pl.when(kj == 0)
    def _():
        m_scr[...] = jnp.full_like(m_scr, NEG)
        l_scr[...] = jnp.zeros_like(l_scr)
        acc_scr[...] = jnp.zeros_like(acc_scr)

    def step(masked):
        s = _dot_nt(q_ref[...], k_ref[...])
        bq, bk = s.shape
        if masked:
            ri = lax.broadcasted_iota(jnp.int32, (bq, bk), 0)
            ci = lax.broadcasted_iota(jnp.int32, (bq, bk), 1)
            s = jnp.where(ci <= ri, s, NEG)
        lanes = m_scr.shape[1]
        m_old = m_scr[...]
        m_new = jnp.maximum(m_old, jnp.max(s, axis=-1, keepdims=True))
        alpha = jnp.exp(m_old - m_new)
        pr = jnp.exp(s - jnp.concatenate([m_new] * (bk // lanes), axis=1))
        l_scr[...] = alpha * l_scr[...] + jnp.sum(pr, axis=-1, keepdims=True)
        acc_scr[...] = alpha * acc_scr[...] + _dot(pr.astype(BF16), v_ref[...])
        m_scr[...] = m_new

    @pl.when(kj < qi)
    def _():
        step(False)

    @pl.when(kj == qi)
    def _():
        step(True)
        o_ref[...] = acc_scr[...] / l_scr[...]


def _flash(n, t, qcat, kcat, v):
    blk = FLASH_BLOCK if t % FLASH_BLOCK == 0 else t
    nq = t // blk
    pairs = [(i, j) for i in range(nq) for j in range(i + 1)]
    qi = jnp.asarray([a for a, _ in pairs], jnp.int32)
    kj = jnp.asarray([b for _, b in pairs], jnp.int32)
    dv = v.shape[1] // D_HEADS
    grid_spec = pltpu.PrefetchScalarGridSpec(
        num_scalar_prefetch=2,
        grid=(n, D_HEADS, len(pairs)),
        in_specs=[pl.BlockSpec((blk, 256), lambda b, h, p, qi, kj: (b * nq + qi[p], h)),
                  pl.BlockSpec((blk, 256), lambda b, h, p, qi, kj: (b * nq + kj[p], h)),
                  pl.BlockSpec((blk, dv), lambda b, h, p, qi, kj: (b * nq + kj[p], h))],
        out_specs=pl.BlockSpec((blk, dv), lambda b, h, p, qi, kj: (b * nq + qi[p], h)),
        scratch_shapes=[pltpu.VMEM((blk, dv), F32), pltpu.VMEM((blk, dv), F32), pltpu.VMEM((blk, dv), F32)],
    )
    assert blk % dv == 0
    return pl.pallas_call(
        _flash_kernel,
        grid_spec=grid_spec,
        out_shape=jax.ShapeDtypeStruct((n * t, v.shape[1]), F32),
        compiler_params=_params("arbitrary", "arbitrary", "arbitrary"),
        name="mla_prompt_flash",
    )(qi, kj, qcat, kcat, v)


def _paged_kernel(pt_ref, u_ref, qr_ref, cn_ref, kn_ref, wukt_ref, wuv_ref, *rest, pages):
    ck_refs = rest[:pages]
    kr_refs = rest[pages:2 * pages]
    o_ref, m_scr, l_scr, acc_scr = rest[2 * pages:]
    g = pl.program_id(1)
    hq = u_ref.shape[1]
    tq = hq // D_HEADS

    @pl.when(g == 0)
    def _():
        m_scr[...] = jnp.full_like(m_scr, NEG)
        l_scr[...] = jnp.zeros_like(l_scr)
        acc_scr[...] = jnp.zeros_like(acc_scr)

    lhs = jnp.concatenate([wukt_ref[...], u_ref[0]], axis=0)
    qr = qr_ref[0]
    nk = D_HEADS * D_NOPE

    def scores(ck_b, kr_b, kr_keys_minor=True):
        big = _dot_nt(lhs, ck_b)
        rows = []
        for h in range(D_HEADS):
            kn_t = big[h * D_NOPE:(h + 1) * D_NOPE, :]
            rinv = lax.rsqrt(jnp.sum(kn_t * kn_t, axis=0, keepdims=True) * (1.0 / D_NOPE) + EPS)
            rows.append(big[nk + h * tq:nk + (h + 1) * tq, :] * rinv)
        rope = _dot(qr, kr_b) if kr_keys_minor else _dot_nt(qr, kr_b)
        return jnp.concatenate(rows, axis=0) + rope

    def update(s, ck_b):
        m_old = m_scr[...]
        m_new = jnp.maximum(m_old, jnp.max(s, axis=-1, keepdims=True))
        alpha = jnp.exp(m_old - m_new)
        pr = jnp.exp(s - m_new)
        l_scr[...] = alpha * l_scr[...] + jnp.sum(pr, axis=-1, keepdims=True)
        acc_scr[...] = alpha * acc_scr[...] + _dot(pr.astype(BF16), ck_b)
        m_scr[...] = m_new

    cks = [r[...].astype(BF16) for r in ck_refs]
    krs = [r[...].astype(BF16) for r in kr_refs]
    s_all = jnp.concatenate(
        [scores(jnp.concatenate(cks[i:i + 2], axis=0), jnp.concatenate(krs[i:i + 2], axis=1))
         for i in range(0, pages, 2)], axis=1)
    update(s_all, jnp.concatenate(cks, axis=0))

    @pl.when(g == pl.num_programs(1) - 1)
    def _():
        fill = PAGE_SIZE - tq
        ck_b = jnp.concatenate([cn_ref[...], jnp.zeros((fill, cn_ref.shape[1]), F32)], axis=0).astype(BF16)
        kr_b = jnp.concatenate([kn_ref[...], jnp.zeros((fill, kn_ref.shape[1]), F32)], axis=0).astype(BF16)
        s = scores(ck_b, kr_b, kr_keys_minor=False)
        ri = lax.broadcasted_iota(jnp.int32, s.shape, 0)
        ci = lax.broadcasted_iota(jnp.int32, s.shape, 1)
        update(jnp.where(ci <= ri % tq, s, NEG), ck_b)
        lat = (acc_scr[...] / l_scr[...]).astype(BF16)
        full = _dot(lat, wuv_ref[...])
        dv = wuv_ref.shape[1] // D_HEADS
        o_ref[...] = jnp.concatenate(
            [full[h * tq:(h + 1) * tq, h * dv:(h + 1) * dv] for h in range(D_HEADS)], axis=1)


def _paged(page_table, li, u3, qr3, ckv_new, kr_new, wuk_t, wuv, cache_ckv, cache_kr):
    n, hq, lat = u3.shape
    tq = hq // D_HEADS
    n_pages = page_table.shape[1]
    pages = min(PAGES_PER_STEP, n_pages)
    assert n_pages % pages == 0 and pages % 2 == 0
    ng = n_pages // pages
    cache_kr_t = jnp.swapaxes(cache_kr, 2, 3)

    def page_spec(i, shape):
        return pl.BlockSpec((None, None) + shape, lambda b, g, pt: (pt[b, g * pages + i], li, 0, 0))

    grid_spec = pltpu.PrefetchScalarGridSpec(
        num_scalar_prefetch=1,
        grid=(n, ng),
        in_specs=[pl.BlockSpec((1, hq, lat), lambda b, g, pt: (b, 0, 0)),
                  pl.BlockSpec((1, hq, D_ROPE), lambda b, g, pt: (b, 0, 0)),
                  pl.BlockSpec((tq, lat), lambda b, g, pt: (b, 0)),
                  pl.BlockSpec((tq, D_ROPE), lambda b, g, pt: (b, 0)),
                  pl.BlockSpec(wuk_t.shape, lambda b, g, pt: (0, 0)),
                  pl.BlockSpec(wuv.shape, lambda b, g, pt: (0, 0))]
                 + [page_spec(i, (PAGE_SIZE, lat)) for i in range(pages)]
                 + [page_spec(i, (D_ROPE, PAGE_SIZE)) for i in range(pages)],
        out_specs=pl.BlockSpec((tq, wuv.shape[1]), lambda b, g, pt: (b, 0)),
        scratch_shapes=[pltpu.VMEM((hq, 1), F32), pltpu.VMEM((hq, 1), F32), pltpu.VMEM((hq, lat), F32)],
    )
    return pl.pallas_call(
        functools.partial(_paged_kernel, pages=pages),
        grid_spec=grid_spec,
        out_shape=jax.ShapeDtypeStruct((n * tq, wuv.shape[1]), F32),
        compiler_params=_params("arbitrary", "arbitrary"),
        name="mla_sample_paged",
    )(page_table, u3, qr3, ckv_new, kr_new, wuk_t, wuv, *([cache_ckv] * pages), *([cache_kr_t] * pages))


def _np_seg(n_seg, seg_in, seg_out):
    mat = np.zeros((n_seg * seg_in, n_seg * seg_out), np.float32)
    for s in range(n_seg):
        mat[s * seg_in:(s + 1) * seg_in, s * seg_out:(s + 1) * seg_out] = 1.0
    return mat


def _rot_cols(w):
    half = w.shape[-1] // 2
    return jnp.concatenate([-w[..., half:], w[..., :half]], axis=-1)


def _swap_halves(g):
    half = g.shape[-1] // 2
    return jnp.concatenate([g[..., half:], g[..., :half]], axis=-1)


def _rope_tables(pos, reps):
    half = D_ROPE // 2
    inv = ROPE_BASE ** (-jnp.arange(half, dtype=F32) / half)
    ang = pos.astype(F32)[:, None] * inv
    cos = jnp.concatenate([jnp.cos(ang), jnp.cos(ang)], axis=-1)
    sin = jnp.concatenate([jnp.sin(ang), jnp.sin(ang)], axis=-1)
    return jnp.tile(cos, (1, reps)), jnp.tile(sin, (1, reps))


def _trunk(x, c_mod, pos0, gla0, mc0, mn0, mm0, p, sample_ctx):
    n, t, d = x.shape
    m = n * t
    rt = _Rows(n, t)
    eye_r = jnp.eye(rt.rows, dtype=BF16)
    triu_r = jnp.asarray(np.triu(np.ones((rt.rows, rt.rows), np.float32)), BF16)
    dh = mc0.shape[3]
    eye_hk = jnp.eye(B_HEADS * gla0.shape[3], dtype=BF16)
    eye_dh = jnp.eye(dh, dtype=BF16)
    results = {}

    layer, li = 0, 0
    mod = c_mod[layer]
    w_in = p['w_in_even'][li]
    w_in = jnp.concatenate([w_in, jnp.zeros((d, 128 - B_GATE_RANK), F32)], axis=1).astype(BF16)
    a_ws = jnp.tril(p['a_ws'][li])
    a_bs = p['a_bs'][li]
    if t % CHUNK_A == 0:
        ws = a_ws
        bs = jnp.repeat(a_bs.T, CHUNK_A, axis=1)
    else:
        ws = jnp.stack([jnp.kron(jnp.eye(CHUNK_A // t, dtype=F32), a_ws[g, :t, :t]) for g in range(A_GROUPS)])
        bs = jnp.repeat(jnp.tile(a_bs[:, :t], (1, CHUNK_A // t)).T, CHUNK_A, axis=1)
    out_a, v_rows, bq, bk, bv, br, bg = _even_in(
        x, mod, p['norm_mix_g'][layer].reshape(1, d), w_in, p['a_norm_g'][li].reshape(1, -1),
        ws.astype(BF16), bs)
    hk = bq.shape[1]
    dv = bv.shape[1] // B_HEADS
    wg2 = jnp.concatenate([p['b_w_gate2'][li], jnp.zeros((128 - B_GATE_RANK, hk), F32)], axis=0)
    seg = jnp.asarray(_np_seg(B_HEADS, hk // B_HEADS, dv), BF16)
    out_b, s_new = _gla(n, t, bq, bk, bv, br, bg, wg2, p['b_gate_bias'][li].reshape(1, hk),
                        jnp.tile(p['b_norm_g'][li], B_HEADS).reshape(1, -1), gla0[li], seg, eye_hk)
    results['gla'] = s_new
    results['v_rows'] = v_rows
    routed = _out_route(
        x, mod, out_a, out_b, p['w_out'][layer].astype(BF16), p['norm_ffn_g'][layer].reshape(1, d),
        p['w_router'][layer].T, p['b_router'][layer].reshape(-1, 1), eye_r, triu_r)
    x = _moe(layer, routed[0], mod, *routed[1:], p)

    layer, li = 1, 0
    mod = c_mod[layer]
    w = p['w_in_odd'][li]
    hw = C_HEADS * dh
    o_g = 3 * hw
    o_o = o_g + 2 * C_HEADS
    o_qa = o_o + hw
    o_kva = o_qa + p['d_g_qa'].shape[1]
    o_kr = o_kva + p['d_g_kva'].shape[1]
    w_kr = w[:, o_kr:o_kr + D_ROPE]
    w_odd = jnp.concatenate(
        [w[:, :o_g], w[:, o_o:o_qa], w[:, o_qa:o_kva], w[:, o_kva:o_kr], w_kr, _rot_cols(w_kr),
         w[:, o_g:o_o], jnp.zeros((d, 128 - 2 * C_HEADS), F32)], axis=1).astype(BF16)
    wqb = p['d_w_qb'][li].reshape(-1, D_HEADS, D_NOPE + D_ROPE)
    wqb_r = wqb[:, :, D_NOPE:]
    wqb2 = jnp.concatenate([wqb[:, :, :D_NOPE].reshape(-1, D_HEADS * D_NOPE),
                            wqb_r.reshape(-1, D_HEADS * D_ROPE),
                            _rot_cols(wqb_r).reshape(-1, D_HEADS * D_ROPE)], axis=1).astype(BF16)
    w_uk = p['d_w_uk'][li]
    lat = w_uk.shape[0]
    g_qr = p['d_g_qr'][li]
    g_kr = p['d_g_kr'][li]
    cos, sin = _rope_tables(pos0 + jnp.arange(t), D_HEADS)
    prompt = sample_ctx is None
    if not prompt:
        cos, sin = jnp.tile(cos, (rt.s, 1)), jnp.tile(sin, (rt.s, 1))
    consts = [p['d_g_qa'][li].reshape(1, -1), wqb2, p['d_g_qn'][li].reshape(1, -1),
              jnp.tile(g_qr, D_HEADS).reshape(1, -1), jnp.tile(_swap_halves(g_qr), D_HEADS).reshape(1, -1),
              p['d_g_kva'][li].reshape(1, -1), g_kr.reshape(1, -1), _swap_halves(g_kr).reshape(1, -1),
              p['d_g_kn'][li].reshape(1, -1),
              jnp.asarray(_np_seg(D_HEADS, D_ROPE, D_ROPE), BF16),
              w_uk.reshape(lat, -1).astype(BF16), p['d_w_uv'][li].reshape(lat, -1).astype(BF16)]
    outs = _odd_in(x, mod, p['norm_mix_g'][layer].reshape(1, d), w_odd, consts, cos, sin, prompt)
    cq, ck, cv, co, gates, ckv, kr = outs[:7]
    gb = jnp.concatenate([p['c_ig_bias'][li], p['c_fg_bias'][li],
                          jnp.zeros((128 - 2 * C_HEADS,), F32)]).reshape(1, 128)
    chunk = CHUNK_C if t % CHUNK_C == 0 else t
    tri = jnp.asarray(np.tril(np.ones((chunk, chunk), np.float32)), BF16)
    m0b =jnp.broadcast_to(mm0[li][:, :, None], (n, C_HEADS, dh))
    out_c, c_new, n_new, m_new = _mlstm(n, t, cq, ck, cv, co, gates, gb,
                                        p['c_norm_g'][li].reshape(1, -1), mc0[li], mn0[li], m0b, tri, eye_dh)
    results['mlstm'] = (c_new, n_new, m_new[:, :, 0])
    results['ckv'] = ckv.reshape(n, t, -1)
    results['kr'] = kr.reshape(n, t, -1)
    if prompt:
        qcat, kcat, vv = outs[7:]
        out_d = _flash(n, t, qcat, kcat, vv)
    else:
        u, qr = outs[7:]
        cache_ckv, cache_kr, page_table = sample_ctx
        u3 = u.reshape(n, t, D_HEADS, lat).transpose(0, 2, 1, 3).reshape(n, D_HEADS * t, lat)
        qr3 = qr.reshape(n, t, D_HEADS, D_ROPE).transpose(0, 2, 1, 3).reshape(n, D_HEADS * t, D_ROPE)
        wuk_t = w_uk.transpose(1, 2, 0).reshape(-1, lat).astype(BF16)
        out_d = _paged(page_table, li, u3, qr3, ckv, kr, wuk_t, consts[11], cache_ckv, cache_kr)
    routed = _out_route(
        x, mod, out_c, out_d, p['w_out'][layer].astype(BF16), p['norm_ffn_g'][layer].reshape(1, d),
        p['w_router'][layer].T, p['b_router'][layer].reshape(-1, 1), eye_r, triu_r)
    x = _moe(layer, routed[0], mod, *routed[1:], p)
    return x, results


def kernel(x_prompt, x_sample, state_gla, state_mlstm_c, state_mlstm_n, state_mlstm_m,
           cache_ckv, cache_krope, page_table, c_prompt, c_sample,
           norm_mix_g, norm_ffn_g, w_ada, b_ada, w_out,
           w_in_even, a_norm_g, a_ws, a_bs, b_w_gate2, b_gate_bias, b_norm_g,
           w_in_odd, c_ig_bias, c_fg_bias, c_norm_g,
           d_g_qa, d_w_qb, d_g_kva, d_g_qn, d_g_qr, d_g_kr, d_g_kn, d_w_uk, d_w_uv,
           w_router, b_router, w_gate_e, w_up_e, w_down_e, w_gate_s, w_up_s, w_down_s):
    p = dict(norm_mix_g=norm_mix_g, norm_ffn_g=norm_ffn_g, w_out=w_out,
             w_in_even=w_in_even, a_norm_g=a_norm_g, a_ws=a_ws, a_bs=a_bs, b_w_gate2=b_w_gate2,
             b_gate_bias=b_gate_bias, b_norm_g=b_norm_g, w_in_odd=w_in_odd, c_ig_bias=c_ig_bias,
             c_fg_bias=c_fg_bias, c_norm_g=c_norm_g, d_g_qa=d_g_qa, d_w_qb=d_w_qb, d_g_kva=d_g_kva,
             d_g_qn=d_g_qn, d_g_qr=d_g_qr, d_g_kr=d_g_kr, d_g_kn=d_g_kn, d_w_uk=d_w_uk, d_w_uv=d_w_uv,
             w_router=w_router, b_router=b_router, w_gate_e=w_gate_e, w_up_e=w_up_e, w_down_e=w_down_e,
             w_gate_s=w_gate_s, w_up_s=w_up_s, w_down_s=w_down_s)
    n_p, t_p, d = x_prompt.shape
    n_s, t_s, _ = x_sample.shape
    depth = w_ada.shape[0]
    pad_p = (-n_p) % 8
    c_all = jnp.concatenate([c_prompt, jnp.zeros((pad_p, d), F32), c_sample], axis=0)
    mod_all = _ada(c_all, w_ada, b_ada).reshape(depth, c_all.shape[0], 6, d)
    mod_p = mod_all[:, :n_p]
    mod_s = mod_all[:, n_p + pad_p:]

    n_even, _, bh, bdk, bdv = state_gla.shape
    n_odd, _, chh, cdh, _ = state_mlstm_c.shape
    gla0_p = jnp.zeros((n_even, n_p, bh, bdk, bdv), F32)
    mc0_p = jnp.zeros((n_odd, n_p, chh, cdh, cdh), F32)
    mn0_p = jnp.zeros((n_odd, n_p, chh, cdh), F32)
    mm0_p = jnp.full((n_odd, n_p, chh), NEG, F32)
    past_len = page_table.shape[1] * PAGE_SIZE

    y_p, rp = _trunk(x_prompt, mod_p, 0, gla0_p, mc0_p, mn0_p, mm0_p, p, None)
    y_s, rs = _trunk(x_sample, mod_s, past_len, state_gla, state_mlstm_c, state_mlstm_n, state_mlstm_m, p,
                     (cache_ckv, cache_krope, page_table))
    aw = rs['v_rows'].shape[1]
    return (y_p, y_s, rp['gla'][None], rs['gla'][None], rs['v_rows'].reshape(1, n_s, t_s, aw),
            rp['mlstm'][0][None], rs['mlstm'][0][None], rp['mlstm'][1][None], rs['mlstm'][1][None],
            rp['mlstm'][2][None], rs['mlstm'][2][None],
            rp['ckv'][:, None], rs['ckv'][:, None], rp['kr'][:, None], rs['kr'][:, None])
```

```python
import functools

import numpy as np
import jax
import jax.numpy as jnp
from jax import lax
from jax.experimental import pallas as pl
from jax.experimental.pallas import tpu as pltpu

F32 = jnp.float32
BF16 = jnp.bfloat16

EPS = 1e-6
NEG = -1e30

A_GROUPS = 4
CHUNK_A = 128
B_HEADS = 4
B_GATE_RANK = 16
B_GATE_TAU = 16.0
CHUNK_B = 16
C_HEADS = 4
CHUNK_C = 128
D_HEADS = 4
D_NOPE = 128
D_ROPE = 64
ROPE_BASE = 10000.0
MLA_SCALE = (D_NOPE + D_ROPE) ** -0.5
PAGE_SIZE = 128
N_EXPERT_GROUPS = 8
TOPK_GROUPS = 4
TOP_K = 8
ROUTED_SCALE = 2.5

ROW_TILE = 256
SAMPLE_SEQS = 32
FLASH_BLOCK = 512
PAGES_PER_STEP = 16
VMEM_LIMIT = 56 * 1024 * 1024


def _dot(a, b):
    return jnp.dot(a, b, preferred_element_type=F32)


def _dot_nt(a, b):
    return lax.dot_general(a, b, (((1,), (1,)), ((), ())), preferred_element_type=F32)


def _split2(x):
    hi = x.astype(BF16)
    lo = (x - hi.astype(F32)).astype(BF16)
    return hi, lo


def _split3(x):
    hi = x.astype(BF16)
    r = x - hi.astype(F32)
    mid = r.astype(BF16)
    lo = (r - mid.astype(F32)).astype(BF16)
    return hi, mid, lo


def _dot3(a, b):
    ah, al = _split2(a)
    bh, bl = _split2(b)
    return _dot(ah, bh) + _dot(ah, bl) + _dot(al, bh)


def _dot3_nt(a, b):
    ah, al = _split2(a)
    bh, bl = _split2(b)
    return _dot_nt(ah, bh) + _dot_nt(ah, bl) + _dot_nt(al, bh)


def _dot_sel(x, m01, parts=2):
    ps = _split2(x) if parts == 2 else _split3(x)
    acc = _dot(ps[0], m01)
    for p in ps[1:]:
        acc = acc + _dot(p, m01)
    return acc


def _sigmoid(x):
    return 1.0 / (1.0 + jnp.exp(-x))


def _silu(x):
    return x * _sigmoid(x)


def _log_sigmoid(x):
    return jnp.minimum(x, 0.0) - jnp.log(1.0 + jnp.exp(-jnp.abs(x)))


def _gelu(x):
    return 0.5 * x * (1.0 + jnp.tanh(0.7978845608028654 * (x + 0.044715 * x * x * x)))


def _rms(x, eps=EPS):
    return x * lax.rsqrt(jnp.mean(x * x, axis=-1, keepdims=True) + eps)


def _params(*sem):
    return pltpu.CompilerParams(dimension_semantics=sem, vmem_limit_bytes=VMEM_LIMIT)


def _const_spec(shape):
    nd = len(shape)
    return pl.BlockSpec(shape, lambda *_: (0,) * nd)


class _Rows:
    def __init__(self, n, t):
        self.n, self.t = n, t
        if t % ROW_TILE == 0:
            self.s, self.r = 1, ROW_TILE
            self.tpb = t // ROW_TILE
        else:
            assert t == 8 and n % SAMPLE_SEQS == 0, (n, t)
            self.s, self.r = SAMPLE_SEQS, t
            self.tpb = 1
        self.rows = self.s * self.r
        self.steps = n * t // self.rows

    def seq_spec(self, mid, d):
        s, tpb = self.s, self.tpb
        if mid == self.t:
            return pl.BlockSpec((s, self.r, d), lambda i: (i // tpb, i % tpb, 0))
        return pl.BlockSpec((s, mid, d), lambda i: (i // tpb, 0, 0))

    def row_spec(self, c):
        return pl.BlockSpec((self.rows, c), lambda i: (i, 0))

    def tile_spec(self, sub):
        return pl.BlockSpec((self.rows * sub, 128), lambda i: (i, 0))

    def pos_spec(self, c):
        tpb = self.tpb
        return pl.BlockSpec((self.rows, c), lambda i: (i % tpb, 0))


def _ada_kernel(c_ref, w_ref, b_ref, o_ref):
    o_ref[0] = _dot3(_silu(c_ref[...]), w_ref[0]) + b_ref[0]


def _ada(c_all, w_ada, b_ada):
    depth, d, d6 = w_ada.shape
    nc = c_all.shape[0]
    tn = 1536
    return pl.pallas_call(
        _ada_kernel,
        grid=(depth, d6 // tn),
        in_specs=[pl.BlockSpec((nc, d), lambda l, j: (0, 0)),
                  pl.BlockSpec((1, d, tn), lambda l, j: (l, 0, j)),
                  pl.BlockSpec((1, 1, tn), lambda l, j: (l, 0, j))],
        out_specs=pl.BlockSpec((1, nc, tn), lambda l, j: (l, 0, j)),
        out_shape=jax.ShapeDtypeStruct((depth, nc, d6), F32),
        compiler_params=_params("arbitrary", "arbitrary"),
        name="ada_modulation",
    )(c_all, w_ada, b_ada.reshape(depth, 1, d6))


def _norm_mod(x_ref, mod_ref, g_ref, shift_row, scale_row):
    x = x_ref[...]
    h = _rms(x) * g_ref[...]
    h = h * (1.0 + mod_ref[:, scale_row:scale_row + 1, :]) + mod_ref[:, shift_row:shift_row + 1, :]
    s, r, d = x.shape
    return h.reshape(s * r, d)


def _even_in_kernel(x_ref, mod_ref, g_ref, w_ref, an_ref, ws_ref, bs_ref,
                    oa_ref, v_ref, q_ref, k_ref, bv_ref, r_ref, bg_ref):
    h = _norm_mod(x_ref, mod_ref, g_ref, 0, 1).astype(BF16)
    z = _dot(h, w_ref[...])
    aw = A_GROUPS * CHUNK_A
    rows = z.shape[0]
    u = _gelu(z[:, 0:aw])
    va = _gelu(z[:, aw:2 * aw])
    vn = jnp.concatenate(
        [_rms(va[:, g * CHUNK_A:(g + 1) * CHUNK_A]) for g in range(A_GROUPS)], axis=1) * an_ref[...]
    v_ref[...] = vn
    vb = vn.astype(BF16)
    for c in range(rows // CHUNK_A):
        r0 = c * CHUNK_A
        mixed = jnp.concatenate(
            [_dot(ws_ref[g], vb[r0:r0 + CHUNK_A, g * CHUNK_A:(g + 1) * CHUNK_A]) for g in range(A_GROUPS)],
            axis=1) + bs_ref[...]
        oa_ref[r0:r0 + CHUNK_A, :] = u[r0:r0 + CHUNK_A, :] * mixed
    o = 2 * aw
    q_ref[...] = z[:, o:o + 256]
    k_ref[...] = z[:, o + 256:o + 512]
    bv_ref[...] = z[:, o + 512:o + 1024]
    r_ref[...] = z[:, o + 1024:o + 1536]
    bg_ref[...] = z[:, o + 1536:o + 1664]


def _even_in(x, mod, g, w, an, ws, bs):
    n, t, d = x.shape
    rt = _Rows(n, t)
    m = n * t
    widths = (512, 512, 256, 256, 512, 512, 128)
    return pl.pallas_call(
        _even_in_kernel,
        grid=(rt.steps,),
        in_specs=[rt.seq_spec(t, d), rt.seq_spec(6, d), _const_spec((1, d)), _const_spec(w.shape),
                  _const_spec(an.shape), _const_spec(ws.shape), _const_spec(bs.shape)],
        out_specs=[rt.row_spec(c) for c in widths],
        out_shape=[jax.ShapeDtypeStruct((m, c), F32) for c in widths],
        compiler_params=_params("arbitrary"),
        name="even_in_chunk_gate",
    )(x, mod, g, w, an, ws, bs)


def _gla_kernel(q_ref, k_ref, v_ref, r_ref, g_ref, wg_ref, gb_ref, ng_ref, s0_ref, seg_ref, eye_ref,
                o_ref, so_ref, s_scr, lg_scr, o_scr, *, chunk, n_chunks):
    hk = q_ref.shape[1]
    dk = hk // B_HEADS
    dv = v_ref.shape[1] // B_HEADS
    t = pl.program_id(1)

    @pl.when(t == 0)
    def _():
        s_scr[...] = s0_ref[0].reshape(hk, dv)

    lg_scr[...] = _log_sigmoid(_dot3(g_ref[...], wg_ref[...]) + gb_ref[...]) * (1.0 / B_GATE_TAU)
    row = lax.broadcasted_iota(jnp.int32, (chunk, hk), 0)
    lane = lax.broadcasted_iota(jnp.int32, (chunk, hk), 1)
    pad = 16 - chunk if chunk < 16 else 0

    def pad_rows(a):
        if pad:
            return jnp.concatenate([a, jnp.zeros((pad, a.shape[1]), a.dtype)], axis=0)
        return a

    def body(c, carry):
        r0 = pl.multiple_of(c * chunk, chunk)
        lg = lg_scr[pl.ds(r0, chunk), :]
        q = q_ref[pl.ds(r0, chunk), :] * (dk ** -0.5)
        k = k_ref[pl.ds(r0, chunk), :]
        v = v_ref[pl.ds(r0, chunk), :]
        b = jnp.zeros_like(lg)
        for s in range(chunk):
            b = b + jnp.where(row >= s, lg[s:s + 1, :], 0.0)
        prods = []
        for l in range(chunk):
            e = jnp.exp(jnp.where(row <= l, b[l:l + 1, :] - b, NEG))
            prods.append(q[l:l + 1, :] * k * e)
        att = _dot_sel(jnp.concatenate(prods, axis=0), seg_ref[...])
        o_intra = jnp.concatenate(
            [jnp.sum(att[l * chunk:(l + 1) * chunk, :] * v, axis=0, keepdims=True) for l in range(chunk)], axis=0)
        s_old = s_scr[...]
        s_b = s_old.astype(BF16)
        qe = q * jnp.exp(b)
        o_inter = jnp.concatenate(
            [_dot(jnp.where(lane // dk == h, qe, 0.0).astype(BF16), s_b) for h in range(B_HEADS)], axis=1)
        o_scr[pl.ds(r0, chunk), :] = o_intra + o_inter
        b_last = b[chunk - 1:chunk, :]
        k_end = pad_rows((k * jnp.exp(b_last - b)).astype(BF16))
        k_t = _dot_nt(eye_ref[...], k_end).astype(BF16)
        dh, dm, dl = _split3(jnp.exp(b_last))
        dec3 = _dot_nt(eye_ref[...], jnp.concatenate([dh, dm, dl, jnp.zeros((13, hk), BF16)], axis=0))
        dec = dec3[:, 0:1] + dec3[:, 1:2] + dec3[:, 2:3]
        v_b = pad_rows(v.astype(BF16))
        kv = jnp.concatenate(
            [_dot(k_t[h * dk:(h + 1) * dk, :], v_b[:, h * dv:(h + 1) * dv]) for h in range(B_HEADS)], axis=0)
        s_scr[...] = dec * s_old + kv
        return carry

    lax.fori_loop(0, n_chunks, body, 0)

    o = o_scr[...]
    on = jnp.concatenate([_rms(o[:, h * dv:(h + 1) * dv]) for h in range(B_HEADS)], axis=1) * ng_ref[...]
    o_ref[...] = on * _silu(r_ref[...])

    @pl.when(t == pl.num_programs(1) - 1)
    def _():
        so_ref[0] = s_scr[...].reshape(B_HEADS, dk, dv)


def _gla(n, t, q, k, v, r, g, wg, gb, ng, s0, seg, eye):
    tt = ROW_TILE if t % ROW_TILE == 0 else t
    chunk = CHUNK_B if tt % CHUNK_B == 0 else tt
    nt = t // tt
    hk, hv = q.shape[1], v.shape[1]
    dk, dv = hk // B_HEADS, hv // B_HEADS

    def rows(c):
        return pl.BlockSpec((tt, c), lambda i, j: (i * nt + j, 0))

    st = pl.BlockSpec((1, B_HEADS, dk, dv), lambda i, j: (i, 0, 0, 0))
    return pl.pallas_call(
        functools.partial(_gla_kernel, chunk=chunk, n_chunks=tt // chunk),
        grid=(n, nt),
        in_specs=[rows(hk), rows(hk), rows(hv), rows(hv), rows(g.shape[1]),
                  pl.BlockSpec(wg.shape, lambda i, j: (0, 0)), pl.BlockSpec(gb.shape, lambda i, j: (0, 0)),
                  pl.BlockSpec(ng.shape, lambda i, j: (0, 0)), st,
                  pl.BlockSpec(seg.shape, lambda i, j: (0, 0)), pl.BlockSpec(eye.shape, lambda i, j: (0, 0))],
        out_specs=[rows(hv), st],
        out_shape=[jax.ShapeDtypeStruct((n * t, hv), F32), jax.ShapeDtypeStruct(s0.shape, F32)],
        scratch_shapes=[pltpu.VMEM((hk, dv), F32), pltpu.VMEM((tt, hk), F32), pltpu.VMEM((tt, hv), F32)],
        compiler_params=_params("arbitrary", "arbitrary"),
        name="gla_scan",
    )(q, k, v, r, g, wg, gb, ng, s0, seg, eye)


def _route(logits_t, bias_col):
    n_e, r = logits_t.shape
    per = n_e // N_EXPERT_GROUPS
    scores = _sigmoid(logits_t)
    biased = scores + bias_col
    sub = lax.broadcasted_iota(jnp.int32, (per, r), 0).astype(F32)
    ninf = -jnp.inf
    gs = []
    for g in range(N_EXPERT_GROUPS):
        blk = biased[g * per:(g + 1) * per, :]
        m1 = jnp.max(blk, axis=0, keepdims=True)
        i1 = jnp.min(jnp.where(blk == m1, sub, float(per)), axis=0, keepdims=True)
        m2 = jnp.max(jnp.where(sub == i1, ninf, blk), axis=0, keepdims=True)
        gs.append(m1 + m2)
    cur = jnp.concatenate(gs, axis=0)
    gsub = lax.broadcasted_iota(jnp.int32, (N_EXPERT_GROUPS, r), 0).astype(F32)
    gsel = jnp.zeros((N_EXPERT_GROUPS, r), F32)
    for _ in range(TOPK_GROUPS):
        m = jnp.max(cur, axis=0, keepdims=True)
        i = jnp.min(jnp.where(cur == m, gsub, float(N_EXPERT_GROUPS)), axis=0, keepdims=True)
        hit = gsub == i
        gsel = jnp.where(hit, 1.0, gsel)
        cur = jnp.where(hit, ninf, cur)
    cur = jnp.concatenate(
        [jnp.where(gsel[g:g + 1, :] > 0.5, biased[g * per:(g + 1) * per, :], ninf) for g in range(N_EXPERT_GROUPS)],
        axis=0)
    esub = lax.broadcasted_iota(jnp.int32, (n_e, r), 0).astype(F32)
    idx, wts, hits = [], [], []
    for _ in range(TOP_K):
        m = jnp.max(cur, axis=0, keepdims=True)
        i = jnp.min(jnp.where(cur == m, esub, float(n_e)), axis=0, keepdims=True)
        hit = esub == i
        idx.append(i)
        hits.append(hit)
        wts.append(jnp.sum(jnp.where(hit, scores, 0.0), axis=0, keepdims=True))
        cur = jnp.where(hit, ninf, cur)
    w = jnp.concatenate(wts, axis=0)
    w = w / jnp.sum(w, axis=0, keepdims=True) * ROUTED_SCALE
    return jnp.concatenate(idx, axis=0), w, hits


def _store_row_tiles(ref, x):
    rows, d = x.shape
    sub = d // 128
    xr = x.astype(BF16).astype(F32)
    for c in range(sub):
        ref[pl.ds(c, rows, stride=sub), :] = xr[:, c * 128:(c + 1) * 128]


def _load_row_tiles(ref, sub):
    rows = ref.shape[0] // sub
    return jnp.concatenate([ref[pl.ds(c, rows, stride=sub), :] for c in range(sub)], axis=1)


def _out_kernel(x_ref, mod_ref, ma_ref, mb_ref, wo_ref, g_ref, wr_ref, br_ref, eye_ref, triu_ref,
                x1_ref, hp_ref, idx_ref, rank_ref, w3_ref, cnt_ref, run_scr):
    step = pl.program_id(0)

    @pl.when(step == 0)
    def _():
        run_scr[...] = jnp.zeros_like(run_scr)

    half = ma_ref.shape[1]
    y = _dot(ma_ref[...].astype(BF16), wo_ref[0:half, :]) + _dot(mb_ref[...].astype(BF16), wo_ref[half:, :])
    x = x_ref[...]
    s, r, d = x.shape
    x1 = x + mod_ref[:, 2:3, :] * y.reshape(s, r, d)
    x1_ref[...] = x1
    h2 = (_rms(x1) * g_ref[...] * (1.0 + mod_ref[:, 4:5, :]) + mod_ref[:, 3:4, :]).reshape(s * r, d)
    _store_row_tiles(hp_ref, h2)
    idx, w, hits = _route(_dot3_nt(wr_ref[...], h2), br_ref[...])
    idx_ref[0] = idx.astype(jnp.int32)
    sel = jnp.zeros(hits[0].shape, F32)
    for hit in hits:
        sel = jnp.where(hit, 1.0, sel)
    before = run_scr[:, 0:1] + _dot(sel.astype(BF16), triu_ref[...]) - sel
    rank_ref[0] = jnp.concatenate(
        [jnp.sum(jnp.where(hit, before, 0.0), axis=0, keepdims=True) for hit in hits], axis=0).astype(jnp.int32)
    run_scr[...] = run_scr[...] + jnp.sum(sel, axis=1, keepdims=True)
    cnt_ref[...] = run_scr[...]
    wh, wm, wl = _split3(w)
    stack = jnp.concatenate([wh, wm, wl, jnp.zeros((128 - 3 * TOP_K, s * r), BF16)], axis=0)
    w3_ref[...] = _dot_nt(eye_ref[...], stack)


def _out_route(x, mod, mix_a, mix_b, wo, g, wr_t, br, eye, triu):
    n, t, d = x.shape
    rt = _Rows(n, t)
    m = n * t
    n_e = wr_t.shape[0]
    slot = pl.BlockSpec((1, TOP_K, rt.rows), lambda i: (i, 0, 0))
    return pl.pallas_call(
        _out_kernel,
        grid=(rt.steps,),
        in_specs=[rt.seq_spec(t, d), rt.seq_spec(6, d), rt.row_spec(mix_a.shape[1]), rt.row_spec(mix_b.shape[1]),
                  _const_spec(wo.shape), _const_spec((1, d)), _const_spec(wr_t.shape), _const_spec(br.shape),
                  _const_spec(eye.shape), _const_spec(triu.shape)],
        out_specs=[rt.seq_spec(t, d), rt.tile_spec(d // 128), slot, slot, rt.row_spec(128), _const_spec((n_e, 128))],
        out_shape=[jax.ShapeDtypeStruct((n, t, d), F32), jax.ShapeDtypeStruct((m * (d // 128), 128), F32),
                   jax.ShapeDtypeStruct((rt.steps, TOP_K, rt.rows), jnp.int32),
                   jax.ShapeDtypeStruct((rt.steps, TOP_K, rt.rows), jnp.int32),
                   jax.ShapeDtypeStruct((m, 128), F32), jax.ShapeDtypeStruct((n_e, 128), F32)],
        scratch_shapes=[pltpu.VMEM((n_e, 128), F32)],
        compiler_params=_params("arbitrary"),
        name="out_proj_route",
    )(x, mod, mix_a, mix_b, wo, g, wr_t, br, eye, triu)


ROW_SUB = 8


def _row_copy(src_ref, src_row, dst_ref, dst_row, sem):
    src = src_ref.at[pl.ds(pl.multiple_of(src_row * ROW_SUB, ROW_SUB), ROW_SUB), :]
    dst = dst_ref.at[pl.ds(pl.multiple_of(dst_row * ROW_SUB, ROW_SUB), ROW_SUB), :]
    return pltpu.make_async_copy(src, dst, sem)


def _dispatch_kernel(dest_ref, hp_ref, xs_ref, sem):
    rows = hp_ref.shape[0] // ROW_SUB

    def issue(tok, carry):
        for k in range(TOP_K):
            _row_copy(hp_ref, tok, xs_ref, dest_ref[0, k, tok], sem).start(priority=k % 2)
        return carry

    lax.fori_loop(0, rows, issue, 0, unroll=2)

    def drain(j, carry):
        _row_copy(hp_ref, 0, xs_ref, 0, sem).wait()
        return carry

    lax.fori_loop(0, TOP_K * rows, drain, 0, unroll=16)


def _dispatch(dest3, hp, n_rows):
    steps, _, rows = dest3.shape
    assert rows & (rows - 1) == 0
    return pl.pallas_call(
        _dispatch_kernel,
        grid=(steps,),
        in_specs=[pl.BlockSpec((1, TOP_K, rows), lambda i: (i, 0, 0), memory_space=pltpu.SMEM),
                  pl.BlockSpec((rows * ROW_SUB, 128), lambda i: (i, 0))],
        out_specs=pl.BlockSpec(memory_space=pl.ANY),
        out_shape=jax.ShapeDtypeStruct((n_rows * ROW_SUB, 128), F32),
        scratch_shapes=[pltpu.SemaphoreType.DMA(())],
        compiler_params=_params("arbitrary"),
        name="moe_dispatch_rows",
    )(dest3, hp)


def _gmm_kernel(be_ref, va_ref, nu_ref, x_ref, wg_ref, wu_ref, wd_ref, y_ref):
    i = pl.program_id(0)

    @pl.when(i < nu_ref[0])
    def _():
        x = _load_row_tiles(x_ref, ROW_SUB)
        live = lax.broadcasted_iota(jnp.int32, x.shape, 0) < va_ref[i]
        x = jnp.where(live, x, 0.0).astype(BF16)
        a = _silu(_dot(x, wg_ref[...].astype(BF16))) * _dot(x, wu_ref[...].astype(BF16))
        _store_row_tiles(y_ref, _dot(a.astype(BF16), wd_ref[...].astype(BF16)))

    @pl.when(i >= nu_ref[0])
    def _():
        y_ref[...] = jnp.zeros_like(y_ref)


def _gmm(layer, block_e, valid, n_used, xs, wg, wu, wd, tm):
    r = xs.shape[0] // ROW_SUB
    d, de = wg.shape[-2:]
    assert d == ROW_SUB * 128
    nb = r // tm
    grid_spec = pltpu.PrefetchScalarGridSpec(
        num_scalar_prefetch=3,
        grid=(nb,),
        in_specs=[pl.BlockSpec((tm * ROW_SUB, 128), lambda i, be, va, nu: (i, 0)),
                  pl.BlockSpec((None, None, d, de), lambda i, be, va, nu: (layer, be[i], 0, 0)),
                  pl.BlockSpec((None, None, d, de), lambda i, be, va, nu: (layer, be[i], 0, 0)),
                  pl.BlockSpec((None, None, de, d), lambda i, be, va, nu: (layer, be[i], 0, 0))],
        out_specs=pl.BlockSpec((tm * ROW_SUB, 128), lambda i, be, va, nu: (i, 0)),
    )
    return pl.pallas_call(
        _gmm_kernel,
        grid_spec=grid_spec,
        out_shape=jax.ShapeDtypeStruct(xs.shape, F32),
        compiler_params=_params("arbitrary"),
        name="moe_grouped_swiglu",
    )(block_e, valid, n_used, xs, wg, wu, wd)


def _combine_kernel(dest_ref, x_ref, mod_ref, hp_ref, w3_ref, wg_ref, wu_ref, wd_ref, ys_ref, o_ref, yg_scr, sem):
    rows = hp_ref.shape[0] // ROW_SUB

    def issue(tok, carry):
        for k in range(TOP_K):
            _row_copy(ys_ref, dest_ref[0, k, tok], yg_scr.at[k], tok, sem).start(priority=k % 2)
        return carry

    lax.fori_loop(0, rows, issue, 0, unroll=2)
    h = _load_row_tiles(hp_ref, ROW_SUB).astype(BF16)
    a = _silu(_dot(h, wg_ref[...])) * _dot(h, wu_ref[...])
    acc = _dot(a.astype(BF16), wd_ref[...])

    def drain(j, carry):
        _row_copy(ys_ref, 0, yg_scr.at[0], 0, sem).wait()
        return carry

    lax.fori_loop(0, TOP_K * rows, drain, 0, unroll=16)
    w3 = w3_ref[...]
    for k in range(TOP_K):
        wk = w3[:, k:k + 1] + w3[:, TOP_K + k:TOP_K + k + 1] + w3[:, 2 * TOP_K + k:2 * TOP_K + k + 1]
        acc = acc + wk * _load_row_tiles(yg_scr.at[k], ROW_SUB)
    x = x_ref[...]
    o_ref[...] = x + mod_ref[:, 5:6, :] * acc.reshape(x.shape)


def _combine(dest3, x1, mod, hp, w3, wg, wu, wd, ys):
    n, t, d = x1.shape
    rt = _Rows(n, t)
    assert rt.rows & (rt.rows - 1) == 0
    return pl.pallas_call(
        _combine_kernel,
        grid=(rt.steps,),
        in_specs=[pl.BlockSpec((1, TOP_K, rt.rows), lambda i: (i, 0, 0), memory_space=pltpu.SMEM),
                  rt.seq_spec(t, d), rt.seq_spec(6, d), rt.tile_spec(ROW_SUB), rt.row_spec(128),
                  _const_spec(wg.shape), _const_spec(wu.shape), _const_spec(wd.shape),
                  pl.BlockSpec(memory_space=pl.ANY)],
        out_specs=rt.seq_spec(t, d),
        out_shape=jax.ShapeDtypeStruct((n, t, d), F32),
        scratch_shapes=[pltpu.VMEM((TOP_K, rt.rows * ROW_SUB, 128), F32), pltpu.SemaphoreType.DMA(())],
        compiler_params=_params("arbitrary"),
        name="moe_combine_shared",
    )(dest3, x1, mod, hp, w3, wg, wu, wd, ys)


def _moe(layer, x1, mod, hp, idx3, rank3, w3, counts, p):
    n, t, d = x1.shape
    m = n * t
    n_e = p['w_gate_e'].shape[1]
    a = m * TOP_K
    tm = 256 if a // n_e >= 512 else 128
    counts = counts[:, 0].astype(jnp.int32)
    padded = (counts + tm - 1) // tm * tm
    pend = jnp.cumsum(padded)
    pstart = pend - padded
    onehot = idx3[None] == jnp.arange(n_e, dtype=jnp.int32)[:, None, None, None]
    dest3 = rank3 + jnp.sum(jnp.where(onehot, pstart[:, None, None, None], 0), axis=0)
    n_blocks = a // tm + n_e
    first = jnp.arange(n_blocks, dtype=jnp.int32) * tm
    block_e = jnp.minimum(jnp.sum((pend[None, :] <= first[:, None]).astype(jnp.int32), axis=1), n_e - 1)
    mine = block_e[:, None] == jnp.arange(n_e, dtype=jnp.int32)[None, :]
    last = jnp.sum(jnp.where(mine, (pstart + counts)[None, :], 0), axis=1)
    valid = jnp.clip(last - first, 0, tm).astype(jnp.int32)
    n_used = (pend[-1] // tm).astype(jnp.int32).reshape(1)
    xs = _dispatch(dest3, hp, n_blocks * tm)
    ys = _gmm(layer, block_e, valid, n_used, xs, p['w_gate_e'], p['w_up_e'], p['w_down_e'], tm)
    return _combine(dest3, x1, mod, hp, w3, p['w_gate_s'][layer].astype(BF16), p['w_up_s'][layer].astype(BF16),
                    p['w_down_s'][layer].astype(BF16), ys)


def _odd_in_kernel(x_ref, mod_ref, g_ref, w_ref, gqa_ref, wqb_ref, gqn_ref, gqr_ref, gqrs_ref, gkva_ref,
                   gkr_ref, gkrs_ref, gkn_ref, cos_ref, sin_ref, seg_ref, wuk_ref, wuv_ref,
                   cq_ref, ck_ref, cv_ref, co_ref, gt_ref, ckv_ref, kr_ref, *outs, prompt):
    h = _norm_mod(x_ref, mod_ref, g_ref, 0, 1).astype(BF16)
    z = _dot(h, w_ref[...])
    cq_ref[...] = z[:, 0:512]
    ck_ref[...] = z[:, 512:1024]
    cv_ref[...] = z[:, 1024:1536]
    co_ref[...] = z[:, 1536:2048]
    gt_ref[...] = z[:, 2560:2688]
    cos = cos_ref[...]
    sin = sin_ref[...]
    ckv = _rms(z[:, 2304:2432]) * gkva_ref[...]
    ckv_ref[...] = ckv
    xr = z[:, 2432:2496]
    xr_rot = z[:, 2496:2560]
    rr = lax.rsqrt(jnp.mean(xr * xr, axis=-1, keepdims=True) + EPS)
    kr = rr * (xr * gkr_ref[...] * cos[:, 0:D_ROPE] + xr_rot * gkrs_ref[...] * sin[:, 0:D_ROPE])
    kr_ref[...] = kr
    qa = (_rms(z[:, 2048:2304]) * gqa_ref[...]).astype(BF16)
    qd = _dot(qa, wqb_ref[...])
    nw = D_HEADS * D_NOPE
    rw = D_HEADS * D_ROPE
    qn = [_rms(qd[:, hh * D_NOPE:(hh + 1) * D_NOPE]) * gqn_ref[...] * MLA_SCALE for hh in range(D_HEADS)]
    xq = qd[:, nw:nw + rw]
    xq_rot = qd[:, nw + rw:nw + 2 * rw]
    rq = lax.rsqrt(_dot_sel(xq * xq, seg_ref[...]) * (1.0 / D_ROPE) + EPS)
    qr = rq * (xq * gqr_ref[...] * cos + xq_rot * gqrs_ref[...] * sin) * MLA_SCALE
    ckv_b = ckv.astype(BF16)
    if prompt:
        qcat_ref, kcat_ref, v_ref = outs
        kn = _dot(ckv_b, wuk_ref[...])
        zpad = jnp.zeros((z.shape[0], 256 - D_NOPE - D_ROPE), F32)
        qcat_ref[...] = jnp.concatenate(
            [piece for hh in range(D_HEADS) for piece in (qn[hh], qr[:, hh * D_ROPE:(hh + 1) * D_ROPE], zpad)],
            axis=1).astype(BF16)
        kcat_ref[...] = jnp.concatenate(
            [piece for hh in range(D_HEADS)
             for piece in (_rms(kn[:, hh * D_NOPE:(hh + 1) * D_NOPE]) * gkn_ref[...], kr, zpad)],
            axis=1).astype(BF16)
        v_ref[...] = _dot(ckv_b, wuv_ref[...]).astype(BF16)
    else:
        u_ref, qr_ref = outs
        u_ref[...] = jnp.concatenate(
            [_dot_nt((qn[hh] * gkn_ref[...]).astype(BF16), wuk_ref[:, hh * D_NOPE:(hh + 1) * D_NOPE])
             for hh in range(D_HEADS)], axis=1).astype(BF16)
        qr_ref[...] = qr.astype(BF16)


def _odd_in(x, mod, g, w, consts, cos, sin, prompt):
    n, t, d = x.shape
    rt = _Rows(n, t)
    m = n * t
    widths = [(512, F32)] * 4 + [(128, F32), (128, F32), (D_ROPE, F32)]
    if prompt:
        widths += [(1024, BF16), (1024, BF16), (512, BF16)]
    else:
        widths += [(512, BF16), (256, BF16)]
    return pl.pallas_call(
        functools.partial(_odd_in_kernel, prompt=prompt),
        grid=(rt.steps,),
        in_specs=[rt.seq_spec(t, d), rt.seq_spec(6, d), _const_spec((1, d)), _const_spec(w.shape)]
                 + [_const_spec(c.shape) for c in consts[:9]]
                 + [rt.pos_spec(cos.shape[1]), rt.pos_spec(sin.shape[1])]
                 + [_const_spec(c.shape) for c in consts[9:]],
        out_specs=[rt.row_spec(c) for c, _ in widths],
        out_shape=[jax.ShapeDtypeStruct((m, c), dt) for c, dt in widths],
        compiler_params=_params("arbitrary"),
        name="odd_in_latent_prep",
    )(x, mod, g, w, *consts[:9], cos, sin, *consts[9:])


def _mlstm_kernel(q_ref, k_ref, v_ref, og_ref, gt_ref, gb_ref, ng_ref, c0_ref, n0_ref, m0_ref, tri_ref, eye_ref,
                  o_ref, co_ref, no_ref, mo_ref, c_scr, n_scr, m_scr, *, chunk):
    dh = q_ref.shape[1] // C_HEADS
    t = pl.program_id(1)

    @pl.when(t == 0)
    def _():
        c_scr[...] = c0_ref[0]
        n_scr[...] = n0_ref[0]
        m_scr[...] = m0_ref[0]

    pre = gt_ref[...] + gb_ref[...]
    lf = _log_sigmoid(pre)
    if chunk >= 16:
        l_hi, l_mid, l_lo = _split3(lf)
        f_cum = _dot(tri_ref[...], l_hi) + _dot(tri_ref[...], l_mid) + _dot(tri_ref[...], l_lo)
    else:
        rowg = lax.broadcasted_iota(jnp.int32, lf.shape, 0)
        f_cum = jnp.zeros_like(lf)
        for s in range(chunk):
            f_cum = f_cum + jnp.where(rowg >= s, lf[s:s + 1, :], 0.0)
    ri = lax.broadcasted_iota(jnp.int32, (chunk, chunk), 0)
    ci = lax.broadcasted_iota(jnp.int32, (chunk, chunk), 1)
    pad = 16 - chunk if chunk < 16 else 0

    def pad_rows(a):
        if pad:
            return jnp.concatenate([a, jnp.zeros((pad, a.shape[1]), a.dtype)], axis=0)
        return a

    for h in range(C_HEADS):
        sl = slice(h * dh, (h + 1) * dh)
        q = q_ref[:, sl]
        k = k_ref[:, sl] * (dh ** -0.5)
        v = v_ref[:, sl]
        qb, kb, vb = q.astype(BF16), k.astype(BF16), v.astype(BF16)
        f_col = f_cum[:, C_HEADS + h:C_HEADS + h + 1]
        i_col = pre[:, h:h + 1]
        m_prev = m_scr[h:h + 1, 0:1]
        a_row = jnp.sum(jnp.where(ri == ci, i_col - f_col, 0.0), axis=0, keepdims=True)
        log_d = jnp.where(ci <= ri, f_col + a_row, NEG)
        inter = f_col + m_prev
        m_t = jnp.maximum(inter, jnp.max(log_d, axis=-1, keepdims=True))
        w_inter = jnp.exp(inter - m_t)
        qk = _dot_nt(qb, kb) * jnp.exp(log_d - m_t)
        c_old = c_scr[h]
        n_old = n_scr[h:h + 1, :]
        num = _dot(qk.astype(BF16), vb) + w_inter * _dot_nt(qb, c_old.astype(BF16))
        den = jnp.sum(qk, axis=-1, keepdims=True) + w_inter * jnp.sum(q * n_old, axis=-1, keepdims=True)
        hh = num / jnp.maximum(jnp.abs(den), jnp.exp(-m_t))
        o_ref[:, sl] = _rms(hh) * ng_ref[...] * _sigmoid(og_ref[:, sl])
        f_last = f_col[chunk - 1:chunk, :]
        a_end = f_last - f_col + i_col
        m_new = jnp.maximum(f_last + m_prev, jnp.max(a_end, axis=0, keepdims=True))
        w = jnp.exp(a_end - m_new)
        dec = jnp.exp(f_last + m_prev - m_new)
        wv_t = _dot_nt(eye_ref[...], pad_rows((w * v).astype(BF16))).astype(BF16)
        c_scr[h] = dec * c_old + _dot(wv_t, pad_rows(kb))
        n_scr[h:h + 1, :] = dec * n_old + jnp.sum(w * k, axis=0, keepdims=True)
        m_scr[h:h + 1, :] = jnp.broadcast_to(m_new, (1, m_scr.shape[1]))

    @pl.when(t == pl.num_programs(1) - 1)
    def _():
        co_ref[0] = c_scr[...]
        no_ref[0] = n_scr[...]
        mo_ref[0] = m_scr[...]


def _mlstm(n, t, q, k, v, og, gt, gb, ng, c0, n0, m0, tri, eye):
    chunk = CHUNK_C if t % CHUNK_C == 0 else t
    nt = t // chunk
    w = q.shape[1]
    dh = w // C_HEADS

    def rows(c):
        return pl.BlockSpec((chunk, c), lambda i, j: (i * nt + j, 0))

    cst = pl.BlockSpec((1, C_HEADS, dh, dh), lambda i, j: (i, 0, 0, 0))
    nst = pl.BlockSpec((1, C_HEADS, dh), lambda i, j: (i, 0, 0))
    return pl.pallas_call(
        functools.partial(_mlstm_kernel, chunk=chunk),
        grid=(n, nt),
        in_specs=[rows(w), rows(w), rows(w), rows(w), rows(128),
                  pl.BlockSpec(gb.shape, lambda i, j: (0, 0)), pl.BlockSpec(ng.shape, lambda i, j: (0, 0)),
                  cst, nst, nst,
                  pl.BlockSpec(tri.shape, lambda i, j: (0, 0)), pl.BlockSpec(eye.shape, lambda i, j: (0, 0))],
        out_specs=[rows(w), cst, nst, nst],
        out_shape=[jax.ShapeDtypeStruct((n * t, w), F32), jax.ShapeDtypeStruct(c0.shape, F32),
                   jax.ShapeDtypeStruct(n0.shape, F32), jax.ShapeDtypeStruct(m0.shape, F32)],
        scratch_shapes=[pltpu.VMEM((C_HEADS, dh, dh), F32), pltpu.VMEM((C_HEADS, dh), F32),
                        pltpu.VMEM((C_HEADS, dh), F32)],
        compiler_params=_params("arbitrary", "arbitrary"),
        name="mlstm_scan",
    )(q, k, v, og, gt, gb, ng, c0, n0, m0, tri, eye)


def _flash_kernel(qi_ref, kj_ref, q_ref, k_ref, v_ref, o_ref, m_scr, l_scr, acc_scr):
    p = pl.program_id(2)
    qi = qi_ref[p]
    kj = kj_ref[p]

    @pl.when(kj == 0)
    def _():
        m_scr[...] = jnp.full_like(m_scr, NEG)
        l_scr[...] = jnp.zeros_like(l_scr)
        acc_scr[...] = jnp.zeros_like(acc_scr)

    def step(masked):
        s = _dot_nt(q_ref[...], k_ref[...])
        bq, bk = s.shape
        if masked:
            ri = lax.broadcasted_iota(jnp.int32, (bq, bk), 0)
            ci = lax.broadcasted_iota(jnp.int32, (bq, bk), 1)
            s = jnp.where(ci <= ri, s, NEG)
        lanes = m_scr.shape[1]
        m_old = m_scr[...]
        m_new = jnp.maximum(m_old, jnp.max(s, axis=-1, keepdims=True))
        alpha = jnp.exp(m_old - m_new)
        pr = jnp.exp(s - jnp.concatenate([m_new] * (bk // lanes), axis=1))
        l_scr[...] = alpha * l_scr[...] + jnp.sum(pr, axis=-1, keepdims=True)
        acc_scr[...] = alpha * acc_scr[...] + _dot(pr.astype(BF16), v_ref[...])
        m_scr[...] = m_new

    @pl.when(kj < qi)
    def _():
        step(False)

    @pl.when(kj == qi)
    def _():
        step(True)
        o_ref[...] = acc_scr[...] / l_scr[...]


def _flash(n, t, qcat, kcat, v):
    blk = FLASH_BLOCK if t % FLASH_BLOCK == 0 else t
    nq = t // blk
    pairs = [(i, j) for i in range(nq) for j in range(i + 1)]
    qi = jnp.asarray([a for a, _ in pairs], jnp.int32)
    kj = jnp.asarray([b for _, b in pairs], jnp.int32)
    dv = v.shape[1] // D_HEADS
    grid_spec = pltpu.PrefetchScalarGridSpec(
        num_scalar_prefetch=2,
        grid=(n, D_HEADS, len(pairs)),
        in_specs=[pl.BlockSpec((blk, 256), lambda b, h, p, qi, kj: (b * nq + qi[p], h)),
                  pl.BlockSpec((blk, 256), lambda b, h, p, qi, kj: (b * nq + kj[p], h)),
                  pl.BlockSpec((blk, dv), lambda b, h, p, qi, kj: (b * nq + kj[p], h))],
        out_specs=pl.BlockSpec((blk, dv), lambda b, h, p, qi, kj: (b * nq + qi[p], h)),
        scratch_shapes=[pltpu.VMEM((blk, dv), F32), pltpu.VMEM((blk, dv), F32), pltpu.VMEM((blk, dv), F32)],
    )
    assert blk % dv == 0
    return pl.pallas_call(
        _flash_kernel,
        grid_spec=grid_spec,
        out_shape=jax.ShapeDtypeStruct((n * t, v.shape[1]), F32),
        compiler_params=_params("arbitrary", "arbitrary", "arbitrary"),
        name="mla_prompt_flash",
    )(qi, kj, qcat, kcat, v)


def _paged_kernel(pt_ref, u_ref, qr_ref, cn_ref, kn_ref, wukt_ref, wuv_ref, ckv_hbm, krt_hbm,
                  o_ref, ck_buf, kr_buf, sem, m_scr, l_scr, acc_scr, *, pages, li, n_groups, n_steps):
    b = pl.program_id(0)
    g = pl.program_id(1)
    hq = u_ref.shape[1]
    tq = hq // D_HEADS
    step = b * n_groups + g
    slot = lax.rem(step, 2)

    def page_copies(seq, group, sl, lookup):
        copies = []
        for i in range(pages):
            page = pt_ref[seq, group * pages + i] if lookup else 0
            rows = pl.ds(i * PAGE_SIZE, PAGE_SIZE)
            copies.append(pltpu.make_async_copy(ckv_hbm.at[page, li], ck_buf.at[sl, rows, :], sem.at[sl]))
            copies.append(pltpu.make_async_copy(krt_hbm.at[page, li], kr_buf.at[sl, :, rows], sem.at[sl]))
        return copies

    @pl.when(step == 0)
    def _():
        for cp in page_copies(0, 0, 0, True):
            cp.start()

    @pl.when(step + 1 < n_steps)
    def _():
        nxt = step + 1
        for cp in page_copies(lax.div(nxt, n_groups), lax.rem(nxt, n_groups), 1 - slot, True):
            cp.start()

    for cp in page_copies(0, 0, slot, False):
        cp.wait()

    @pl.when(g == 0)
    def _():
        m_scr[...] = jnp.full_like(m_scr, NEG)
        l_scr[...] = jnp.zeros_like(l_scr)
        acc_scr[...] = jnp.zeros_like(acc_scr)

    lhs = jnp.concatenate([wukt_ref[...], u_ref[0]], axis=0)
    qr = qr_ref[0]
    nk = D_HEADS * D_NOPE

    def scores(ck_b, kr_b, kr_keys_minor=True):
        big = _dot_nt(lhs, ck_b)
        rows = []
        for h in range(D_HEADS):
            kn_t = big[h * D_NOPE:(h + 1) * D_NOPE, :]
            rinv = lax.rsqrt(jnp.sum(kn_t * kn_t, axis=0, keepdims=True) * (1.0 / D_NOPE) + EPS)
            rows.append(big[nk + h * tq:nk + (h + 1) * tq, :] * rinv)
        rope = _dot(qr, kr_b) if kr_keys_minor else _dot_nt(qr, kr_b)
        return jnp.concatenate(rows, axis=0) + rope

    def update(s, ck_b):
        m_old = m_scr[...]
        m_new = jnp.maximum(m_old, jnp.max(s, axis=-1, keepdims=True))
        alpha = jnp.exp(m_old - m_new)
        pr = jnp.exp(s - m_new)
        l_scr[...] = alpha * l_scr[...] + jnp.sum(pr, axis=-1, keepdims=True)
        acc_scr[...] = alpha * acc_scr[...] + _dot(pr.astype(BF16), ck_b)
        m_scr[...] = m_new

    span = 2 * PAGE_SIZE
    cks = [ck_buf[slot, pl.ds(i * span, span), :].astype(BF16) for i in range(pages // 2)]
    kr_all = kr_buf[slot].astype(BF16)
    s_all = jnp.concatenate(
        [scores(cks[i], kr_all[:, i * span:(i + 1) * span]) for i in range(pages // 2)], axis=1)
    update(s_all, jnp.concatenate(cks, axis=0))

    @pl.when(g == pl.num_programs(1) - 1)
    def _():
        fill = PAGE_SIZE - tq
        ck_b = jnp.concatenate([cn_ref[...], jnp.zeros((fill, cn_ref.shape[1]), F32)], axis=0).astype(BF16)
        kr_b = jnp.concatenate([kn_ref[...], jnp.zeros((fill, kn_ref.shape[1]), F32)], axis=0).astype(BF16)
        s = scores(ck_b, kr_b, kr_keys_minor=False)
        ri = lax.broadcasted_iota(jnp.int32, s.shape, 0)
        ci = lax.broadcasted_iota(jnp.int32, s.shape, 1)
        update(jnp.where(ci <= ri % tq, s, NEG), ck_b)
        lat = (acc_scr[...] / l_scr[...]).astype(BF16)
        full = _dot(lat, wuv_ref[...])
        dv = wuv_ref.shape[1] // D_HEADS
        o_ref[...] = jnp.concatenate(
            [full[h * tq:(h + 1) * tq, h * dv:(h + 1) * dv] for h in range(D_HEADS)], axis=1)


def _paged(page_table, li, u3, qr3, ckv_new, kr_new, wuk_t, wuv, cache_ckv, cache_kr):
    n, hq, lat = u3.shape
    tq = hq // D_HEADS
    n_pages = page_table.shape[1]
    pages = min(PAGES_PER_STEP, n_pages)
    assert n_pages % pages == 0 and pages % 2 == 0
    ng = n_pages // pages
    cache_kr_t = jnp.swapaxes(cache_kr, 2, 3)
    keys = pages * PAGE_SIZE

    grid_spec = pltpu.PrefetchScalarGridSpec(
        num_scalar_prefetch=1,
        grid=(n, ng),
        in_specs=[pl.BlockSpec((1, hq, lat), lambda b, g, pt: (b, 0, 0)),
                  pl.BlockSpec((1, hq, D_ROPE), lambda b, g, pt: (b, 0, 0)),
                  pl.BlockSpec((tq, lat), lambda b, g, pt: (b, 0)),
                  pl.BlockSpec((tq, D_ROPE), lambda b, g, pt: (b, 0)),
                  pl.BlockSpec(wuk_t.shape, lambda b, g, pt: (0, 0)),
                  pl.BlockSpec(wuv.shape, lambda b, g, pt: (0, 0)),
                  pl.BlockSpec(memory_space=pl.ANY), pl.BlockSpec(memory_space=pl.ANY)],
        out_specs=pl.BlockSpec((tq, wuv.shape[1]), lambda b, g, pt: (b, 0)),
        scratch_shapes=[pltpu.VMEM((2, keys, lat), F32), pltpu.VMEM((2, D_ROPE, keys), F32),
                        pltpu.SemaphoreType.DMA((2,)),
                        pltpu.VMEM((hq, 1), F32), pltpu.VMEM((hq, 1), F32), pltpu.VMEM((hq, lat), F32)],
    )
    return pl.pallas_call(
        functools.partial(_paged_kernel, pages=pages, li=li, n_groups=ng, n_steps=n * ng),
        grid_spec=grid_spec,
        out_shape=jax.ShapeDtypeStruct((n * tq, wuv.shape[1]), F32),
        compiler_params=_params("arbitrary", "arbitrary"),
        name="mla_sample_paged",
    )(page_table, u3, qr3, ckv_new, kr_new, wuk_t, wuv, cache_ckv, cache_kr_t)


def _np_seg(n_seg, seg_in, seg_out):
    mat = np.zeros((n_seg * seg_in, n_seg * seg_out), np.float32)
    for s in range(n_seg):
        mat[s * seg_in:(s + 1) * seg_in, s * seg_out:(s + 1) * seg_out] = 1.0
    return mat


def _rot_cols(w):
    half = w.shape[-1] // 2
    return jnp.concatenate([-w[..., half:], w[..., :half]], axis=-1)


def _swap_halves(g):
    half = g.shape[-1] // 2
    return jnp.concatenate([g[..., half:], g[..., :half]], axis=-1)


def _rope_tables(pos, reps):
    half = D_ROPE // 2
    inv = ROPE_BASE ** (-jnp.arange(half, dtype=F32) / half)
    ang = pos.astype(F32)[:, None] * inv
    cos = jnp.concatenate([jnp.cos(ang), jnp.cos(ang)], axis=-1)
    sin = jnp.concatenate([jnp.sin(ang), jnp.sin(ang)], axis=-1)
    return jnp.tile(cos, (1, reps)), jnp.tile(sin, (1, reps))


def _trunk(x, c_mod, pos0, gla0, mc0, mn0, mm0, p, sample_ctx):
    n, t, d = x.shape
    m = n * t
    rt = _Rows(n, t)
    eye_r = jnp.eye(rt.rows, dtype=BF16)
    triu_r = jnp.asarray(np.triu(np.ones((rt.rows, rt.rows), np.float32)), BF16)
    dh = mc0.shape[3]
    eye_hk = jnp.eye(B_HEADS * gla0.shape[3], dtype=BF16)
    eye_dh = jnp.eye(dh, dtype=BF16)
    results = {}

    layer, li = 0, 0
    mod = c_mod[layer]
    w_in = p['w_in_even'][li]
    w_in = jnp.concatenate([w_in, jnp.zeros((d, 128 - B_GATE_RANK), F32)], axis=1).astype(BF16)
    a_ws = jnp.tril(p['a_ws'][li])
    a_bs = p['a_bs'][li]
    if t % CHUNK_A == 0:
        ws = a_ws
        bs = jnp.repeat(a_bs.T, CHUNK_A, axis=1)
    else:
        ws = jnp.stack([jnp.kron(jnp.eye(CHUNK_A // t, dtype=F32), a_ws[g, :t, :t]) for g in range(A_GROUPS)])
        bs = jnp.repeat(jnp.tile(a_bs[:, :t], (1, CHUNK_A // t)).T, CHUNK_A, axis=1)
    out_a, v_rows, bq, bk, bv, br, bg = _even_in(
        x, mod, p['norm_mix_g'][layer].reshape(1, d), w_in, p['a_norm_g'][li].reshape(1, -1),
        ws.astype(BF16), bs)
    hk = bq.shape[1]
    dv = bv.shape[1] // B_HEADS
    wg2 = jnp.concatenate([p['b_w_gate2'][li], jnp.zeros((128 - B_GATE_RANK, hk), F32)], axis=0)
    seg = jnp.asarray(_np_seg(B_HEADS, hk // B_HEADS, dv), BF16)
    out_b, s_new = _gla(n, t, bq, bk, bv, br, bg, wg2, p['b_gate_bias'][li].reshape(1, hk),
                        jnp.tile(p['b_norm_g'][li], B_HEADS).reshape(1, -1), gla0[li], seg, eye_hk)
    results['gla'] = s_new
    results['v_rows'] = v_rows
    routed = _out_route(
        x, mod, out_a, out_b, p['w_out'][layer].astype(BF16), p['norm_ffn_g'][layer].reshape(1, d),
        p['w_router'][layer].T, p['b_router'][layer].reshape(-1, 1), eye_r, triu_r)
    x = _moe(layer, routed[0], mod, *routed[1:], p)

    layer, li = 1, 0
    mod = c_mod[layer]
    w = p['w_in_odd'][li]
    hw = C_HEADS * dh
    o_g = 3 * hw
    o_o = o_g + 2 * C_HEADS
    o_qa = o_o + hw
    o_kva = o_qa + p['d_g_qa'].shape[1]
    o_kr = o_kva + p['d_g_kva'].shape[1]
    w_kr = w[:, o_kr:o_kr + D_ROPE]
    w_odd = jnp.concatenate(
        [w[:, :o_g], w[:, o_o:o_qa], w[:, o_qa:o_kva], w[:, o_kva:o_kr], w_kr, _rot_cols(w_kr),
         w[:, o_g:o_o], jnp.zeros((d, 128 - 2 * C_HEADS), F32)], axis=1).astype(BF16)
    wqb = p['d_w_qb'][li].reshape(-1, D_HEADS, D_NOPE + D_ROPE)
    wqb_r = wqb[:, :, D_NOPE:]
    wqb2 = jnp.concatenate([wqb[:, :, :D_NOPE].reshape(-1, D_HEADS * D_NOPE),
                            wqb_r.reshape(-1, D_HEADS * D_ROPE),
                            _rot_cols(wqb_r).reshape(-1, D_HEADS * D_ROPE)], axis=1).astype(BF16)
    w_uk = p['d_w_uk'][li]
    lat = w_uk.shape[0]
    g_qr = p['d_g_qr'][li]
    g_kr = p['d_g_kr'][li]
    cos, sin = _rope_tables(pos0 + jnp.arange(t), D_HEADS)
    prompt = sample_ctx is None
    if not prompt:
        cos, sin = jnp.tile(cos, (rt.s, 1)), jnp.tile(sin, (rt.s, 1))
    consts = [p['d_g_qa'][li].reshape(1, -1), wqb2, p['d_g_qn'][li].reshape(1, -1),
              jnp.tile(g_qr, D_HEADS).reshape(1, -1), jnp.tile(_swap_halves(g_qr), D_HEADS).reshape(1, -1),
              p['d_g_kva'][li].reshape(1, -1), g_kr.reshape(1, -1), _swap_halves(g_kr).reshape(1, -1),
              p['d_g_kn'][li].reshape(1, -1),
              jnp.asarray(_np_seg(D_HEADS, D_ROPE, D_ROPE), BF16),
              w_uk.reshape(lat, -1).astype(BF16), p['d_w_uv'][li].reshape(lat, -1).astype(BF16)]
    outs = _odd_in(x, mod, p['norm_mix_g'][layer].reshape(1, d), w_odd, consts, cos, sin, prompt)
    cq, ck, cv, co, gates, ckv, kr = outs[:7]
    gb = jnp.concatenate([p['c_ig_bias'][li], p['c_fg_bias'][li],
                          jnp.zeros((128 - 2 * C_HEADS,), F32)]).reshape(1, 128)
    chunk = CHUNK_C if t % CHUNK_C == 0 else t
    tri = jnp.asarray(np.tril(np.ones((chunk, chunk), np.float32)), BF16)
    m0b =jnp.broadcast_to(mm0[li][:, :, None], (n, C_HEADS, dh))
    out_c, c_new, n_new, m_new = _mlstm(n, t, cq, ck, cv, co, gates, gb,
                                        p['c_norm_g'][li].reshape(1, -1), mc0[li], mn0[li], m0b, tri, eye_dh)
    results['mlstm'] = (c_new, n_new, m_new[:, :, 0])
    results['ckv'] = ckv.reshape(n, t, -1)
    results['kr'] = kr.reshape(n, t, -1)
    if prompt:
        qcat, kcat, vv = outs[7:]
        out_d = _flash(n, t, qcat, kcat, vv)
    else:
        u, qr = outs[7:]
        cache_ckv, cache_kr, page_table = sample_ctx
        u3 = u.reshape(n, t, D_HEADS, lat).transpose(0, 2, 1, 3).reshape(n, D_HEADS * t, lat)
        qr3 = qr.reshape(n, t, D_HEADS, D_ROPE).transpose(0, 2, 1, 3).reshape(n, D_HEADS * t, D_ROPE)
        wuk_t = w_uk.transpose(1, 2, 0).reshape(-1, lat).astype(BF16)
        out_d = _paged(page_table, li, u3, qr3, ckv, kr, wuk_t, consts[11], cache_ckv, cache_kr)
    routed = _out_route(
        x, mod, out_c, out_d, p['w_out'][layer].astype(BF16), p['norm_ffn_g'][layer].reshape(1, d),
        p['w_router'][layer].T, p['b_router'][layer].reshape(-1, 1), eye_r, triu_r)
    x = _moe(layer, routed[0], mod, *routed[1:], p)
    return x, results


def kernel(x_prompt, x_sample, state_gla, state_mlstm_c, state_mlstm_n, state_mlstm_m,
           cache_ckv, cache_krope, page_table, c_prompt, c_sample,
           norm_mix_g, norm_ffn_g, w_ada, b_ada, w_out,
           w_in_even, a_norm_g, a_ws, a_bs, b_w_gate2, b_gate_bias, b_norm_g,
           w_in_odd, c_ig_bias, c_fg_bias, c_norm_g,
           d_g_qa, d_w_qb, d_g_kva, d_g_qn, d_g_qr, d_g_kr, d_g_kn, d_w_uk, d_w_uv,
           w_router, b_router, w_gate_e, w_up_e, w_down_e, w_gate_s, w_up_s, w_down_s):
    p = dict(norm_mix_g=norm_mix_g, norm_ffn_g=norm_ffn_g, w_out=w_out,
             w_in_even=w_in_even, a_norm_g=a_norm_g, a_ws=a_ws, a_bs=a_bs, b_w_gate2=b_w_gate2,
             b_gate_bias=b_gate_bias, b_norm_g=b_norm_g, w_in_odd=w_in_odd, c_ig_bias=c_ig_bias,
             c_fg_bias=c_fg_bias, c_norm_g=c_norm_g, d_g_qa=d_g_qa, d_w_qb=d_w_qb, d_g_kva=d_g_kva,
             d_g_qn=d_g_qn, d_g_qr=d_g_qr, d_g_kr=d_g_kr, d_g_kn=d_g_kn, d_w_uk=d_w_uk, d_w_uv=d_w_uv,
             w_router=w_router, b_router=b_router, w_gate_e=w_gate_e, w_up_e=w_up_e, w_down_e=w_down_e,
             w_gate_s=w_gate_s, w_up_s=w_up_s, w_down_s=w_down_s)
    n_p, t_p, d = x_prompt.shape
    n_s, t_s, _ = x_sample.shape
    depth = w_ada.shape[0]
    pad_p = (-n_p) % 8
    c_all = jnp.concatenate([c_prompt, jnp.zeros((pad_p, d), F32), c_sample], axis=0)
    mod_all = _ada(c_all, w_ada, b_ada).reshape(depth, c_all.shape[0], 6, d)
    mod_p = mod_all[:, :n_p]
    mod_s = mod_all[:, n_p + pad_p:]

    n_even, _, bh, bdk, bdv = state_gla.shape
    n_odd, _, chh, cdh, _ = state_mlstm_c.shape
    gla0_p = jnp.zeros((n_even, n_p, bh, bdk, bdv), F32)
    mc0_p = jnp.zeros((n_odd, n_p, chh, cdh, cdh), F32)
    mn0_p = jnp.zeros((n_odd, n_p, chh, cdh), F32)
    mm0_p = jnp.full((n_odd, n_p, chh), NEG, F32)
    past_len = page_table.shape[1] * PAGE_SIZE

    y_p, rp = _trunk(x_prompt, mod_p, 0, gla0_p, mc0_p, mn0_p, mm0_p, p, None)
    y_s, rs = _trunk(x_sample, mod_s, past_len, state_gla, state_mlstm_c, state_mlstm_n, state_mlstm_m, p,
                     (cache_ckv, cache_krope, page_table))
    aw = rs['v_rows'].shape[1]
    return (y_p, y_s, rp['gla'][None], rs['gla'][None], rs['v_rows'].reshape(1, n_s, t_s, aw),
            rp['mlstm'][0][None], rs['mlstm'][0][None], rp['mlstm'][1][None], rs['mlstm'][1][None],
            rp['mlstm'][2][None], rs['mlstm'][2][None],
            rp['ckv'][:, None], rs['ckv'][:, None], rp['kr'][:, None], rs['kr'][:, None])
```

```python
import functools

import numpy as np
import jax
import jax.numpy as jnp
from jax import lax
from jax.experimental import pallas as pl
from jax.experimental.pallas import tpu as pltpu

F32 = jnp.float32
BF16 = jnp.bfloat16

EPS = 1e-6
NEG = -1e30

A_GROUPS = 4
CHUNK_A = 128
B_HEADS = 4
B_GATE_RANK = 16
B_GATE_TAU = 16.0
CHUNK_B = 16
C_HEADS = 4
CHUNK_C = 128
D_HEADS = 4
D_NOPE = 128
D_ROPE = 64
ROPE_BASE = 10000.0
MLA_SCALE = (D_NOPE + D_ROPE) ** -0.5
PAGE_SIZE = 128
N_EXPERT_GROUPS = 8
TOPK_GROUPS = 4
TOP_K = 8
ROUTED_SCALE = 2.5

ROW_TILE = 256
SAMPLE_SEQS = 32
FLASH_BLOCK = 512
PAGES_PER_STEP = 32
VMEM_LIMIT = 56 * 1024 * 1024


def _dot(a, b):
    return jnp.dot(a, b, preferred_element_type=F32)


def _dot_nt(a, b):
    return lax.dot_general(a, b, (((1,), (1,)), ((), ())), preferred_element_type=F32)


def _split2(x):
    hi = x.astype(BF16)
    lo = (x - hi.astype(F32)).astype(BF16)
    return hi, lo


def _split3(x):
    hi = x.astype(BF16)
    r = x - hi.astype(F32)
    mid = r.astype(BF16)
    lo = (r - mid.astype(F32)).astype(BF16)
    return hi, mid, lo


def _dot3(a, b):
    ah, al = _split2(a)
    bh, bl = _split2(b)
    return _dot(ah, bh) + _dot(ah, bl) + _dot(al, bh)


def _dot3_nt(a, b):
    ah, al = _split2(a)
    bh, bl = _split2(b)
    return _dot_nt(ah, bh) + _dot_nt(ah, bl) + _dot_nt(al, bh)


def _dot_sel(x, m01, parts=2):
    ps = _split2(x) if parts == 2 else _split3(x)
    acc = _dot(ps[0], m01)
    for p in ps[1:]:
        acc = acc + _dot(p, m01)
    return acc


def _sigmoid(x):
    return 1.0 / (1.0 + jnp.exp(-x))


def _silu(x):
    return x * _sigmoid(x)


def _log_sigmoid(x):
    return jnp.minimum(x, 0.0) - jnp.log(1.0 + jnp.exp(-jnp.abs(x)))


def _gelu(x):
    return 0.5 * x * (1.0 + jnp.tanh(0.7978845608028654 * (x + 0.044715 * x * x * x)))


def _rms(x, eps=EPS):
    return x * lax.rsqrt(jnp.mean(x * x, axis=-1, keepdims=True) + eps)


def _params(*sem):
    return pltpu.CompilerParams(dimension_semantics=sem, vmem_limit_bytes=VMEM_LIMIT)


def _const_spec(shape):
    nd = len(shape)
    return pl.BlockSpec(shape, lambda *_: (0,) * nd)


class _Rows:
    def __init__(self, n, t):
        self.n, self.t = n, t
        if t % ROW_TILE == 0:
            self.s, self.r = 1, ROW_TILE
            self.tpb = t // ROW_TILE
        else:
            assert t == 8 and n % SAMPLE_SEQS == 0, (n, t)
            self.s, self.r = SAMPLE_SEQS, t
            self.tpb = 1
        self.rows = self.s * self.r
        self.steps = n * t // self.rows

    def seq_spec(self, mid, d):
        s, tpb = self.s, self.tpb
        if mid == self.t:
            return pl.BlockSpec((s, self.r, d), lambda i: (i // tpb, i % tpb, 0))
        return pl.BlockSpec((s, mid, d), lambda i: (i // tpb, 0, 0))

    def row_spec(self, c):
        return pl.BlockSpec((self.rows, c), lambda i: (i, 0))

    def tile_spec(self, sub):
        return pl.BlockSpec((self.rows * sub, 128), lambda i: (i, 0))

    def pos_spec(self, c):
        tpb = self.tpb
        return pl.BlockSpec((self.rows, c), lambda i: (i % tpb, 0))


def _ada_kernel(c_ref, w_ref, b_ref, o_ref):
    o_ref[0] = _dot3(_silu(c_ref[...]), w_ref[0]) + b_ref[0]


def _ada(c_all, w_ada, b_ada):
    depth, d, d6 = w_ada.shape
    nc = c_all.shape[0]
    tn = 1536
    return pl.pallas_call(
        _ada_kernel,
        grid=(depth, d6 // tn),
        in_specs=[pl.BlockSpec((nc, d), lambda l, j: (0, 0)),
                  pl.BlockSpec((1, d, tn), lambda l, j: (l, 0, j)),
                  pl.BlockSpec((1, 1, tn), lambda l, j: (l, 0, j))],
        out_specs=pl.BlockSpec((1, nc, tn), lambda l, j: (l, 0, j)),
        out_shape=jax.ShapeDtypeStruct((depth, nc, d6), F32),
        compiler_params=_params("arbitrary", "arbitrary"),
        name="ada_modulation",
    )(c_all, w_ada, b_ada.reshape(depth, 1, d6))


def _norm_mod(x_ref, mod_ref, g_ref, shift_row, scale_row):
    x = x_ref[...]
    h = _rms(x) * g_ref[...]
    h = h * (1.0 + mod_ref[:, scale_row:scale_row + 1, :]) + mod_ref[:, shift_row:shift_row + 1, :]
    s, r, d = x.shape
    return h.reshape(s * r, d)


def _even_in_kernel(x_ref, mod_ref, g_ref, w_ref, an_ref, ws_ref, bs_ref,
                    oa_ref, v_ref, q_ref, k_ref, bv_ref, r_ref, bg_ref):
    h = _norm_mod(x_ref, mod_ref, g_ref, 0, 1).astype(BF16)
    z = _dot(h, w_ref[...])
    aw = A_GROUPS * CHUNK_A
    rows = z.shape[0]
    u = _gelu(z[:, 0:aw])
    va = _gelu(z[:, aw:2 * aw])
    vn = jnp.concatenate(
        [_rms(va[:, g * CHUNK_A:(g + 1) * CHUNK_A]) for g in range(A_GROUPS)], axis=1) * an_ref[...]
    v_ref[...] = vn
    vb = vn.astype(BF16)
    for c in range(rows // CHUNK_A):
        r0 = c * CHUNK_A
        mixed = jnp.concatenate(
            [_dot(ws_ref[g], vb[r0:r0 + CHUNK_A, g * CHUNK_A:(g + 1) * CHUNK_A]) for g in range(A_GROUPS)],
            axis=1) + bs_ref[...]
        oa_ref[r0:r0 + CHUNK_A, :] = u[r0:r0 + CHUNK_A, :] * mixed
    o = 2 * aw
    q_ref[...] = z[:, o:o + 256]
    k_ref[...] = z[:, o + 256:o + 512]
    bv_ref[...] = z[:, o + 512:o + 1024]
    r_ref[...] = z[:, o + 1024:o + 1536]
    bg_ref[...] = z[:, o + 1536:o + 1664]


def _even_in(x, mod, g, w, an, ws, bs):
    n, t, d = x.shape
    rt = _Rows(n, t)
    m = n * t
    widths = (512, 512, 256, 256, 512, 512, 128)
    return pl.pallas_call(
        _even_in_kernel,
        grid=(rt.steps,),
        in_specs=[rt.seq_spec(t, d), rt.seq_spec(6, d), _const_spec((1, d)), _const_spec(w.shape),
                  _const_spec(an.shape), _const_spec(ws.shape), _const_spec(bs.shape)],
        out_specs=[rt.row_spec(c) for c in widths],
        out_shape=[jax.ShapeDtypeStruct((m, c), F32) for c in widths],
        compiler_params=_params("arbitrary"),
        name="even_in_chunk_gate",
    )(x, mod, g, w, an, ws, bs)


def _gla_kernel(q_ref, k_ref, v_ref, r_ref, g_ref, wg_ref, gb_ref, ng_ref, s0_ref, seg_ref, eye_ref,
                o_ref, so_ref, s_scr, lg_scr, o_scr, *, chunk, n_chunks):
    seqs, _, hk = q_ref.shape
    dk = hk // B_HEADS
    dv = v_ref.shape[2] // B_HEADS
    t = pl.program_id(1)

    @pl.when(t == 0)
    def _():
        for sq in range(seqs):
            s_scr[sq] = s0_ref[sq].reshape(hk, dv)

    for sq in range(seqs):
        lg_scr[sq] = _log_sigmoid(_dot3(g_ref[sq], wg_ref[...]) + gb_ref[...]) * (1.0 / B_GATE_TAU)
    row = lax.broadcasted_iota(jnp.int32, (chunk, hk), 0)
    lane = lax.broadcasted_iota(jnp.int32, (chunk, hk), 1)
    pad = 16 - chunk if chunk < 16 else 0

    def pad_rows(a):
        if pad:
            return jnp.concatenate([a, jnp.zeros((pad, a.shape[1]), a.dtype)], axis=0)
        return a

    def one_chunk(sq, r0):
        lg = lg_scr[sq, pl.ds(r0, chunk), :]
        q = q_ref[sq, pl.ds(r0, chunk), :] * (dk ** -0.5)
        k = k_ref[sq, pl.ds(r0, chunk), :]
        v = v_ref[sq, pl.ds(r0, chunk), :]
        b = jnp.zeros_like(lg)
        for s in range(chunk):
            b = b + jnp.where(row >= s, lg[s:s + 1, :], 0.0)
        prods = []
        for l in range(chunk):
            e = jnp.exp(jnp.where(row <= l, b[l:l + 1, :] - b, NEG))
            prods.append(q[l:l + 1, :] * k * e)
        att = _dot_sel(jnp.concatenate(prods, axis=0), seg_ref[...])
        o_intra = jnp.concatenate(
            [jnp.sum(att[l * chunk:(l + 1) * chunk, :] * v, axis=0, keepdims=True) for l in range(chunk)], axis=0)
        s_old = s_scr[sq]
        s_b = s_old.astype(BF16)
        qe = q * jnp.exp(b)
        o_inter = jnp.concatenate(
            [_dot(jnp.where(lane // dk == h, qe, 0.0).astype(BF16), s_b) for h in range(B_HEADS)], axis=1)
        o_scr[sq, pl.ds(r0, chunk), :] = o_intra + o_inter
        b_last = b[chunk - 1:chunk, :]
        k_end = pad_rows((k * jnp.exp(b_last - b)).astype(BF16))
        k_t = _dot_nt(eye_ref[...], k_end).astype(BF16)
        dh, dm, dl = _split3(jnp.exp(b_last))
        dec3 = _dot_nt(eye_ref[...], jnp.concatenate([dh, dm, dl, jnp.zeros((13, hk), BF16)], axis=0))
        dec = dec3[:, 0:1] + dec3[:, 1:2] + dec3[:, 2:3]
        v_b = pad_rows(v.astype(BF16))
        kv = jnp.concatenate(
            [_dot(k_t[h * dk:(h + 1) * dk, :], v_b[:, h * dv:(h + 1) * dv]) for h in range(B_HEADS)], axis=0)
        s_scr[sq] = dec * s_old + kv

    def body(c, carry):
        r0 = pl.multiple_of(c * chunk, chunk)
        for sq in range(seqs):
            one_chunk(sq, r0)
        return carry

    lax.fori_loop(0, n_chunks, body, 0)

    for sq in range(seqs):
        o = o_scr[sq]
        on = jnp.concatenate([_rms(o[:, h * dv:(h + 1) * dv]) for h in range(B_HEADS)], axis=1) * ng_ref[...]
        o_ref[sq] = on * _silu(r_ref[sq])

    @pl.when(t == pl.num_programs(1) - 1)
    def _():
        for sq in range(seqs):
            so_ref[sq] = s_scr[sq].reshape(B_HEADS, dk, dv)


GLA_SEQS = 2


def _gla(n, t, q, k, v, r, g, wg, gb, ng, s0, seg, eye):
    tt = ROW_TILE if t % ROW_TILE == 0 else t
    chunk = CHUNK_B if tt % CHUNK_B == 0 else tt
    nt = t // tt
    hk, hv = q.shape[1], v.shape[1]
    dk, dv = hk // B_HEADS, hv // B_HEADS
    seqs = GLA_SEQS
    assert n % seqs == 0

    def rows(c):
        return pl.BlockSpec((seqs, tt, c), lambda i, j: (i, j, 0))

    def per_seq(a):
        return a.reshape(n, t, a.shape[1])

    st = pl.BlockSpec((seqs, B_HEADS, dk, dv), lambda i, j: (i, 0, 0, 0))
    o, s_new = pl.pallas_call(
        functools.partial(_gla_kernel, chunk=chunk, n_chunks=tt // chunk),
        grid=(n // seqs, nt),
        in_specs=[rows(hk), rows(hk), rows(hv), rows(hv), rows(g.shape[1]),
                  pl.BlockSpec(wg.shape, lambda i, j: (0, 0)), pl.BlockSpec(gb.shape, lambda i, j: (0, 0)),
                  pl.BlockSpec(ng.shape, lambda i, j: (0, 0)), st,
                  pl.BlockSpec(seg.shape, lambda i, j: (0, 0)), pl.BlockSpec(eye.shape, lambda i, j: (0, 0))],
        out_specs=[rows(hv), st],
        out_shape=[jax.ShapeDtypeStruct((n, t, hv), F32), jax.ShapeDtypeStruct(s0.shape, F32)],
        scratch_shapes=[pltpu.VMEM((seqs, hk, dv), F32), pltpu.VMEM((seqs, tt, hk), F32),
                        pltpu.VMEM((seqs, tt, hv), F32)],
        compiler_params=_params("arbitrary", "arbitrary"),
        name="gla_scan",
    )(per_seq(q), per_seq(k), per_seq(v), per_seq(r), per_seq(g), wg, gb, ng, s0, seg, eye)
    return o.reshape(n * t, hv), s_new


def _route(logits_t, bias_col):
    n_e, r = logits_t.shape
    per = n_e // N_EXPERT_GROUPS
    scores = _sigmoid(logits_t)
    biased = scores + bias_col
    sub = lax.broadcasted_iota(jnp.int32, (per, r), 0).astype(F32)
    ninf = -jnp.inf
    gs = []
    for g in range(N_EXPERT_GROUPS):
        blk = biased[g * per:(g + 1) * per, :]
        m1 = jnp.max(blk, axis=0, keepdims=True)
        i1 = jnp.min(jnp.where(blk == m1, sub, float(per)), axis=0, keepdims=True)
        m2 = jnp.max(jnp.where(sub == i1, ninf, blk), axis=0, keepdims=True)
        gs.append(m1 + m2)
    cur = jnp.concatenate(gs, axis=0)
    gsub = lax.broadcasted_iota(jnp.int32, (N_EXPERT_GROUPS, r), 0).astype(F32)
    gsel = jnp.zeros((N_EXPERT_GROUPS, r), F32)
    for _ in range(TOPK_GROUPS):
        m = jnp.max(cur, axis=0, keepdims=True)
        i = jnp.min(jnp.where(cur == m, gsub, float(N_EXPERT_GROUPS)), axis=0, keepdims=True)
        hit = gsub == i
        gsel = jnp.where(hit, 1.0, gsel)
        cur = jnp.where(hit, ninf, cur)
    cur = jnp.concatenate(
        [jnp.where(gsel[g:g + 1, :] > 0.5, biased[g * per:(g + 1) * per, :], ninf) for g in range(N_EXPERT_GROUPS)],
        axis=0)
    esub = lax.broadcasted_iota(jnp.int32, (n_e, r), 0).astype(F32)
    idx, wts, hits = [], [], []
    for _ in range(TOP_K):
        m = jnp.max(cur, axis=0, keepdims=True)
        i = jnp.min(jnp.where(cur == m, esub, float(n_e)), axis=0, keepdims=True)
        hit = esub == i
        idx.append(i)
        hits.append(hit)
        wts.append(jnp.sum(jnp.where(hit, scores, 0.0), axis=0, keepdims=True))
        cur = jnp.where(hit, ninf, cur)
    w = jnp.concatenate(wts, axis=0)
    w = w / jnp.sum(w, axis=0, keepdims=True) * ROUTED_SCALE
    return jnp.concatenate(idx, axis=0), w, hits


def _pack_pairs(x):
    w = x.shape[1] // 2
    hi = lax.bitcast_convert_type(x[:, :w].astype(BF16).astype(F32), jnp.uint32)
    lo = lax.bitcast_convert_type(x[:, w:].astype(BF16).astype(F32), jnp.uint32)
    return hi | (lo >> 16)


def _unpack_pairs(pk):
    a = lax.bitcast_convert_type(pk & jnp.uint32(0xFFFF0000), F32)
    b = lax.bitcast_convert_type(pk << 16, F32)
    return jnp.concatenate([a, b], axis=1)


def _store_row_tiles(ref, x):
    rows = x.shape[0]
    pk = _pack_pairs(x)
    sub = pk.shape[1] // 128
    for c in range(sub):
        ref[pl.ds(c, rows, stride=sub), :] = pk[:, c * 128:(c + 1) * 128]


def _load_row_tiles(ref, sub):
    rows = ref.shape[0] // sub
    return _unpack_pairs(jnp.concatenate([ref[pl.ds(c, rows, stride=sub), :] for c in range(sub)], axis=1))


def _out_kernel(x_ref, mod_ref, ma_ref, mb_ref, wo_ref, g_ref, wr_ref, br_ref, eye_ref, triu_ref,
                x1_ref, hp_ref, idx_ref, rank_ref, w3_ref, cnt_ref, run_scr):
    step = pl.program_id(0)

    @pl.when(step == 0)
    def _():
        run_scr[...] = jnp.zeros_like(run_scr)

    half = ma_ref.shape[1]
    y = _dot(ma_ref[...].astype(BF16), wo_ref[0:half, :]) + _dot(mb_ref[...].astype(BF16), wo_ref[half:, :])
    x = x_ref[...]
    s, r, d = x.shape
    x1 = x + mod_ref[:, 2:3, :] * y.reshape(s, r, d)
    x1_ref[...] = x1
    h2 = (_rms(x1) * g_ref[...] * (1.0 + mod_ref[:, 4:5, :]) + mod_ref[:, 3:4, :]).reshape(s * r, d)
    _store_row_tiles(hp_ref, h2)
    idx, w, hits = _route(_dot3_nt(wr_ref[...], h2), br_ref[...])
    idx_ref[0] = idx.astype(jnp.int32)
    sel = jnp.zeros(hits[0].shape, F32)
    for hit in hits:
        sel = jnp.where(hit, 1.0, sel)
    before = run_scr[:, 0:1] + _dot(sel.astype(BF16), triu_ref[...]) - sel
    rank_ref[0] = jnp.concatenate(
        [jnp.sum(jnp.where(hit, before, 0.0), axis=0, keepdims=True) for hit in hits], axis=0).astype(jnp.int32)
    run_scr[...] = run_scr[...] + jnp.sum(sel, axis=1, keepdims=True)
    cnt_ref[...] = run_scr[...]
    wh, wm, wl = _split3(w)
    stack = jnp.concatenate([wh, wm, wl, jnp.zeros((128 - 3 * TOP_K, s * r), BF16)], axis=0)
    w3_ref[...] = _dot_nt(eye_ref[...], stack)


def _out_route(x, mod, mix_a, mix_b, wo, g, wr_t, br, eye, triu):
    n, t, d = x.shape
    rt = _Rows(n, t)
    m = n * t
    n_e = wr_t.shape[0]
    slot = pl.BlockSpec((1, TOP_K, rt.rows), lambda i: (i, 0, 0))
    return pl.pallas_call(
        _out_kernel,
        grid=(rt.steps,),
        in_specs=[rt.seq_spec(t, d), rt.seq_spec(6, d), rt.row_spec(mix_a.shape[1]), rt.row_spec(mix_b.shape[1]),
                  _const_spec(wo.shape), _const_spec((1, d)), _const_spec(wr_t.shape), _const_spec(br.shape),
                  _const_spec(eye.shape), _const_spec(triu.shape)],
        out_specs=[rt.seq_spec(t, d), rt.tile_spec(d // 256), slot, slot, rt.row_spec(128), _const_spec((n_e, 128))],
        out_shape=[jax.ShapeDtypeStruct((n, t, d), F32), jax.ShapeDtypeStruct((m * (d // 256), 128), jnp.uint32),
                   jax.ShapeDtypeStruct((rt.steps, TOP_K, rt.rows), jnp.int32),
                   jax.ShapeDtypeStruct((rt.steps, TOP_K, rt.rows), jnp.int32),
                   jax.ShapeDtypeStruct((m, 128), F32), jax.ShapeDtypeStruct((n_e, 128), F32)],
        scratch_shapes=[pltpu.VMEM((n_e, 128), F32)],
        compiler_params=_params("arbitrary"),
        name="out_proj_route",
    )(x, mod, mix_a, mix_b, wo, g, wr_t, br, eye, triu)


ROW_SUB = 4


def _row_copy(src_ref, src_row, dst_ref, dst_row, sem):
    src = src_ref.at[pl.ds(pl.multiple_of(src_row * ROW_SUB, ROW_SUB), ROW_SUB), :]
    dst = dst_ref.at[pl.ds(pl.multiple_of(dst_row * ROW_SUB, ROW_SUB), ROW_SUB), :]
    return pltpu.make_async_copy(src, dst, sem)


def _dispatch_kernel(dest_ref, hp_ref, xs_ref, sem):
    rows = hp_ref.shape[0] // ROW_SUB

    def issue(tok, carry):
        for k in range(TOP_K):
            _row_copy(hp_ref, tok, xs_ref, dest_ref[0, k, tok], sem).start(priority=k % 2)
        return carry

    lax.fori_loop(0, rows, issue, 0, unroll=2)

    def drain(j, carry):
        _row_copy(hp_ref, 0, xs_ref, 0, sem).wait()
        return carry

    lax.fori_loop(0, TOP_K * rows, drain, 0, unroll=16)


def _dispatch(dest3, hp, n_rows):
    steps, _, rows = dest3.shape
    assert rows & (rows - 1) == 0
    return pl.pallas_call(
        _dispatch_kernel,
        grid=(steps,),
        in_specs=[pl.BlockSpec((1, TOP_K, rows), lambda i: (i, 0, 0), memory_space=pltpu.SMEM),
                  pl.BlockSpec((rows * ROW_SUB, 128), lambda i: (i, 0))],
        out_specs=pl.BlockSpec(memory_space=pl.ANY),
        out_shape=jax.ShapeDtypeStruct((n_rows * ROW_SUB, 128), jnp.uint32),
        scratch_shapes=[pltpu.SemaphoreType.DMA(())],
        compiler_params=_params("arbitrary"),
        name="moe_dispatch_rows",
    )(dest3, hp)


def _gmm_kernel(be_ref, va_ref, nu_ref, x_ref, wg_ref, wu_ref, wd_ref, y_ref):
    i = pl.program_id(0)

    @pl.when(i < nu_ref[0])
    def _():
        x = _load_row_tiles(x_ref, ROW_SUB)
        live = lax.broadcasted_iota(jnp.int32, x.shape, 0) < va_ref[i]
        x = jnp.where(live, x, 0.0).astype(BF16)
        a = _silu(_dot(x, wg_ref[...].astype(BF16))) * _dot(x, wu_ref[...].astype(BF16))
        _store_row_tiles(y_ref, _dot(a.astype(BF16), wd_ref[...].astype(BF16)))

    @pl.when(i >= nu_ref[0])
    def _():
        y_ref[...] = jnp.zeros_like(y_ref)


def _gmm(layer, block_e, valid, n_used, xs, wg, wu, wd, tm):
    r = xs.shape[0] // ROW_SUB
    d, de = wg.shape[-2:]
    assert d == ROW_SUB * 256
    nb = r // tm
    grid_spec = pltpu.PrefetchScalarGridSpec(
        num_scalar_prefetch=3,
        grid=(nb,),
        in_specs=[pl.BlockSpec((tm * ROW_SUB, 128), lambda i, be, va, nu: (i, 0)),
                  pl.BlockSpec((None, None, d, de), lambda i, be, va, nu: (layer, be[i], 0, 0)),
                  pl.BlockSpec((None, None, d, de), lambda i, be, va, nu: (layer, be[i], 0, 0)),
                  pl.BlockSpec((None, None, de, d), lambda i, be, va, nu: (layer, be[i], 0, 0))],
        out_specs=pl.BlockSpec((tm * ROW_SUB, 128), lambda i, be, va, nu: (i, 0)),
    )
    return pl.pallas_call(
        _gmm_kernel,
        grid_spec=grid_spec,
        out_shape=jax.ShapeDtypeStruct(xs.shape, jnp.uint32),
        compiler_params=_params("arbitrary"),
        name="moe_grouped_swiglu",
    )(block_e, valid, n_used, xs, wg, wu, wd)


def _combine_kernel(dest_ref, x_ref, mod_ref, hp_ref, w3_ref, wg_ref, wu_ref, wd_ref, ys_ref, o_ref, yg_scr, sem):
    rows = hp_ref.shape[0] // ROW_SUB

    def issue(tok, carry):
        for k in range(TOP_K):
            _row_copy(ys_ref, dest_ref[0, k, tok], yg_scr.at[k], tok, sem).start(priority=k % 2)
        return carry

    lax.fori_loop(0, rows, issue, 0, unroll=2)
    h = _load_row_tiles(hp_ref, ROW_SUB).astype(BF16)
    a = _silu(_dot(h, wg_ref[...])) * _dot(h, wu_ref[...])
    acc = _dot(a.astype(BF16), wd_ref[...])

    def drain(j, carry):
        _row_copy(ys_ref, 0, yg_scr.at[0], 0, sem).wait()
        return carry

    lax.fori_loop(0, TOP_K * rows, drain, 0, unroll=16)
    w3 = w3_ref[...]
    for k in range(TOP_K):
        wk = w3[:, k:k + 1] + w3[:, TOP_K + k:TOP_K + k + 1] + w3[:, 2 * TOP_K + k:2 * TOP_K + k + 1]
        acc = acc + wk * _load_row_tiles(yg_scr.at[k], ROW_SUB)
    x = x_ref[...]
    o_ref[...] = x + mod_ref[:, 5:6, :] * acc.reshape(x.shape)


def _combine(dest3, x1, mod, hp, w3, wg, wu, wd, ys):
    n, t, d = x1.shape
    rt = _Rows(n, t)
    assert rt.rows & (rt.rows - 1) == 0
    return pl.pallas_call(
        _combine_kernel,
        grid=(rt.steps,),
        in_specs=[pl.BlockSpec((1, TOP_K, rt.rows), lambda i: (i, 0, 0), memory_space=pltpu.SMEM),
                  rt.seq_spec(t, d), rt.seq_spec(6, d), rt.tile_spec(ROW_SUB), rt.row_spec(128),
                  _const_spec(wg.shape), _const_spec(wu.shape), _const_spec(wd.shape),
                  pl.BlockSpec(memory_space=pl.ANY)],
        out_specs=rt.seq_spec(t, d),
        out_shape=jax.ShapeDtypeStruct((n, t, d), F32),
        scratch_shapes=[pltpu.VMEM((TOP_K, rt.rows * ROW_SUB, 128), jnp.uint32), pltpu.SemaphoreType.DMA(())],
        compiler_params=_params("arbitrary"),
        name="moe_combine_shared",
    )(dest3, x1, mod, hp, w3, wg, wu, wd, ys)


def _moe(layer, x1, mod, hp, idx3, rank3, w3, counts, p):
    n, t, d = x1.shape
    m = n * t
    n_e = p['w_gate_e'].shape[1]
    a = m * TOP_K
    tm = 256 if a // n_e >= 512 else 128
    counts = counts[:, 0].astype(jnp.int32)
    padded = (counts + tm - 1) // tm * tm
    pend = jnp.cumsum(padded)
    pstart = pend - padded
    onehot = idx3[None] == jnp.arange(n_e, dtype=jnp.int32)[:, None, None, None]
    dest3 = rank3 + jnp.sum(jnp.where(onehot, pstart[:, None, None, None], 0), axis=0)
    n_blocks = a // tm + n_e
    first = jnp.arange(n_blocks, dtype=jnp.int32) * tm
    block_e = jnp.minimum(jnp.sum((pend[None, :] <= first[:, None]).astype(jnp.int32), axis=1), n_e - 1)
    mine = block_e[:, None] == jnp.arange(n_e, dtype=jnp.int32)[None, :]
    last = jnp.sum(jnp.where(mine, (pstart + counts)[None, :], 0), axis=1)
    valid = jnp.clip(last - first, 0, tm).astype(jnp.int32)
    n_used = (pend[-1] // tm).astype(jnp.int32).reshape(1)
    xs = _dispatch(dest3, hp, n_blocks * tm)
    ys = _gmm(layer, block_e, valid, n_used, xs, p['w_gate_e'], p['w_up_e'], p['w_down_e'], tm)
    return _combine(dest3, x1, mod, hp, w3, p['w_gate_s'][layer].astype(BF16), p['w_up_s'][layer].astype(BF16),
                    p['w_down_s'][layer].astype(BF16), ys)


def _odd_in_kernel(x_ref, mod_ref, g_ref, w_ref, gqa_ref, wqb_ref, gqn_ref, gqr_ref, gqrs_ref, gkva_ref,
                   gkr_ref, gkrs_ref, gkn_ref, cos_ref, sin_ref, seg_ref, wuk_ref, wuv_ref,
                   cq_ref, ck_ref, cv_ref, co_ref, gt_ref, ckv_ref, kr_ref, *outs, prompt):
    h = _norm_mod(x_ref, mod_ref, g_ref, 0, 1).astype(BF16)
    z = _dot(h, w_ref[...])
    cq_ref[...] = z[:, 0:512]
    ck_ref[...] = z[:, 512:1024]
    cv_ref[...] = z[:, 1024:1536]
    co_ref[...] = z[:, 1536:2048]
    gt_ref[...] = z[:, 2560:2688]
    cos = cos_ref[...]
    sin = sin_ref[...]
    ckv = _rms(z[:, 2304:2432]) * gkva_ref[...]
    ckv_ref[...] = ckv
    xr = z[:, 2432:2496]
    xr_rot = z[:, 2496:2560]
    rr = lax.rsqrt(jnp.mean(xr * xr, axis=-1, keepdims=True) + EPS)
    kr = rr * (xr * gkr_ref[...] * cos[:, 0:D_ROPE] + xr_rot * gkrs_ref[...] * sin[:, 0:D_ROPE])
    kr_ref[...] = kr
    qa = (_rms(z[:, 2048:2304]) * gqa_ref[...]).astype(BF16)
    qd = _dot(qa, wqb_ref[...])
    nw = D_HEADS * D_NOPE
    rw = D_HEADS * D_ROPE
    qn = [_rms(qd[:, hh * D_NOPE:(hh + 1) * D_NOPE]) * gqn_ref[...] * MLA_SCALE for hh in range(D_HEADS)]
    xq = qd[:, nw:nw + rw]
    xq_rot = qd[:, nw + rw:nw + 2 * rw]
    rq = lax.rsqrt(_dot_sel(xq * xq, seg_ref[...]) * (1.0 / D_ROPE) + EPS)
    qr = rq * (xq * gqr_ref[...] * cos + xq_rot * gqrs_ref[...] * sin) * MLA_SCALE
    ckv_b = ckv.astype(BF16)
    if prompt:
        qcat_ref, kcat_ref, v_ref = outs
        kn = _dot(ckv_b, wuk_ref[...])
        zpad = jnp.zeros((z.shape[0], 256 - D_NOPE - D_ROPE), F32)
        qcat_ref[...] = jnp.concatenate(
            [piece for hh in range(D_HEADS) for piece in (qn[hh], qr[:, hh * D_ROPE:(hh + 1) * D_ROPE], zpad)],
            axis=1).astype(BF16)
        kcat_ref[...] = jnp.concatenate(
            [piece for hh in range(D_HEADS)
             for piece in (_rms(kn[:, hh * D_NOPE:(hh + 1) * D_NOPE]) * gkn_ref[...], kr, zpad)],
            axis=1).astype(BF16)
        v_ref[...] = _dot(ckv_b, wuv_ref[...]).astype(BF16)
    else:
        u_ref, qr_ref = outs
        u_ref[...] = jnp.concatenate(
            [_dot_nt((qn[hh] * gkn_ref[...]).astype(BF16), wuk_ref[:, hh * D_NOPE:(hh + 1) * D_NOPE])
             for hh in range(D_HEADS)], axis=1).astype(BF16)
        qr_ref[...] = qr.astype(BF16)


def _odd_in(x, mod, g, w, consts, cos, sin, prompt):
    n, t, d = x.shape
    rt = _Rows(n, t)
    m = n * t
    widths = [(512, F32)] * 4 + [(128, F32), (128, F32), (D_ROPE, F32)]
    if prompt:
        widths += [(1024, BF16), (1024, BF16), (512, BF16)]
    else:
        widths += [(512, BF16), (256, BF16)]
    return pl.pallas_call(
        functools.partial(_odd_in_kernel, prompt=prompt),
        grid=(rt.steps,),
        in_specs=[rt.seq_spec(t, d), rt.seq_spec(6, d), _const_spec((1, d)), _const_spec(w.shape)]
                 + [_const_spec(c.shape) for c in consts[:9]]
                 + [rt.pos_spec(cos.shape[1]), rt.pos_spec(sin.shape[1])]
                 + [_const_spec(c.shape) for c in consts[9:]],
        out_specs=[rt.row_spec(c) for c, _ in widths],
        out_shape=[jax.ShapeDtypeStruct((m, c), dt) for c, dt in widths],
        compiler_params=_params("arbitrary"),
        name="odd_in_latent_prep",
    )(x, mod, g, w, *consts[:9], cos, sin, *consts[9:])


def _mlstm_kernel(q_ref, k_ref, v_ref, og_ref, gt_ref, gb_ref, ng_ref, c0_ref, n0_ref, m0_ref, tri_ref, eye_ref,
                  o_ref, co_ref, no_ref, mo_ref, c_scr, n_scr, m_scr, *, chunk):
    dh = q_ref.shape[1] // C_HEADS
    t = pl.program_id(1)

    @pl.when(t == 0)
    def _():
        c_scr[...] = c0_ref[0]
        n_scr[...] = n0_ref[0]
        m_scr[...] = m0_ref[0]

    pre = gt_ref[...] + gb_ref[...]
    lf = _log_sigmoid(pre)
    if chunk >= 16:
        l_hi, l_mid, l_lo = _split3(lf)
        f_cum = _dot(tri_ref[...], l_hi) + _dot(tri_ref[...], l_mid) + _dot(tri_ref[...], l_lo)
    else:
        rowg = lax.broadcasted_iota(jnp.int32, lf.shape, 0)
        f_cum = jnp.zeros_like(lf)
        for s in range(chunk):
            f_cum = f_cum + jnp.where(rowg >= s, lf[s:s + 1, :], 0.0)
    ri = lax.broadcasted_iota(jnp.int32, (chunk, chunk), 0)
    ci = lax.broadcasted_iota(jnp.int32, (chunk, chunk), 1)
    pad = 16 - chunk if chunk < 16 else 0

    def pad_rows(a):
        if pad:
            return jnp.concatenate([a, jnp.zeros((pad, a.shape[1]), a.dtype)], axis=0)
        return a

    for h in range(C_HEADS):
        sl = slice(h * dh, (h + 1) * dh)
        q = q_ref[:, sl]
        k = k_ref[:, sl] * (dh ** -0.5)
        v = v_ref[:, sl]
        qb, kb, vb = q.astype(BF16), k.astype(BF16), v.astype(BF16)
        f_col = f_cum[:, C_HEADS + h:C_HEADS + h + 1]
        i_col = pre[:, h:h + 1]
        m_prev = m_scr[h:h + 1, 0:1]
        a_row = jnp.sum(jnp.where(ri == ci, i_col - f_col, 0.0), axis=0, keepdims=True)
        log_d = jnp.where(ci <= ri, f_col + a_row, NEG)
        inter = f_col + m_prev
        m_t = jnp.maximum(inter, jnp.max(log_d, axis=-1, keepdims=True))
        w_inter = jnp.exp(inter - m_t)
        qk = _dot_nt(qb, kb) * jnp.exp(log_d - m_t)
        c_old = c_scr[h]
        n_old = n_scr[h:h + 1, :]
        num = _dot(qk.astype(BF16), vb) + w_inter * _dot_nt(qb, c_old.astype(BF16))
        den = jnp.sum(qk, axis=-1, keepdims=True) + w_inter * jnp.sum(q * n_old, axis=-1, keepdims=True)
        hh = num / jnp.maximum(jnp.abs(den), jnp.exp(-m_t))
        o_ref[:, sl] = _rms(hh) * ng_ref[...] * _sigmoid(og_ref[:, sl])
        f_last = f_col[chunk - 1:chunk, :]
        a_end = f_last - f_col + i_col
        m_new = jnp.maximum(f_last + m_prev, jnp.max(a_end, axis=0, keepdims=True))
        w = jnp.exp(a_end - m_new)
        dec = jnp.exp(f_last + m_prev - m_new)
        wv_t = _dot_nt(eye_ref[...], pad_rows((w * v).astype(BF16))).astype(BF16)
        c_scr[h] = dec * c_old + _dot(wv_t, pad_rows(kb))
        n_scr[h:h + 1, :] = dec * n_old + jnp.sum(w * k, axis=0, keepdims=True)
        m_scr[h:h + 1, :] = jnp.broadcast_to(m_new, (1, m_scr.shape[1]))

    @pl.when(t == pl.num_programs(1) - 1)
    def _():
        co_ref[0] = c_scr[...]
        no_ref[0] = n_scr[...]
        mo_ref[0] = m_scr[...]


def _mlstm(n, t, q, k, v, og, gt, gb, ng, c0, n0, m0, tri, eye):
    chunk = CHUNK_C if t % CHUNK_C == 0 else t
    nt = t // chunk
    w = q.shape[1]
    dh = w // C_HEADS

    def rows(c):
        return pl.BlockSpec((chunk, c), lambda i, j: (i * nt + j, 0))

    cst = pl.BlockSpec((1, C_HEADS, dh, dh), lambda i, j: (i, 0, 0, 0))
    nst = pl.BlockSpec((1, C_HEADS, dh), lambda i, j: (i, 0, 0))
    return pl.pallas_call(
        functools.partial(_mlstm_kernel, chunk=chunk),
        grid=(n, nt),
        in_specs=[rows(w), rows(w), rows(w), rows(w), rows(128),
                  pl.BlockSpec(gb.shape, lambda i, j: (0, 0)), pl.BlockSpec(ng.shape, lambda i, j: (0, 0)),
                  cst, nst, nst,
                  pl.BlockSpec(tri.shape, lambda i, j: (0, 0)), pl.BlockSpec(eye.shape, lambda i, j: (0, 0))],
        out_specs=[rows(w), cst, nst, nst],
        out_shape=[jax.ShapeDtypeStruct((n * t, w), F32), jax.ShapeDtypeStruct(c0.shape, F32),
                   jax.ShapeDtypeStruct(n0.shape, F32), jax.ShapeDtypeStruct(m0.shape, F32)],
        scratch_shapes=[pltpu.VMEM((C_HEADS, dh, dh), F32), pltpu.VMEM((C_HEADS, dh), F32),
                        pltpu.VMEM((C_HEADS, dh), F32)],
        compiler_params=_params("arbitrary", "arbitrary"),
        name="mlstm_scan",
    )(q, k, v, og, gt, gb, ng, c0, n0, m0, tri, eye)


def _flash_kernel(qi_ref, kj_ref, q_ref, k_ref, v_ref, o_ref, m_scr, l_scr, acc_scr):
    p = pl.program_id(2)
    qi = qi_ref[p]
    kj = kj_ref[p]

    @pl.when(kj == 0)
    def _():
        m_scr[...] = jnp.full_like(m_scr, NEG)
        l_scr[...] = jnp.zeros_like(l_scr)
        acc_scr[...] = jnp.zeros_like(acc_scr)

    def step(masked):
        s = _dot_nt(q_ref[...], k_ref[...])
        bq, bk = s.shape
        if masked:
            ri = lax.broadcasted_iota(jnp.int32, (bq, bk), 0)
            ci = lax.broadcasted_iota(jnp.int32, (bq, bk), 1)
            s = jnp.where(ci <= ri, s, NEG)
        lanes = m_scr.shape[1]
        m_old = m_scr[...]
        m_new = jnp.maximum(m_old, jnp.max(s, axis=-1, keepdims=True))
        alpha = jnp.exp(m_old - m_new)
        pr = jnp.exp(s - jnp.concatenate([m_new] * (bk // lanes), axis=1))
        l_scr[...] = alpha * l_scr[...] + jnp.sum(pr, axis=-1, keepdims=True)
        acc_scr[...] = alpha * acc_scr[...] + _dot(pr.astype(BF16), v_ref[...])
        m_scr[...] = m_new

    @pl.when(kj < qi)
    def _():
        step(False)

    @pl.when(kj == qi)
    def _():
        step(True)
        o_ref[...] = acc_scr[...] / l_scr[...]


def _flash(n, t, qcat, kcat, v):
    blk = FLASH_BLOCK if t % FLASH_BLOCK == 0 else t
    nq = t // blk
    pairs = [(i, j) for i in range(nq) for j in range(i + 1)]
    qi = jnp.asarray([a for a, _ in pairs], jnp.int32)
    kj = jnp.asarray([b for _, b in pairs], jnp.int32)
    dv = v.shape[1] // D_HEADS
    grid_spec = pltpu.PrefetchScalarGridSpec(
        num_scalar_prefetch=2,
        grid=(n, D_HEADS, len(pairs)),
        in_specs=[pl.BlockSpec((blk, 256), lambda b, h, p, qi, kj: (b * nq + qi[p], h)),
                  pl.BlockSpec((blk, 256), lambda b, h, p, qi, kj: (b * nq + kj[p], h)),
                  pl.BlockSpec((blk, dv), lambda b, h, p, qi, kj: (b * nq + kj[p], h))],
        out_specs=pl.BlockSpec((blk, dv), lambda b, h, p, qi, kj: (b * nq + qi[p], h)),
        scratch_shapes=[pltpu.VMEM((blk, dv), F32), pltpu.VMEM((blk, dv), F32), pltpu.VMEM((blk, dv), F32)],
    )
    assert blk % dv == 0
    return pl.pallas_call(
        _flash_kernel,
        grid_spec=grid_spec,
        out_shape=jax.ShapeDtypeStruct((n * t, v.shape[1]), F32),
        compiler_params=_params("arbitrary", "arbitrary", "arbitrary"),
        name="mla_prompt_flash",
    )(qi, kj, qcat, kcat, v)


def _paged_kernel(pt_ref, u_ref, qr_ref, cn_ref, kn_ref, wukt_ref, wuv_ref, ckv_hbm, krt_hbm,
                  o_ref, ck_buf, kr_buf, sem, m_scr, l_scr, acc_scr, *, pages, li, n_groups, n_steps):
    b = pl.program_id(0)
    g = pl.program_id(1)
    hq = u_ref.shape[1]
    tq = hq // D_HEADS
    step = b * n_groups + g
    slot = lax.rem(step, 2)

    def page_copies(seq, group, sl, lookup):
        copies = []
        for i in range(pages):
            page = pt_ref[seq, group * pages + i] if lookup else 0
            rows = pl.ds(i * PAGE_SIZE, PAGE_SIZE)
            copies.append(pltpu.make_async_copy(ckv_hbm.at[page, li], ck_buf.at[sl, rows, :], sem.at[sl]))
            copies.append(pltpu.make_async_copy(krt_hbm.at[page, li], kr_buf.at[sl, :, rows], sem.at[sl]))
        return copies

    @pl.when(step == 0)
    def _():
        for cp in page_copies(0, 0, 0, True):
            cp.start()

    @pl.when(step + 1 < n_steps)
    def _():
        nxt = step + 1
        for cp in page_copies(lax.div(nxt, n_groups), lax.rem(nxt, n_groups), 1 - slot, True):
            cp.start()

    for cp in page_copies(0, 0, slot, False):
        cp.wait()

    @pl.when(g == 0)
    def _():
        m_scr[...] = jnp.full_like(m_scr, NEG)
        l_scr[...] = jnp.zeros_like(l_scr)
        acc_scr[...] = jnp.zeros_like(acc_scr)

    lhs = jnp.concatenate([wukt_ref[...], u_ref[0]], axis=0)
    qr = qr_ref[0]
    nk = D_HEADS * D_NOPE

    def scores(ck_b, kr_b, kr_keys_minor=True):
        big = _dot_nt(lhs, ck_b)
        rows = []
        for h in range(D_HEADS):
            kn_t = big[h * D_NOPE:(h + 1) * D_NOPE, :]
            rinv = lax.rsqrt(jnp.sum(kn_t * kn_t, axis=0, keepdims=True) * (1.0 / D_NOPE) + EPS)
            rows.append(big[nk + h * tq:nk + (h + 1) * tq, :] * rinv)
        rope = _dot(qr, kr_b) if kr_keys_minor else _dot_nt(qr, kr_b)
        return jnp.concatenate(rows, axis=0) + rope

    def update(s, ck_b):
        m_old = m_scr[...]
        m_new = jnp.maximum(m_old, jnp.max(s, axis=-1, keepdims=True))
        alpha = jnp.exp(m_old - m_new)
        pr = jnp.exp(s - m_new)
        l_scr[...] = alpha * l_scr[...] + jnp.sum(pr, axis=-1, keepdims=True)
        acc_scr[...] = alpha * acc_scr[...] + _dot(pr.astype(BF16), ck_b)
        m_scr[...] = m_new

    span = 2 * PAGE_SIZE
    cks = [ck_buf[slot, pl.ds(i * span, span), :].astype(BF16) for i in range(pages // 2)]
    kr_all = kr_buf[slot].astype(BF16)
    s_all = jnp.concatenate(
        [scores(cks[i], kr_all[:, i * span:(i + 1) * span]) for i in range(pages // 2)], axis=1)
    update(s_all, jnp.concatenate(cks, axis=0))

    @pl.when(g == pl.num_programs(1) - 1)
    def _():
        fill = PAGE_SIZE - tq
        ck_b = jnp.concatenate([cn_ref[...], jnp.zeros((fill, cn_ref.shape[1]), F32)], axis=0).astype(BF16)
        kr_b = jnp.concatenate([kn_ref[...], jnp.zeros((fill, kn_ref.shape[1]), F32)], axis=0).astype(BF16)
        s = scores(ck_b, kr_b, kr_keys_minor=False)
        ri = lax.broadcasted_iota(jnp.int32, s.shape, 0)
        ci = lax.broadcasted_iota(jnp.int32, s.shape, 1)
        update(jnp.where(ci <= ri % tq, s, NEG), ck_b)
        lat = (acc_scr[...] / l_scr[...]).astype(BF16)
        full = _dot(lat, wuv_ref[...])
        dv = wuv_ref.shape[1] // D_HEADS
        o_ref[...] = jnp.concatenate(
            [full[h * tq:(h + 1) * tq, h * dv:(h + 1) * dv] for h in range(D_HEADS)], axis=1)


def _paged(page_table, li, u3, qr3, ckv_new, kr_new, wuk_t, wuv, cache_ckv, cache_kr):
    n, hq, lat = u3.shape
    tq = hq // D_HEADS
    n_pages = page_table.shape[1]
    pages = min(PAGES_PER_STEP, n_pages)
    assert n_pages % pages == 0 and pages % 2 == 0
    ng = n_pages // pages
    cache_kr_t = jnp.swapaxes(cache_kr, 2, 3)
    keys = pages * PAGE_SIZE

    grid_spec = pltpu.PrefetchScalarGridSpec(
        num_scalar_prefetch=1,
        grid=(n, ng),
        in_specs=[pl.BlockSpec((1, hq, lat), lambda b, g, pt: (b, 0, 0)),
                  pl.BlockSpec((1, hq, D_ROPE), lambda b, g, pt: (b, 0, 0)),
                  pl.BlockSpec((tq, lat), lambda b, g, pt: (b, 0)),
                  pl.BlockSpec((tq, D_ROPE), lambda b, g, pt: (b, 0)),
                  pl.BlockSpec(wuk_t.shape, lambda b, g, pt: (0, 0)),
                  pl.BlockSpec(wuv.shape, lambda b, g, pt: (0, 0)),
                  pl.BlockSpec(memory_space=pl.ANY), pl.BlockSpec(memory_space=pl.ANY)],
        out_specs=pl.BlockSpec((tq, wuv.shape[1]), lambda b, g, pt: (b, 0)),
        scratch_shapes=[pltpu.VMEM((2, keys, lat), F32), pltpu.VMEM((2, D_ROPE, keys), F32),
                        pltpu.SemaphoreType.DMA((2,)),
                        pltpu.VMEM((hq, 1), F32), pltpu.VMEM((hq, 1), F32), pltpu.VMEM((hq, lat), F32)],
    )
    return pl.pallas_call(
        functools.partial(_paged_kernel, pages=pages, li=li, n_groups=ng, n_steps=n * ng),
        grid_spec=grid_spec,
        out_shape=jax.ShapeDtypeStruct((n * tq, wuv.shape[1]), F32),
        compiler_params=_params("arbitrary", "arbitrary"),
        name="mla_sample_paged",
    )(page_table, u3, qr3, ckv_new, kr_new, wuk_t, wuv, cache_ckv, cache_kr_t)


def _np_seg(n_seg, seg_in, seg_out):
    mat = np.zeros((n_seg * seg_in, n_seg * seg_out), np.float32)
    for s in range(n_seg):
        mat[s * seg_in:(s + 1) * seg_in, s * seg_out:(s + 1) * seg_out] = 1.0
    return mat


def _rot_cols(w):
    half = w.shape[-1] // 2
    return jnp.concatenate([-w[..., half:], w[..., :half]], axis=-1)


def _swap_halves(g):
    half = g.shape[-1] // 2
    return jnp.concatenate([g[..., half:], g[..., :half]], axis=-1)


def _rope_tables(pos, reps):
    half = D_ROPE // 2
    inv = ROPE_BASE ** (-jnp.arange(half, dtype=F32) / half)
    ang = pos.astype(F32)[:, None] * inv
    cos = jnp.concatenate([jnp.cos(ang), jnp.cos(ang)], axis=-1)
    sin = jnp.concatenate([jnp.sin(ang), jnp.sin(ang)], axis=-1)
    return jnp.tile(cos, (1, reps)), jnp.tile(sin, (1, reps))


def _trunk(x, c_mod, pos0, gla0, mc0, mn0, mm0, p, sample_ctx):
    n, t, d = x.shape
    m = n * t
    rt = _Rows(n, t)
    eye_r = jnp.eye(rt.rows, dtype=BF16)
    triu_r = jnp.asarray(np.triu(np.ones((rt.rows, rt.rows), np.float32)), BF16)
    dh = mc0.shape[3]
    eye_hk = jnp.eye(B_HEADS * gla0.shape[3], dtype=BF16)
    eye_dh = jnp.eye(dh, dtype=BF16)
    results = {}

    layer, li = 0, 0
    mod = c_mod[layer]
    w_in = p['w_in_even'][li]
    w_in = jnp.concatenate([w_in, jnp.zeros((d, 128 - B_GATE_RANK), F32)], axis=1).astype(BF16)
    a_ws = jnp.tril(p['a_ws'][li])
    a_bs = p['a_bs'][li]
    if t % CHUNK_A == 0:
        ws = a_ws
        bs = jnp.repeat(a_bs.T, CHUNK_A, axis=1)
    else:
        ws = jnp.stack([jnp.kron(jnp.eye(CHUNK_A // t, dtype=F32), a_ws[g, :t, :t]) for g in range(A_GROUPS)])
        bs = jnp.repeat(jnp.tile(a_bs[:, :t], (1, CHUNK_A // t)).T, CHUNK_A, axis=1)
    out_a, v_rows, bq, bk, bv, br, bg = _even_in(
        x, mod, p['norm_mix_g'][layer].reshape(1, d), w_in, p['a_norm_g'][li].reshape(1, -1),
        ws.astype(BF16), bs)
    hk = bq.shape[1]
    dv = bv.shape[1] // B_HEADS
    wg2 = jnp.concatenate([p['b_w_gate2'][li], jnp.zeros((128 - B_GATE_RANK, hk), F32)], axis=0)
    seg = jnp.asarray(_np_seg(B_HEADS, hk // B_HEADS, dv), BF16)
    out_b, s_new = _gla(n, t, bq, bk, bv, br, bg, wg2, p['b_gate_bias'][li].reshape(1, hk),
                        jnp.tile(p['b_norm_g'][li], B_HEADS).reshape(1, -1), gla0[li], seg, eye_hk)
    results['gla'] = s_new
    results['v_rows'] = v_rows
    routed = _out_route(
        x, mod, out_a, out_b, p['w_out'][layer].astype(BF16), p['norm_ffn_g'][layer].reshape(1, d),
        p['w_router'][layer].T, p['b_router'][layer].reshape(-1, 1), eye_r, triu_r)
    x = _moe(layer, routed[0], mod, *routed[1:], p)

    layer, li = 1, 0
    mod = c_mod[layer]
    w = p['w_in_odd'][li]
    hw = C_HEADS * dh
    o_g = 3 * hw
    o_o = o_g + 2 * C_HEADS
    o_qa = o_o + hw
    o_kva = o_qa + p['d_g_qa'].shape[1]
    o_kr = o_kva + p['d_g_kva'].shape[1]
    w_kr = w[:, o_kr:o_kr + D_ROPE]
    w_odd = jnp.concatenate(
        [w[:, :o_g], w[:, o_o:o_qa], w[:, o_qa:o_kva], w[:, o_kva:o_kr], w_kr, _rot_cols(w_kr),
         w[:, o_g:o_o], jnp.zeros((d, 128 - 2 * C_HEADS), F32)], axis=1).astype(BF16)
    wqb = p['d_w_qb'][li].reshape(-1, D_HEADS, D_NOPE + D_ROPE)
    wqb_r = wqb[:, :, D_NOPE:]
    wqb2 = jnp.concatenate([wqb[:, :, :D_NOPE].reshape(-1, D_HEADS * D_NOPE),
                            wqb_r.reshape(-1, D_HEADS * D_ROPE),
                            _rot_cols(wqb_r).reshape(-1, D_HEADS * D_ROPE)], axis=1).astype(BF16)
    w_uk = p['d_w_uk'][li]
    lat = w_uk.shape[0]
    g_qr = p['d_g_qr'][li]
    g_kr = p['d_g_kr'][li]
    cos, sin = _rope_tables(pos0 + jnp.arange(t), D_HEADS)
    prompt = sample_ctx is None
    if not prompt:
        cos, sin = jnp.tile(cos, (rt.s, 1)), jnp.tile(sin, (rt.s, 1))
    consts = [p['d_g_qa'][li].reshape(1, -1), wqb2, p['d_g_qn'][li].reshape(1, -1),
              jnp.tile(g_qr, D_HEADS).reshape(1, -1), jnp.tile(_swap_halves(g_qr), D_HEADS).reshape(1, -1),
              p['d_g_kva'][li].reshape(1, -1), g_kr.reshape(1, -1), _swap_halves(g_kr).reshape(1, -1),
              p['d_g_kn'][li].reshape(1, -1),
              jnp.asarray(_np_seg(D_HEADS, D_ROPE, D_ROPE), BF16),
              w_uk.reshape(lat, -1).astype(BF16), p['d_w_uv'][li].reshape(lat, -1).astype(BF16)]
    outs = _odd_in(x, mod, p['norm_mix_g'][layer].reshape(1, d), w_odd, consts, cos, sin, prompt)
    cq, ck, cv, co, gates, ckv, kr = outs[:7]
    gb = jnp.concatenate([p['c_ig_bias'][li], p['c_fg_bias'][li],
                          jnp.zeros((128 - 2 * C_HEADS,), F32)]).reshape(1, 128)
    chunk = CHUNK_C if t % CHUNK_C == 0 else t
    tri = jnp.asarray(np.tril(np.ones((chunk, chunk), np.float32)), BF16)
    m0b =jnp.broadcast_to(mm0[li][:, :, None], (n, C_HEADS, dh))
    out_c, c_new, n_new, m_new = _mlstm(n, t, cq, ck, cv, co, gates, gb,
                                        p['c_norm_g'][li].reshape(1, -1), mc0[li], mn0[li], m0b, tri, eye_dh)
    results['mlstm'] = (c_new, n_new, m_new[:, :, 0])
    results['ckv'] = ckv.reshape(n, t, -1)
    results['kr'] = kr.reshape(n, t, -1)
    if prompt:
        qcat, kcat, vv = outs[7:]
        out_d = _flash(n, t, qcat, kcat, vv)
    else:
        u, qr = outs[7:]
        cache_ckv, cache_kr, page_table = sample_ctx
        u3 = u.reshape(n, t, D_HEADS, lat).transpose(0, 2, 1, 3).reshape(n, D_HEADS * t, lat)
        qr3 = qr.reshape(n, t, D_HEADS, D_ROPE).transpose(0, 2, 1, 3).reshape(n, D_HEADS * t, D_ROPE)
        wuk_t = w_uk.transpose(1, 2, 0).reshape(-1, lat).astype(BF16)
        out_d = _paged(page_table, li, u3, qr3, ckv, kr, wuk_t, consts[11], cache_ckv, cache_kr)
    routed = _out_route(
        x, mod, out_c, out_d, p['w_out'][layer].astype(BF16), p['norm_ffn_g'][layer].reshape(1, d),
        p['w_router'][layer].T, p['b_router'][layer].reshape(-1, 1), eye_r, triu_r)
    x = _moe(layer, routed[0], mod, *routed[1:], p)
    return x, results


def kernel(x_prompt, x_sample, state_gla, state_mlstm_c, state_mlstm_n, state_mlstm_m,
           cache_ckv, cache_krope, page_table, c_prompt, c_sample,
           norm_mix_g, norm_ffn_g, w_ada, b_ada, w_out,
           w_in_even, a_norm_g, a_ws, a_bs, b_w_gate2, b_gate_bias, b_norm_g,
           w_in_odd, c_ig_bias, c_fg_bias, c_norm_g,
           d_g_qa, d_w_qb, d_g_kva, d_g_qn, d_g_qr, d_g_kr, d_g_kn, d_w_uk, d_w_uv,
           w_router, b_router, w_gate_e, w_up_e, w_down_e, w_gate_s, w_up_s, w_down_s):
    p = dict(norm_mix_g=norm_mix_g, norm_ffn_g=norm_ffn_g, w_out=w_out,
             w_in_even=w_in_even, a_norm_g=a_norm_g, a_ws=a_ws, a_bs=a_bs, b_w_gate2=b_w_gate2,
             b_gate_bias=b_gate_bias, b_norm_g=b_norm_g, w_in_odd=w_in_odd, c_ig_bias=c_ig_bias,
             c_fg_bias=c_fg_bias, c_norm_g=c_norm_g, d_g_qa=d_g_qa, d_w_qb=d_w_qb, d_g_kva=d_g_kva,
             d_g_qn=d_g_qn, d_g_qr=d_g_qr, d_g_kr=d_g_kr, d_g_kn=d_g_kn, d_w_uk=d_w_uk, d_w_uv=d_w_uv,
             w_router=w_router, b_router=b_router, w_gate_e=w_gate_e, w_up_e=w_up_e, w_down_e=w_down_e,
             w_gate_s=w_gate_s, w_up_s=w_up_s, w_down_s=w_down_s)
    n_p, t_p, d = x_prompt.shape
    n_s, t_s, _ = x_sample.shape
    depth = w_ada.shape[0]
    pad_p = (-n_p) % 8
    c_all = jnp.concatenate([c_prompt, jnp.zeros((pad_p, d), F32), c_sample], axis=0)
    mod_all = _ada(c_all, w_ada, b_ada).reshape(depth, c_all.shape[0], 6, d)
    mod_p = mod_all[:, :n_p]
    mod_s = mod_all[:, n_p + pad_p:]

    n_even, _, bh, bdk, bdv = state_gla.shape
    n_odd, _, chh, cdh, _ = state_mlstm_c.shape
    gla0_p = jnp.zeros((n_even, n_p, bh, bdk, bdv), F32)
    mc0_p = jnp.zeros((n_odd, n_p, chh, cdh, cdh), F32)
    mn0_p = jnp.zeros((n_odd, n_p, chh, cdh), F32)
    mm0_p = jnp.full((n_odd, n_p, chh), NEG, F32)
    past_len = page_table.shape[1] * PAGE_SIZE

    y_p, rp = _trunk(x_prompt, mod_p, 0, gla0_p, mc0_p, mn0_p, mm0_p, p, None)
    y_s, rs = _trunk(x_sample, mod_s, past_len, state_gla, state_mlstm_c, state_mlstm_n, state_mlstm_m, p,
                     (cache_ckv, cache_krope, page_table))
    aw = rs['v_rows'].shape[1]
    return (y_p, y_s, rp['gla'][None], rs['gla'][None], rs['v_rows'].reshape(1, n_s, t_s, aw),
            rp['mlstm'][0][None], rs['mlstm'][0][None], rp['mlstm'][1][None], rs['mlstm'][1][None],
            rp['mlstm'][2][None], rs['mlstm'][2][None],
            rp['ckv'][:, None], rs['ckv'][:, None], rp['kr'][:, None], rs['kr'][:, None])
```

```python
import functools

import numpy as np
import jax
import jax.numpy as jnp
from jax import lax
from jax.experimental import pallas as pl
from jax.experimental.pallas import tpu as pltpu

F32 = jnp.float32
BF16 = jnp.bfloat16

EPS = 1e-6
NEG = -1e30

A_GROUPS = 4
CHUNK_A = 128
B_HEADS = 4
B_GATE_RANK = 16
B_GATE_TAU = 16.0
CHUNK_B = 16
C_HEADS = 4
CHUNK_C = 128
D_HEADS = 4
D_NOPE = 128
D_ROPE = 64
ROPE_BASE = 10000.0
MLA_SCALE = (D_NOPE + D_ROPE) ** -0.5
PAGE_SIZE = 128
N_EXPERT_GROUPS = 8
TOPK_GROUPS = 4
TOP_K = 8
ROUTED_SCALE = 2.5

ROW_TILE = 256
SAMPLE_SEQS = 32
FLASH_BLOCK = 1024
PAGES_PER_STEP = 32
VMEM_LIMIT = 56 * 1024 * 1024


def _dot(a, b):
    return jnp.dot(a, b, preferred_element_type=F32)


def _dot_nt(a, b):
    return lax.dot_general(a, b, (((1,), (1,)), ((), ())), preferred_element_type=F32)


def _split2(x):
    hi = x.astype(BF16)
    lo = (x - hi.astype(F32)).astype(BF16)
    return hi, lo


def _split3(x):
    hi = x.astype(BF16)
    r = x - hi.astype(F32)
    mid = r.astype(BF16)
    lo = (r - mid.astype(F32)).astype(BF16)
    return hi, mid, lo


def _dot3(a, b):
    ah, al = _split2(a)
    bh, bl = _split2(b)
    return _dot(ah, bh) + _dot(ah, bl) + _dot(al, bh)


def _dot3_nt(a, b):
    ah, al = _split2(a)
    bh, bl = _split2(b)
    return _dot_nt(ah, bh) + _dot_nt(ah, bl) + _dot_nt(al, bh)


def _dot_sel(x, m01, parts=2):
    ps = _split2(x) if parts == 2 else _split3(x)
    acc = _dot(ps[0], m01)
    for p in ps[1:]:
        acc = acc + _dot(p, m01)
    return acc


def _sigmoid(x):
    return 1.0 / (1.0 + jnp.exp(-x))


def _silu(x):
    return x * _sigmoid(x)


def _log_sigmoid(x):
    return jnp.minimum(x, 0.0) - jnp.log(1.0 + jnp.exp(-jnp.abs(x)))


def _gelu(x):
    return 0.5 * x * (1.0 + jnp.tanh(0.7978845608028654 * (x + 0.044715 * x * x * x)))


def _rms(x, eps=EPS):
    return x * lax.rsqrt(jnp.mean(x * x, axis=-1, keepdims=True) + eps)


def _params(*sem):
    return pltpu.CompilerParams(dimension_semantics=sem, vmem_limit_bytes=VMEM_LIMIT)


def _const_spec(shape):
    nd = len(shape)
    return pl.BlockSpec(shape, lambda *_: (0,) * nd)


class _Rows:
    def __init__(self, n, t):
        self.n, self.t = n, t
        if t % ROW_TILE == 0:
            self.s, self.r = 1, ROW_TILE
            self.tpb = t // ROW_TILE
        else:
            assert t == 8 and n % SAMPLE_SEQS == 0, (n, t)
            self.s, self.r = SAMPLE_SEQS, t
            self.tpb = 1
        self.rows = self.s * self.r
        self.steps = n * t // self.rows

    def seq_spec(self, mid, d):
        s, tpb = self.s, self.tpb
        if mid == self.t:
            return pl.BlockSpec((s, self.r, d), lambda i: (i // tpb, i % tpb, 0))
        return pl.BlockSpec((s, mid, d), lambda i: (i // tpb, 0, 0))

    def row_spec(self, c):
        return pl.BlockSpec((self.rows, c), lambda i: (i, 0))

    def tile_spec(self, sub):
        return pl.BlockSpec((self.rows * sub, 128), lambda i: (i, 0))

    def pos_spec(self, c):
        tpb = self.tpb
        return pl.BlockSpec((self.rows, c), lambda i: (i % tpb, 0))


def _ada_kernel(c_ref, w_ref, b_ref, o_ref):
    o_ref[0] = _dot3(_silu(c_ref[...]), w_ref[0]) + b_ref[0]


def _ada(c_all, w_ada, b_ada):
    depth, d, d6 = w_ada.shape
    nc = c_all.shape[0]
    tn = 1536
    return pl.pallas_call(
        _ada_kernel,
        grid=(depth, d6 // tn),
        in_specs=[pl.BlockSpec((nc, d), lambda l, j: (0, 0)),
                  pl.BlockSpec((1, d, tn), lambda l, j: (l, 0, j)),
                  pl.BlockSpec((1, 1, tn), lambda l, j: (l, 0, j))],
        out_specs=pl.BlockSpec((1, nc, tn), lambda l, j: (l, 0, j)),
        out_shape=jax.ShapeDtypeStruct((depth, nc, d6), F32),
        compiler_params=_params("arbitrary", "arbitrary"),
        name="ada_modulation",
    )(c_all, w_ada, b_ada.reshape(depth, 1, d6))


def _norm_mod(x_ref, mod_ref, g_ref, shift_row, scale_row):
    x = x_ref[...]
    h = _rms(x) * g_ref[...]
    h = h * (1.0 + mod_ref[:, scale_row:scale_row + 1, :]) + mod_ref[:, shift_row:shift_row + 1, :]
    s, r, d = x.shape
    return h.reshape(s * r, d)


def _even_in_kernel(x_ref, mod_ref, g_ref, w_ref, an_ref, ws_ref, bs_ref,
                    oa_ref, v_ref, q_ref, k_ref, bv_ref, r_ref, bg_ref):
    h = _norm_mod(x_ref, mod_ref, g_ref, 0, 1).astype(BF16)
    z = _dot(h, w_ref[...])
    aw = A_GROUPS * CHUNK_A
    rows = z.shape[0]
    u = _gelu(z[:, 0:aw])
    va = _gelu(z[:, aw:2 * aw])
    vn = jnp.concatenate(
        [_rms(va[:, g * CHUNK_A:(g + 1) * CHUNK_A]) for g in range(A_GROUPS)], axis=1) * an_ref[...]
    v_ref[...] = vn
    vb = vn.astype(BF16)
    for c in range(rows // CHUNK_A):
        r0 = c * CHUNK_A
        mixed = jnp.concatenate(
            [_dot(ws_ref[g], vb[r0:r0 + CHUNK_A, g * CHUNK_A:(g + 1) * CHUNK_A]) for g in range(A_GROUPS)],
            axis=1) + bs_ref[...]
        oa_ref[r0:r0 + CHUNK_A, :] = u[r0:r0 + CHUNK_A, :] * mixed
    o = 2 * aw
    q_ref[...] = z[:, o:o + 256]
    k_ref[...] = z[:, o + 256:o + 512]
    bv_ref[...] = z[:, o + 512:o + 1024]
    r_ref[...] = z[:, o + 1024:o + 1536]
    bg_ref[...] = z[:, o + 1536:o + 1664]


def _even_in(x, mod, g, w, an, ws, bs):
    n, t, d = x.shape
    rt = _Rows(n, t)
    m = n * t
    widths = (512, 512, 256, 256, 512, 512, 128)
    return pl.pallas_call(
        _even_in_kernel,
        grid=(rt.steps,),
        in_specs=[rt.seq_spec(t, d), rt.seq_spec(6, d), _const_spec((1, d)), _const_spec(w.shape),
                  _const_spec(an.shape), _const_spec(ws.shape), _const_spec(bs.shape)],
        out_specs=[rt.row_spec(c) for c in widths],
        out_shape=[jax.ShapeDtypeStruct((m, c), F32) for c in widths],
        compiler_params=_params("arbitrary"),
        name="even_in_chunk_gate",
    )(x, mod, g, w, an, ws, bs)


def _gla_kernel(q_ref, k_ref, v_ref, r_ref, g_ref, wg_ref, gb_ref, ng_ref, s0_ref, seg_ref, eye_ref,
                o_ref, so_ref, s_scr, lg_scr, o_scr, *, chunk, n_chunks):
    seqs, _, hk = q_ref.shape
    dk = hk // B_HEADS
    dv = v_ref.shape[2] // B_HEADS
    t = pl.program_id(1)

    @pl.when(t == 0)
    def _():
        for sq in range(seqs):
            s_scr[sq] = s0_ref[sq].reshape(hk, dv)

    for sq in range(seqs):
        lg_scr[sq] = _log_sigmoid(_dot3(g_ref[sq], wg_ref[...]) + gb_ref[...]) * (1.0 / B_GATE_TAU)
    row = lax.broadcasted_iota(jnp.int32, (chunk, hk), 0)
    lane = lax.broadcasted_iota(jnp.int32, (chunk, hk), 1)
    pad = 16 - chunk if chunk < 16 else 0

    def pad_rows(a):
        if pad:
            return jnp.concatenate([a, jnp.zeros((pad, a.shape[1]), a.dtype)], axis=0)
        return a

    def one_chunk(sq, r0):
        lg = lg_scr[sq, pl.ds(r0, chunk), :]
        q = q_ref[sq, pl.ds(r0, chunk), :] * (dk ** -0.5)
        k = k_ref[sq, pl.ds(r0, chunk), :]
        v = v_ref[sq, pl.ds(r0, chunk), :]
        b = jnp.zeros_like(lg)
        for s in range(chunk):
            b = b + jnp.where(row >= s, lg[s:s + 1, :], 0.0)
        prods = []
        for l in range(chunk):
            e = jnp.exp(jnp.where(row <= l, b[l:l + 1, :] - b, NEG))
            prods.append(q[l:l + 1, :] * k * e)
        att = _dot_sel(jnp.concatenate(prods, axis=0), seg_ref[...])
        o_intra = jnp.concatenate(
            [jnp.sum(att[l * chunk:(l + 1) * chunk, :] * v, axis=0, keepdims=True) for l in range(chunk)], axis=0)
        s_old = s_scr[sq]
        s_b = s_old.astype(BF16)
        qe = q * jnp.exp(b)
        o_inter = jnp.concatenate(
            [_dot(jnp.where(lane // dk == h, qe, 0.0).astype(BF16), s_b) for h in range(B_HEADS)], axis=1)
        o_scr[sq, pl.ds(r0, chunk), :] = o_intra + o_inter
        b_last = b[chunk - 1:chunk, :]
        k_end = pad_rows((k * jnp.exp(b_last - b)).astype(BF16))
        k_t = _dot_nt(eye_ref[...], k_end).astype(BF16)
        dh, dm, dl = _split3(jnp.exp(b_last))
        dec3 = _dot_nt(eye_ref[...], jnp.concatenate([dh, dm, dl, jnp.zeros((13, hk), BF16)], axis=0))
        dec = dec3[:, 0:1] + dec3[:, 1:2] + dec3[:, 2:3]
        v_b = pad_rows(v.astype(BF16))
        kv = jnp.concatenate(
            [_dot(k_t[h * dk:(h + 1) * dk, :], v_b[:, h * dv:(h + 1) * dv]) for h in range(B_HEADS)], axis=0)
        s_scr[sq] = dec * s_old + kv

    def body(c, carry):
        r0 = pl.multiple_of(c * chunk, chunk)
        for sq in range(seqs):
            one_chunk(sq, r0)
        return carry

    lax.fori_loop(0, n_chunks, body, 0)

    for sq in range(seqs):
        o = o_scr[sq]
        on = jnp.concatenate([_rms(o[:, h * dv:(h + 1) * dv]) for h in range(B_HEADS)], axis=1) * ng_ref[...]
        o_ref[sq] = on * _silu(r_ref[sq])

    @pl.when(t == pl.num_programs(1) - 1)
    def _():
        for sq in range(seqs):
            so_ref[sq] = s_scr[sq].reshape(B_HEADS, dk, dv)


GLA_SEQS = 2


def _gla(n, t, q, k, v, r, g, wg, gb, ng, s0, seg, eye):
    tt = ROW_TILE if t % ROW_TILE == 0 else t
    chunk = CHUNK_B if tt % CHUNK_B == 0 else tt
    nt = t // tt
    hk, hv = q.shape[1], v.shape[1]
    dk, dv = hk // B_HEADS, hv // B_HEADS
    seqs = GLA_SEQS
    assert n % seqs == 0

    def rows(c):
        return pl.BlockSpec((seqs, tt, c), lambda i, j: (i, j, 0))

    def per_seq(a):
        return a.reshape(n, t, a.shape[1])

    st = pl.BlockSpec((seqs, B_HEADS, dk, dv), lambda i, j: (i, 0, 0, 0))
    o, s_new = pl.pallas_call(
        functools.partial(_gla_kernel, chunk=chunk, n_chunks=tt // chunk),
        grid=(n // seqs, nt),
        in_specs=[rows(hk), rows(hk), rows(hv), rows(hv), rows(g.shape[1]),
                  pl.BlockSpec(wg.shape, lambda i, j: (0, 0)), pl.BlockSpec(gb.shape, lambda i, j: (0, 0)),
                  pl.BlockSpec(ng.shape, lambda i, j: (0, 0)), st,
                  pl.BlockSpec(seg.shape, lambda i, j: (0, 0)), pl.BlockSpec(eye.shape, lambda i, j: (0, 0))],
        out_specs=[rows(hv), st],
        out_shape=[jax.ShapeDtypeStruct((n, t, hv), F32), jax.ShapeDtypeStruct(s0.shape, F32)],
        scratch_shapes=[pltpu.VMEM((seqs, hk, dv), F32), pltpu.VMEM((seqs, tt, hk), F32),
                        pltpu.VMEM((seqs, tt, hv), F32)],
        compiler_params=_params("arbitrary", "arbitrary"),
        name="gla_scan",
    )(per_seq(q), per_seq(k), per_seq(v), per_seq(r), per_seq(g), wg, gb, ng, s0, seg, eye)
    return o.reshape(n * t, hv), s_new


def _route(logits_t, bias_col):
    n_e, r = logits_t.shape
    per = n_e // N_EXPERT_GROUPS
    scores = _sigmoid(logits_t)
    biased = scores + bias_col
    sub = lax.broadcasted_iota(jnp.int32, (per, r), 0).astype(F32)
    ninf = -jnp.inf
    gs = []
    for g in range(N_EXPERT_GROUPS):
        blk = biased[g * per:(g + 1) * per, :]
        m1 = jnp.max(blk, axis=0, keepdims=True)
        i1 = jnp.min(jnp.where(blk == m1, sub, float(per)), axis=0, keepdims=True)
        m2 = jnp.max(jnp.where(sub == i1, ninf, blk), axis=0, keepdims=True)
        gs.append(m1 + m2)
    cur = jnp.concatenate(gs, axis=0)
    gsub = lax.broadcasted_iota(jnp.int32, (N_EXPERT_GROUPS, r), 0).astype(F32)
    gsel = jnp.zeros((N_EXPERT_GROUPS, r), F32)
    for _ in range(TOPK_GROUPS):
        m = jnp.max(cur, axis=0, keepdims=True)
        i = jnp.min(jnp.where(cur == m, gsub, float(N_EXPERT_GROUPS)), axis=0, keepdims=True)
        hit = gsub == i
        gsel = jnp.where(hit, 1.0, gsel)
        cur = jnp.where(hit, ninf, cur)
    cur = jnp.concatenate(
        [jnp.where(gsel[g:g + 1, :] > 0.5, biased[g * per:(g + 1) * per, :], ninf) for g in range(N_EXPERT_GROUPS)],
        axis=0)
    esub = lax.broadcasted_iota(jnp.int32, (n_e, r), 0).astype(F32)
    idx, wts, hits = [], [], []
    for _ in range(TOP_K):
        m = jnp.max(cur, axis=0, keepdims=True)
        i = jnp.min(jnp.where(cur == m, esub, float(n_e)), axis=0, keepdims=True)
        hit = esub == i
        idx.append(i)
        hits.append(hit)
        wts.append(jnp.sum(jnp.where(hit, scores, 0.0), axis=0, keepdims=True))
        cur = jnp.where(hit, ninf, cur)
    w = jnp.concatenate(wts, axis=0)
    w = w / jnp.sum(w, axis=0, keepdims=True) * ROUTED_SCALE
    return jnp.concatenate(idx, axis=0), w, hits


def _pack_pairs(x):
    w = x.shape[1] // 2
    hi = lax.bitcast_convert_type(x[:, :w].astype(BF16).astype(F32), jnp.uint32)
    lo = lax.bitcast_convert_type(x[:, w:].astype(BF16).astype(F32), jnp.uint32)
    return hi | (lo >> 16)


def _unpack_pairs(pk):
    a = lax.bitcast_convert_type(pk & jnp.uint32(0xFFFF0000), F32)
    b = lax.bitcast_convert_type(pk << 16, F32)
    return jnp.concatenate([a, b], axis=1)


def _store_row_tiles(ref, x):
    rows = x.shape[0]
    pk = _pack_pairs(x)
    sub = pk.shape[1] // 128
    for c in range(sub):
        ref[pl.ds(c, rows, stride=sub), :] = pk[:, c * 128:(c + 1) * 128]


def _load_row_tiles(ref, sub):
    rows = ref.shape[0] // sub
    return _unpack_pairs(jnp.concatenate([ref[pl.ds(c, rows, stride=sub), :] for c in range(sub)], axis=1))


def _out_kernel(x_ref, mod_ref, ma_ref, mb_ref, wo_ref, g_ref, wr_ref, br_ref, eye_ref, triu_ref,
                x1_ref, hp_ref, idx_ref, rank_ref, w3_ref, cnt_ref, run_scr):
    step = pl.program_id(0)

    @pl.when(step == 0)
    def _():
        run_scr[...] = jnp.zeros_like(run_scr)

    half = ma_ref.shape[1]
    y = _dot(ma_ref[...].astype(BF16), wo_ref[0:half, :]) + _dot(mb_ref[...].astype(BF16), wo_ref[half:, :])
    x = x_ref[...]
    s, r, d = x.shape
    x1 = x + mod_ref[:, 2:3, :] * y.reshape(s, r, d)
    x1_ref[...] = x1
    h2 = (_rms(x1) * g_ref[...] * (1.0 + mod_ref[:, 4:5, :]) + mod_ref[:, 3:4, :]).reshape(s * r, d)
    _store_row_tiles(hp_ref, h2)
    idx, w, hits = _route(_dot3_nt(wr_ref[...], h2), br_ref[...])
    idx_ref[0] = idx.astype(jnp.int32)
    sel = jnp.zeros(hits[0].shape, F32)
    for hit in hits:
        sel = jnp.where(hit, 1.0, sel)
    before = run_scr[:, 0:1] + _dot(sel.astype(BF16), triu_ref[...]) - sel
    rank_ref[0] = jnp.concatenate(
        [jnp.sum(jnp.where(hit, before, 0.0), axis=0, keepdims=True) for hit in hits], axis=0).astype(jnp.int32)
    run_scr[...] = run_scr[...] + jnp.sum(sel, axis=1, keepdims=True)
    cnt_ref[...] = run_scr[...]
    wh, wm, wl = _split3(w)
    stack = jnp.concatenate([wh, wm, wl, jnp.zeros((128 - 3 * TOP_K, s * r), BF16)], axis=0)
    w3_ref[...] = _dot_nt(eye_ref[...], stack)


def _out_route(x, mod, mix_a, mix_b, wo, g, wr_t, br, eye, triu):
    n, t, d = x.shape
    rt = _Rows(n, t)
    m = n * t
    n_e = wr_t.shape[0]
    slot = pl.BlockSpec((1, TOP_K, rt.rows), lambda i: (i, 0, 0))
    return pl.pallas_call(
        _out_kernel,
        grid=(rt.steps,),
        in_specs=[rt.seq_spec(t, d), rt.seq_spec(6, d), rt.row_spec(mix_a.shape[1]), rt.row_spec(mix_b.shape[1]),
                  _const_spec(wo.shape), _const_spec((1, d)), _const_spec(wr_t.shape), _const_spec(br.shape),
                  _const_spec(eye.shape), _const_spec(triu.shape)],
        out_specs=[rt.seq_spec(t, d), rt.tile_spec(d // 256), slot, slot, rt.row_spec(128), _const_spec((n_e, 128))],
        out_shape=[jax.ShapeDtypeStruct((n, t, d), F32), jax.ShapeDtypeStruct((m * (d // 256), 128), jnp.uint32),
                   jax.ShapeDtypeStruct((rt.steps, TOP_K, rt.rows), jnp.int32),
                   jax.ShapeDtypeStruct((rt.steps, TOP_K, rt.rows), jnp.int32),
                   jax.ShapeDtypeStruct((m, 128), F32), jax.ShapeDtypeStruct((n_e, 128), F32)],
        scratch_shapes=[pltpu.VMEM((n_e, 128), F32)],
        compiler_params=_params("arbitrary"),
        name="out_proj_route",
    )(x, mod, mix_a, mix_b, wo, g, wr_t, br, eye, triu)


ROW_SUB = 4


def _row_copy(src_ref, src_row, dst_ref, dst_row, sem):
    src = src_ref.at[pl.ds(pl.multiple_of(src_row * ROW_SUB, ROW_SUB), ROW_SUB), :]
    dst = dst_ref.at[pl.ds(pl.multiple_of(dst_row * ROW_SUB, ROW_SUB), ROW_SUB), :]
    return pltpu.make_async_copy(src, dst, sem)


def _dispatch_kernel(dest_ref, hp_ref, xs_ref, sem):
    rows = hp_ref.shape[0] // ROW_SUB

    def issue(tok, carry):
        for k in range(TOP_K):
            _row_copy(hp_ref, tok, xs_ref, dest_ref[0, k, tok], sem).start(priority=k % 2)
        return carry

    lax.fori_loop(0, rows, issue, 0, unroll=2)

    def drain(j, carry):
        _row_copy(hp_ref, 0, xs_ref, 0, sem).wait()
        return carry

    lax.fori_loop(0, TOP_K * rows, drain, 0, unroll=16)


def _dispatch(dest3, hp, n_rows):
    steps, _, rows = dest3.shape
    assert rows & (rows - 1) == 0
    return pl.pallas_call(
        _dispatch_kernel,
        grid=(steps,),
        in_specs=[pl.BlockSpec((1, TOP_K, rows), lambda i: (i, 0, 0), memory_space=pltpu.SMEM),
                  pl.BlockSpec((rows * ROW_SUB, 128), lambda i: (i, 0))],
        out_specs=pl.BlockSpec(memory_space=pl.ANY),
        out_shape=jax.ShapeDtypeStruct((n_rows * ROW_SUB, 128), jnp.uint32),
        scratch_shapes=[pltpu.SemaphoreType.DMA(())],
        compiler_params=_params("arbitrary"),
        name="moe_dispatch_rows",
    )(dest3, hp)


def _gmm_kernel(be_ref, va_ref, nu_ref, x_ref, wg_ref, wu_ref, wd_ref, y_ref):
    i = pl.program_id(0)

    @pl.when(i < nu_ref[0])
    def _():
        x = _load_row_tiles(x_ref, ROW_SUB)
        live = lax.broadcasted_iota(jnp.int32, x.shape, 0) < va_ref[i]
        x = jnp.where(live, x, 0.0).astype(BF16)
        a = _silu(_dot(x, wg_ref[...].astype(BF16))) * _dot(x, wu_ref[...].astype(BF16))
        _store_row_tiles(y_ref, _dot(a.astype(BF16), wd_ref[...].astype(BF16)))

    @pl.when(i >= nu_ref[0])
    def _():
        y_ref[...] = jnp.zeros_like(y_ref)


def _gmm(layer, block_e, valid, n_used, xs, wg, wu, wd, tm):
    r = xs.shape[0] // ROW_SUB
    d, de = wg.shape[-2:]
    assert d == ROW_SUB * 256
    nb = r // tm
    grid_spec = pltpu.PrefetchScalarGridSpec(
        num_scalar_prefetch=3,
        grid=(nb,),
        in_specs=[pl.BlockSpec((tm * ROW_SUB, 128), lambda i, be, va, nu: (i, 0)),
                  pl.BlockSpec((None, None, d, de), lambda i, be, va, nu: (layer, be[i], 0, 0)),
                  pl.BlockSpec((None, None, d, de), lambda i, be, va, nu: (layer, be[i], 0, 0)),
                  pl.BlockSpec((None, None, de, d), lambda i, be, va, nu: (layer, be[i], 0, 0))],
        out_specs=pl.BlockSpec((tm * ROW_SUB, 128), lambda i, be, va, nu: (i, 0)),
    )
    return pl.pallas_call(
        _gmm_kernel,
        grid_spec=grid_spec,
        out_shape=jax.ShapeDtypeStruct(xs.shape, jnp.uint32),
        compiler_params=_params("arbitrary"),
        name="moe_grouped_swiglu",
    )(block_e, valid, n_used, xs, wg, wu, wd)


def _combine_kernel(dest_ref, x_ref, mod_ref, hp_ref, w3_ref, wg_ref, wu_ref, wd_ref, ys_ref, o_ref, yg_scr, sem):
    rows = hp_ref.shape[0] // ROW_SUB

    def issue(tok, carry):
        for k in range(TOP_K):
            _row_copy(ys_ref, dest_ref[0, k, tok], yg_scr.at[k], tok, sem).start(priority=k % 2)
        return carry

    lax.fori_loop(0, rows, issue, 0, unroll=2)
    h = _load_row_tiles(hp_ref, ROW_SUB).astype(BF16)
    a = _silu(_dot(h, wg_ref[...])) * _dot(h, wu_ref[...])
    acc = _dot(a.astype(BF16), wd_ref[...])

    def drain(j, carry):
        _row_copy(ys_ref, 0, yg_scr.at[0], 0, sem).wait()
        return carry

    lax.fori_loop(0, TOP_K * rows, drain, 0, unroll=16)
    w3 = w3_ref[...]
    for k in range(TOP_K):
        wk = w3[:, k:k + 1] + w3[:, TOP_K + k:TOP_K + k + 1] + w3[:, 2 * TOP_K + k:2 * TOP_K + k + 1]
        acc = acc + wk * _load_row_tiles(yg_scr.at[k], ROW_SUB)
    x = x_ref[...]
    o_ref[...] = x + mod_ref[:, 5:6, :] * acc.reshape(x.shape)


def _combine(dest3, x1, mod, hp, w3, wg, wu, wd, ys):
    n, t, d = x1.shape
    rt = _Rows(n, t)
    assert rt.rows & (rt.rows - 1) == 0
    return pl.pallas_call(
        _combine_kernel,
        grid=(rt.steps,),
        in_specs=[pl.BlockSpec((1, TOP_K, rt.rows), lambda i: (i, 0, 0), memory_space=pltpu.SMEM),
                  rt.seq_spec(t, d), rt.seq_spec(6, d), rt.tile_spec(ROW_SUB), rt.row_spec(128),
                  _const_spec(wg.shape), _const_spec(wu.shape), _const_spec(wd.shape),
                  pl.BlockSpec(memory_space=pl.ANY)],
        out_specs=rt.seq_spec(t, d),
        out_shape=jax.ShapeDtypeStruct((n, t, d), F32),
        scratch_shapes=[pltpu.VMEM((TOP_K, rt.rows * ROW_SUB, 128), jnp.uint32), pltpu.SemaphoreType.DMA(())],
        compiler_params=_params("arbitrary"),
        name="moe_combine_shared",
    )(dest3, x1, mod, hp, w3, wg, wu, wd, ys)


def _moe(layer, x1, mod, hp, idx3, rank3, w3, counts, p):
    n, t, d = x1.shape
    m = n * t
    n_e = p['w_gate_e'].shape[1]
    a = m * TOP_K
    tm = 512 if a // n_e >= 1024 else 128
    counts = counts[:, 0].astype(jnp.int32)
    padded = (counts + tm - 1) // tm * tm
    pend = jnp.cumsum(padded)
    pstart = pend - padded
    onehot = idx3[None] == jnp.arange(n_e, dtype=jnp.int32)[:, None, None, None]
    dest3 = rank3 + jnp.sum(jnp.where(onehot, pstart[:, None, None, None], 0), axis=0)
    n_blocks = a // tm + n_e
    first = jnp.arange(n_blocks, dtype=jnp.int32) * tm
    block_e = jnp.minimum(jnp.sum((pend[None, :] <= first[:, None]).astype(jnp.int32), axis=1), n_e - 1)
    mine = block_e[:, None] == jnp.arange(n_e, dtype=jnp.int32)[None, :]
    last = jnp.sum(jnp.where(mine, (pstart + counts)[None, :], 0), axis=1)
    valid = jnp.clip(last - first, 0, tm).astype(jnp.int32)
    n_used = (pend[-1] // tm).astype(jnp.int32).reshape(1)
    xs = _dispatch(dest3, hp, n_blocks * tm)
    ys = _gmm(layer, block_e, valid, n_used, xs, p['w_gate_e'], p['w_up_e'], p['w_down_e'], tm)
    return _combine(dest3, x1, mod, hp, w3, p['w_gate_s'][layer].astype(BF16), p['w_up_s'][layer].astype(BF16),
                    p['w_down_s'][layer].astype(BF16), ys)


def _odd_in_kernel(x_ref, mod_ref, g_ref, w_ref, gqa_ref, wqb_ref, gqn_ref, gqr_ref, gqrs_ref, gkva_ref,
                   gkr_ref, gkrs_ref, gkn_ref, cos_ref, sin_ref, seg_ref, wuk_ref, wuv_ref,
                   cq_ref, ck_ref, cv_ref, co_ref, gt_ref, ckv_ref, kr_ref, *outs, prompt):
    h = _norm_mod(x_ref, mod_ref, g_ref, 0, 1).astype(BF16)
    z = _dot(h, w_ref[...])
    cq_ref[...] = z[:, 0:512]
    ck_ref[...] = z[:, 512:1024]
    cv_ref[...] = z[:, 1024:1536]
    co_ref[...] = z[:, 1536:2048]
    gt_ref[...] = z[:, 2560:2688]
    cos = cos_ref[...]
    sin = sin_ref[...]
    ckv = _rms(z[:, 2304:2432]) * gkva_ref[...]
    ckv_ref[...] = ckv
    xr = z[:, 2432:2496]
    xr_rot = z[:, 2496:2560]
    rr = lax.rsqrt(jnp.mean(xr * xr, axis=-1, keepdims=True) + EPS)
    kr = rr * (xr * gkr_ref[...] * cos[:, 0:D_ROPE] + xr_rot * gkrs_ref[...] * sin[:, 0:D_ROPE])
    kr_ref[...] = kr
    qa = (_rms(z[:, 2048:2304]) * gqa_ref[...]).astype(BF16)
    qd = _dot(qa, wqb_ref[...])
    nw = D_HEADS * D_NOPE
    rw = D_HEADS * D_ROPE
    qn = [_rms(qd[:, hh * D_NOPE:(hh + 1) * D_NOPE]) * gqn_ref[...] * MLA_SCALE for hh in range(D_HEADS)]
    xq = qd[:, nw:nw + rw]
    xq_rot = qd[:, nw + rw:nw + 2 * rw]
    rq = lax.rsqrt(_dot_sel(xq * xq, seg_ref[...]) * (1.0 / D_ROPE) + EPS)
    qr = rq * (xq * gqr_ref[...] * cos + xq_rot * gqrs_ref[...] * sin) * MLA_SCALE
    ckv_b = ckv.astype(BF16)
    if prompt:
        qcat_ref, kcat_ref, v_ref = outs
        kn = _dot(ckv_b, wuk_ref[...])
        zpad = jnp.zeros((z.shape[0], 256 - D_NOPE - D_ROPE), F32)
        qcat_ref[...] = jnp.concatenate(
            [piece for hh in range(D_HEADS) for piece in (qn[hh], qr[:, hh * D_ROPE:(hh + 1) * D_ROPE], zpad)],
            axis=1).astype(BF16)
        kcat_ref[...] = jnp.concatenate(
            [piece for hh in range(D_HEADS)
             for piece in (_rms(kn[:, hh * D_NOPE:(hh + 1) * D_NOPE]) * gkn_ref[...], kr, zpad)],
            axis=1).astype(BF16)
        v_ref[...] = _dot(ckv_b, wuv_ref[...]).astype(BF16)
    else:
        u_ref, qr_ref = outs
        u_ref[...] = jnp.concatenate(
            [_dot_nt((qn[hh] * gkn_ref[...]).astype(BF16), wuk_ref[:, hh * D_NOPE:(hh + 1) * D_NOPE])
             for hh in range(D_HEADS)], axis=1).astype(BF16)
        qr_ref[...] = qr.astype(BF16)


def _odd_in(x, mod, g, w, consts, cos, sin, prompt):
    n, t, d = x.shape
    rt = _Rows(n, t)
    m = n * t
    widths = [(512, F32)] * 4 + [(128, F32), (128, F32), (D_ROPE, F32)]
    if prompt:
        widths += [(1024, BF16), (1024, BF16), (512, BF16)]
    else:
        widths += [(512, BF16), (256, BF16)]
    return pl.pallas_call(
        functools.partial(_odd_in_kernel, prompt=prompt),
        grid=(rt.steps,),
        in_specs=[rt.seq_spec(t, d), rt.seq_spec(6, d), _const_spec((1, d)), _const_spec(w.shape)]
                 + [_const_spec(c.shape) for c in consts[:9]]
                 + [rt.pos_spec(cos.shape[1]), rt.pos_spec(sin.shape[1])]
                 + [_const_spec(c.shape) for c in consts[9:]],
        out_specs=[rt.row_spec(c) for c, _ in widths],
        out_shape=[jax.ShapeDtypeStruct((m, c), dt) for c, dt in widths],
        compiler_params=_params("arbitrary"),
        name="odd_in_latent_prep",
    )(x, mod, g, w, *consts[:9], cos, sin, *consts[9:])


def _mlstm_kernel(q_ref, k_ref, v_ref, og_ref, gt_ref, gb_ref, ng_ref, c0_ref, n0_ref, m0_ref, tri_ref, eye_ref,
                  o_ref, co_ref, no_ref, mo_ref, c_scr, n_scr, m_scr, *, chunk):
    dh = q_ref.shape[1] // C_HEADS
    t = pl.program_id(1)

    @pl.when(t == 0)
    def _():
        c_scr[...] = c0_ref[0]
        n_scr[...] = n0_ref[0]
        m_scr[...] = m0_ref[0]

    pre = gt_ref[...] + gb_ref[...]
    lf = _log_sigmoid(pre)
    if chunk >= 16:
        l_hi, l_mid, l_lo = _split3(lf)
        f_cum = _dot(tri_ref[...], l_hi) + _dot(tri_ref[...], l_mid) + _dot(tri_ref[...], l_lo)
    else:
        rowg = lax.broadcasted_iota(jnp.int32, lf.shape, 0)
        f_cum = jnp.zeros_like(lf)
        for s in range(chunk):
            f_cum = f_cum + jnp.where(rowg >= s, lf[s:s + 1, :], 0.0)
    ri = lax.broadcasted_iota(jnp.int32, (chunk, chunk), 0)
    ci = lax.broadcasted_iota(jnp.int32, (chunk, chunk), 1)
    pad = 16 - chunk if chunk < 16 else 0

    def pad_rows(a):
        if pad:
            return jnp.concatenate([a, jnp.zeros((pad, a.shape[1]), a.dtype)], axis=0)
        return a

    for h in range(C_HEADS):
        sl = slice(h * dh, (h + 1) * dh)
        q = q_ref[:, sl]
        k = k_ref[:, sl] * (dh ** -0.5)
        v = v_ref[:, sl]
        qb, kb, vb = q.astype(BF16), k.astype(BF16), v.astype(BF16)
        f_col = f_cum[:, C_HEADS + h:C_HEADS + h + 1]
        i_col = pre[:, h:h + 1]
        m_prev = m_scr[h:h + 1, 0:1]
        a_row = jnp.sum(jnp.where(ri == ci, i_col - f_col, 0.0), axis=0, keepdims=True)
        log_d = jnp.where(ci <= ri, f_col + a_row, NEG)
        inter = f_col + m_prev
        m_t = jnp.maximum(inter, jnp.max(log_d, axis=-1, keepdims=True))
        w_inter = jnp.exp(inter - m_t)
        qk = _dot_nt(qb, kb) * jnp.exp(log_d - m_t)
        c_old = c_scr[h]
        n_old = n_scr[h:h + 1, :]
        num = _dot(qk.astype(BF16), vb) + w_inter * _dot_nt(qb, c_old.astype(BF16))
        den = jnp.sum(qk, axis=-1, keepdims=True) + w_inter * jnp.sum(q * n_old, axis=-1, keepdims=True)
        hh = num / jnp.maximum(jnp.abs(den), jnp.exp(-m_t))
        o_ref[:, sl] = _rms(hh) * ng_ref[...] * _sigmoid(og_ref[:, sl])
        f_last = f_col[chunk - 1:chunk, :]
        a_end = f_last - f_col + i_col
        m_new = jnp.maximum(f_last + m_prev, jnp.max(a_end, axis=0, keepdims=True))
        w = jnp.exp(a_end - m_new)
        dec = jnp.exp(f_last + m_prev - m_new)
        wv_t = _dot_nt(eye_ref[...], pad_rows((w * v).astype(BF16))).astype(BF16)
        c_scr[h] = dec * c_old + _dot(wv_t, pad_rows(kb))
        n_scr[h:h + 1, :] = dec * n_old + jnp.sum(w * k, axis=0, keepdims=True)
        m_scr[h:h + 1, :] = jnp.broadcast_to(m_new, (1, m_scr.shape[1]))

    @pl.when(t == pl.num_programs(1) - 1)
    def _():
        co_ref[0] = c_scr[...]
        no_ref[0] = n_scr[...]
        mo_ref[0] = m_scr[...]


def _mlstm(n, t, q, k, v, og, gt, gb, ng, c0, n0, m0, tri, eye):
    chunk = CHUNK_C if t % CHUNK_C == 0 else t
    nt = t // chunk
    w = q.shape[1]
    dh = w // C_HEADS

    def rows(c):
        return pl.BlockSpec((chunk, c), lambda i, j: (i * nt + j, 0))

    cst = pl.BlockSpec((1, C_HEADS, dh, dh), lambda i, j: (i, 0, 0, 0))
    nst = pl.BlockSpec((1, C_HEADS, dh), lambda i, j: (i, 0, 0))
    return pl.pallas_call(
        functools.partial(_mlstm_kernel, chunk=chunk),
        grid=(n, nt),
        in_specs=[rows(w), rows(w), rows(w), rows(w), rows(128),
                  pl.BlockSpec(gb.shape, lambda i, j: (0, 0)), pl.BlockSpec(ng.shape, lambda i, j: (0, 0)),
                  cst, nst, nst,
                  pl.BlockSpec(tri.shape, lambda i, j: (0, 0)), pl.BlockSpec(eye.shape, lambda i, j: (0, 0))],
        out_specs=[rows(w), cst, nst, nst],
        out_shape=[jax.ShapeDtypeStruct((n * t, w), F32), jax.ShapeDtypeStruct(c0.shape, F32),
                   jax.ShapeDtypeStruct(n0.shape, F32), jax.ShapeDtypeStruct(m0.shape, F32)],
        scratch_shapes=[pltpu.VMEM((C_HEADS, dh, dh), F32), pltpu.VMEM((C_HEADS, dh), F32),
                        pltpu.VMEM((C_HEADS, dh), F32)],
        compiler_params=_params("arbitrary", "arbitrary"),
        name="mlstm_scan",
    )(q, k, v, og, gt, gb, ng, c0, n0, m0, tri, eye)


def _flash_kernel(qi_ref, kj_ref, q_ref, k_ref, v_ref, o_ref, m_scr, l_scr, acc_scr):
    p = pl.program_id(2)
    qi = qi_ref[p]
    kj = kj_ref[p]

    @pl.when(kj == 0)
    def _():
        m_scr[...] = jnp.full_like(m_scr, NEG)
        l_scr[...] = jnp.zeros_like(l_scr)
        acc_scr[...] = jnp.zeros_like(acc_scr)

    def step(masked):
        s = _dot_nt(q_ref[...], k_ref[...])
        bq, bk = s.shape
        if masked:
            ri = lax.broadcasted_iota(jnp.int32, (bq, bk), 0)
            ci = lax.broadcasted_iota(jnp.int32, (bq, bk), 1)
            s = jnp.where(ci <= ri, s, NEG)
        lanes = m_scr.shape[1]
        m_old = m_scr[...]
        m_new = jnp.maximum(m_old, jnp.max(s, axis=-1, keepdims=True))
        alpha = jnp.exp(m_old - m_new)
        pr = jnp.exp(s - jnp.concatenate([m_new] * (bk // lanes), axis=1))
        l_scr[...] = alpha * l_scr[...] + jnp.sum(pr, axis=-1, keepdims=True)
        acc_scr[...] = alpha * acc_scr[...] + _dot(pr.astype(BF16), v_ref[...])
        m_scr[...] = m_new

    @pl.when(kj < qi)
    def _():
        step(False)

    @pl.when(kj == qi)
    def _():
        step(True)
        o_ref[...] = acc_scr[...] / l_scr[...]


def _flash(n, t, qcat, kcat, v):
    blk = FLASH_BLOCK if t % FLASH_BLOCK == 0 else t
    nq = t // blk
    pairs = [(i, j) for i in range(nq) for j in range(i + 1)]
    qi = jnp.asarray([a for a, _ in pairs], jnp.int32)
    kj = jnp.asarray([b for _, b in pairs], jnp.int32)
    dv = v.shape[1] // D_HEADS
    grid_spec = pltpu.PrefetchScalarGridSpec(
        num_scalar_prefetch=2,
        grid=(n, D_HEADS, len(pairs)),
        in_specs=[pl.BlockSpec((blk, 256), lambda b, h, p, qi, kj: (b * nq + qi[p], h)),
                  pl.BlockSpec((blk, 256), lambda b, h, p, qi, kj: (b * nq + kj[p], h)),
                  pl.BlockSpec((blk, dv), lambda b, h, p, qi, kj: (b * nq + kj[p], h))],
        out_specs=pl.BlockSpec((blk, dv), lambda b, h, p, qi, kj: (b * nq + qi[p], h)),
        scratch_shapes=[pltpu.VMEM((blk, dv), F32), pltpu.VMEM((blk, dv), F32), pltpu.VMEM((blk, dv), F32)],
    )
    assert blk % dv == 0
    return pl.pallas_call(
        _flash_kernel,
        grid_spec=grid_spec,
        out_shape=jax.ShapeDtypeStruct((n * t, v.shape[1]), F32),
        compiler_params=_params("arbitrary", "arbitrary", "arbitrary"),
        name="mla_prompt_flash",
    )(qi, kj, qcat, kcat, v)


def _paged_kernel(pt_ref, u_ref, qr_ref, cn_ref, kn_ref, wukt_ref, wuv_ref, ckv_hbm, krt_hbm,
                  o_ref, ck_buf, kr_buf, sem, m_scr, l_scr, acc_scr, *, pages, li, n_groups, n_steps):
    b = pl.program_id(0)
    g = pl.program_id(1)
    hq = u_ref.shape[1]
    tq = hq // D_HEADS
    step = b * n_groups + g
    slot = lax.rem(step, 2)

    def page_copies(seq, group, sl, lookup):
        copies = []
        for i in range(pages):
            page = pt_ref[seq, group * pages + i] if lookup else 0
            rows = pl.ds(i * PAGE_SIZE, PAGE_SIZE)
            copies.append(pltpu.make_async_copy(ckv_hbm.at[page, li], ck_buf.at[sl, rows, :], sem.at[sl]))
            copies.append(pltpu.make_async_copy(krt_hbm.at[page, li], kr_buf.at[sl, :, rows], sem.at[sl]))
        return copies

    @pl.when(step == 0)
    def _():
        for cp in page_copies(0, 0, 0, True):
            cp.start()

    @pl.when(step + 1 < n_steps)
    def _():
        nxt = step + 1
        for cp in page_copies(lax.div(nxt, n_groups), lax.rem(nxt, n_groups), 1 - slot, True):
            cp.start()

    for cp in page_copies(0, 0, slot, False):
        cp.wait()

    @pl.when(g == 0)
    def _():
        m_scr[...] = jnp.full_like(m_scr, NEG)
        l_scr[...] = jnp.zeros_like(l_scr)
        acc_scr[...] = jnp.zeros_like(acc_scr)

    lhs = jnp.concatenate([wukt_ref[...], u_ref[0]], axis=0)
    qr = qr_ref[0]
    nk = D_HEADS * D_NOPE

    def scores(ck_b, kr_b, kr_keys_minor=True):
        big = _dot_nt(lhs, ck_b)
        rows = []
        for h in range(D_HEADS):
            kn_t = big[h * D_NOPE:(h + 1) * D_NOPE, :]
            rinv = lax.rsqrt(jnp.sum(kn_t * kn_t, axis=0, keepdims=True) * (1.0 / D_NOPE) + EPS)
            rows.append(big[nk + h * tq:nk + (h + 1) * tq, :] * rinv)
        rope = _dot(qr, kr_b) if kr_keys_minor else _dot_nt(qr, kr_b)
        return jnp.concatenate(rows, axis=0) + rope

    def update(s, ck_b):
        m_old = m_scr[...]
        m_new = jnp.maximum(m_old, jnp.max(s, axis=-1, keepdims=True))
        alpha = jnp.exp(m_old - m_new)
        pr = jnp.exp(s - m_new)
        l_scr[...] = alpha * l_scr[...] + jnp.sum(pr, axis=-1, keepdims=True)
        acc_scr[...] = alpha * acc_scr[...] + _dot(pr.astype(BF16), ck_b)
        m_scr[...] = m_new

    span = 2 * PAGE_SIZE
    cks = [ck_buf[slot, pl.ds(i * span, span), :].astype(BF16) for i in range(pages // 2)]
    kr_all = kr_buf[slot].astype(BF16)
    s_all = jnp.concatenate(
        [scores(cks[i], kr_all[:, i * span:(i + 1) * span]) for i in range(pages // 2)], axis=1)
    update(s_all, jnp.concatenate(cks, axis=0))

    @pl.when(g == pl.num_programs(1) - 1)
    def _():
        fill = PAGE_SIZE - tq
        ck_b = jnp.concatenate([cn_ref[...], jnp.zeros((fill, cn_ref.shape[1]), F32)], axis=0).astype(BF16)
        kr_b = jnp.concatenate([kn_ref[...], jnp.zeros((fill, kn_ref.shape[1]), F32)], axis=0).astype(BF16)
        s = scores(ck_b, kr_b, kr_keys_minor=False)
        ri = lax.broadcasted_iota(jnp.int32, s.shape, 0)
        ci = lax.broadcasted_iota(jnp.int32, s.shape, 1)
        update(jnp.where(ci <= ri % tq, s, NEG), ck_b)
        lat = (acc_scr[...] / l_scr[...]).astype(BF16)
        full = _dot(lat, wuv_ref[...])
        dv = wuv_ref.shape[1] // D_HEADS
        o_ref[...] = jnp.concatenate(
            [full[h * tq:(h + 1) * tq, h * dv:(h + 1) * dv] for h in range(D_HEADS)], axis=1)


def _paged(page_table, li, u3, qr3, ckv_new, kr_new, wuk_t, wuv, cache_ckv, cache_kr):
    n, hq, lat = u3.shape
    tq = hq // D_HEADS
    n_pages = page_table.shape[1]
    pages = min(PAGES_PER_STEP, n_pages)
    assert n_pages % pages == 0 and pages % 2 == 0
    ng = n_pages // pages
    cache_kr_t = jnp.swapaxes(cache_kr, 2, 3)
    keys = pages * PAGE_SIZE

    grid_spec = pltpu.PrefetchScalarGridSpec(
        num_scalar_prefetch=1,
        grid=(n, ng),
        in_specs=[pl.BlockSpec((1, hq, lat), lambda b, g, pt: (b, 0, 0)),
                  pl.BlockSpec((1, hq, D_ROPE), lambda b, g, pt: (b, 0, 0)),
                  pl.BlockSpec((tq, lat), lambda b, g, pt: (b, 0)),
                  pl.BlockSpec((tq, D_ROPE), lambda b, g, pt: (b, 0)),
                  pl.BlockSpec(wuk_t.shape, lambda b, g, pt: (0, 0)),
                  pl.BlockSpec(wuv.shape, lambda b, g, pt: (0, 0)),
                  pl.BlockSpec(memory_space=pl.ANY), pl.BlockSpec(memory_space=pl.ANY)],
        out_specs=pl.BlockSpec((tq, wuv.shape[1]), lambda b, g, pt: (b, 0)),
        scratch_shapes=[pltpu.VMEM((2, keys, lat), F32), pltpu.VMEM((2, D_ROPE, keys), F32),
                        pltpu.SemaphoreType.DMA((2,)),
                        pltpu.VMEM((hq, 1), F32), pltpu.VMEM((hq, 1), F32), pltpu.VMEM((hq, lat), F32)],
    )
    return pl.pallas_call(
        functools.partial(_paged_kernel, pages=pages, li=li, n_groups=ng, n_steps=n * ng),
        grid_spec=grid_spec,
        out_shape=jax.ShapeDtypeStruct((n * tq, wuv.shape[1]), F32),
        compiler_params=_params("arbitrary", "arbitrary"),
        name="mla_sample_paged",
    )(page_table, u3, qr3, ckv_new, kr_new, wuk_t, wuv, cache_ckv, cache_kr_t)


def _np_seg(n_seg, seg_in, seg_out):
    mat = np.zeros((n_seg * seg_in, n_seg * seg_out), np.float32)
    for s in range(n_seg):
        mat[s * seg_in:(s + 1) * seg_in, s * seg_out:(s + 1) * seg_out] = 1.0
    return mat


def _rot_cols(w):
    half = w.shape[-1] // 2
    return jnp.concatenate([-w[..., half:], w[..., :half]], axis=-1)


def _swap_halves(g):
    half = g.shape[-1] // 2
    return jnp.concatenate([g[..., half:], g[..., :half]], axis=-1)


def _rope_tables(pos, reps):
    half = D_ROPE // 2
    inv = ROPE_BASE ** (-jnp.arange(half, dtype=F32) / half)
    ang = pos.astype(F32)[:, None] * inv
    cos = jnp.concatenate([jnp.cos(ang), jnp.cos(ang)], axis=-1)
    sin = jnp.concatenate([jnp.sin(ang), jnp.sin(ang)], axis=-1)
    return jnp.tile(cos, (1, reps)), jnp.tile(sin, (1, reps))


def _trunk(x, c_mod, pos0, gla0, mc0, mn0, mm0, p, sample_ctx):
    n, t, d = x.shape
    m = n * t
    rt = _Rows(n, t)
    eye_r = jnp.eye(rt.rows, dtype=BF16)
    triu_r = jnp.asarray(np.triu(np.ones((rt.rows, rt.rows), np.float32)), BF16)
    dh = mc0.shape[3]
    eye_hk = jnp.eye(B_HEADS * gla0.shape[3], dtype=BF16)
    eye_dh = jnp.eye(dh, dtype=BF16)
    results = {}

    layer, li = 0, 0
    mod = c_mod[layer]
    w_in = p['w_in_even'][li]
    w_in = jnp.concatenate([w_in, jnp.zeros((d, 128 - B_GATE_RANK), F32)], axis=1).astype(BF16)
    a_ws = jnp.tril(p['a_ws'][li])
    a_bs = p['a_bs'][li]
    if t % CHUNK_A == 0:
        ws = a_ws
        bs = jnp.repeat(a_bs.T, CHUNK_A, axis=1)
    else:
        ws = jnp.stack([jnp.kron(jnp.eye(CHUNK_A // t, dtype=F32), a_ws[g, :t, :t]) for g in range(A_GROUPS)])
        bs = jnp.repeat(jnp.tile(a_bs[:, :t], (1, CHUNK_A // t)).T, CHUNK_A, axis=1)
    out_a, v_rows, bq, bk, bv, br, bg = _even_in(
        x, mod, p['norm_mix_g'][layer].reshape(1, d), w_in, p['a_norm_g'][li].reshape(1, -1),
        ws.astype(BF16), bs)
    hk = bq.shape[1]
    dv = bv.shape[1] // B_HEADS
    wg2 = jnp.concatenate([p['b_w_gate2'][li], jnp.zeros((128 - B_GATE_RANK, hk), F32)], axis=0)
    seg = jnp.asarray(_np_seg(B_HEADS, hk // B_HEADS, dv), BF16)
    out_b, s_new = _gla(n, t, bq, bk, bv, br, bg, wg2, p['b_gate_bias'][li].reshape(1, hk),
                        jnp.tile(p['b_norm_g'][li], B_HEADS).reshape(1, -1), gla0[li], seg, eye_hk)
    results['gla'] = s_new
    results['v_rows'] = v_rows
    routed = _out_route(
        x, mod, out_a, out_b, p['w_out'][layer].astype(BF16), p['norm_ffn_g'][layer].reshape(1, d),
        p['w_router'][layer].T, p['b_router'][layer].reshape(-1, 1), eye_r, triu_r)
    x = _moe(layer, routed[0], mod, *routed[1:], p)

    layer, li = 1, 0
    mod = c_mod[layer]
    w = p['w_in_odd'][li]
    hw = C_HEADS * dh
    o_g = 3 * hw
    o_o = o_g + 2 * C_HEADS
    o_qa = o_o + hw
    o_kva = o_qa + p['d_g_qa'].shape[1]
    o_kr = o_kva + p['d_g_kva'].shape[1]
    w_kr = w[:, o_kr:o_kr + D_ROPE]
    w_odd = jnp.concatenate(
        [w[:, :o_g], w[:, o_o:o_qa], w[:, o_qa:o_kva], w[:, o_kva:o_kr], w_kr, _rot_cols(w_kr),
         w[:, o_g:o_o], jnp.zeros((d, 128 - 2 * C_HEADS), F32)], axis=1).astype(BF16)
    wqb = p['d_w_qb'][li].reshape(-1, D_HEADS, D_NOPE + D_ROPE)
    wqb_r = wqb[:, :, D_NOPE:]
    wqb2 = jnp.concatenate([wqb[:, :, :D_NOPE].reshape(-1, D_HEADS * D_NOPE),
                            wqb_r.reshape(-1, D_HEADS * D_ROPE),
                            _rot_cols(wqb_r).reshape(-1, D_HEADS * D_ROPE)], axis=1).astype(BF16)
    w_uk = p['d_w_uk'][li]
    lat = w_uk.shape[0]
    g_qr = p['d_g_qr'][li]
    g_kr = p['d_g_kr'][li]
    cos, sin = _rope_tables(pos0 + jnp.arange(t), D_HEADS)
    prompt = sample_ctx is None
    if not prompt:
        cos, sin = jnp.tile(cos, (rt.s, 1)), jnp.tile(sin, (rt.s, 1))
    consts = [p['d_g_qa'][li].reshape(1, -1), wqb2, p['d_g_qn'][li].reshape(1, -1),
              jnp.tile(g_qr, D_HEADS).reshape(1, -1), jnp.tile(_swap_halves(g_qr), D_HEADS).reshape(1, -1),
              p['d_g_kva'][li].reshape(1, -1), g_kr.reshape(1, -1), _swap_halves(g_kr).reshape(1, -1),
              p['d_g_kn'][li].reshape(1, -1),
              jnp.asarray(_np_seg(D_HEADS, D_ROPE, D_ROPE), BF16),
              w_uk.reshape(lat, -1).astype(BF16), p['d_w_uv'][li].reshape(lat, -1).astype(BF16)]
    outs = _odd_in(x, mod, p['norm_mix_g'][layer].reshape(1, d), w_odd, consts, cos, sin, prompt)
    cq, ck, cv, co, gates, ckv, kr = outs[:7]
    gb = jnp.concatenate([p['c_ig_bias'][li], p['c_fg_bias'][li],
                          jnp.zeros((128 - 2 * C_HEADS,), F32)]).reshape(1, 128)
    chunk = CHUNK_C if t % CHUNK_C == 0 else t
    tri = jnp.asarray(np.tril(np.ones((chunk, chunk), np.float32)), BF16)
    m0b =jnp.broadcast_to(mm0[li][:, :, None], (n, C_HEADS, dh))
    out_c, c_new, n_new, m_new = _mlstm(n, t, cq, ck, cv, co, gates, gb,
                                        p['c_norm_g'][li].reshape(1, -1), mc0[li], mn0[li], m0b, tri, eye_dh)
    results['mlstm'] = (c_new, n_new, m_new[:, :, 0])
    results['ckv'] = ckv.reshape(n, t, -1)
    results['kr'] = kr.reshape(n, t, -1)
    if prompt:
        qcat, kcat, vv = outs[7:]
        out_d = _flash(n, t, qcat, kcat, vv)
    else:
        u, qr = outs[7:]
        cache_ckv, cache_kr, page_table = sample_ctx
        u3 = u.reshape(n, t, D_HEADS, lat).transpose(0, 2, 1, 3).reshape(n, D_HEADS * t, lat)
        qr3 = qr.reshape(n, t, D_HEADS, D_ROPE).transpose(0, 2, 1, 3).reshape(n, D_HEADS * t, D_ROPE)
        wuk_t = w_uk.transpose(1, 2, 0).reshape(-1, lat).astype(BF16)
        out_d = _paged(page_table, li, u3, qr3, ckv, kr, wuk_t, consts[11], cache_ckv, cache_kr)
    routed = _out_route(
        x, mod, out_c, out_d, p['w_out'][layer].astype(BF16), p['norm_ffn_g'][layer].reshape(1, d),
        p['w_router'][layer].T, p['b_router'][layer].reshape(-1, 1), eye_r, triu_r)
    x = _moe(layer, routed[0], mod, *routed[1:], p)
    return x, results


def kernel(x_prompt, x_sample, state_gla, state_mlstm_c, state_mlstm_n, state_mlstm_m,
           cache_ckv, cache_krope, page_table, c_prompt, c_sample,
           norm_mix_g, norm_ffn_g, w_ada, b_ada, w_out,
           w_in_even, a_norm_g, a_ws, a_bs, b_w_gate2, b_gate_bias, b_norm_g,
           w_in_odd, c_ig_bias, c_fg_bias, c_norm_g,
           d_g_qa, d_w_qb, d_g_kva, d_g_qn, d_g_qr, d_g_kr, d_g_kn, d_w_uk, d_w_uv,
           w_router, b_router, w_gate_e, w_up_e, w_down_e, w_gate_s, w_up_s, w_down_s):
    p = dict(norm_mix_g=norm_mix_g, norm_ffn_g=norm_ffn_g, w_out=w_out,
             w_in_even=w_in_even, a_norm_g=a_norm_g, a_ws=a_ws, a_bs=a_bs, b_w_gate2=b_w_gate2,
             b_gate_bias=b_gate_bias, b_norm_g=b_norm_g, w_in_odd=w_in_odd, c_ig_bias=c_ig_bias,
             c_fg_bias=c_fg_bias, c_norm_g=c_norm_g, d_g_qa=d_g_qa, d_w_qb=d_w_qb, d_g_kva=d_g_kva,
             d_g_qn=d_g_qn, d_g_qr=d_g_qr, d_g_kr=d_g_kr, d_g_kn=d_g_kn, d_w_uk=d_w_uk, d_w_uv=d_w_uv,
             w_router=w_router, b_router=b_router, w_gate_e=w_gate_e, w_up_e=w_up_e, w_down_e=w_down_e,
             w_gate_s=w_gate_s, w_up_s=w_up_s, w_down_s=w_down_s)
    n_p, t_p, d = x_prompt.shape
    n_s, t_s, _ = x_sample.shape
    depth = w_ada.shape[0]
    pad_p = (-n_p) % 8
    c_all = jnp.concatenate([c_prompt, jnp.zeros((pad_p, d), F32), c_sample], axis=0)
    mod_all = _ada(c_all, w_ada, b_ada).reshape(depth, c_all.shape[0], 6, d)
    mod_p = mod_all[:, :n_p]
    mod_s = mod_all[:, n_p + pad_p:]

    n_even, _, bh, bdk, bdv = state_gla.shape
    n_odd, _, chh, cdh, _ = state_mlstm_c.shape
    gla0_p = jnp.zeros((n_even, n_p, bh, bdk, bdv), F32)
    mc0_p = jnp.zeros((n_odd, n_p, chh, cdh, cdh), F32)
    mn0_p = jnp.zeros((n_odd, n_p, chh, cdh), F32)
    mm0_p = jnp.full((n_odd, n_p, chh), NEG, F32)
    past_len = page_table.shape[1] * PAGE_SIZE

    y_p, rp = _trunk(x_prompt, mod_p, 0, gla0_p, mc0_p, mn0_p, mm0_p, p, None)
    y_s, rs = _trunk(x_sample, mod_s, past_len, state_gla, state_mlstm_c, state_mlstm_n, state_mlstm_m, p,
                     (cache_ckv, cache_krope, page_table))
    aw = rs['v_rows'].shape[1]
    return (y_p, y_s, rp['gla'][None], rs['gla'][None], rs['v_rows'].reshape(1, n_s, t_s, aw),
            rp['mlstm'][0][None], rs['mlstm'][0][None], rp['mlstm'][1][None], rs['mlstm'][1][None],
            rp['mlstm'][2][None], rs['mlstm'][2][None],
            rp['ckv'][:, None], rs['ckv'][:, None], rp['kr'][:, None], rs['kr'][:, None])
```

```python
import functools

import numpy as np
import jax
import jax.numpy as jnp
from jax import lax
from jax.experimental import pallas as pl
from jax.experimental.pallas import tpu as pltpu

F32 = jnp.float32
BF16 = jnp.bfloat16

EPS = 1e-6
NEG = -1e30

A_GROUPS = 4
CHUNK_A = 128
B_HEADS = 4
B_GATE_RANK = 16
B_GATE_TAU = 16.0
CHUNK_B = 16
C_HEADS = 4
CHUNK_C = 128
D_HEADS = 4
D_NOPE = 128
D_ROPE = 64
ROPE_BASE = 10000.0
MLA_SCALE = (D_NOPE + D_ROPE) ** -0.5
PAGE_SIZE = 128
N_EXPERT_GROUPS = 8
TOPK_GROUPS = 4
TOP_K = 8
ROUTED_SCALE = 2.5

ROW_TILE = 256
SAMPLE_SEQS = 32
FLASH_BLOCK = 1024
PAGES_PER_STEP = 32
VMEM_LIMIT = 56 * 1024 * 1024


def _dot(a, b):
    return jnp.dot(a, b, preferred_element_type=F32)


def _dot_nt(a, b):
    return lax.dot_general(a, b, (((1,), (1,)), ((), ())), preferred_element_type=F32)


def _split2(x):
    hi = x.astype(BF16)
    lo = (x - hi.astype(F32)).astype(BF16)
    return hi, lo


def _split3(x):
    hi = x.astype(BF16)
    r = x - hi.astype(F32)
    mid = r.astype(BF16)
    lo = (r - mid.astype(F32)).astype(BF16)
    return hi, mid, lo


def _dot3(a, b):
    ah, al = _split2(a)
    bh, bl = _split2(b)
    return _dot(ah, bh) + _dot(ah, bl) + _dot(al, bh)


def _dot3_nt(a, b):
    ah, al = _split2(a)
    bh, bl = _split2(b)
    return _dot_nt(ah, bh) + _dot_nt(ah, bl) + _dot_nt(al, bh)


def _dot_sel(x, m01, parts=2):
    ps = _split2(x) if parts == 2 else _split3(x)
    acc = _dot(ps[0], m01)
    for p in ps[1:]:
        acc = acc + _dot(p, m01)
    return acc


def _sigmoid(x):
    return 1.0 / (1.0 + jnp.exp(-x))


def _silu(x):
    return x * _sigmoid(x)


def _log_sigmoid(x):
    return jnp.minimum(x, 0.0) - jnp.log(1.0 + jnp.exp(-jnp.abs(x)))


def _gelu(x):
    return 0.5 * x * (1.0 + jnp.tanh(0.7978845608028654 * (x + 0.044715 * x * x * x)))


def _rms(x, eps=EPS):
    return x * lax.rsqrt(jnp.mean(x * x, axis=-1, keepdims=True) + eps)


def _params(*sem):
    return pltpu.CompilerParams(dimension_semantics=sem, vmem_limit_bytes=VMEM_LIMIT)


def _const_spec(shape):
    nd = len(shape)
    return pl.BlockSpec(shape, lambda *_: (0,) * nd)


class _Rows:
    def __init__(self, n, t):
        self.n, self.t = n, t
        if t % ROW_TILE == 0:
            self.s, self.r = 1, ROW_TILE
            self.tpb = t // ROW_TILE
        else:
            assert t == 8 and n % SAMPLE_SEQS == 0, (n, t)
            self.s, self.r = SAMPLE_SEQS, t
            self.tpb = 1
        self.rows = self.s * self.r
        self.steps = n * t // self.rows

    def seq_spec(self, mid, d):
        s, tpb = self.s, self.tpb
        if mid == self.t:
            return pl.BlockSpec((s, self.r, d), lambda i: (i // tpb, i % tpb, 0))
        return pl.BlockSpec((s, mid, d), lambda i: (i // tpb, 0, 0))

    def row_spec(self, c):
        return pl.BlockSpec((self.rows, c), lambda i: (i, 0))

    def tile_spec(self, sub):
        return pl.BlockSpec((self.rows * sub, 128), lambda i: (i, 0))

    def pos_spec(self, c):
        tpb = self.tpb
        return pl.BlockSpec((self.rows, c), lambda i: (i % tpb, 0))


def _ada_kernel(c_ref, w_ref, b_ref, o_ref):
    o_ref[0] = _dot3(_silu(c_ref[...]), w_ref[0]) + b_ref[0]


def _ada(c_all, w_ada, b_ada):
    depth, d, d6 = w_ada.shape
    nc = c_all.shape[0]
    tn = 1536
    return pl.pallas_call(
        _ada_kernel,
        grid=(depth, d6 // tn),
        in_specs=[pl.BlockSpec((nc, d), lambda l, j: (0, 0)),
                  pl.BlockSpec((1, d, tn), lambda l, j: (l, 0, j)),
                  pl.BlockSpec((1, 1, tn), lambda l, j: (l, 0, j))],
        out_specs=pl.BlockSpec((1, nc, tn), lambda l, j: (l, 0, j)),
        out_shape=jax.ShapeDtypeStruct((depth, nc, d6), F32),
        compiler_params=_params("arbitrary", "arbitrary"),
        name="ada_modulation",
    )(c_all, w_ada, b_ada.reshape(depth, 1, d6))


def _norm_mod(x_ref, mod_ref, g_ref, shift_row, scale_row):
    x = x_ref[...]
    h = _rms(x) * g_ref[...]
    h = h * (1.0 + mod_ref[:, scale_row:scale_row + 1, :]) + mod_ref[:, shift_row:shift_row + 1, :]
    s, r, d = x.shape
    return h.reshape(s * r, d)


def _even_in_kernel(x_ref, mod_ref, g_ref, w_ref, an_ref, ws_ref, bs_ref,
                    oa_ref, v_ref, q_ref, k_ref, bv_ref, r_ref, bg_ref):
    h = _norm_mod(x_ref, mod_ref, g_ref, 0, 1).astype(BF16)
    z = _dot(h, w_ref[...])
    aw = A_GROUPS * CHUNK_A
    rows = z.shape[0]
    u = _gelu(z[:, 0:aw])
    va = _gelu(z[:, aw:2 * aw])
    vn = jnp.concatenate(
        [_rms(va[:, g * CHUNK_A:(g + 1) * CHUNK_A]) for g in range(A_GROUPS)], axis=1) * an_ref[...]
    v_ref[...] = vn
    vb = vn.astype(BF16)
    for c in range(rows // CHUNK_A):
        r0 = c * CHUNK_A
        mixed = jnp.concatenate(
            [_dot(ws_ref[g], vb[r0:r0 + CHUNK_A, g * CHUNK_A:(g + 1) * CHUNK_A]) for g in range(A_GROUPS)],
            axis=1) + bs_ref[...]
        oa_ref[r0:r0 + CHUNK_A, :] = u[r0:r0 + CHUNK_A, :] * mixed
    o = 2 * aw
    q_ref[...] = z[:, o:o + 256]
    k_ref[...] = z[:, o + 256:o + 512]
    bv_ref[...] = z[:, o + 512:o + 1024]
    r_ref[...] = z[:, o + 1024:o + 1536]
    bg_ref[...] = z[:, o + 1536:o + 1664]


def _even_in(x, mod, g, w, an, ws, bs):
    n, t, d = x.shape
    rt = _Rows(n, t)
    m = n * t
    widths = (512, 512, 256, 256, 512, 512, 128)
    return pl.pallas_call(
        _even_in_kernel,
        grid=(rt.steps,),
        in_specs=[rt.seq_spec(t, d), rt.seq_spec(6, d), _const_spec((1, d)), _const_spec(w.shape),
                  _const_spec(an.shape), _const_spec(ws.shape), _const_spec(bs.shape)],
        out_specs=[rt.row_spec(c) for c in widths],
        out_shape=[jax.ShapeDtypeStruct((m, c), F32) for c in widths],
        compiler_params=_params("arbitrary"),
        name="even_in_chunk_gate",
    )(x, mod, g, w, an, ws, bs)


def _gla_kernel(q_ref, k_ref, v_ref, r_ref, g_ref, wg_ref, gb_ref, ng_ref, s0_ref, seg_ref, eye_ref,
                o_ref, so_ref, s_scr, lg_scr, o_scr, *, chunk, n_chunks):
    seqs, _, hk = q_ref.shape
    dk = hk // B_HEADS
    dv = v_ref.shape[2] // B_HEADS
    t = pl.program_id(1)

    @pl.when(t == 0)
    def _():
        for sq in range(seqs):
            s_scr[sq] = s0_ref[sq].reshape(hk, dv)

    for sq in range(seqs):
        lg_scr[sq] = _log_sigmoid(_dot3(g_ref[sq], wg_ref[...]) + gb_ref[...]) * (1.0 / B_GATE_TAU)
    row = lax.broadcasted_iota(jnp.int32, (chunk, hk), 0)
    lane = lax.broadcasted_iota(jnp.int32, (chunk, hk), 1)
    pad = 16 - chunk if chunk < 16 else 0

    def pad_rows(a):
        if pad:
            return jnp.concatenate([a, jnp.zeros((pad, a.shape[1]), a.dtype)], axis=0)
        return a

    def one_chunk(sq, r0):
        lg = lg_scr[sq, pl.ds(r0, chunk), :]
        q = q_ref[sq, pl.ds(r0, chunk), :] * (dk ** -0.5)
        k = k_ref[sq, pl.ds(r0, chunk), :]
        v = v_ref[sq, pl.ds(r0, chunk), :]
        b = jnp.zeros_like(lg)
        for s in range(chunk):
            b = b + jnp.where(row >= s, lg[s:s + 1, :], 0.0)
        prods = []
        for l in range(chunk):
            e = jnp.exp(jnp.where(row <= l, b[l:l + 1, :] - b, NEG))
            prods.append(q[l:l + 1, :] * k * e)
        att = _dot_sel(jnp.concatenate(prods, axis=0), seg_ref[...])
        o_intra = jnp.concatenate(
            [jnp.sum(att[l * chunk:(l + 1) * chunk, :] * v, axis=0, keepdims=True) for l in range(chunk)], axis=0)
        s_old = s_scr[sq]
        s_b = s_old.astype(BF16)
        qe = q * jnp.exp(b)
        o_inter = jnp.concatenate(
            [_dot(jnp.where(lane // dk == h, qe, 0.0).astype(BF16), s_b) for h in range(B_HEADS)], axis=1)
        o_scr[sq, pl.ds(r0, chunk), :] = o_intra + o_inter
        b_last = b[chunk - 1:chunk, :]
        k_end = pad_rows((k * jnp.exp(b_last - b)).astype(BF16))
        k_t = _dot_nt(eye_ref[...], k_end).astype(BF16)
        dh, dm, dl = _split3(jnp.exp(b_last))
        dec3 = _dot_nt(eye_ref[...], jnp.concatenate([dh, dm, dl, jnp.zeros((13, hk), BF16)], axis=0))
        dec = dec3[:, 0:1] + dec3[:, 1:2] + dec3[:, 2:3]
        v_b = pad_rows(v.astype(BF16))
        kv = jnp.concatenate(
            [_dot(k_t[h * dk:(h + 1) * dk, :], v_b[:, h * dv:(h + 1) * dv]) for h in range(B_HEADS)], axis=0)
        s_scr[sq] = dec * s_old + kv

    def body(c, carry):
        r0 = pl.multiple_of(c * chunk, chunk)
        for sq in range(seqs):
            one_chunk(sq, r0)
        return carry

    lax.fori_loop(0, n_chunks, body, 0)

    for sq in range(seqs):
        o = o_scr[sq]
        on = jnp.concatenate([_rms(o[:, h * dv:(h + 1) * dv]) for h in range(B_HEADS)], axis=1) * ng_ref[...]
        o_ref[sq] = on * _silu(r_ref[sq])

    @pl.when(t == pl.num_programs(1) - 1)
    def _():
        for sq in range(seqs):
            so_ref[sq] = s_scr[sq].reshape(B_HEADS, dk, dv)


GLA_SEQS = 2


def _gla(n, t, q, k, v, r, g, wg, gb, ng, s0, seg, eye):
    tt = ROW_TILE if t % ROW_TILE == 0 else t
    chunk = CHUNK_B if tt % CHUNK_B == 0 else tt
    nt = t // tt
    hk, hv = q.shape[1], v.shape[1]
    dk, dv = hk // B_HEADS, hv // B_HEADS
    seqs = GLA_SEQS
    assert n % seqs == 0

    def rows(c):
        return pl.BlockSpec((seqs, tt, c), lambda i, j: (i, j, 0))

    def per_seq(a):
        return a.reshape(n, t, a.shape[1])

    st = pl.BlockSpec((seqs, B_HEADS, dk, dv), lambda i, j: (i, 0, 0, 0))
    o, s_new = pl.pallas_call(
        functools.partial(_gla_kernel, chunk=chunk, n_chunks=tt // chunk),
        grid=(n // seqs, nt),
        in_specs=[rows(hk), rows(hk), rows(hv), rows(hv), rows(g.shape[1]),
                  pl.BlockSpec(wg.shape, lambda i, j: (0, 0)), pl.BlockSpec(gb.shape, lambda i, j: (0, 0)),
                  pl.BlockSpec(ng.shape, lambda i, j: (0, 0)), st,
                  pl.BlockSpec(seg.shape, lambda i, j: (0, 0)), pl.BlockSpec(eye.shape, lambda i, j: (0, 0))],
        out_specs=[rows(hv), st],
        out_shape=[jax.ShapeDtypeStruct((n, t, hv), F32), jax.ShapeDtypeStruct(s0.shape, F32)],
        scratch_shapes=[pltpu.VMEM((seqs, hk, dv), F32), pltpu.VMEM((seqs, tt, hk), F32),
                        pltpu.VMEM((seqs, tt, hv), F32)],
        compiler_params=_params("arbitrary", "arbitrary"),
        name="gla_scan",
    )(per_seq(q), per_seq(k), per_seq(v), per_seq(r), per_seq(g), wg, gb, ng, s0, seg, eye)
    return o.reshape(n * t, hv), s_new


def _route(logits_t, bias_col):
    n_e, r = logits_t.shape
    per = n_e // N_EXPERT_GROUPS
    scores = _sigmoid(logits_t)
    biased = scores + bias_col
    sub = lax.broadcasted_iota(jnp.int32, (per, r), 0).astype(F32)
    ninf = -jnp.inf
    gs = []
    for g in range(N_EXPERT_GROUPS):
        blk = biased[g * per:(g + 1) * per, :]
        m1 = jnp.max(blk, axis=0, keepdims=True)
        i1 = jnp.min(jnp.where(blk == m1, sub, float(per)), axis=0, keepdims=True)
        m2 = jnp.max(jnp.where(sub == i1, ninf, blk), axis=0, keepdims=True)
        gs.append(m1 + m2)
    cur = jnp.concatenate(gs, axis=0)
    gsub = lax.broadcasted_iota(jnp.int32, (N_EXPERT_GROUPS, r), 0).astype(F32)
    gsel = jnp.zeros((N_EXPERT_GROUPS, r), F32)
    for _ in range(TOPK_GROUPS):
        m = jnp.max(cur, axis=0, keepdims=True)
        i = jnp.min(jnp.where(cur == m, gsub, float(N_EXPERT_GROUPS)), axis=0, keepdims=True)
        hit = gsub == i
        gsel = jnp.where(hit, 1.0, gsel)
        cur = jnp.where(hit, ninf, cur)
    cur = jnp.concatenate(
        [jnp.where(gsel[g:g + 1, :] > 0.5, biased[g * per:(g + 1) * per, :], ninf) for g in range(N_EXPERT_GROUPS)],
        axis=0)
    esub = lax.broadcasted_iota(jnp.int32, (n_e, r), 0).astype(F32)
    idx, wts, hits = [], [], []
    for _ in range(TOP_K):
        m = jnp.max(cur, axis=0, keepdims=True)
        i = jnp.min(jnp.where(cur == m, esub, float(n_e)), axis=0, keepdims=True)
        hit = esub == i
        idx.append(i)
        hits.append(hit)
        wts.append(jnp.sum(jnp.where(hit, scores, 0.0), axis=0, keepdims=True))
        cur = jnp.where(hit, ninf, cur)
    w = jnp.concatenate(wts, axis=0)
    w = w / jnp.sum(w, axis=0, keepdims=True) * ROUTED_SCALE
    return jnp.concatenate(idx, axis=0), w, hits


def _pack_pairs(x):
    w = x.shape[1] // 2
    hi = lax.bitcast_convert_type(x[:, :w].astype(BF16).astype(F32), jnp.uint32)
    lo = lax.bitcast_convert_type(x[:, w:].astype(BF16).astype(F32), jnp.uint32)
    return hi | (lo >> 16)


def _unpack_pairs(pk):
    a = lax.bitcast_convert_type(pk & jnp.uint32(0xFFFF0000), F32)
    b = lax.bitcast_convert_type(pk << 16, F32)
    return jnp.concatenate([a, b], axis=1)


def _store_row_tiles(ref, x):
    rows = x.shape[0]
    pk = _pack_pairs(x)
    sub = pk.shape[1] // 128
    for c in range(sub):
        ref[pl.ds(c, rows, stride=sub), :] = pk[:, c * 128:(c + 1) * 128]


def _load_row_tiles(ref, sub):
    rows = ref.shape[0] // sub
    return _unpack_pairs(jnp.concatenate([ref[pl.ds(c, rows, stride=sub), :] for c in range(sub)], axis=1))


def _out_kernel(x_ref, mod_ref, ma_ref, mb_ref, wo_ref, g_ref, wr_ref, br_ref, eye_ref, triu_ref,
                x1_ref, hp_ref, idx_ref, rank_ref, w3_ref, cnt_ref, run_scr):
    step = pl.program_id(0)

    @pl.when(step == 0)
    def _():
        run_scr[...] = jnp.zeros_like(run_scr)

    half = ma_ref.shape[1]
    y = _dot(ma_ref[...].astype(BF16), wo_ref[0:half, :]) + _dot(mb_ref[...].astype(BF16), wo_ref[half:, :])
    x = x_ref[...]
    s, r, d = x.shape
    x1 = x + mod_ref[:, 2:3, :] * y.reshape(s, r, d)
    x1_ref[...] = x1
    h2 = (_rms(x1) * g_ref[...] * (1.0 + mod_ref[:, 4:5, :]) + mod_ref[:, 3:4, :]).reshape(s * r, d)
    _store_row_tiles(hp_ref, h2)
    idx, w, hits = _route(_dot3_nt(wr_ref[...], h2), br_ref[...])
    idx_ref[0] = idx.astype(jnp.int32)
    sel = jnp.zeros(hits[0].shape, F32)
    for hit in hits:
        sel = jnp.where(hit, 1.0, sel)
    before = run_scr[:, 0:1] + _dot(sel.astype(BF16), triu_ref[...]) - sel
    rank_ref[0] = jnp.concatenate(
        [jnp.sum(jnp.where(hit, before, 0.0), axis=0, keepdims=True) for hit in hits], axis=0).astype(jnp.int32)
    run_scr[...] = run_scr[...] + jnp.sum(sel, axis=1, keepdims=True)
    cnt_ref[...] = run_scr[...]
    wh, wm, wl = _split3(w)
    stack = jnp.concatenate([wh, wm, wl, jnp.zeros((128 - 3 * TOP_K, s * r), BF16)], axis=0)
    w3_ref[...] = _dot_nt(eye_ref[...], stack)


def _out_route(x, mod, mix_a, mix_b, wo, g, wr_t, br, eye, triu):
    n, t, d = x.shape
    rt = _Rows(n, t)
    m = n * t
    n_e = wr_t.shape[0]
    slot = pl.BlockSpec((1, TOP_K, rt.rows), lambda i: (i, 0, 0))
    return pl.pallas_call(
        _out_kernel,
        grid=(rt.steps,),
        in_specs=[rt.seq_spec(t, d), rt.seq_spec(6, d), rt.row_spec(mix_a.shape[1]), rt.row_spec(mix_b.shape[1]),
                  _const_spec(wo.shape), _const_spec((1, d)), _const_spec(wr_t.shape), _const_spec(br.shape),
                  _const_spec(eye.shape), _const_spec(triu.shape)],
        out_specs=[rt.seq_spec(t, d), rt.tile_spec(d // 256), slot, slot, rt.row_spec(128), _const_spec((n_e, 128))],
        out_shape=[jax.ShapeDtypeStruct((n, t, d), F32), jax.ShapeDtypeStruct((m * (d // 256), 128), jnp.uint32),
                   jax.ShapeDtypeStruct((rt.steps, TOP_K, rt.rows), jnp.int32),
                   jax.ShapeDtypeStruct((rt.steps, TOP_K, rt.rows), jnp.int32),
                   jax.ShapeDtypeStruct((m, 128), F32), jax.ShapeDtypeStruct((n_e, 128), F32)],
        scratch_shapes=[pltpu.VMEM((n_e, 128), F32)],
        compiler_params=_params("arbitrary"),
        name="out_proj_route",
    )(x, mod, mix_a, mix_b, wo, g, wr_t, br, eye, triu)


ROW_SUB = 4


def _row_copy(src_ref, src_row, dst_ref, dst_row, sem):
    src = src_ref.at[pl.ds(pl.multiple_of(src_row * ROW_SUB, ROW_SUB), ROW_SUB), :]
    dst = dst_ref.at[pl.ds(pl.multiple_of(dst_row * ROW_SUB, ROW_SUB), ROW_SUB), :]
    return pltpu.make_async_copy(src, dst, sem)


def _dispatch_kernel(dest_ref, hp_ref, *rest):
    xs_ref, sem = rest[-2:]
    rows = hp_ref.shape[0] // ROW_SUB

    def issue(tok, carry):
        for k in range(TOP_K):
            _row_copy(hp_ref, tok, xs_ref, dest_ref[0, k, tok], sem).start(priority=k % 2)
        return carry

    lax.fori_loop(0, rows, issue, 0, unroll=2)

    def drain(j, carry):
        _row_copy(hp_ref, 0, xs_ref, 0, sem).wait()
        return carry

    lax.fori_loop(0, TOP_K * rows, drain, 0, unroll=16)


def _dispatch(dest3, hp, n_rows, xs_prev=None):
    steps, _, rows = dest3.shape
    assert rows & (rows - 1) == 0
    chained = xs_prev is not None
    return pl.pallas_call(
        _dispatch_kernel,
        grid=(steps,),
        in_specs=[pl.BlockSpec((1, TOP_K, rows), lambda i: (i, 0, 0), memory_space=pltpu.SMEM),
                  pl.BlockSpec((rows * ROW_SUB, 128), lambda i: (i, 0))]
                 + ([pl.BlockSpec(memory_space=pl.ANY)] if chained else []),
        out_specs=pl.BlockSpec(memory_space=pl.ANY),
        out_shape=jax.ShapeDtypeStruct((n_rows * ROW_SUB, 128), jnp.uint32),
        scratch_shapes=[pltpu.SemaphoreType.DMA(())],
        input_output_aliases={2: 0} if chained else {},
        compiler_params=_params("arbitrary"),
        name="moe_dispatch_rows",
    )(dest3, hp, *([xs_prev] if chained else []))


def _gmm_kernel(be_ref, va_ref, nu_ref, x_ref, wg_ref, wu_ref, wd_ref, y_ref):
    i = pl.program_id(0)

    @pl.when(i < nu_ref[0])
    def _():
        x = _load_row_tiles(x_ref, ROW_SUB)
        live = lax.broadcasted_iota(jnp.int32, x.shape, 0) < va_ref[i]
        x = jnp.where(live, x, 0.0).astype(BF16)
        a = _silu(_dot(x, wg_ref[...].astype(BF16))) * _dot(x, wu_ref[...].astype(BF16))
        _store_row_tiles(y_ref, _dot(a.astype(BF16), wd_ref[...].astype(BF16)))

    @pl.when(i >= nu_ref[0])
    def _():
        y_ref[...] = jnp.zeros_like(y_ref)


def _gmm(layer, block_e, valid, n_used, xs, wg, wu, wd, tm):
    r = xs.shape[0] // ROW_SUB
    d, de = wg.shape[-2:]
    assert d == ROW_SUB * 256
    nb = r // tm
    grid_spec = pltpu.PrefetchScalarGridSpec(
        num_scalar_prefetch=3,
        grid=(nb,),
        in_specs=[pl.BlockSpec((tm * ROW_SUB, 128), lambda i, be, va, nu: (i, 0)),
                  pl.BlockSpec((None, None, d, de), lambda i, be, va, nu: (layer, be[i], 0, 0)),
                  pl.BlockSpec((None, None, d, de), lambda i, be, va, nu: (layer, be[i], 0, 0)),
                  pl.BlockSpec((None, None, de, d), lambda i, be, va, nu: (layer, be[i], 0, 0))],
        out_specs=pl.BlockSpec((tm * ROW_SUB, 128), lambda i, be, va, nu: (i, 0)),
    )
    return pl.pallas_call(
        _gmm_kernel,
        grid_spec=grid_spec,
        out_shape=jax.ShapeDtypeStruct(xs.shape, jnp.uint32),
        compiler_params=_params("arbitrary"),
        name="moe_grouped_swiglu",
    )(block_e, valid, n_used, xs, wg, wu, wd)


def _combine_kernel(dest_ref, x_ref, mod_ref, hp_ref, w3_ref, wg_ref, wu_ref, wd_ref, ys_ref, o_ref, yg_scr, sem):
    rows = hp_ref.shape[0] // ROW_SUB

    def issue(tok, carry):
        for k in range(TOP_K):
            _row_copy(ys_ref, dest_ref[0, k, tok], yg_scr.at[k], tok, sem).start(priority=k % 2)
        return carry

    lax.fori_loop(0, rows, issue, 0, unroll=2)
    h = _load_row_tiles(hp_ref, ROW_SUB).astype(BF16)
    a = _silu(_dot(h, wg_ref[...])) * _dot(h, wu_ref[...])
    acc = _dot(a.astype(BF16), wd_ref[...])

    def drain(j, carry):
        _row_copy(ys_ref, 0, yg_scr.at[0], 0, sem).wait()
        return carry

    lax.fori_loop(0, TOP_K * rows, drain, 0, unroll=16)
    w3 = w3_ref[...]
    for k in range(TOP_K):
        wk = w3[:, k:k + 1] + w3[:, TOP_K + k:TOP_K + k + 1] + w3[:, 2 * TOP_K + k:2 * TOP_K + k + 1]
        acc = acc + wk * _load_row_tiles(yg_scr.at[k], ROW_SUB)
    x = x_ref[...]
    o_ref[...] = x + mod_ref[:, 5:6, :] * acc.reshape(x.shape)


def _combine(dest3, x1, mod, hp, w3, wg, wu, wd, ys):
    n, t, d = x1.shape
    rt = _Rows(n, t)
    assert rt.rows & (rt.rows - 1) == 0
    return pl.pallas_call(
        _combine_kernel,
        grid=(rt.steps,),
        in_specs=[pl.BlockSpec((1, TOP_K, rt.rows), lambda i: (i, 0, 0), memory_space=pltpu.SMEM),
                  rt.seq_spec(t, d), rt.seq_spec(6, d), rt.tile_spec(ROW_SUB), rt.row_spec(128),
                  _const_spec(wg.shape), _const_spec(wu.shape), _const_spec(wd.shape),
                  pl.BlockSpec(memory_space=pl.ANY)],
        out_specs=rt.seq_spec(t, d),
        out_shape=jax.ShapeDtypeStruct((n, t, d), F32),
        scratch_shapes=[pltpu.VMEM((TOP_K, rt.rows * ROW_SUB, 128), jnp.uint32), pltpu.SemaphoreType.DMA(())],
        compiler_params=_params("arbitrary"),
        name="moe_combine_shared",
    )(dest3, x1, mod, hp, w3, wg, wu, wd, ys)


def _moe(layer, groups, p):
    n_e = p['w_gate_e'].shape[1]
    a = sum(g[1][2].size for g in groups)
    tm = 512 if a // n_e >= 1024 else 128
    counts = [g[1][5][:, 0].astype(jnp.int32) for g in groups]
    total = sum(counts)
    padded = (total + tm - 1) // tm * tm
    pend = jnp.cumsum(padded)
    pstart = pend - padded
    experts = jnp.arange(n_e, dtype=jnp.int32)
    dests = []
    base = pstart
    for (_, routed), cnt in zip(groups, counts):
        onehot = routed[2][None] == experts[:, None, None, None]
        dests.append(routed[3] + jnp.sum(jnp.where(onehot, base[:, None, None, None], 0), axis=0))
        base = base + cnt
    n_blocks = -(-a // tm) + n_e
    first = jnp.arange(n_blocks, dtype=jnp.int32) * tm
    block_e = jnp.minimum(jnp.sum((pend[None, :] <= first[:, None]).astype(jnp.int32), axis=1), n_e - 1)
    mine = block_e[:, None] == experts[None, :]
    last = jnp.sum(jnp.where(mine, (pstart + total)[None, :], 0), axis=1)
    valid = jnp.clip(last - first, 0, tm).astype(jnp.int32)
    n_used = (pend[-1] // tm).astype(jnp.int32).reshape(1)
    xs = None
    for (_, routed), dest3 in zip(groups, dests):
        xs = _dispatch(dest3, routed[1], n_blocks * tm, xs)
    ys = _gmm(layer, block_e, valid, n_used, xs, p['w_gate_e'], p['w_up_e'], p['w_down_e'], tm)
    shared = [p[name][layer].astype(BF16) for name in ('w_gate_s', 'w_up_s', 'w_down_s')]
    return [_combine(dest3, routed[0], mod, routed[1], routed[4], *shared, ys)
            for (mod, routed), dest3 in zip(groups, dests)]


def _odd_in_kernel(x_ref, mod_ref, g_ref, w_ref, gqa_ref, wqb_ref, gqn_ref, gqr_ref, gqrs_ref, gkva_ref,
                   gkr_ref, gkrs_ref, gkn_ref, cos_ref, sin_ref, seg_ref, wuk_ref, wuv_ref,
                   cq_ref, ck_ref, cv_ref, co_ref, gt_ref, ckv_ref, kr_ref, *outs, prompt):
    h = _norm_mod(x_ref, mod_ref, g_ref, 0, 1).astype(BF16)
    z = _dot(h, w_ref[...])
    cq_ref[...] = z[:, 0:512]
    ck_ref[...] = z[:, 512:1024]
    cv_ref[...] = z[:, 1024:1536]
    co_ref[...] = z[:, 1536:2048]
    gt_ref[...] = z[:, 2560:2688]
    cos = cos_ref[...]
    sin = sin_ref[...]
    ckv = _rms(z[:, 2304:2432]) * gkva_ref[...]
    ckv_ref[...] = ckv
    xr = z[:, 2432:2496]
    xr_rot = z[:, 2496:2560]
    rr = lax.rsqrt(jnp.mean(xr * xr, axis=-1, keepdims=True) + EPS)
    kr = rr * (xr * gkr_ref[...] * cos[:, 0:D_ROPE] + xr_rot * gkrs_ref[...] * sin[:, 0:D_ROPE])
    kr_ref[...] = kr
    qa = (_rms(z[:, 2048:2304]) * gqa_ref[...]).astype(BF16)
    qd = _dot(qa, wqb_ref[...])
    nw = D_HEADS * D_NOPE
    rw = D_HEADS * D_ROPE
    qn = [_rms(qd[:, hh * D_NOPE:(hh + 1) * D_NOPE]) * gqn_ref[...] * MLA_SCALE for hh in range(D_HEADS)]
    xq = qd[:, nw:nw + rw]
    xq_rot = qd[:, nw + rw:nw + 2 * rw]
    rq = lax.rsqrt(_dot_sel(xq * xq, seg_ref[...]) * (1.0 / D_ROPE) + EPS)
    qr = rq * (xq * gqr_ref[...] * cos + xq_rot * gqrs_ref[...] * sin) * MLA_SCALE
    ckv_b = ckv.astype(BF16)
    if prompt:
        qcat_ref, kcat_ref, v_ref = outs
        kn = _dot(ckv_b, wuk_ref[...])
        zpad = jnp.zeros((z.shape[0], 256 - D_NOPE - D_ROPE), F32)
        qcat_ref[...] = jnp.concatenate(
            [piece for hh in range(D_HEADS) for piece in (qn[hh], qr[:, hh * D_ROPE:(hh + 1) * D_ROPE], zpad)],
            axis=1).astype(BF16)
        kcat_ref[...] = jnp.concatenate(
            [piece for hh in range(D_HEADS)
             for piece in (_rms(kn[:, hh * D_NOPE:(hh + 1) * D_NOPE]) * gkn_ref[...], kr, zpad)],
            axis=1).astype(BF16)
        v_ref[...] = _dot(ckv_b, wuv_ref[...]).astype(BF16)
    else:
        u_ref, qr_ref = outs
        u_ref[...] = jnp.concatenate(
            [_dot_nt((qn[hh] * gkn_ref[...]).astype(BF16), wuk_ref[:, hh * D_NOPE:(hh + 1) * D_NOPE])
             for hh in range(D_HEADS)], axis=1).astype(BF16)
        qr_ref[...] = qr.astype(BF16)


def _odd_in(x, mod, g, w, consts, cos, sin, prompt):
    n, t, d = x.shape
    rt = _Rows(n, t)
    m = n * t
    widths = [(512, F32)] * 4 + [(128, F32), (128, F32), (D_ROPE, F32)]
    if prompt:
        widths += [(1024, BF16), (1024, BF16), (512, BF16)]
    else:
        widths += [(512, BF16), (256, BF16)]
    return pl.pallas_call(
        functools.partial(_odd_in_kernel, prompt=prompt),
        grid=(rt.steps,),
        in_specs=[rt.seq_spec(t, d), rt.seq_spec(6, d), _const_spec((1, d)), _const_spec(w.shape)]
                 + [_const_spec(c.shape) for c in consts[:9]]
                 + [rt.pos_spec(cos.shape[1]), rt.pos_spec(sin.shape[1])]
                 + [_const_spec(c.shape) for c in consts[9:]],
        out_specs=[rt.row_spec(c) for c, _ in widths],
        out_shape=[jax.ShapeDtypeStruct((m, c), dt) for c, dt in widths],
        compiler_params=_params("arbitrary"),
        name="odd_in_latent_prep",
    )(x, mod, g, w, *consts[:9], cos, sin, *consts[9:])


def _mlstm_kernel(q_ref, k_ref, v_ref, og_ref, gt_ref, gb_ref, ng_ref, c0_ref, n0_ref, m0_ref, tri_ref, eye_ref,
                  o_ref, co_ref, no_ref, mo_ref, c_scr, n_scr, m_scr, *, chunk):
    dh = q_ref.shape[1] // C_HEADS
    t = pl.program_id(1)

    @pl.when(t == 0)
    def _():
        c_scr[...] = c0_ref[0]
        n_scr[...] = n0_ref[0]
        m_scr[...] = m0_ref[0]

    pre = gt_ref[...] + gb_ref[...]
    lf = _log_sigmoid(pre)
    if chunk >= 16:
        l_hi, l_mid, l_lo = _split3(lf)
        f_cum = _dot(tri_ref[...], l_hi) + _dot(tri_ref[...], l_mid) + _dot(tri_ref[...], l_lo)
    else:
        rowg = lax.broadcasted_iota(jnp.int32, lf.shape, 0)
        f_cum = jnp.zeros_like(lf)
        for s in range(chunk):
            f_cum = f_cum + jnp.where(rowg >= s, lf[s:s + 1, :], 0.0)
    ri = lax.broadcasted_iota(jnp.int32, (chunk, chunk), 0)
    ci = lax.broadcasted_iota(jnp.int32, (chunk, chunk), 1)
    pad = 16 - chunk if chunk < 16 else 0

    def pad_rows(a):
        if pad:
            return jnp.concatenate([a, jnp.zeros((pad, a.shape[1]), a.dtype)], axis=0)
        return a

    for h in range(C_HEADS):
        sl = slice(h * dh, (h + 1) * dh)
        q = q_ref[:, sl]
        k = k_ref[:, sl] * (dh ** -0.5)
        v = v_ref[:, sl]
        qb, kb, vb = q.astype(BF16), k.astype(BF16), v.astype(BF16)
        f_col = f_cum[:, C_HEADS + h:C_HEADS + h + 1]
        i_col = pre[:, h:h + 1]
        m_prev = m_scr[h:h + 1, 0:1]
        a_row = jnp.sum(jnp.where(ri == ci, i_col - f_col, 0.0), axis=0, keepdims=True)
        log_d = jnp.where(ci <= ri, f_col + a_row, NEG)
        inter = f_col + m_prev
        m_t = jnp.maximum(inter, jnp.max(log_d, axis=-1, keepdims=True))
        w_inter = jnp.exp(inter - m_t)
        qk = _dot_nt(qb, kb) * jnp.exp(log_d - m_t)
        c_old = c_scr[h]
        n_old = n_scr[h:h + 1, :]
        num = _dot(qk.astype(BF16), vb) + w_inter * _dot_nt(qb, c_old.astype(BF16))
        den = jnp.sum(qk, axis=-1, keepdims=True) + w_inter * jnp.sum(q * n_old, axis=-1, keepdims=True)
        hh = num / jnp.maximum(jnp.abs(den), jnp.exp(-m_t))
        o_ref[:, sl] = _rms(hh) * ng_ref[...] * _sigmoid(og_ref[:, sl])
        f_last = f_col[chunk - 1:chunk, :]
        a_end = f_last - f_col + i_col
        m_new = jnp.maximum(f_last + m_prev, jnp.max(a_end, axis=0, keepdims=True))
        w = jnp.exp(a_end - m_new)
        dec = jnp.exp(f_last + m_prev - m_new)
        wv_t = _dot_nt(eye_ref[...], pad_rows((w * v).astype(BF16))).astype(BF16)
        c_scr[h] = dec * c_old + _dot(wv_t, pad_rows(kb))
        n_scr[h:h + 1, :] = dec * n_old + jnp.sum(w * k, axis=0, keepdims=True)
        m_scr[h:h + 1, :] = jnp.broadcast_to(m_new, (1, m_scr.shape[1]))

    @pl.when(t == pl.num_programs(1) - 1)
    def _():
        co_ref[0] = c_scr[...]
        no_ref[0] = n_scr[...]
        mo_ref[0] = m_scr[...]


def _mlstm(n, t, q, k, v, og, gt, gb, ng, c0, n0, m0, tri, eye):
    chunk = CHUNK_C if t % CHUNK_C == 0 else t
    nt = t // chunk
    w = q.shape[1]
    dh = w // C_HEADS

    def rows(c):
        return pl.BlockSpec((chunk, c), lambda i, j: (i * nt + j, 0))

    cst = pl.BlockSpec((1, C_HEADS, dh, dh), lambda i, j: (i, 0, 0, 0))
    nst = pl.BlockSpec((1, C_HEADS, dh), lambda i, j: (i, 0, 0))
    return pl.pallas_call(
        functools.partial(_mlstm_kernel, chunk=chunk),
        grid=(n, nt),
        in_specs=[rows(w), rows(w), rows(w), rows(w), rows(128),
                  pl.BlockSpec(gb.shape, lambda i, j: (0, 0)), pl.BlockSpec(ng.shape, lambda i, j: (0, 0)),
                  cst, nst, nst,
                  pl.BlockSpec(tri.shape, lambda i, j: (0, 0)), pl.BlockSpec(eye.shape, lambda i, j: (0, 0))],
        out_specs=[rows(w), cst, nst, nst],
        out_shape=[jax.ShapeDtypeStruct((n * t, w), F32), jax.ShapeDtypeStruct(c0.shape, F32),
                   jax.ShapeDtypeStruct(n0.shape, F32), jax.ShapeDtypeStruct(m0.shape, F32)],
        scratch_shapes=[pltpu.VMEM((C_HEADS, dh, dh), F32), pltpu.VMEM((C_HEADS, dh), F32),
                        pltpu.VMEM((C_HEADS, dh), F32)],
        compiler_params=_params("arbitrary", "arbitrary"),
        name="mlstm_scan",
    )(q, k, v, og, gt, gb, ng, c0, n0, m0, tri, eye)


def _flash_kernel(qi_ref, kj_ref, q_ref, k_ref, v_ref, o_ref, m_scr, l_scr, acc_scr):
    p = pl.program_id(2)
    qi = qi_ref[p]
    kj = kj_ref[p]

    @pl.when(kj == 0)
    def _():
        m_scr[...] = jnp.full_like(m_scr, NEG)
        l_scr[...] = jnp.zeros_like(l_scr)
        acc_scr[...] = jnp.zeros_like(acc_scr)

    def step(masked):
        s = _dot_nt(q_ref[...], k_ref[...])
        bq, bk = s.shape
        if masked:
            ri = lax.broadcasted_iota(jnp.int32, (bq, bk), 0)
            ci = lax.broadcasted_iota(jnp.int32, (bq, bk), 1)
            s = jnp.where(ci <= ri, s, NEG)
        lanes = m_scr.shape[1]
        m_old = m_scr[...]
        m_new = jnp.maximum(m_old, jnp.max(s, axis=-1, keepdims=True))
        alpha = jnp.exp(m_old - m_new)
        pr = jnp.exp(s - jnp.concatenate([m_new] * (bk // lanes), axis=1))
        l_scr[...] = alpha * l_scr[...] + jnp.sum(pr, axis=-1, keepdims=True)
        acc_scr[...] = alpha * acc_scr[...] + _dot(pr.astype(BF16), v_ref[...])
        m_scr[...] = m_new

    @pl.when(kj < qi)
    def _():
        step(False)

    @pl.when(kj == qi)
    def _():
        step(True)
        o_ref[...] = acc_scr[...] / l_scr[...]


def _flash(n, t, qcat, kcat, v):
    blk = FLASH_BLOCK if t % FLASH_BLOCK == 0 else t
    nq = t // blk
    pairs = [(i, j) for i in range(nq) for j in range(i + 1)]
    qi = jnp.asarray([a for a, _ in pairs], jnp.int32)
    kj = jnp.asarray([b for _, b in pairs], jnp.int32)
    dv = v.shape[1] // D_HEADS
    grid_spec = pltpu.PrefetchScalarGridSpec(
        num_scalar_prefetch=2,
        grid=(n, D_HEADS, len(pairs)),
        in_specs=[pl.BlockSpec((blk, 256), lambda b, h, p, qi, kj: (b * nq + qi[p], h)),
                  pl.BlockSpec((blk, 256), lambda b, h, p, qi, kj: (b * nq + kj[p], h)),
                  pl.BlockSpec((blk, dv), lambda b, h, p, qi, kj: (b * nq + kj[p], h))],
        out_specs=pl.BlockSpec((blk, dv), lambda b, h, p, qi, kj: (b * nq + qi[p], h)),
        scratch_shapes=[pltpu.VMEM((blk, dv), F32), pltpu.VMEM((blk, dv), F32), pltpu.VMEM((blk, dv), F32)],
    )
    assert blk % dv == 0
    return pl.pallas_call(
        _flash_kernel,
        grid_spec=grid_spec,
        out_shape=jax.ShapeDtypeStruct((n * t, v.shape[1]), F32),
        compiler_params=_params("arbitrary", "arbitrary", "arbitrary"),
        name="mla_prompt_flash",
    )(qi, kj, qcat, kcat, v)


def _paged_kernel(pt_ref, u_ref, qr_ref, cn_ref, kn_ref, wukt_ref, wuv_ref, ckv_hbm, krt_hbm,
                  o_ref, ck_buf, kr_buf, sem, m_scr, l_scr, acc_scr, *, pages, li, n_groups, n_steps):
    b = pl.program_id(0)
    g = pl.program_id(1)
    hq = u_ref.shape[1]
    tq = hq // D_HEADS
    step = b * n_groups + g
    slot = lax.rem(step, 2)

    def page_copies(seq, group, sl, lookup):
        copies = []
        for i in range(pages):
            page = pt_ref[seq, group * pages + i] if lookup else 0
            rows = pl.ds(i * PAGE_SIZE, PAGE_SIZE)
            copies.append(pltpu.make_async_copy(ckv_hbm.at[page, li], ck_buf.at[sl, rows, :], sem.at[sl]))
            copies.append(pltpu.make_async_copy(krt_hbm.at[page, li], kr_buf.at[sl, :, rows], sem.at[sl]))
        return copies

    @pl.when(step == 0)
    def _():
        for cp in page_copies(0, 0, 0, True):
            cp.start()

    @pl.when(step + 1 < n_steps)
    def _():
        nxt = step + 1
        for cp in page_copies(lax.div(nxt, n_groups), lax.rem(nxt, n_groups), 1 - slot, True):
            cp.start()

    for cp in page_copies(0, 0, slot, False):
        cp.wait()

    @pl.when(g == 0)
    def _():
        m_scr[...] = jnp.full_like(m_scr, NEG)
        l_scr[...] = jnp.zeros_like(l_scr)
        acc_scr[...] = jnp.zeros_like(acc_scr)

    lhs = jnp.concatenate([wukt_ref[...], u_ref[0]], axis=0)
    qr = qr_ref[0]
    nk = D_HEADS * D_NOPE

    def scores(ck_b, kr_b, kr_keys_minor=True):
        big = _dot_nt(lhs, ck_b)
        rows = []
        for h in range(D_HEADS):
            kn_t = big[h * D_NOPE:(h + 1) * D_NOPE, :]
            rinv = lax.rsqrt(jnp.sum(kn_t * kn_t, axis=0, keepdims=True) * (1.0 / D_NOPE) + EPS)
            rows.append(big[nk + h * tq:nk + (h + 1) * tq, :] * rinv)
        rope = _dot(qr, kr_b) if kr_keys_minor else _dot_nt(qr, kr_b)
        return jnp.concatenate(rows, axis=0) + rope

    def update(s, ck_b):
        m_old = m_scr[...]
        m_new = jnp.maximum(m_old, jnp.max(s, axis=-1, keepdims=True))
        alpha = jnp.exp(m_old - m_new)
        pr = jnp.exp(s - m_new)
        l_scr[...] = alpha * l_scr[...] + jnp.sum(pr, axis=-1, keepdims=True)
        acc_scr[...] = alpha * acc_scr[...] + _dot(pr.astype(BF16), ck_b)
        m_scr[...] = m_new

    span = 2 * PAGE_SIZE
    cks = [ck_buf[slot, pl.ds(i * span, span), :].astype(BF16) for i in range(pages // 2)]
    kr_all = kr_buf[slot].astype(BF16)
    s_all = jnp.concatenate(
        [scores(cks[i], kr_all[:, i * span:(i + 1) * span]) for i in range(pages // 2)], axis=1)
    update(s_all, jnp.concatenate(cks, axis=0))

    @pl.when(g == pl.num_programs(1) - 1)
    def _():
        fill = PAGE_SIZE - tq
        ck_b = jnp.concatenate([cn_ref[...], jnp.zeros((fill, cn_ref.shape[1]), F32)], axis=0).astype(BF16)
        kr_b = jnp.concatenate([kn_ref[...], jnp.zeros((fill, kn_ref.shape[1]), F32)], axis=0).astype(BF16)
        s = scores(ck_b, kr_b, kr_keys_minor=False)
        ri = lax.broadcasted_iota(jnp.int32, s.shape, 0)
        ci = lax.broadcasted_iota(jnp.int32, s.shape, 1)
        update(jnp.where(ci <= ri % tq, s, NEG), ck_b)
        lat = (acc_scr[...] / l_scr[...]).astype(BF16)
        full = _dot(lat, wuv_ref[...])
        dv = wuv_ref.shape[1] // D_HEADS
        o_ref[...] = jnp.concatenate(
            [full[h * tq:(h + 1) * tq, h * dv:(h + 1) * dv] for h in range(D_HEADS)], axis=1)


def _paged(page_table, li, u3, qr3, ckv_new, kr_new, wuk_t, wuv, cache_ckv, cache_kr):
    n, hq, lat = u3.shape
    tq = hq // D_HEADS
    n_pages = page_table.shape[1]
    pages = min(PAGES_PER_STEP, n_pages)
    assert n_pages % pages == 0 and pages % 2 == 0
    ng = n_pages // pages
    cache_kr_t = jnp.swapaxes(cache_kr, 2, 3)
    keys = pages * PAGE_SIZE

    grid_spec = pltpu.PrefetchScalarGridSpec(
        num_scalar_prefetch=1,
        grid=(n, ng),
        in_specs=[pl.BlockSpec((1, hq, lat), lambda b, g, pt: (b, 0, 0)),
                  pl.BlockSpec((1, hq, D_ROPE), lambda b, g, pt: (b, 0, 0)),
                  pl.BlockSpec((tq, lat), lambda b, g, pt: (b, 0)),
                  pl.BlockSpec((tq, D_ROPE), lambda b, g, pt: (b, 0)),
                  pl.BlockSpec(wuk_t.shape, lambda b, g, pt: (0, 0)),
                  pl.BlockSpec(wuv.shape, lambda b, g, pt: (0, 0)),
                  pl.BlockSpec(memory_space=pl.ANY), pl.BlockSpec(memory_space=pl.ANY)],
        out_specs=pl.BlockSpec((tq, wuv.shape[1]), lambda b, g, pt: (b, 0)),
        scratch_shapes=[pltpu.VMEM((2, keys, lat), F32), pltpu.VMEM((2, D_ROPE, keys), F32),
                        pltpu.SemaphoreType.DMA((2,)),
                        pltpu.VMEM((hq, 1), F32), pltpu.VMEM((hq, 1), F32), pltpu.VMEM((hq, lat), F32)],
    )
    return pl.pallas_call(
        functools.partial(_paged_kernel, pages=pages, li=li, n_groups=ng, n_steps=n * ng),
        grid_spec=grid_spec,
        out_shape=jax.ShapeDtypeStruct((n * tq, wuv.shape[1]), F32),
        compiler_params=_params("arbitrary", "arbitrary"),
        name="mla_sample_paged",
    )(page_table, u3, qr3, ckv_new, kr_new, wuk_t, wuv, cache_ckv, cache_kr_t)


def _np_seg(n_seg, seg_in, seg_out):
    mat = np.zeros((n_seg * seg_in, n_seg * seg_out), np.float32)
    for s in range(n_seg):
        mat[s * seg_in:(s + 1) * seg_in, s * seg_out:(s + 1) * seg_out] = 1.0
    return mat


def _rot_cols(w):
    half = w.shape[-1] // 2
    return jnp.concatenate([-w[..., half:], w[..., :half]], axis=-1)


def _swap_halves(g):
    half = g.shape[-1] // 2
    return jnp.concatenate([g[..., half:], g[..., :half]], axis=-1)


def _rope_tables(pos, reps):
    half = D_ROPE // 2
    inv = ROPE_BASE ** (-jnp.arange(half, dtype=F32) / half)
    ang = pos.astype(F32)[:, None] * inv
    cos = jnp.concatenate([jnp.cos(ang), jnp.cos(ang)], axis=-1)
    sin = jnp.concatenate([jnp.sin(ang), jnp.sin(ang)], axis=-1)
    return jnp.tile(cos, (1, reps)), jnp.tile(sin, (1, reps))


def _trunk(x, c_mod, pos0, gla0, mc0, mn0, mm0, p, sample_ctx):
    n, t, d = x.shape
    m = n * t
    rt = _Rows(n, t)
    eye_r = jnp.eye(rt.rows, dtype=BF16)
    triu_r = jnp.asarray(np.triu(np.ones((rt.rows, rt.rows), np.float32)), BF16)
    dh = mc0.shape[3]
    eye_hk = jnp.eye(B_HEADS * gla0.shape[3], dtype=BF16)
    eye_dh = jnp.eye(dh, dtype=BF16)
    results = {}

    layer, li = 0, 0
    mod = c_mod[layer]
    w_in = p['w_in_even'][li]
    w_in = jnp.concatenate([w_in, jnp.zeros((d, 128 - B_GATE_RANK), F32)], axis=1).astype(BF16)
    a_ws = jnp.tril(p['a_ws'][li])
    a_bs = p['a_bs'][li]
    if t % CHUNK_A == 0:
        ws = a_ws
        bs = jnp.repeat(a_bs.T, CHUNK_A, axis=1)
    else:
        ws = jnp.stack([jnp.kron(jnp.eye(CHUNK_A // t, dtype=F32), a_ws[g, :t, :t]) for g in range(A_GROUPS)])
        bs = jnp.repeat(jnp.tile(a_bs[:, :t], (1, CHUNK_A // t)).T, CHUNK_A, axis=1)
    out_a, v_rows, bq, bk, bv, br, bg = _even_in(
        x, mod, p['norm_mix_g'][layer].reshape(1, d), w_in, p['a_norm_g'][li].reshape(1, -1),
        ws.astype(BF16), bs)
    hk = bq.shape[1]
    dv = bv.shape[1] // B_HEADS
    wg2 = jnp.concatenate([p['b_w_gate2'][li], jnp.zeros((128 - B_GATE_RANK, hk), F32)], axis=0)
    seg = jnp.asarray(_np_seg(B_HEADS, hk // B_HEADS, dv), BF16)
    out_b, s_new = _gla(n, t, bq, bk, bv, br, bg, wg2, p['b_gate_bias'][li].reshape(1, hk),
                        jnp.tile(p['b_norm_g'][li], B_HEADS).reshape(1, -1), gla0[li], seg, eye_hk)
    results['gla'] = s_new
    results['v_rows'] = v_rows
    routed = _out_route(
        x, mod, out_a, out_b, p['w_out'][layer].astype(BF16), p['norm_ffn_g'][layer].reshape(1, d),
        p['w_router'][layer].T, p['b_router'][layer].reshape(-1, 1), eye_r, triu_r)
    x = yield mod, routed

    layer, li = 1, 0
    mod = c_mod[layer]
    w = p['w_in_odd'][li]
    hw = C_HEADS * dh
    o_g = 3 * hw
    o_o = o_g + 2 * C_HEADS
    o_qa = o_o + hw
    o_kva = o_qa + p['d_g_qa'].shape[1]
    o_kr = o_kva + p['d_g_kva'].shape[1]
    w_kr = w[:, o_kr:o_kr + D_ROPE]
    w_odd = jnp.concatenate(
        [w[:, :o_g], w[:, o_o:o_qa], w[:, o_qa:o_kva], w[:, o_kva:o_kr], w_kr, _rot_cols(w_kr),
         w[:, o_g:o_o], jnp.zeros((d, 128 - 2 * C_HEADS), F32)], axis=1).astype(BF16)
    wqb = p['d_w_qb'][li].reshape(-1, D_HEADS, D_NOPE + D_ROPE)
    wqb_r = wqb[:, :, D_NOPE:]
    wqb2 = jnp.concatenate([wqb[:, :, :D_NOPE].reshape(-1, D_HEADS * D_NOPE),
                            wqb_r.reshape(-1, D_HEADS * D_ROPE),
                            _rot_cols(wqb_r).reshape(-1, D_HEADS * D_ROPE)], axis=1).astype(BF16)
    w_uk = p['d_w_uk'][li]
    lat = w_uk.shape[0]
    g_qr = p['d_g_qr'][li]
    g_kr = p['d_g_kr'][li]
    cos, sin = _rope_tables(pos0 + jnp.arange(t), D_HEADS)
    prompt = sample_ctx is None
    if not prompt:
        cos, sin = jnp.tile(cos, (rt.s, 1)), jnp.tile(sin, (rt.s, 1))
    consts = [p['d_g_qa'][li].reshape(1, -1), wqb2, p['d_g_qn'][li].reshape(1, -1),
              jnp.tile(g_qr, D_HEADS).reshape(1, -1), jnp.tile(_swap_halves(g_qr), D_HEADS).reshape(1, -1),
              p['d_g_kva'][li].reshape(1, -1), g_kr.reshape(1, -1), _swap_halves(g_kr).reshape(1, -1),
              p['d_g_kn'][li].reshape(1, -1),
              jnp.asarray(_np_seg(D_HEADS, D_ROPE, D_ROPE), BF16),
              w_uk.reshape(lat, -1).astype(BF16), p['d_w_uv'][li].reshape(lat, -1).astype(BF16)]
    outs = _odd_in(x, mod, p['norm_mix_g'][layer].reshape(1, d), w_odd, consts, cos, sin, prompt)
    cq, ck, cv, co, gates, ckv, kr = outs[:7]
    gb = jnp.concatenate([p['c_ig_bias'][li], p['c_fg_bias'][li],
                          jnp.zeros((128 - 2 * C_HEADS,), F32)]).reshape(1, 128)
    chunk = CHUNK_C if t % CHUNK_C == 0 else t
    tri = jnp.asarray(np.tril(np.ones((chunk, chunk), np.float32)), BF16)
    m0b =jnp.broadcast_to(mm0[li][:, :, None], (n, C_HEADS, dh))
    out_c, c_new, n_new, m_new = _mlstm(n, t, cq, ck, cv, co, gates, gb,
                                        p['c_norm_g'][li].reshape(1, -1), mc0[li], mn0[li], m0b, tri, eye_dh)
    results['mlstm'] = (c_new, n_new, m_new[:, :, 0])
    results['ckv'] = ckv.reshape(n, t, -1)
    results['kr'] = kr.reshape(n, t, -1)
    if prompt:
        qcat, kcat, vv = outs[7:]
        out_d = _flash(n, t, qcat, kcat, vv)
    else:
        u, qr = outs[7:]
        cache_ckv, cache_kr, page_table = sample_ctx
        u3 = u.reshape(n, t, D_HEADS, lat).transpose(0, 2, 1, 3).reshape(n, D_HEADS * t, lat)
        qr3 = qr.reshape(n, t, D_HEADS, D_ROPE).transpose(0, 2, 1, 3).reshape(n, D_HEADS * t, D_ROPE)
        wuk_t = w_uk.transpose(1, 2, 0).reshape(-1, lat).astype(BF16)
        out_d = _paged(page_table, li, u3, qr3, ckv, kr, wuk_t, consts[11], cache_ckv, cache_kr)
    routed = _out_route(
        x, mod, out_c, out_d, p['w_out'][layer].astype(BF16), p['norm_ffn_g'][layer].reshape(1, d),
        p['w_router'][layer].T, p['b_router'][layer].reshape(-1, 1), eye_r, triu_r)
    x = yield mod, routed
    return x, results


def kernel(x_prompt, x_sample, state_gla, state_mlstm_c, state_mlstm_n, state_mlstm_m,
           cache_ckv, cache_krope, page_table, c_prompt, c_sample,
           norm_mix_g, norm_ffn_g, w_ada, b_ada, w_out,
           w_in_even, a_norm_g, a_ws, a_bs, b_w_gate2, b_gate_bias, b_norm_g,
           w_in_odd, c_ig_bias, c_fg_bias, c_norm_g,
           d_g_qa, d_w_qb, d_g_kva, d_g_qn, d_g_qr, d_g_kr, d_g_kn, d_w_uk, d_w_uv,
           w_router, b_router, w_gate_e, w_up_e, w_down_e, w_gate_s, w_up_s, w_down_s):
    p = dict(norm_mix_g=norm_mix_g, norm_ffn_g=norm_ffn_g, w_out=w_out,
             w_in_even=w_in_even, a_norm_g=a_norm_g, a_ws=a_ws, a_bs=a_bs, b_w_gate2=b_w_gate2,
             b_gate_bias=b_gate_bias, b_norm_g=b_norm_g, w_in_odd=w_in_odd, c_ig_bias=c_ig_bias,
             c_fg_bias=c_fg_bias, c_norm_g=c_norm_g, d_g_qa=d_g_qa, d_w_qb=d_w_qb, d_g_kva=d_g_kva,
             d_g_qn=d_g_qn, d_g_qr=d_g_qr, d_g_kr=d_g_kr, d_g_kn=d_g_kn, d_w_uk=d_w_uk, d_w_uv=d_w_uv,
             w_router=w_router, b_router=b_router, w_gate_e=w_gate_e, w_up_e=w_up_e, w_down_e=w_down_e,
             w_gate_s=w_gate_s, w_up_s=w_up_s, w_down_s=w_down_s)
    n_p, t_p, d = x_prompt.shape
    n_s, t_s, _ = x_sample.shape
    depth = w_ada.shape[0]
    pad_p = (-n_p) % 8
    c_all = jnp.concatenate([c_prompt, jnp.zeros((pad_p, d), F32), c_sample], axis=0)
    mod_all = _ada(c_all, w_ada, b_ada).reshape(depth, c_all.shape[0], 6, d)
    mod_p = mod_all[:, :n_p]
    mod_s = mod_all[:, n_p + pad_p:]

    n_even, _, bh, bdk, bdv = state_gla.shape
    n_odd, _, chh, cdh, _ = state_mlstm_c.shape
    gla0_p = jnp.zeros((n_even, n_p, bh, bdk, bdv), F32)
    mc0_p = jnp.zeros((n_odd, n_p, chh, cdh, cdh), F32)
    mn0_p = jnp.zeros((n_odd, n_p, chh, cdh), F32)
    mm0_p = jnp.full((n_odd, n_p, chh), NEG, F32)
    past_len = page_table.shape[1] * PAGE_SIZE

    trunks = [_trunk(x_prompt, mod_p, 0, gla0_p, mc0_p, mn0_p, mm0_p, p, None),
              _trunk(x_sample, mod_s, past_len, state_gla, state_mlstm_c, state_mlstm_n, state_mlstm_m, p,
                     (cache_ckv, cache_krope, page_table))]
    pending = [next(tr) for tr in trunks]
    for layer in range(depth):
        mixed = _moe(layer, pending, p)
        pending = []
        for tr, x_new in zip(trunks, mixed):
            try:
                pending.append(tr.send(x_new))
            except StopIteration as done:
                pending.append(done.value)
    (y_p, rp), (y_s, rs) = pending
    aw = rs['v_rows'].shape[1]
    return (y_p, y_s, rp['gla'][None], rs['gla'][None], rs['v_rows'].reshape(1, n_s, t_s, aw),
            rp['mlstm'][0][None], rs['mlstm'][0][None], rp['mlstm'][1][None], rs['mlstm'][1][None],
            rp['mlstm'][2][None], rs['mlstm'][2][None],
            rp['ckv'][:, None], rs['ckv'][:, None], rp['kr'][:, None], rs['kr'][:, None])
```

```python
import functools

import numpy as np
import jax
import jax.numpy as jnp
from jax import lax
from jax.experimental import pallas as pl
from jax.experimental.pallas import tpu as pltpu

F32 = jnp.float32
BF16 = jnp.bfloat16

EPS = 1e-6
NEG = -1e30

A_GROUPS = 4
CHUNK_A = 128
B_HEADS = 4
B_GATE_RANK = 16
B_GATE_TAU = 16.0
CHUNK_B = 16
C_HEADS = 4
CHUNK_C = 128
D_HEADS = 4
D_NOPE = 128
D_ROPE = 64
ROPE_BASE = 10000.0
MLA_SCALE = (D_NOPE + D_ROPE) ** -0.5
PAGE_SIZE = 128
N_EXPERT_GROUPS = 8
TOPK_GROUPS = 4
TOP_K = 8
ROUTED_SCALE = 2.5

ROW_TILE = 256
SAMPLE_SEQS = 32
FLASH_BLOCK = 1024
PAGES_PER_STEP = 32
VMEM_LIMIT = 56 * 1024 * 1024


def _dot(a, b):
    return jnp.dot(a, b, preferred_element_type=F32)


def _dot_nt(a, b):
    return lax.dot_general(a, b, (((1,), (1,)), ((), ())), preferred_element_type=F32)


def _split2(x):
    hi = x.astype(BF16)
    lo = (x - hi.astype(F32)).astype(BF16)
    return hi, lo


def _split3(x):
    hi = x.astype(BF16)
    r = x - hi.astype(F32)
    mid = r.astype(BF16)
    lo = (r - mid.astype(F32)).astype(BF16)
    return hi, mid, lo


def _dot3(a, b):
    ah, al = _split2(a)
    bh, bl = _split2(b)
    return _dot(ah, bh) + _dot(ah, bl) + _dot(al, bh)


def _dot3_nt(a, b):
    ah, al = _split2(a)
    bh, bl = _split2(b)
    return _dot_nt(ah, bh) + _dot_nt(ah, bl) + _dot_nt(al, bh)


def _dot_sel(x, m01, parts=2):
    ps = _split2(x) if parts == 2 else _split3(x)
    acc = _dot(ps[0], m01)
    for p in ps[1:]:
        acc = acc + _dot(p, m01)
    return acc


def _sigmoid(x):
    return 1.0 / (1.0 + jnp.exp(-x))


def _silu(x):
    return x * _sigmoid(x)


def _log_sigmoid(x):
    return jnp.minimum(x, 0.0) - jnp.log(1.0 + jnp.exp(-jnp.abs(x)))


def _gelu(x):
    return 0.5 * x * (1.0 + jnp.tanh(0.7978845608028654 * (x + 0.044715 * x * x * x)))


def _rms(x, eps=EPS):
    return x * lax.rsqrt(jnp.mean(x * x, axis=-1, keepdims=True) + eps)


def _params(*sem):
    return pltpu.CompilerParams(dimension_semantics=sem, vmem_limit_bytes=VMEM_LIMIT)


def _const_spec(shape):
    nd = len(shape)
    return pl.BlockSpec(shape, lambda *_: (0,) * nd)


class _Rows:
    def __init__(self, n, t):
        self.n, self.t = n, t
        if t % ROW_TILE == 0:
            self.s, self.r = 1, ROW_TILE
            self.tpb = t // ROW_TILE
        else:
            assert t == 8 and n % SAMPLE_SEQS == 0, (n, t)
            self.s, self.r = SAMPLE_SEQS, t
            self.tpb = 1
        self.rows = self.s * self.r
        self.steps = n * t // self.rows

    def seq_spec(self, mid, d):
        s, tpb = self.s, self.tpb
        if mid == self.t:
            return pl.BlockSpec((s, self.r, d), lambda i: (i // tpb, i % tpb, 0))
        return pl.BlockSpec((s, mid, d), lambda i: (i // tpb, 0, 0))

    def row_spec(self, c):
        return pl.BlockSpec((self.rows, c), lambda i: (i, 0))

    def tile_spec(self, sub):
        return pl.BlockSpec((self.rows * sub, 128), lambda i: (i, 0))

    def pos_spec(self, c):
        tpb = self.tpb
        return pl.BlockSpec((self.rows, c), lambda i: (i % tpb, 0))


def _ada_kernel(c_ref, w_ref, b_ref, o_ref):
    o_ref[0] = _dot3(_silu(c_ref[...]), w_ref[0]) + b_ref[0]


def _ada(c_all, w_ada, b_ada):
    depth, d, d6 = w_ada.shape
    nc = c_all.shape[0]
    tn = 1536
    return pl.pallas_call(
        _ada_kernel,
        grid=(depth, d6 // tn),
        in_specs=[pl.BlockSpec((nc, d), lambda l, j: (0, 0)),
                  pl.BlockSpec((1, d, tn), lambda l, j: (l, 0, j)),
                  pl.BlockSpec((1, 1, tn), lambda l, j: (l, 0, j))],
        out_specs=pl.BlockSpec((1, nc, tn), lambda l, j: (l, 0, j)),
        out_shape=jax.ShapeDtypeStruct((depth, nc, d6), F32),
        compiler_params=_params("arbitrary", "arbitrary"),
        name="ada_modulation",
    )(c_all, w_ada, b_ada.reshape(depth, 1, d6))


def _norm_mod(x_ref, mod_ref, g_ref, shift_row, scale_row):
    x = x_ref[...]
    h = _rms(x) * g_ref[...]
    h = h * (1.0 + mod_ref[:, scale_row:scale_row + 1, :]) + mod_ref[:, shift_row:shift_row + 1, :]
    s, r, d = x.shape
    return h.reshape(s * r, d)


def _even_in_kernel(x_ref, mod_ref, g_ref, w_ref, an_ref, ws_ref, bs_ref,
                    oa_ref, v_ref, q_ref, k_ref, bv_ref, r_ref, bg_ref):
    h = _norm_mod(x_ref, mod_ref, g_ref, 0, 1).astype(BF16)
    z = _dot(h, w_ref[...])
    aw = A_GROUPS * CHUNK_A
    rows = z.shape[0]
    u = _gelu(z[:, 0:aw])
    va = _gelu(z[:, aw:2 * aw])
    vn = jnp.concatenate(
        [_rms(va[:, g * CHUNK_A:(g + 1) * CHUNK_A]) for g in range(A_GROUPS)], axis=1) * an_ref[...]
    v_ref[...] = vn
    vb = vn.astype(BF16)
    for c in range(rows // CHUNK_A):
        r0 = c * CHUNK_A
        mixed = jnp.concatenate(
            [_dot(ws_ref[g], vb[r0:r0 + CHUNK_A, g * CHUNK_A:(g + 1) * CHUNK_A]) for g in range(A_GROUPS)],
            axis=1) + bs_ref[...]
        oa_ref[r0:r0 + CHUNK_A, :] = u[r0:r0 + CHUNK_A, :] * mixed
    o = 2 * aw
    q_ref[...] = z[:, o:o + 256]
    k_ref[...] = z[:, o + 256:o + 512]
    bv_ref[...] = z[:, o + 512:o + 1024]
    r_ref[...] = z[:, o + 1024:o + 1536]
    bg_ref[...] = z[:, o + 1536:o + 1664]


def _even_in(x, mod, g, w, an, ws, bs):
    n, t, d = x.shape
    rt = _Rows(n, t)
    m = n * t
    widths = (512, 512, 256, 256, 512, 512, 128)
    return pl.pallas_call(
        _even_in_kernel,
        grid=(rt.steps,),
        in_specs=[rt.seq_spec(t, d), rt.seq_spec(6, d), _const_spec((1, d)), _const_spec(w.shape),
                  _const_spec(an.shape), _const_spec(ws.shape), _const_spec(bs.shape)],
        out_specs=[rt.row_spec(c) for c in widths],
        out_shape=[jax.ShapeDtypeStruct((m, c), F32) for c in widths],
        compiler_params=_params("arbitrary"),
        name="even_in_chunk_gate",
    )(x, mod, g, w, an, ws, bs)


def _gla_kernel(q_ref, k_ref, v_ref, r_ref, g_ref, wg_ref, gb_ref, ng_ref, s0_ref, seg_ref, eye_ref,
                o_ref, so_ref, s_scr, lg_scr, o_scr, *, chunk, n_chunks):
    seqs, _, hk = q_ref.shape
    dk = hk // B_HEADS
    dv = v_ref.shape[2] // B_HEADS
    t = pl.program_id(1)

    @pl.when(t == 0)
    def _():
        for sq in range(seqs):
            s_scr[sq] = s0_ref[sq].reshape(hk, dv)

    for sq in range(seqs):
        lg_scr[sq] = _log_sigmoid(_dot3(g_ref[sq], wg_ref[...]) + gb_ref[...]) * (1.0 / B_GATE_TAU)
    row = lax.broadcasted_iota(jnp.int32, (chunk, hk), 0)
    lane = lax.broadcasted_iota(jnp.int32, (chunk, hk), 1)
    pad = 16 - chunk if chunk < 16 else 0

    def pad_rows(a):
        if pad:
            return jnp.concatenate([a, jnp.zeros((pad, a.shape[1]), a.dtype)], axis=0)
        return a

    def one_chunk(sq, r0):
        lg = lg_scr[sq, pl.ds(r0, chunk), :]
        q = q_ref[sq, pl.ds(r0, chunk), :] * (dk ** -0.5)
        k = k_ref[sq, pl.ds(r0, chunk), :]
        v = v_ref[sq, pl.ds(r0, chunk), :]
        b = jnp.zeros_like(lg)
        for s in range(chunk):
            b = b + jnp.where(row >= s, lg[s:s + 1, :], 0.0)
        prods = []
        for l in range(chunk):
            e = jnp.exp(jnp.where(row <= l, b[l:l + 1, :] - b, NEG))
            prods.append(q[l:l + 1, :] * k * e)
        att = _dot(jnp.concatenate(prods, axis=0).astype(BF16), seg_ref[...])
        o_intra = jnp.concatenate(
            [jnp.sum(att[l * chunk:(l + 1) * chunk, :] * v, axis=0, keepdims=True) for l in range(chunk)], axis=0)
        s_old = s_scr[sq]
        s_b = s_old.astype(BF16)
        qe = q * jnp.exp(b)
        o_inter = jnp.concatenate(
            [_dot(jnp.where(lane // dk == h, qe, 0.0).astype(BF16), s_b) for h in range(B_HEADS)], axis=1)
        o_scr[sq, pl.ds(r0, chunk), :] = o_intra + o_inter
        b_last = b[chunk - 1:chunk, :]
        k_end = pad_rows((k * jnp.exp(b_last - b)).astype(BF16))
        k_t = _dot_nt(eye_ref[...], k_end).astype(BF16)
        dh, dm, dl = _split3(jnp.exp(b_last))
        dec3 = _dot_nt(eye_ref[...], jnp.concatenate([dh, dm, dl, jnp.zeros((13, hk), BF16)], axis=0))
        dec = dec3[:, 0:1] + dec3[:, 1:2] + dec3[:, 2:3]
        v_b = pad_rows(v.astype(BF16))
        kv = jnp.concatenate(
            [_dot(k_t[h * dk:(h + 1) * dk, :], v_b[:, h * dv:(h + 1) * dv]) for h in range(B_HEADS)], axis=0)
        s_scr[sq] = dec * s_old + kv

    def body(c, carry):
        r0 = pl.multiple_of(c * chunk, chunk)
        for sq in range(seqs):
            one_chunk(sq, r0)
        return carry

    lax.fori_loop(0, n_chunks, body, 0)

    for sq in range(seqs):
        o = o_scr[sq]
        on = jnp.concatenate([_rms(o[:, h * dv:(h + 1) * dv]) for h in range(B_HEADS)], axis=1) * ng_ref[...]
        o_ref[sq] = on * _silu(r_ref[sq])

    @pl.when(t == pl.num_programs(1) - 1)
    def _():
        for sq in range(seqs):
            so_ref[sq] = s_scr[sq].reshape(B_HEADS, dk, dv)


GLA_SEQS = 2


def _gla(n, t, q, k, v, r, g, wg, gb, ng, s0, seg, eye):
    tt = ROW_TILE if t % ROW_TILE == 0 else t
    chunk = CHUNK_B if tt % CHUNK_B == 0 else tt
    nt = t // tt
    hk, hv = q.shape[1], v.shape[1]
    dk, dv = hk // B_HEADS, hv // B_HEADS
    seqs = GLA_SEQS
    assert n % seqs == 0

    def rows(c):
        return pl.BlockSpec((seqs, tt, c), lambda i, j: (i, j, 0))

    def per_seq(a):
        return a.reshape(n, t, a.shape[1])

    st = pl.BlockSpec((seqs, B_HEADS, dk, dv), lambda i, j: (i, 0, 0, 0))
    o, s_new = pl.pallas_call(
        functools.partial(_gla_kernel, chunk=chunk, n_chunks=tt // chunk),
        grid=(n // seqs, nt),
        in_specs=[rows(hk), rows(hk), rows(hv), rows(hv), rows(g.shape[1]),
                  pl.BlockSpec(wg.shape, lambda i, j: (0, 0)), pl.BlockSpec(gb.shape, lambda i, j: (0, 0)),
                  pl.BlockSpec(ng.shape, lambda i, j: (0, 0)), st,
                  pl.BlockSpec(seg.shape, lambda i, j: (0, 0)), pl.BlockSpec(eye.shape, lambda i, j: (0, 0))],
        out_specs=[rows(hv), st],
        out_shape=[jax.ShapeDtypeStruct((n, t, hv), F32), jax.ShapeDtypeStruct(s0.shape, F32)],
        scratch_shapes=[pltpu.VMEM((seqs, hk, dv), F32), pltpu.VMEM((seqs, tt, hk), F32),
                        pltpu.VMEM((seqs, tt, hv), F32)],
        compiler_params=_params("arbitrary", "arbitrary"),
        name="gla_scan",
    )(per_seq(q), per_seq(k), per_seq(v), per_seq(r), per_seq(g), wg, gb, ng, s0, seg, eye)
    return o.reshape(n * t, hv), s_new


def _route(logits_t, bias_col):
    n_e, r = logits_t.shape
    per = n_e // N_EXPERT_GROUPS
    scores = _sigmoid(logits_t)
    biased = scores + bias_col
    sub = lax.broadcasted_iota(jnp.int32, (per, r), 0).astype(F32)
    ninf = -jnp.inf
    gs = []
    for g in range(N_EXPERT_GROUPS):
        blk = biased[g * per:(g + 1) * per, :]
        m1 = jnp.max(blk, axis=0, keepdims=True)
        i1 = jnp.min(jnp.where(blk == m1, sub, float(per)), axis=0, keepdims=True)
        m2 = jnp.max(jnp.where(sub == i1, ninf, blk), axis=0, keepdims=True)
        gs.append(m1 + m2)
    cur = jnp.concatenate(gs, axis=0)
    gsub = lax.broadcasted_iota(jnp.int32, (N_EXPERT_GROUPS, r), 0).astype(F32)
    gsel = jnp.zeros((N_EXPERT_GROUPS, r), F32)
    for _ in range(TOPK_GROUPS):
        m = jnp.max(cur, axis=0, keepdims=True)
        i = jnp.min(jnp.where(cur == m, gsub, float(N_EXPERT_GROUPS)), axis=0, keepdims=True)
        hit = gsub == i
        gsel = jnp.where(hit, 1.0, gsel)
        cur = jnp.where(hit, ninf, cur)
    cur = jnp.concatenate(
        [jnp.where(gsel[g:g + 1, :] > 0.5, biased[g * per:(g + 1) * per, :], ninf) for g in range(N_EXPERT_GROUPS)],
        axis=0)
    esub = lax.broadcasted_iota(jnp.int32, (n_e, r), 0).astype(F32)
    idx, wts, hits = [], [], []
    for _ in range(TOP_K):
        m = jnp.max(cur, axis=0, keepdims=True)
        i = jnp.min(jnp.where(cur == m, esub, float(n_e)), axis=0, keepdims=True)
        hit = esub == i
        idx.append(i)
        hits.append(hit)
        wts.append(jnp.sum(jnp.where(hit, scores, 0.0), axis=0, keepdims=True))
        cur = jnp.where(hit, ninf, cur)
    w = jnp.concatenate(wts, axis=0)
    w = w / jnp.sum(w, axis=0, keepdims=True) * ROUTED_SCALE
    return jnp.concatenate(idx, axis=0), w, hits


def _pack_pairs(x):
    w = x.shape[1] // 2
    hi = lax.bitcast_convert_type(x[:, :w].astype(BF16).astype(F32), jnp.uint32)
    lo = lax.bitcast_convert_type(x[:, w:].astype(BF16).astype(F32), jnp.uint32)
    return hi | (lo >> 16)


def _unpack_pairs(pk):
    a = lax.bitcast_convert_type(pk & jnp.uint32(0xFFFF0000), F32)
    b = lax.bitcast_convert_type(pk << 16, F32)
    return jnp.concatenate([a, b], axis=1)


def _store_row_tiles(ref, x):
    rows = x.shape[0]
    pk = _pack_pairs(x)
    sub = pk.shape[1] // 128
    for c in range(sub):
        ref[pl.ds(c, rows, stride=sub), :] = pk[:, c * 128:(c + 1) * 128]


def _load_row_tiles(ref, sub):
    rows = ref.shape[0] // sub
    return _unpack_pairs(jnp.concatenate([ref[pl.ds(c, rows, stride=sub), :] for c in range(sub)], axis=1))


def _out_kernel(x_ref, mod_ref, ma_ref, mb_ref, wo_ref, g_ref, wr_ref, br_ref, eye_ref, triu_ref,
                x1_ref, hp_ref, idx_ref, rank_ref, w3_ref, cnt_ref, run_scr):
    step = pl.program_id(0)

    @pl.when(step == 0)
    def _():
        run_scr[...] = jnp.zeros_like(run_scr)

    half = ma_ref.shape[1]
    y = _dot(ma_ref[...].astype(BF16), wo_ref[0:half, :]) + _dot(mb_ref[...].astype(BF16), wo_ref[half:, :])
    x = x_ref[...]
    s, r, d = x.shape
    x1 = x + mod_ref[:, 2:3, :] * y.reshape(s, r, d)
    x1_ref[...] = x1
    h2 = (_rms(x1) * g_ref[...] * (1.0 + mod_ref[:, 4:5, :]) + mod_ref[:, 3:4, :]).reshape(s * r, d)
    _store_row_tiles(hp_ref, h2)
    idx, w, hits = _route(_dot3_nt(wr_ref[...], h2), br_ref[...])
    idx_ref[0] = idx.astype(jnp.int32)
    sel = jnp.zeros(hits[0].shape, F32)
    for hit in hits:
        sel = jnp.where(hit, 1.0, sel)
    before = run_scr[:, 0:1] + _dot(sel.astype(BF16), triu_ref[...]) - sel
    rank_ref[0] = jnp.concatenate(
        [jnp.sum(jnp.where(hit, before, 0.0), axis=0, keepdims=True) for hit in hits], axis=0).astype(jnp.int32)
    run_scr[...] = run_scr[...] + jnp.sum(sel, axis=1, keepdims=True)
    cnt_ref[...] = run_scr[...]
    wh, wm, wl = _split3(w)
    stack = jnp.concatenate([wh, wm, wl, jnp.zeros((128 - 3 * TOP_K, s * r), BF16)], axis=0)
    w3_ref[...] = _dot_nt(eye_ref[...], stack)


def _out_route(x, mod, mix_a, mix_b, wo, g, wr_t, br, eye, triu):
    n, t, d = x.shape
    rt = _Rows(n, t)
    m = n * t
    n_e = wr_t.shape[0]
    slot = pl.BlockSpec((1, TOP_K, rt.rows), lambda i: (i, 0, 0))
    return pl.pallas_call(
        _out_kernel,
        grid=(rt.steps,),
        in_specs=[rt.seq_spec(t, d), rt.seq_spec(6, d), rt.row_spec(mix_a.shape[1]), rt.row_spec(mix_b.shape[1]),
                  _const_spec(wo.shape), _const_spec((1, d)), _const_spec(wr_t.shape), _const_spec(br.shape),
                  _const_spec(eye.shape), _const_spec(triu.shape)],
        out_specs=[rt.seq_spec(t, d), rt.tile_spec(d // 256), slot, slot, rt.row_spec(128), _const_spec((n_e, 128))],
        out_shape=[jax.ShapeDtypeStruct((n, t, d), F32), jax.ShapeDtypeStruct((m * (d // 256), 128), jnp.uint32),
                   jax.ShapeDtypeStruct((rt.steps, TOP_K, rt.rows), jnp.int32),
                   jax.ShapeDtypeStruct((rt.steps, TOP_K, rt.rows), jnp.int32),
                   jax.ShapeDtypeStruct((m, 128), F32), jax.ShapeDtypeStruct((n_e, 128), F32)],
        scratch_shapes=[pltpu.VMEM((n_e, 128), F32)],
        compiler_params=_params("arbitrary"),
        name="out_proj_route",
    )(x, mod, mix_a, mix_b, wo, g, wr_t, br, eye, triu)


ROW_SUB = 4


def _row_copy(src_ref, src_row, dst_ref, dst_row, sem):
    src = src_ref.at[pl.ds(pl.multiple_of(src_row * ROW_SUB, ROW_SUB), ROW_SUB), :]
    dst = dst_ref.at[pl.ds(pl.multiple_of(dst_row * ROW_SUB, ROW_SUB), ROW_SUB), :]
    return pltpu.make_async_copy(src, dst, sem)


def _dispatch_kernel(dest_ref, hp_ref, *rest):
    xs_ref, sem = rest[-2:]
    rows = hp_ref.shape[0] // ROW_SUB

    def issue(tok, carry):
        for k in range(TOP_K):
            _row_copy(hp_ref, tok, xs_ref, dest_ref[0, k, tok], sem).start(priority=k % 2)
        return carry

    lax.fori_loop(0, rows, issue, 0, unroll=2)

    def drain(j, carry):
        _row_copy(hp_ref, 0, xs_ref, 0, sem).wait()
        return carry

    lax.fori_loop(0, TOP_K * rows, drain, 0, unroll=16)


def _dispatch(dest3, hp, n_rows, xs_prev=None):
    steps, _, rows = dest3.shape
    assert rows & (rows - 1) == 0
    chained = xs_prev is not None
    return pl.pallas_call(
        _dispatch_kernel,
        grid=(steps,),
        in_specs=[pl.BlockSpec((1, TOP_K, rows), lambda i: (i, 0, 0), memory_space=pltpu.SMEM),
                  pl.BlockSpec((rows * ROW_SUB, 128), lambda i: (i, 0))]
                 + ([pl.BlockSpec(memory_space=pl.ANY)] if chained else []),
        out_specs=pl.BlockSpec(memory_space=pl.ANY),
        out_shape=jax.ShapeDtypeStruct((n_rows * ROW_SUB, 128), jnp.uint32),
        scratch_shapes=[pltpu.SemaphoreType.DMA(())],
        input_output_aliases={2: 0} if chained else {},
        compiler_params=_params("arbitrary"),
        name="moe_dispatch_rows",
    )(dest3, hp, *([xs_prev] if chained else []))


def _gmm_kernel(be_ref, va_ref, nu_ref, x_ref, wg_ref, wu_ref, wd_ref, y_ref):
    i = pl.program_id(0)

    @pl.when(i < nu_ref[0])
    def _():
        x = _load_row_tiles(x_ref, ROW_SUB)
        live = lax.broadcasted_iota(jnp.int32, x.shape, 0) < va_ref[i]
        x = jnp.where(live, x, 0.0).astype(BF16)
        a = _silu(_dot(x, wg_ref[...].astype(BF16))) * _dot(x, wu_ref[...].astype(BF16))
        _store_row_tiles(y_ref, _dot(a.astype(BF16), wd_ref[...].astype(BF16)))

    @pl.when(i >= nu_ref[0])
    def _():
        y_ref[...] = jnp.zeros_like(y_ref)


def _gmm(layer, block_e, valid, n_used, xs, wg, wu, wd, tm):
    r = xs.shape[0] // ROW_SUB
    d, de = wg.shape[-2:]
    assert d == ROW_SUB * 256
    nb = r // tm
    grid_spec = pltpu.PrefetchScalarGridSpec(
        num_scalar_prefetch=3,
        grid=(nb,),
        in_specs=[pl.BlockSpec((tm * ROW_SUB, 128), lambda i, be, va, nu: (i, 0)),
                  pl.BlockSpec((None, None, d, de), lambda i, be, va, nu: (layer, be[i], 0, 0)),
                  pl.BlockSpec((None, None, d, de), lambda i, be, va, nu: (layer, be[i], 0, 0)),
                  pl.BlockSpec((None, None, de, d), lambda i, be, va, nu: (layer, be[i], 0, 0))],
        out_specs=pl.BlockSpec((tm * ROW_SUB, 128), lambda i, be, va, nu: (i, 0)),
    )
    return pl.pallas_call(
        _gmm_kernel,
        grid_spec=grid_spec,
        out_shape=jax.ShapeDtypeStruct(xs.shape, jnp.uint32),
        compiler_params=_params("arbitrary"),
        name="moe_grouped_swiglu",
    )(block_e, valid, n_used, xs, wg, wu, wd)


def _combine_kernel(dest_ref, x_ref, mod_ref, hp_ref, w3_ref, wg_ref, wu_ref, wd_ref, ys_ref, o_ref, yg_scr, sem):
    rows = hp_ref.shape[0] // ROW_SUB

    def issue(tok, carry):
        for k in range(TOP_K):
            _row_copy(ys_ref, dest_ref[0, k, tok], yg_scr.at[k], tok, sem).start(priority=k % 2)
        return carry

    lax.fori_loop(0, rows, issue, 0, unroll=2)
    h = _load_row_tiles(hp_ref, ROW_SUB).astype(BF16)
    a = _silu(_dot(h, wg_ref[...])) * _dot(h, wu_ref[...])
    acc = _dot(a.astype(BF16), wd_ref[...])

    def drain(j, carry):
        _row_copy(ys_ref, 0, yg_scr.at[0], 0, sem).wait()
        return carry

    lax.fori_loop(0, TOP_K * rows, drain, 0, unroll=16)
    w3 = w3_ref[...]
    for k in range(TOP_K):
        wk = w3[:, k:k + 1] + w3[:, TOP_K + k:TOP_K + k + 1] + w3[:, 2 * TOP_K + k:2 * TOP_K + k + 1]
        acc = acc + wk * _load_row_tiles(yg_scr.at[k], ROW_SUB)
    x = x_ref[...]
    o_ref[...] = x + mod_ref[:, 5:6, :] * acc.reshape(x.shape)


def _combine(dest3, x1, mod, hp, w3, wg, wu, wd, ys):
    n, t, d = x1.shape
    rt = _Rows(n, t)
    assert rt.rows & (rt.rows - 1) == 0
    return pl.pallas_call(
        _combine_kernel,
        grid=(rt.steps,),
        in_specs=[pl.BlockSpec((1, TOP_K, rt.rows), lambda i: (i, 0, 0), memory_space=pltpu.SMEM),
                  rt.seq_spec(t, d), rt.seq_spec(6, d), rt.tile_spec(ROW_SUB), rt.row_spec(128),
                  _const_spec(wg.shape), _const_spec(wu.shape), _const_spec(wd.shape),
                  pl.BlockSpec(memory_space=pl.ANY)],
        out_specs=rt.seq_spec(t, d),
        out_shape=jax.ShapeDtypeStruct((n, t, d), F32),
        scratch_shapes=[pltpu.VMEM((TOP_K, rt.rows * ROW_SUB, 128), jnp.uint32), pltpu.SemaphoreType.DMA(())],
        compiler_params=_params("arbitrary"),
        name="moe_combine_shared",
    )(dest3, x1, mod, hp, w3, wg, wu, wd, ys)


def _moe(layer, groups, p):
    n_e = p['w_gate_e'].shape[1]
    a = sum(g[1][2].size for g in groups)
    tm = 512 if a // n_e >= 1024 else 128
    counts = [g[1][5][:, 0].astype(jnp.int32) for g in groups]
    total = sum(counts)
    padded = (total + tm - 1) // tm * tm
    pend = jnp.cumsum(padded)
    pstart = pend - padded
    experts = jnp.arange(n_e, dtype=jnp.int32)
    dests = []
    base = pstart
    for (_, routed), cnt in zip(groups, counts):
        onehot = routed[2][None] == experts[:, None, None, None]
        dests.append(routed[3] + jnp.sum(jnp.where(onehot, base[:, None, None, None], 0), axis=0))
        base = base + cnt
    n_blocks = -(-a // tm) + n_e
    first = jnp.arange(n_blocks, dtype=jnp.int32) * tm
    block_e = jnp.minimum(jnp.sum((pend[None, :] <= first[:, None]).astype(jnp.int32), axis=1), n_e - 1)
    mine = block_e[:, None] == experts[None, :]
    last = jnp.sum(jnp.where(mine, (pstart + total)[None, :], 0), axis=1)
    valid = jnp.clip(last - first, 0, tm).astype(jnp.int32)
    n_used = (pend[-1] // tm).astype(jnp.int32).reshape(1)
    xs = None
    for (_, routed), dest3 in zip(groups, dests):
        xs = _dispatch(dest3, routed[1], n_blocks * tm, xs)
    ys = _gmm(layer, block_e, valid, n_used, xs, p['w_gate_e'], p['w_up_e'], p['w_down_e'], tm)
    shared = [p[name][layer].astype(BF16) for name in ('w_gate_s', 'w_up_s', 'w_down_s')]
    return [_combine(dest3, routed[0], mod, routed[1], routed[4], *shared, ys)
            for (mod, routed), dest3 in zip(groups, dests)]


def _odd_in_kernel(x_ref, mod_ref, g_ref, w_ref, gqa_ref, wqb_ref, gqn_ref, gqr_ref, gqrs_ref, gkva_ref,
                   gkr_ref, gkrs_ref, gkn_ref, cos_ref, sin_ref, seg_ref, wuk_ref, wuv_ref,
                   cq_ref, ck_ref, cv_ref, co_ref, gt_ref, ckv_ref, kr_ref, *outs, prompt):
    h = _norm_mod(x_ref, mod_ref, g_ref, 0, 1).astype(BF16)
    z = _dot(h, w_ref[...])
    cq_ref[...] = z[:, 0:512]
    ck_ref[...] = z[:, 512:1024]
    cv_ref[...] = z[:, 1024:1536]
    co_ref[...] = z[:, 1536:2048]
    gt_ref[...] = z[:, 2560:2688]
    cos = cos_ref[...]
    sin = sin_ref[...]
    ckv = _rms(z[:, 2304:2432]) * gkva_ref[...]
    ckv_ref[...] = ckv
    xr = z[:, 2432:2496]
    xr_rot = z[:, 2496:2560]
    rr = lax.rsqrt(jnp.mean(xr * xr, axis=-1, keepdims=True) + EPS)
    kr = rr * (xr * gkr_ref[...] * cos[:, 0:D_ROPE] + xr_rot * gkrs_ref[...] * sin[:, 0:D_ROPE])
    kr_ref[...] = kr
    qa = (_rms(z[:, 2048:2304]) * gqa_ref[...]).astype(BF16)
    qd = _dot(qa, wqb_ref[...])
    nw = D_HEADS * D_NOPE
    rw = D_HEADS * D_ROPE
    qn = [_rms(qd[:, hh * D_NOPE:(hh + 1) * D_NOPE]) * gqn_ref[...] * MLA_SCALE for hh in range(D_HEADS)]
    xq = qd[:, nw:nw + rw]
    xq_rot = qd[:, nw + rw:nw + 2 * rw]
    rq = lax.rsqrt(_dot_sel(xq * xq, seg_ref[...]) * (1.0 / D_ROPE) + EPS)
    qr = rq * (xq * gqr_ref[...] * cos + xq_rot * gqrs_ref[...] * sin) * MLA_SCALE
    ckv_b = ckv.astype(BF16)
    if prompt:
        qcat_ref, kcat_ref, v_ref = outs
        kn = _dot(ckv_b, wuk_ref[...])
        zpad = jnp.zeros((z.shape[0], 256 - D_NOPE - D_ROPE), F32)
        qcat_ref[...] = jnp.concatenate(
            [piece for hh in range(D_HEADS) for piece in (qn[hh], qr[:, hh * D_ROPE:(hh + 1) * D_ROPE], zpad)],
            axis=1).astype(BF16)
        kcat_ref[...] = jnp.concatenate(
            [piece for hh in range(D_HEADS)
             for piece in (_rms(kn[:, hh * D_NOPE:(hh + 1) * D_NOPE]) * gkn_ref[...], kr, zpad)],
            axis=1).astype(BF16)
        v_ref[...] = _dot(ckv_b, wuv_ref[...]).astype(BF16)
    else:
        u_ref, qr_ref = outs
        u_ref[...] = jnp.concatenate(
            [_dot_nt((qn[hh] * gkn_ref[...]).astype(BF16), wuk_ref[:, hh * D_NOPE:(hh + 1) * D_NOPE])
             for hh in range(D_HEADS)], axis=1).astype(BF16)
        qr_ref[...] = qr.astype(BF16)


def _odd_in(x, mod, g, w, consts, cos, sin, prompt):
    n, t, d = x.shape
    rt = _Rows(n, t)
    m = n * t
    widths = [(512, F32)] * 4 + [(128, F32), (128, F32), (D_ROPE, F32)]
    if prompt:
        widths += [(1024, BF16), (1024, BF16), (512, BF16)]
    else:
        widths += [(512, BF16), (256, BF16)]
    return pl.pallas_call(
        functools.partial(_odd_in_kernel, prompt=prompt),
        grid=(rt.steps,),
        in_specs=[rt.seq_spec(t, d), rt.seq_spec(6, d), _const_spec((1, d)), _const_spec(w.shape)]
                 + [_const_spec(c.shape) for c in consts[:9]]
                 + [rt.pos_spec(cos.shape[1]), rt.pos_spec(sin.shape[1])]
                 + [_const_spec(c.shape) for c in consts[9:]],
        out_specs=[rt.row_spec(c) for c, _ in widths],
        out_shape=[jax.ShapeDtypeStruct((m, c), dt) for c, dt in widths],
        compiler_params=_params("arbitrary"),
        name="odd_in_latent_prep",
    )(x, mod, g, w, *consts[:9], cos, sin, *consts[9:])


def _mlstm_kernel(q_ref, k_ref, v_ref, og_ref, gt_ref, gb_ref, ng_ref, c0_ref, n0_ref, m0_ref, tri_ref, eye_ref,
                  o_ref, co_ref, no_ref, mo_ref, c_scr, n_scr, m_scr, *, chunk):
    t = pl.program_id(1)

    @pl.when(t == 0)
    def _():
        c_scr[...] = c0_ref[...]
        n_scr[...] = n0_ref[...]
        m_scr[...] = m0_ref[...]

    for sq in range(q_ref.shape[0]):
        _mlstm_chunk(q_ref.at[sq], k_ref.at[sq], v_ref.at[sq], og_ref.at[sq], gt_ref.at[sq], gb_ref, ng_ref,
                     tri_ref, eye_ref, o_ref.at[sq], c_scr.at[sq], n_scr.at[sq], m_scr.at[sq], chunk)

    @pl.when(t == pl.num_programs(1) - 1)
    def _():
        co_ref[...] = c_scr[...]
        no_ref[...] = n_scr[...]
        mo_ref[...] = m_scr[...]


def _mlstm_chunk(q_ref, k_ref, v_ref, og_ref, gt_ref, gb_ref, ng_ref, tri_ref, eye_ref,
                 o_ref, c_scr, n_scr, m_scr, chunk):
    dh = q_ref.shape[1] // C_HEADS
    pre = gt_ref[...] + gb_ref[...]
    lf = _log_sigmoid(pre)
    if chunk >= 16:
        l_hi, l_mid, l_lo = _split3(lf)
        f_cum = _dot(tri_ref[...], l_hi) + _dot(tri_ref[...], l_mid) + _dot(tri_ref[...], l_lo)
    else:
        rowg = lax.broadcasted_iota(jnp.int32, lf.shape, 0)
        f_cum = jnp.zeros_like(lf)
        for s in range(chunk):
            f_cum = f_cum + jnp.where(rowg >= s, lf[s:s + 1, :], 0.0)
    ri = lax.broadcasted_iota(jnp.int32, (chunk, chunk), 0)
    ci = lax.broadcasted_iota(jnp.int32, (chunk, chunk), 1)
    pad = 16 - chunk if chunk < 16 else 0

    def pad_rows(a):
        if pad:
            return jnp.concatenate([a, jnp.zeros((pad, a.shape[1]), a.dtype)], axis=0)
        return a

    for h in range(C_HEADS):
        sl = slice(h * dh, (h + 1) * dh)
        q = q_ref[:, sl]
        k = k_ref[:, sl] * (dh ** -0.5)
        v = v_ref[:, sl]
        qb, kb, vb = q.astype(BF16), k.astype(BF16), v.astype(BF16)
        f_col = f_cum[:, C_HEADS + h:C_HEADS + h + 1]
        i_col = pre[:, h:h + 1]
        m_prev = m_scr[h:h + 1, 0:1]
        a_row = jnp.sum(jnp.where(ri == ci, i_col - f_col, 0.0), axis=0, keepdims=True)
        log_d = jnp.where(ci <= ri, f_col + a_row, NEG)
        inter = f_col + m_prev
        m_t = jnp.maximum(inter, jnp.max(log_d, axis=-1, keepdims=True))
        w_inter = jnp.exp(inter - m_t)
        qk = _dot_nt(qb, kb) * jnp.exp(log_d - m_t)
        c_old = c_scr[h]
        n_old = n_scr[h:h + 1, :]
        num = _dot(qk.astype(BF16), vb) + w_inter * _dot_nt(qb, c_old.astype(BF16))
        den = jnp.sum(qk, axis=-1, keepdims=True) + w_inter * jnp.sum(q * n_old, axis=-1, keepdims=True)
        hh = num / jnp.maximum(jnp.abs(den), jnp.exp(-m_t))
        o_ref[:, sl] = _rms(hh) * ng_ref[...] * _sigmoid(og_ref[:, sl])
        f_last = f_col[chunk - 1:chunk, :]
        a_end = f_last - f_col + i_col
        m_new = jnp.maximum(f_last + m_prev, jnp.max(a_end, axis=0, keepdims=True))
        w = jnp.exp(a_end - m_new)
        dec = jnp.exp(f_last + m_prev - m_new)
        wv_t = _dot_nt(eye_ref[...], pad_rows((w * v).astype(BF16))).astype(BF16)
        c_scr[h] = dec * c_old + _dot(wv_t, pad_rows(kb))
        n_scr[h:h + 1, :] = dec * n_old + jnp.sum(w * k, axis=0, keepdims=True)
        m_scr[h:h + 1, :] = jnp.broadcast_to(m_new, (1, m_scr.shape[1]))


MLSTM_SEQS = 2


def _mlstm(n, t, q, k, v, og, gt, gb, ng, c0, n0, m0, tri, eye):
    chunk = CHUNK_C if t % CHUNK_C == 0 else t
    nt = t // chunk
    w = q.shape[1]
    dh = w // C_HEADS
    seqs = MLSTM_SEQS
    assert n % seqs == 0

    def rows(c):
        return pl.BlockSpec((seqs, chunk, c), lambda i, j: (i, j, 0))

    def per_seq(a):
        return a.reshape(n, t, a.shape[1])

    cst = pl.BlockSpec((seqs, C_HEADS, dh, dh), lambda i, j: (i, 0, 0, 0))
    nst = pl.BlockSpec((seqs, C_HEADS, dh), lambda i, j: (i, 0, 0))
    o, c_new, n_new, m_new = pl.pallas_call(
        functools.partial(_mlstm_kernel, chunk=chunk),
        grid=(n // seqs, nt),
        in_specs=[rows(w), rows(w), rows(w), rows(w), rows(128),
                  pl.BlockSpec(gb.shape, lambda i, j: (0, 0)), pl.BlockSpec(ng.shape, lambda i, j: (0, 0)),
                  cst, nst, nst,
                  pl.BlockSpec(tri.shape, lambda i, j: (0, 0)), pl.BlockSpec(eye.shape, lambda i, j: (0, 0))],
        out_specs=[rows(w), cst, nst, nst],
        out_shape=[jax.ShapeDtypeStruct((n, t, w), F32), jax.ShapeDtypeStruct(c0.shape, F32),
                   jax.ShapeDtypeStruct(n0.shape, F32), jax.ShapeDtypeStruct(m0.shape, F32)],
        scratch_shapes=[pltpu.VMEM((seqs, C_HEADS, dh, dh), F32), pltpu.VMEM((seqs, C_HEADS, dh), F32),
                        pltpu.VMEM((seqs, C_HEADS, dh), F32)],
        compiler_params=_params("arbitrary", "arbitrary"),
        name="mlstm_scan",
    )(per_seq(q), per_seq(k), per_seq(v), per_seq(og), per_seq(gt), gb, ng, c0, n0, m0, tri, eye)
    return o.reshape(n * t, w), c_new, n_new, m_new


def _flash_kernel(qi_ref, kj_ref, q_ref, k_ref, v_ref, o_ref, m_scr, l_scr, acc_scr):
    p = pl.program_id(2)
    qi = qi_ref[p]
    kj = kj_ref[p]

    @pl.when(kj == 0)
    def _():
        m_scr[...] = jnp.full_like(m_scr, NEG)
        l_scr[...] = jnp.zeros_like(l_scr)
        acc_scr[...] = jnp.zeros_like(acc_scr)

    def step(masked):
        s = _dot_nt(q_ref[...], k_ref[...])
        bq, bk = s.shape
        if masked:
            ri = lax.broadcasted_iota(jnp.int32, (bq, bk), 0)
            ci = lax.broadcasted_iota(jnp.int32, (bq, bk), 1)
            s = jnp.where(ci <= ri, s, NEG)
        lanes = m_scr.shape[1]
        m_old = m_scr[...]
        m_new = jnp.maximum(m_old, jnp.max(s, axis=-1, keepdims=True))
        alpha = jnp.exp(m_old - m_new)
        pr = jnp.exp(s - jnp.concatenate([m_new] * (bk // lanes), axis=1))
        l_scr[...] = alpha * l_scr[...] + jnp.sum(pr, axis=-1, keepdims=True)
        acc_scr[...] = alpha * acc_scr[...] + _dot(pr.astype(BF16), v_ref[...])
        m_scr[...] = m_new

    @pl.when(kj < qi)
    def _():
        step(False)

    @pl.when(kj == qi)
    def _():
        step(True)
        o_ref[...] = acc_scr[...] / l_scr[...]


def _flash(n, t, qcat, kcat, v):
    blk = FLASH_BLOCK if t % FLASH_BLOCK == 0 else t
    nq = t // blk
    pairs = [(i, j) for i in range(nq) for j in range(i + 1)]
    qi = jnp.asarray([a for a, _ in pairs], jnp.int32)
    kj = jnp.asarray([b for _, b in pairs], jnp.int32)
    dv = v.shape[1] // D_HEADS
    grid_spec = pltpu.PrefetchScalarGridSpec(
        num_scalar_prefetch=2,
        grid=(n, D_HEADS, len(pairs)),
        in_specs=[pl.BlockSpec((blk, 256), lambda b, h, p, qi, kj: (b * nq + qi[p], h)),
                  pl.BlockSpec((blk, 256), lambda b, h, p, qi, kj: (b * nq + kj[p], h)),
                  pl.BlockSpec((blk, dv), lambda b, h, p, qi, kj: (b * nq + kj[p], h))],
        out_specs=pl.BlockSpec((blk, dv), lambda b, h, p, qi, kj: (b * nq + qi[p], h)),
        scratch_shapes=[pltpu.VMEM((blk, dv), F32), pltpu.VMEM((blk, dv), F32), pltpu.VMEM((blk, dv), F32)],
    )
    assert blk % dv == 0
    return pl.pallas_call(
        _flash_kernel,
        grid_spec=grid_spec,
        out_shape=jax.ShapeDtypeStruct((n * t, v.shape[1]), F32),
        compiler_params=_params("arbitrary", "arbitrary", "arbitrary"),
        name="mla_prompt_flash",
    )(qi, kj, qcat, kcat, v)


def _paged_kernel(pt_ref, u_ref, qr_ref, cn_ref, kn_ref, wukt_ref, wuv_ref, ckv_hbm, krt_hbm,
                  o_ref, ck_buf, kr_buf, sem, m_scr, l_scr, acc_scr, *, pages, li, n_groups, n_steps):
    b = pl.program_id(0)
    g = pl.program_id(1)
    hq = u_ref.shape[1]
    tq = hq // D_HEADS
    step = b * n_groups + g
    slot = lax.rem(step, 2)

    def page_copies(seq, group, sl, lookup):
        copies = []
        for i in range(pages):
            page = pt_ref[seq, group * pages + i] if lookup else 0
            rows = pl.ds(i * PAGE_SIZE, PAGE_SIZE)
            copies.append(pltpu.make_async_copy(ckv_hbm.at[page, li], ck_buf.at[sl, rows, :], sem.at[sl]))
            copies.append(pltpu.make_async_copy(krt_hbm.at[page, li], kr_buf.at[sl, :, rows], sem.at[sl]))
        return copies

    @pl.when(step == 0)
    def _():
        for cp in page_copies(0, 0, 0, True):
            cp.start()

    @pl.when(step + 1 < n_steps)
    def _():
        nxt = step + 1
        for cp in page_copies(lax.div(nxt, n_groups), lax.rem(nxt, n_groups), 1 - slot, True):
            cp.start()

    for cp in page_copies(0, 0, slot, False):
        cp.wait()

    @pl.when(g == 0)
    def _():
        m_scr[...] = jnp.full_like(m_scr, NEG)
        l_scr[...] = jnp.zeros_like(l_scr)
        acc_scr[...] = jnp.zeros_like(acc_scr)

    lhs = jnp.concatenate([wukt_ref[...], u_ref[0]], axis=0)
    qr = qr_ref[0]
    nk = D_HEADS * D_NOPE

    def scores(ck_b, kr_b, kr_keys_minor=True):
        big = _dot_nt(lhs, ck_b)
        rows = []
        for h in range(D_HEADS):
            kn_t = big[h * D_NOPE:(h + 1) * D_NOPE, :]
            rinv = lax.rsqrt(jnp.sum(kn_t * kn_t, axis=0, keepdims=True) * (1.0 / D_NOPE) + EPS)
            rows.append(big[nk + h * tq:nk + (h + 1) * tq, :] * rinv)
        rope = _dot(qr, kr_b) if kr_keys_minor else _dot_nt(qr, kr_b)
        return jnp.concatenate(rows, axis=0) + rope

    def update(s, ck_b):
        m_old = m_scr[...]
        m_new = jnp.maximum(m_old, jnp.max(s, axis=-1, keepdims=True))
        alpha = jnp.exp(m_old - m_new)
        pr = jnp.exp(s - m_new)
        l_scr[...] = alpha * l_scr[...] + jnp.sum(pr, axis=-1, keepdims=True)
        acc_scr[...] = alpha * acc_scr[...] + _dot(pr.astype(BF16), ck_b)
        m_scr[...] = m_new

    span = 2 * PAGE_SIZE
    cks = [ck_buf[slot, pl.ds(i * span, span), :].astype(BF16) for i in range(pages // 2)]
    kr_all = kr_buf[slot].astype(BF16)
    s_all = jnp.concatenate(
        [scores(cks[i], kr_all[:, i * span:(i + 1) * span]) for i in range(pages // 2)], axis=1)
    update(s_all, jnp.concatenate(cks, axis=0))

    @pl.when(g == pl.num_programs(1) - 1)
    def _():
        fill = PAGE_SIZE - tq
        ck_b = jnp.concatenate([cn_ref[...], jnp.zeros((fill, cn_ref.shape[1]), F32)], axis=0).astype(BF16)
        kr_b = jnp.concatenate([kn_ref[...], jnp.zeros((fill, kn_ref.shape[1]), F32)], axis=0).astype(BF16)
        s = scores(ck_b, kr_b, kr_keys_minor=False)
        ri = lax.broadcasted_iota(jnp.int32, s.shape, 0)
        ci = lax.broadcasted_iota(jnp.int32, s.shape, 1)
        update(jnp.where(ci <= ri % tq, s, NEG), ck_b)
        lat = (acc_scr[...] / l_scr[...]).astype(BF16)
        full = _dot(lat, wuv_ref[...])
        dv = wuv_ref.shape[1] // D_HEADS
        o_ref[...] = jnp.concatenate(
            [full[h * tq:(h + 1) * tq, h * dv:(h + 1) * dv] for h in range(D_HEADS)], axis=1)


def _paged(page_table, li, u3, qr3, ckv_new, kr_new, wuk_t, wuv, cache_ckv, cache_kr):
    n, hq, lat = u3.shape
    tq = hq // D_HEADS
    n_pages = page_table.shape[1]
    pages = min(PAGES_PER_STEP, n_pages)
    assert n_pages % pages == 0 and pages % 2 == 0
    ng = n_pages // pages
    cache_kr_t = jnp.swapaxes(cache_kr, 2, 3)
    keys = pages * PAGE_SIZE

    grid_spec = pltpu.PrefetchScalarGridSpec(
        num_scalar_prefetch=1,
        grid=(n, ng),
        in_specs=[pl.BlockSpec((1, hq, lat), lambda b, g, pt: (b, 0, 0)),
                  pl.BlockSpec((1, hq, D_ROPE), lambda b, g, pt: (b, 0, 0)),
                  pl.BlockSpec((tq, lat), lambda b, g, pt: (b, 0)),
                  pl.BlockSpec((tq, D_ROPE), lambda b, g, pt: (b, 0)),
                  pl.BlockSpec(wuk_t.shape, lambda b, g, pt: (0, 0)),
                  pl.BlockSpec(wuv.shape, lambda b, g, pt: (0, 0)),
                  pl.BlockSpec(memory_space=pl.ANY), pl.BlockSpec(memory_space=pl.ANY)],
        out_specs=pl.BlockSpec((tq, wuv.shape[1]), lambda b, g, pt: (b, 0)),
        scratch_shapes=[pltpu.VMEM((2, keys, lat), F32), pltpu.VMEM((2, D_ROPE, keys), F32),
                        pltpu.SemaphoreType.DMA((2,)),
                        pltpu.VMEM((hq, 1), F32), pltpu.VMEM((hq, 1), F32), pltpu.VMEM((hq, lat), F32)],
    )
    return pl.pallas_call(
        functools.partial(_paged_kernel, pages=pages, li=li, n_groups=ng, n_steps=n * ng),
        grid_spec=grid_spec,
        out_shape=jax.ShapeDtypeStruct((n * tq, wuv.shape[1]), F32),
        compiler_params=_params("arbitrary", "arbitrary"),
        name="mla_sample_paged",
    )(page_table, u3, qr3, ckv_new, kr_new, wuk_t, wuv, cache_ckv, cache_kr_t)


def _np_seg(n_seg, seg_in, seg_out):
    mat = np.zeros((n_seg * seg_in, n_seg * seg_out), np.float32)
    for s in range(n_seg):
        mat[s * seg_in:(s + 1) * seg_in, s * seg_out:(s + 1) * seg_out] = 1.0
    return mat


def _rot_cols(w):
    half = w.shape[-1] // 2
    return jnp.concatenate([-w[..., half:], w[..., :half]], axis=-1)


def _swap_halves(g):
    half = g.shape[-1] // 2
    return jnp.concatenate([g[..., half:], g[..., :half]], axis=-1)


def _rope_tables(pos, reps):
    half = D_ROPE // 2
    inv = ROPE_BASE ** (-jnp.arange(half, dtype=F32) / half)
    ang = pos.astype(F32)[:, None] * inv
    cos = jnp.concatenate([jnp.cos(ang), jnp.cos(ang)], axis=-1)
    sin = jnp.concatenate([jnp.sin(ang), jnp.sin(ang)], axis=-1)
    return jnp.tile(cos, (1, reps)), jnp.tile(sin, (1, reps))


def _trunk(x, c_mod, pos0, gla0, mc0, mn0, mm0, p, sample_ctx):
    n, t, d = x.shape
    m = n * t
    rt = _Rows(n, t)
    eye_r = jnp.eye(rt.rows, dtype=BF16)
    triu_r = jnp.asarray(np.triu(np.ones((rt.rows, rt.rows), np.float32)), BF16)
    dh = mc0.shape[3]
    eye_hk = jnp.eye(B_HEADS * gla0.shape[3], dtype=BF16)
    eye_dh = jnp.eye(dh, dtype=BF16)
    results = {}

    layer, li = 0, 0
    mod = c_mod[layer]
    w_in = p['w_in_even'][li]
    w_in = jnp.concatenate([w_in, jnp.zeros((d, 128 - B_GATE_RANK), F32)], axis=1).astype(BF16)
    a_ws = jnp.tril(p['a_ws'][li])
    a_bs = p['a_bs'][li]
    if t % CHUNK_A == 0:
        ws = a_ws
        bs = jnp.repeat(a_bs.T, CHUNK_A, axis=1)
    else:
        ws = jnp.stack([jnp.kron(jnp.eye(CHUNK_A // t, dtype=F32), a_ws[g, :t, :t]) for g in range(A_GROUPS)])
        bs = jnp.repeat(jnp.tile(a_bs[:, :t], (1, CHUNK_A // t)).T, CHUNK_A, axis=1)
    out_a, v_rows, bq, bk, bv, br, bg = _even_in(
        x, mod, p['norm_mix_g'][layer].reshape(1, d), w_in, p['a_norm_g'][li].reshape(1, -1),
        ws.astype(BF16), bs)
    hk = bq.shape[1]
    dv = bv.shape[1] // B_HEADS
    wg2 = jnp.concatenate([p['b_w_gate2'][li], jnp.zeros((128 - B_GATE_RANK, hk), F32)], axis=0)
    seg = jnp.asarray(_np_seg(B_HEADS, hk // B_HEADS, dv), BF16)
    out_b, s_new = _gla(n, t, bq, bk, bv, br, bg, wg2, p['b_gate_bias'][li].reshape(1, hk),
                        jnp.tile(p['b_norm_g'][li], B_HEADS).reshape(1, -1), gla0[li], seg, eye_hk)
    results['gla'] = s_new
    results['v_rows'] = v_rows
    routed = _out_route(
        x, mod, out_a, out_b, p['w_out'][layer].astype(BF16), p['norm_ffn_g'][layer].reshape(1, d),
        p['w_router'][layer].T, p['b_router'][layer].reshape(-1, 1), eye_r, triu_r)
    x = yield mod, routed

    layer, li = 1, 0
    mod = c_mod[layer]
    w = p['w_in_odd'][li]
    hw = C_HEADS * dh
    o_g = 3 * hw
    o_o = o_g + 2 * C_HEADS
    o_qa = o_o + hw
    o_kva = o_qa + p['d_g_qa'].shape[1]
    o_kr = o_kva + p['d_g_kva'].shape[1]
    w_kr = w[:, o_kr:o_kr + D_ROPE]
    w_odd = jnp.concatenate(
        [w[:, :o_g], w[:, o_o:o_qa], w[:, o_qa:o_kva], w[:, o_kva:o_kr], w_kr, _rot_cols(w_kr),
         w[:, o_g:o_o], jnp.zeros((d, 128 - 2 * C_HEADS), F32)], axis=1).astype(BF16)
    wqb = p['d_w_qb'][li].reshape(-1, D_HEADS, D_NOPE + D_ROPE)
    wqb_r = wqb[:, :, D_NOPE:]
    wqb2 = jnp.concatenate([wqb[:, :, :D_NOPE].reshape(-1, D_HEADS * D_NOPE),
                            wqb_r.reshape(-1, D_HEADS * D_ROPE),
                            _rot_cols(wqb_r).reshape(-1, D_HEADS * D_ROPE)], axis=1).astype(BF16)
    w_uk = p['d_w_uk'][li]
    lat = w_uk.shape[0]
    g_qr = p['d_g_qr'][li]
    g_kr = p['d_g_kr'][li]
    cos, sin = _rope_tables(pos0 + jnp.arange(t), D_HEADS)
    prompt = sample_ctx is None
    if not prompt:
        cos, sin = jnp.tile(cos, (rt.s, 1)), jnp.tile(sin, (rt.s, 1))
    consts = [p['d_g_qa'][li].reshape(1, -1), wqb2, p['d_g_qn'][li].reshape(1, -1),
              jnp.tile(g_qr, D_HEADS).reshape(1, -1), jnp.tile(_swap_halves(g_qr), D_HEADS).reshape(1, -1),
              p['d_g_kva'][li].reshape(1, -1), g_kr.reshape(1, -1), _swap_halves(g_kr).reshape(1, -1),
              p['d_g_kn'][li].reshape(1, -1),
              jnp.asarray(_np_seg(D_HEADS, D_ROPE, D_ROPE), BF16),
              w_uk.reshape(lat, -1).astype(BF16), p['d_w_uv'][li].reshape(lat, -1).astype(BF16)]
    outs = _odd_in(x, mod, p['norm_mix_g'][layer].reshape(1, d), w_odd, consts, cos, sin, prompt)
    cq, ck, cv, co, gates, ckv, kr = outs[:7]
    gb = jnp.concatenate([p['c_ig_bias'][li], p['c_fg_bias'][li],
                          jnp.zeros((128 - 2 * C_HEADS,), F32)]).reshape(1, 128)
    chunk = CHUNK_C if t % CHUNK_C == 0 else t
    tri = jnp.asarray(np.tril(np.ones((chunk, chunk), np.float32)), BF16)
    m0b =jnp.broadcast_to(mm0[li][:, :, None], (n, C_HEADS, dh))
    out_c, c_new, n_new, m_new = _mlstm(n, t, cq, ck, cv, co, gates, gb,
                                        p['c_norm_g'][li].reshape(1, -1), mc0[li], mn0[li], m0b, tri, eye_dh)
    results['mlstm'] = (c_new, n_new, m_new[:, :, 0])
    results['ckv'] = ckv.reshape(n, t, -1)
    results['kr'] = kr.reshape(n, t, -1)
    if prompt:
        qcat, kcat, vv = outs[7:]
        out_d = _flash(n, t, qcat, kcat, vv)
    else:
        u, qr = outs[7:]
        cache_ckv, cache_kr, page_table = sample_ctx
        u3 = u.reshape(n, t, D_HEADS, lat).transpose(0, 2, 1, 3).reshape(n, D_HEADS * t, lat)
        qr3 = qr.reshape(n, t, D_HEADS, D_ROPE).transpose(0, 2, 1, 3).reshape(n, D_HEADS * t, D_ROPE)
        wuk_t = w_uk.transpose(1, 2, 0).reshape(-1, lat).astype(BF16)
        out_d = _paged(page_table, li, u3, qr3, ckv, kr, wuk_t, consts[11], cache_ckv, cache_kr)
    routed = _out_route(
        x, mod, out_c, out_d, p['w_out'][layer].astype(BF16), p['norm_ffn_g'][layer].reshape(1, d),
        p['w_router'][layer].T, p['b_router'][layer].reshape(-1, 1), eye_r, triu_r)
    x = yield mod, routed
    return x, results


def kernel(x_prompt, x_sample, state_gla, state_mlstm_c, state_mlstm_n, state_mlstm_m,
           cache_ckv, cache_krope, page_table, c_prompt, c_sample,
           norm_mix_g, norm_ffn_g, w_ada, b_ada, w_out,
           w_in_even, a_norm_g, a_ws, a_bs, b_w_gate2, b_gate_bias, b_norm_g,
           w_in_odd, c_ig_bias, c_fg_bias, c_norm_g,
           d_g_qa, d_w_qb, d_g_kva, d_g_qn, d_g_qr, d_g_kr, d_g_kn, d_w_uk, d_w_uv,
           w_router, b_router, w_gate_e, w_up_e, w_down_e, w_gate_s, w_up_s, w_down_s):
    p = dict(norm_mix_g=norm_mix_g, norm_ffn_g=norm_ffn_g, w_out=w_out,
             w_in_even=w_in_even, a_norm_g=a_norm_g, a_ws=a_ws, a_bs=a_bs, b_w_gate2=b_w_gate2,
             b_gate_bias=b_gate_bias, b_norm_g=b_norm_g, w_in_odd=w_in_odd, c_ig_bias=c_ig_bias,
             c_fg_bias=c_fg_bias, c_norm_g=c_norm_g, d_g_qa=d_g_qa, d_w_qb=d_w_qb, d_g_kva=d_g_kva,
             d_g_qn=d_g_qn, d_g_qr=d_g_qr, d_g_kr=d_g_kr, d_g_kn=d_g_kn, d_w_uk=d_w_uk, d_w_uv=d_w_uv,
             w_router=w_router, b_router=b_router, w_gate_e=w_gate_e, w_up_e=w_up_e, w_down_e=w_down_e,
             w_gate_s=w_gate_s, w_up_s=w_up_s, w_down_s=w_down_s)
    n_p, t_p, d = x_prompt.shape
    n_s, t_s, _ = x_sample.shape
    depth = w_ada.shape[0]
    pad_p = (-n_p) % 8
    c_all = jnp.concatenate([c_prompt, jnp.zeros((pad_p, d), F32), c_sample], axis=0)
    mod_all = _ada(c_all, w_ada, b_ada).reshape(depth, c_all.shape[0], 6, d)
    mod_p = mod_all[:, :n_p]
    mod_s = mod_all[:, n_p + pad_p:]

    n_even, _, bh, bdk, bdv = state_gla.shape
    n_odd, _, chh, cdh, _ = state_mlstm_c.shape
    gla0_p = jnp.zeros((n_even, n_p, bh, bdk, bdv), F32)
    mc0_p = jnp.zeros((n_odd, n_p, chh, cdh, cdh), F32)
    mn0_p = jnp.zeros((n_odd, n_p, chh, cdh), F32)
    mm0_p = jnp.full((n_odd, n_p, chh), NEG, F32)
    past_len = page_table.shape[1] * PAGE_SIZE

    trunks = [_trunk(x_prompt, mod_p, 0, gla0_p, mc0_p, mn0_p, mm0_p, p, None),
              _trunk(x_sample, mod_s, past_len, state_gla, state_mlstm_c, state_mlstm_n, state_mlstm_m, p,
                     (cache_ckv, cache_krope, page_table))]
    pending = [next(tr) for tr in trunks]
    for layer in range(depth):
        mixed = _moe(layer, pending, p)
        pending = []
        for tr, x_new in zip(trunks, mixed):
            try:
                pending.append(tr.send(x_new))
            except StopIteration as done:
                pending.append(done.value)
    (y_p, rp), (y_s, rs) = pending
    aw = rs['v_rows'].shape[1]
    return (y_p, y_s, rp['gla'][None], rs['gla'][None], rs['v_rows'].reshape(1, n_s, t_s, aw),
            rp['mlstm'][0][None], rs['mlstm'][0][None], rp['mlstm'][1][None], rs['mlstm'][1][None],
            rp['mlstm'][2][None], rs['mlstm'][2][None],
            rp['ckv'][:, None], rs['ckv'][:, None], rp['kr'][:, None], rs['kr'][:, None])
```

```python
import functools

import numpy as np
import jax
import jax.numpy as jnp
from jax import lax
from jax.experimental import pallas as pl
from jax.experimental.pallas import tpu as pltpu

F32 = jnp.float32
BF16 = jnp.bfloat16

EPS = 1e-6
NEG = -1e30

A_GROUPS = 4
CHUNK_A = 128
B_HEADS = 4
B_GATE_RANK = 16
B_GATE_TAU = 16.0
CHUNK_B = 16
C_HEADS = 4
CHUNK_C = 128
D_HEADS = 4
D_NOPE = 128
D_ROPE = 64
ROPE_BASE = 10000.0
MLA_SCALE = (D_NOPE + D_ROPE) ** -0.5
PAGE_SIZE = 128
N_EXPERT_GROUPS = 8
TOPK_GROUPS = 4
TOP_K = 8
ROUTED_SCALE = 2.5

ROW_TILE = 256
SAMPLE_SEQS = 32
FLASH_BLOCK = 1024
PAGES_PER_STEP = 32
VMEM_LIMIT = 56 * 1024 * 1024


def _dot(a, b):
    return jnp.dot(a, b, preferred_element_type=F32)


def _dot_nt(a, b):
    return lax.dot_general(a, b, (((1,), (1,)), ((), ())), preferred_element_type=F32)


def _split2(x):
    hi = x.astype(BF16)
    lo = (x - hi.astype(F32)).astype(BF16)
    return hi, lo


def _split3(x):
    hi = x.astype(BF16)
    r = x - hi.astype(F32)
    mid = r.astype(BF16)
    lo = (r - mid.astype(F32)).astype(BF16)
    return hi, mid, lo


def _dot3(a, b):
    ah, al = _split2(a)
    bh, bl = _split2(b)
    return _dot(ah, bh) + _dot(ah, bl) + _dot(al, bh)


def _dot3_nt(a, b):
    ah, al = _split2(a)
    bh, bl = _split2(b)
    return _dot_nt(ah, bh) + _dot_nt(ah, bl) + _dot_nt(al, bh)


def _dot_sel(x, m01, parts=2):
    ps = _split2(x) if parts == 2 else _split3(x)
    acc = _dot(ps[0], m01)
    for p in ps[1:]:
        acc = acc + _dot(p, m01)
    return acc


def _sigmoid(x):
    return 1.0 / (1.0 + jnp.exp(-x))


def _silu(x):
    return x * _sigmoid(x)


def _log_sigmoid(x):
    return jnp.minimum(x, 0.0) - jnp.log(1.0 + jnp.exp(-jnp.abs(x)))


def _gelu(x):
    return 0.5 * x * (1.0 + jnp.tanh(0.7978845608028654 * (x + 0.044715 * x * x * x)))


def _rms(x, eps=EPS):
    return x * lax.rsqrt(jnp.mean(x * x, axis=-1, keepdims=True) + eps)


def _params(*sem):
    return pltpu.CompilerParams(dimension_semantics=sem, vmem_limit_bytes=VMEM_LIMIT)


def _const_spec(shape):
    nd = len(shape)
    return pl.BlockSpec(shape, lambda *_: (0,) * nd)


class _Rows:
    def __init__(self, n, t):
        self.n, self.t = n, t
        if t % ROW_TILE == 0:
            self.s, self.r = 1, ROW_TILE
            self.tpb = t // ROW_TILE
        else:
            assert t == 8 and n % SAMPLE_SEQS == 0, (n, t)
            self.s, self.r = SAMPLE_SEQS, t
            self.tpb = 1
        self.rows = self.s * self.r
        self.steps = n * t // self.rows

    def seq_spec(self, mid, d):
        s, tpb = self.s, self.tpb
        if mid == self.t:
            return pl.BlockSpec((s, self.r, d), lambda i: (i // tpb, i % tpb, 0))
        return pl.BlockSpec((s, mid, d), lambda i: (i // tpb, 0, 0))

    def row_spec(self, c):
        return pl.BlockSpec((self.rows, c), lambda i: (i, 0))

    def tile_spec(self, sub):
        return pl.BlockSpec((self.rows * sub, 128), lambda i: (i, 0))

    def pos_spec(self, c):
        tpb = self.tpb
        return pl.BlockSpec((self.rows, c), lambda i: (i % tpb, 0))


def _ada_kernel(c_ref, w_ref, b_ref, o_ref):
    o_ref[0] = _dot3(_silu(c_ref[...]), w_ref[0]) + b_ref[0]


def _ada(c_all, w_ada, b_ada):
    depth, d, d6 = w_ada.shape
    nc = c_all.shape[0]
    tn = 1536
    return pl.pallas_call(
        _ada_kernel,
        grid=(depth, d6 // tn),
        in_specs=[pl.BlockSpec((nc, d), lambda l, j: (0, 0)),
                  pl.BlockSpec((1, d, tn), lambda l, j: (l, 0, j)),
                  pl.BlockSpec((1, 1, tn), lambda l, j: (l, 0, j))],
        out_specs=pl.BlockSpec((1, nc, tn), lambda l, j: (l, 0, j)),
        out_shape=jax.ShapeDtypeStruct((depth, nc, d6), F32),
        compiler_params=_params("arbitrary", "arbitrary"),
        name="ada_modulation",
    )(c_all, w_ada, b_ada.reshape(depth, 1, d6))


def _norm_mod(x_ref, mod_ref, g_ref, shift_row, scale_row):
    x = x_ref[...]
    h = _rms(x) * g_ref[...]
    h = h * (1.0 + mod_ref[:, scale_row:scale_row + 1, :]) + mod_ref[:, shift_row:shift_row + 1, :]
    s, r, d = x.shape
    return h.reshape(s * r, d)


def _even_in_kernel(x_ref, mod_ref, g_ref, w_ref, an_ref, ws_ref, bs_ref,
                    oa_ref, v_ref, q_ref, k_ref, bv_ref, r_ref, bg_ref):
    h = _norm_mod(x_ref, mod_ref, g_ref, 0, 1).astype(BF16)
    z = _dot(h, w_ref[...])
    aw = A_GROUPS * CHUNK_A
    rows = z.shape[0]
    u = _gelu(z[:, 0:aw])
    va = _gelu(z[:, aw:2 * aw])
    vn = jnp.concatenate(
        [_rms(va[:, g * CHUNK_A:(g + 1) * CHUNK_A]) for g in range(A_GROUPS)], axis=1) * an_ref[...]
    v_ref[...] = vn
    vb = vn.astype(BF16)
    for c in range(rows // CHUNK_A):
        r0 = c * CHUNK_A
        mixed = jnp.concatenate(
            [_dot(ws_ref[g], vb[r0:r0 + CHUNK_A, g * CHUNK_A:(g + 1) * CHUNK_A]) for g in range(A_GROUPS)],
            axis=1) + bs_ref[...]
        oa_ref[r0:r0 + CHUNK_A, :] = u[r0:r0 + CHUNK_A, :] * mixed
    o = 2 * aw
    q_ref[...] = z[:, o:o + 256]
    k_ref[...] = z[:, o + 256:o + 512]
    bv_ref[...] = z[:, o + 512:o + 1024]
    r_ref[...] = z[:, o + 1024:o + 1536]
    bg_ref[...] = z[:, o + 1536:o + 1664]


def _even_in(x, mod, g, w, an, ws, bs):
    n, t, d = x.shape
    rt = _Rows(n, t)
    m = n * t
    widths = (512, 512, 256, 256, 512, 512, 128)
    return pl.pallas_call(
        _even_in_kernel,
        grid=(rt.steps,),
        in_specs=[rt.seq_spec(t, d), rt.seq_spec(6, d), _const_spec((1, d)), _const_spec(w.shape),
                  _const_spec(an.shape), _const_spec(ws.shape), _const_spec(bs.shape)],
        out_specs=[rt.row_spec(c) for c in widths],
        out_shape=[jax.ShapeDtypeStruct((m, c), F32) for c in widths],
        compiler_params=_params("arbitrary"),
        name="even_in_chunk_gate",
    )(x, mod, g, w, an, ws, bs)


def _gla_kernel(q_ref, k_ref, v_ref, r_ref, g_ref, wg_ref, gb_ref, ng_ref, s0_ref, seg_ref, eye_ref,
                o_ref, so_ref, s_scr, lg_scr, o_scr, *, chunk, n_chunks):
    seqs, _, hk = q_ref.shape
    dk = hk // B_HEADS
    dv = v_ref.shape[2] // B_HEADS
    t = pl.program_id(1)

    @pl.when(t == 0)
    def _():
        for sq in range(seqs):
            s_scr[sq] = s0_ref[sq].reshape(hk, dv)

    for sq in range(seqs):
        lg_scr[sq] = _log_sigmoid(_dot3(g_ref[sq], wg_ref[...]) + gb_ref[...]) * (1.0 / B_GATE_TAU)
    row = lax.broadcasted_iota(jnp.int32, (chunk, hk), 0)
    lane = lax.broadcasted_iota(jnp.int32, (chunk, hk), 1)
    pad = 16 - chunk if chunk < 16 else 0

    def pad_rows(a):
        if pad:
            return jnp.concatenate([a, jnp.zeros((pad, a.shape[1]), a.dtype)], axis=0)
        return a

    def one_chunk(sq, r0):
        lg = lg_scr[sq, pl.ds(r0, chunk), :]
        q = q_ref[sq, pl.ds(r0, chunk), :] * (dk ** -0.5)
        k = k_ref[sq, pl.ds(r0, chunk), :]
        v = v_ref[sq, pl.ds(r0, chunk), :]
        b = jnp.zeros_like(lg)
        for s in range(chunk):
            b = b + jnp.where(row >= s, lg[s:s + 1, :], 0.0)
        prods = []
        for l in range(chunk):
            e = jnp.exp(jnp.where(row <= l, b[l:l + 1, :] - b, NEG))
            prods.append(q[l:l + 1, :] * k * e)
        att = _dot(jnp.concatenate(prods, axis=0).astype(BF16), seg_ref[...])
        o_intra = jnp.concatenate(
            [jnp.sum(att[l * chunk:(l + 1) * chunk, :] * v, axis=0, keepdims=True) for l in range(chunk)], axis=0)
        s_old = s_scr[sq]
        s_b = s_old.astype(BF16)
        qe = q * jnp.exp(b)
        o_inter = jnp.concatenate(
            [_dot(jnp.where(lane // dk == h, qe, 0.0).astype(BF16), s_b) for h in range(B_HEADS)], axis=1)
        o_scr[sq, pl.ds(r0, chunk), :] = o_intra + o_inter
        b_last = b[chunk - 1:chunk, :]
        k_end = pad_rows((k * jnp.exp(b_last - b)).astype(BF16))
        k_t = _dot_nt(eye_ref[...], k_end).astype(BF16)
        dh, dm, dl = _split3(jnp.exp(b_last))
        dec3 = _dot_nt(eye_ref[...], jnp.concatenate([dh, dm, dl, jnp.zeros((13, hk), BF16)], axis=0))
        dec = dec3[:, 0:1] + dec3[:, 1:2] + dec3[:, 2:3]
        v_b = pad_rows(v.astype(BF16))
        kv = jnp.concatenate(
            [_dot(k_t[h * dk:(h + 1) * dk, :], v_b[:, h * dv:(h + 1) * dv]) for h in range(B_HEADS)], axis=0)
        s_scr[sq] = dec * s_old + kv

    def body(c, carry):
        r0 = pl.multiple_of(c * chunk, chunk)
        for sq in range(seqs):
            one_chunk(sq, r0)
        return carry

    lax.fori_loop(0, n_chunks, body, 0)

    for sq in range(seqs):
        o = o_scr[sq]
        on = jnp.concatenate([_rms(o[:, h * dv:(h + 1) * dv]) for h in range(B_HEADS)], axis=1) * ng_ref[...]
        o_ref[sq] = on * _silu(r_ref[sq])

    @pl.when(t == pl.num_programs(1) - 1)
    def _():
        for sq in range(seqs):
            so_ref[sq] = s_scr[sq].reshape(B_HEADS, dk, dv)


GLA_SEQS = 2


def _gla(n, t, q, k, v, r, g, wg, gb, ng, s0, seg, eye):
    tt = ROW_TILE if t % ROW_TILE == 0 else t
    chunk = CHUNK_B if tt % CHUNK_B == 0 else tt
    nt = t // tt
    hk, hv = q.shape[1], v.shape[1]
    dk, dv = hk // B_HEADS, hv // B_HEADS
    seqs = GLA_SEQS
    assert n % seqs == 0

    def rows(c):
        return pl.BlockSpec((seqs, tt, c), lambda i, j: (i, j, 0))

    def per_seq(a):
        return a.reshape(n, t, a.shape[1])

    st = pl.BlockSpec((seqs, B_HEADS, dk, dv), lambda i, j: (i, 0, 0, 0))
    o, s_new = pl.pallas_call(
        functools.partial(_gla_kernel, chunk=chunk, n_chunks=tt // chunk),
        grid=(n // seqs, nt),
        in_specs=[rows(hk), rows(hk), rows(hv), rows(hv), rows(g.shape[1]),
                  pl.BlockSpec(wg.shape, lambda i, j: (0, 0)), pl.BlockSpec(gb.shape, lambda i, j: (0, 0)),
                  pl.BlockSpec(ng.shape, lambda i, j: (0, 0)), st,
                  pl.BlockSpec(seg.shape, lambda i, j: (0, 0)), pl.BlockSpec(eye.shape, lambda i, j: (0, 0))],
        out_specs=[rows(hv), st],
        out_shape=[jax.ShapeDtypeStruct((n, t, hv), F32), jax.ShapeDtypeStruct(s0.shape, F32)],
        scratch_shapes=[pltpu.VMEM((seqs, hk, dv), F32), pltpu.VMEM((seqs, tt, hk), F32),
                        pltpu.VMEM((seqs, tt, hv), F32)],
        compiler_params=_params("arbitrary", "arbitrary"),
        name="gla_scan",
    )(per_seq(q), per_seq(k), per_seq(v), per_seq(r), per_seq(g), wg, gb, ng, s0, seg, eye)
    return o.reshape(n * t, hv), s_new


def _route(logits_t, bias_col):
    n_e, r = logits_t.shape
    per = n_e // N_EXPERT_GROUPS
    scores = _sigmoid(logits_t)
    biased = scores + bias_col
    sub = lax.broadcasted_iota(jnp.int32, (per, r), 0).astype(F32)
    ninf = -jnp.inf
    gs = []
    for g in range(N_EXPERT_GROUPS):
        blk = biased[g * per:(g + 1) * per, :]
        m1 = jnp.max(blk, axis=0, keepdims=True)
        i1 = jnp.min(jnp.where(blk == m1, sub, float(per)), axis=0, keepdims=True)
        m2 = jnp.max(jnp.where(sub == i1, ninf, blk), axis=0, keepdims=True)
        gs.append(m1 + m2)
    cur = jnp.concatenate(gs, axis=0)
    gsub = lax.broadcasted_iota(jnp.int32, (N_EXPERT_GROUPS, r), 0).astype(F32)
    gsel = jnp.zeros((N_EXPERT_GROUPS, r), F32)
    for _ in range(TOPK_GROUPS):
        m = jnp.max(cur, axis=0, keepdims=True)
        i = jnp.min(jnp.where(cur == m, gsub, float(N_EXPERT_GROUPS)), axis=0, keepdims=True)
        hit = gsub == i
        gsel = jnp.where(hit, 1.0, gsel)
        cur = jnp.where(hit, ninf, cur)
    cur = jnp.concatenate(
        [jnp.where(gsel[g:g + 1, :] > 0.5, biased[g * per:(g + 1) * per, :], ninf) for g in range(N_EXPERT_GROUPS)],
        axis=0)
    esub = lax.broadcasted_iota(jnp.int32, (n_e, r), 0).astype(F32)
    idx, wts, hits = [], [], []
    for _ in range(TOP_K):
        m = jnp.max(cur, axis=0, keepdims=True)
        i = jnp.min(jnp.where(cur == m, esub, float(n_e)), axis=0, keepdims=True)
        hit = esub == i
        idx.append(i)
        hits.append(hit)
        wts.append(jnp.sum(jnp.where(hit, scores, 0.0), axis=0, keepdims=True))
        cur = jnp.where(hit, ninf, cur)
    w = jnp.concatenate(wts, axis=0)
    w = w / jnp.sum(w, axis=0, keepdims=True) * ROUTED_SCALE
    return jnp.concatenate(idx, axis=0), w, hits


def _pack_pairs(x):
    w = x.shape[1] // 2
    hi = lax.bitcast_convert_type(x[:, :w].astype(BF16).astype(F32), jnp.uint32)
    lo = lax.bitcast_convert_type(x[:, w:].astype(BF16).astype(F32), jnp.uint32)
    return hi | (lo >> 16)


def _unpack_pairs(pk):
    a = lax.bitcast_convert_type(pk & jnp.uint32(0xFFFF0000), F32)
    b = lax.bitcast_convert_type(pk << 16, F32)
    return jnp.concatenate([a, b], axis=1)


def _store_row_tiles(ref, x):
    rows = x.shape[0]
    pk = _pack_pairs(x)
    sub = pk.shape[1] // 128
    for c in range(sub):
        ref[pl.ds(c, rows, stride=sub), :] = pk[:, c * 128:(c + 1) * 128]


def _load_row_tiles(ref, sub):
    rows = ref.shape[0] // sub
    return _unpack_pairs(jnp.concatenate([ref[pl.ds(c, rows, stride=sub), :] for c in range(sub)], axis=1))


def _out_kernel(x_ref, mod_ref, ma_ref, mb_ref, wo_ref, g_ref, wr_ref, br_ref, eye_ref, triu_ref,
                x1_ref, hp_ref, idx_ref, rank_ref, w3_ref, cnt_ref, run_scr):
    step = pl.program_id(0)

    @pl.when(step == 0)
    def _():
        run_scr[...] = jnp.zeros_like(run_scr)

    half = ma_ref.shape[1]
    y = _dot(ma_ref[...].astype(BF16), wo_ref[0:half, :]) + _dot(mb_ref[...].astype(BF16), wo_ref[half:, :])
    x = x_ref[...]
    s, r, d = x.shape
    x1 = x + mod_ref[:, 2:3, :] * y.reshape(s, r, d)
    x1_ref[...] = x1
    h2 = (_rms(x1) * g_ref[...] * (1.0 + mod_ref[:, 4:5, :]) + mod_ref[:, 3:4, :]).reshape(s * r, d)
    _store_row_tiles(hp_ref, h2)
    idx, w, hits = _route(_dot3_nt(wr_ref[...], h2), br_ref[...])
    idx_ref[0] = idx.astype(jnp.int32)
    sel = jnp.zeros(hits[0].shape, F32)
    for hit in hits:
        sel = jnp.where(hit, 1.0, sel)
    before = run_scr[:, 0:1] + _dot(sel.astype(BF16), triu_ref[...]) - sel
    rank_ref[0] = jnp.concatenate(
        [jnp.sum(jnp.where(hit, before, 0.0), axis=0, keepdims=True) for hit in hits], axis=0).astype(jnp.int32)
    run_scr[...] = run_scr[...] + jnp.sum(sel, axis=1, keepdims=True)
    cnt_ref[...] = run_scr[...]
    wh, wm, wl = _split3(w)
    stack = jnp.concatenate([wh, wm, wl, jnp.zeros((128 - 3 * TOP_K, s * r), BF16)], axis=0)
    w3_ref[...] = _dot_nt(eye_ref[...], stack)


def _out_route(x, mod, mix_a, mix_b, wo, g, wr_t, br, eye, triu):
    n, t, d = x.shape
    rt = _Rows(n, t)
    m = n * t
    n_e = wr_t.shape[0]
    slot = pl.BlockSpec((1, TOP_K, rt.rows), lambda i: (i, 0, 0))
    return pl.pallas_call(
        _out_kernel,
        grid=(rt.steps,),
        in_specs=[rt.seq_spec(t, d), rt.seq_spec(6, d), rt.row_spec(mix_a.shape[1]), rt.row_spec(mix_b.shape[1]),
                  _const_spec(wo.shape), _const_spec((1, d)), _const_spec(wr_t.shape), _const_spec(br.shape),
                  _const_spec(eye.shape), _const_spec(triu.shape)],
        out_specs=[rt.seq_spec(t, d), rt.tile_spec(d // 256), slot, slot, rt.row_spec(128), _const_spec((n_e, 128))],
        out_shape=[jax.ShapeDtypeStruct((n, t, d), F32), jax.ShapeDtypeStruct((m * (d // 256), 128), jnp.uint32),
                   jax.ShapeDtypeStruct((rt.steps, TOP_K, rt.rows), jnp.int32),
                   jax.ShapeDtypeStruct((rt.steps, TOP_K, rt.rows), jnp.int32),
                   jax.ShapeDtypeStruct((m, 128), F32), jax.ShapeDtypeStruct((n_e, 128), F32)],
        scratch_shapes=[pltpu.VMEM((n_e, 128), F32)],
        compiler_params=_params("arbitrary"),
        name="out_proj_route",
    )(x, mod, mix_a, mix_b, wo, g, wr_t, br, eye, triu)


ROW_SUB = 4


def _row_copy(src_ref, src_row, dst_ref, dst_row, sem):
    src = src_ref.at[pl.ds(pl.multiple_of(src_row * ROW_SUB, ROW_SUB), ROW_SUB), :]
    dst = dst_ref.at[pl.ds(pl.multiple_of(dst_row * ROW_SUB, ROW_SUB), ROW_SUB), :]
    return pltpu.make_async_copy(src, dst, sem)


def _dispatch_kernel(dest_ref, hp_ref, *rest):
    xs_ref, sem = rest[-2:]
    rows = hp_ref.shape[0] // ROW_SUB

    def issue(tok, carry):
        for k in range(TOP_K):
            _row_copy(hp_ref, tok, xs_ref, dest_ref[0, k, tok], sem).start(priority=k % 2)
        return carry

    lax.fori_loop(0, rows, issue, 0, unroll=2)

    def drain(j, carry):
        _row_copy(hp_ref, 0, xs_ref, 0, sem).wait()
        return carry

    lax.fori_loop(0, TOP_K * rows, drain, 0, unroll=16)


def _dispatch(dest3, hp, n_rows, xs_prev=None):
    steps, _, rows = dest3.shape
    assert rows & (rows - 1) == 0
    chained = xs_prev is not None
    return pl.pallas_call(
        _dispatch_kernel,
        grid=(steps,),
        in_specs=[pl.BlockSpec((1, TOP_K, rows), lambda i: (i, 0, 0), memory_space=pltpu.SMEM),
                  pl.BlockSpec((rows * ROW_SUB, 128), lambda i: (i, 0))]
                 + ([pl.BlockSpec(memory_space=pl.ANY)] if chained else []),
        out_specs=pl.BlockSpec(memory_space=pl.ANY),
        out_shape=jax.ShapeDtypeStruct((n_rows * ROW_SUB, 128), jnp.uint32),
        scratch_shapes=[pltpu.SemaphoreType.DMA(())],
        input_output_aliases={2: 0} if chained else {},
        compiler_params=_params("arbitrary"),
        name="moe_dispatch_rows",
    )(dest3, hp, *([xs_prev] if chained else []))


def _gmm_kernel(be_ref, va_ref, nu_ref, x_ref, wg_ref, wu_ref, wd_ref, y_ref):
    i = pl.program_id(0)

    @pl.when(i < nu_ref[0])
    def _():
        x = _load_row_tiles(x_ref, ROW_SUB)
        live = lax.broadcasted_iota(jnp.int32, x.shape, 0) < va_ref[i]
        x = jnp.where(live, x, 0.0).astype(BF16)
        a = _silu(_dot(x, wg_ref[...].astype(BF16))) * _dot(x, wu_ref[...].astype(BF16))
        _store_row_tiles(y_ref, _dot(a.astype(BF16), wd_ref[...].astype(BF16)))

    @pl.when(i >= nu_ref[0])
    def _():
        y_ref[...] = jnp.zeros_like(y_ref)


def _gmm(layer, block_e, valid, n_used, xs, wg, wu, wd, tm):
    r = xs.shape[0] // ROW_SUB
    d, de = wg.shape[-2:]
    assert d == ROW_SUB * 256
    nb = r // tm
    grid_spec = pltpu.PrefetchScalarGridSpec(
        num_scalar_prefetch=3,
        grid=(nb,),
        in_specs=[pl.BlockSpec((tm * ROW_SUB, 128), lambda i, be, va, nu: (i, 0)),
                  pl.BlockSpec((None, None, d, de), lambda i, be, va, nu: (layer, be[i], 0, 0)),
                  pl.BlockSpec((None, None, d, de), lambda i, be, va, nu: (layer, be[i], 0, 0)),
                  pl.BlockSpec((None, None, de, d), lambda i, be, va, nu: (layer, be[i], 0, 0))],
        out_specs=pl.BlockSpec((tm * ROW_SUB, 128), lambda i, be, va, nu: (i, 0)),
    )
    return pl.pallas_call(
        _gmm_kernel,
        grid_spec=grid_spec,
        out_shape=jax.ShapeDtypeStruct(xs.shape, jnp.uint32),
        compiler_params=_params("arbitrary"),
        name="moe_grouped_swiglu",
    )(block_e, valid, n_used, xs, wg, wu, wd)


def _combine_kernel(dest_ref, x_ref, mod_ref, hp_ref, w3_ref, wg_ref, wu_ref, wd_ref, ys_ref, o_ref, yg_scr, sem):
    rows = hp_ref.shape[0] // ROW_SUB

    def issue(tok, carry):
        for k in range(TOP_K):
            _row_copy(ys_ref, dest_ref[0, k, tok], yg_scr.at[k], tok, sem).start(priority=k % 2)
        return carry

    lax.fori_loop(0, rows, issue, 0, unroll=2)
    h = _load_row_tiles(hp_ref, ROW_SUB).astype(BF16)
    a = _silu(_dot(h, wg_ref[...])) * _dot(h, wu_ref[...])
    acc = _dot(a.astype(BF16), wd_ref[...])

    def drain(j, carry):
        _row_copy(ys_ref, 0, yg_scr.at[0], 0, sem).wait()
        return carry

    lax.fori_loop(0, TOP_K * rows, drain, 0, unroll=16)
    w3 = w3_ref[...]
    for k in range(TOP_K):
        wk = w3[:, k:k + 1] + w3[:, TOP_K + k:TOP_K + k + 1] + w3[:, 2 * TOP_K + k:2 * TOP_K + k + 1]
        acc = acc + wk * _load_row_tiles(yg_scr.at[k], ROW_SUB)
    x = x_ref[...]
    o_ref[...] = x + mod_ref[:, 5:6, :] * acc.reshape(x.shape)


def _combine(dest3, x1, mod, hp, w3, wg, wu, wd, ys):
    n, t, d = x1.shape
    rt = _Rows(n, t)
    assert rt.rows & (rt.rows - 1) == 0
    return pl.pallas_call(
        _combine_kernel,
        grid=(rt.steps,),
        in_specs=[pl.BlockSpec((1, TOP_K, rt.rows), lambda i: (i, 0, 0), memory_space=pltpu.SMEM),
                  rt.seq_spec(t, d), rt.seq_spec(6, d), rt.tile_spec(ROW_SUB), rt.row_spec(128),
                  _const_spec(wg.shape), _const_spec(wu.shape), _const_spec(wd.shape),
                  pl.BlockSpec(memory_space=pl.ANY)],
        out_specs=rt.seq_spec(t, d),
        out_shape=jax.ShapeDtypeStruct((n, t, d), F32),
        scratch_shapes=[pltpu.VMEM((TOP_K, rt.rows * ROW_SUB, 128), jnp.uint32), pltpu.SemaphoreType.DMA(())],
        compiler_params=_params("arbitrary"),
        name="moe_combine_shared",
    )(dest3, x1, mod, hp, w3, wg, wu, wd, ys)


def _moe(layer, groups, p):
    n_e = p['w_gate_e'].shape[1]
    a = sum(g[1][2].size for g in groups)
    tm = 512 if a // n_e >= 1024 else 128
    counts = [g[1][5][:, 0].astype(jnp.int32) for g in groups]
    total = sum(counts)
    padded = (total + tm - 1) // tm * tm
    pend = jnp.cumsum(padded)
    pstart = pend - padded
    experts = jnp.arange(n_e, dtype=jnp.int32)
    dests = []
    base = pstart
    for (_, routed), cnt in zip(groups, counts):
        onehot = routed[2][None] == experts[:, None, None, None]
        dests.append(routed[3] + jnp.sum(jnp.where(onehot, base[:, None, None, None], 0), axis=0))
        base = base + cnt
    n_blocks = -(-a // tm) + n_e
    first = jnp.arange(n_blocks, dtype=jnp.int32) * tm
    block_e = jnp.minimum(jnp.sum((pend[None, :] <= first[:, None]).astype(jnp.int32), axis=1), n_e - 1)
    mine = block_e[:, None] == experts[None, :]
    last = jnp.sum(jnp.where(mine, (pstart + total)[None, :], 0), axis=1)
    valid = jnp.clip(last - first, 0, tm).astype(jnp.int32)
    n_used = (pend[-1] // tm).astype(jnp.int32).reshape(1)
    xs = None
    for (_, routed), dest3 in zip(groups, dests):
        xs = _dispatch(dest3, routed[1], n_blocks * tm, xs)
    ys = _gmm(layer, block_e, valid, n_used, xs, p['w_gate_e'], p['w_up_e'], p['w_down_e'], tm)
    shared = [p[name][layer].astype(BF16) for name in ('w_gate_s', 'w_up_s', 'w_down_s')]
    return [_combine(dest3, routed[0], mod, routed[1], routed[4], *shared, ys)
            for (mod, routed), dest3 in zip(groups, dests)]


def _odd_in_kernel(x_ref, mod_ref, g_ref, w_ref, gqa_ref, wqb_ref, gqn_ref, gqr_ref, gqrs_ref, gkva_ref,
                   gkr_ref, gkrs_ref, gkn_ref, cos_ref, sin_ref, seg_ref, wuk_ref, wuv_ref,
                   cq_ref, ck_ref, cv_ref, co_ref, gt_ref, ckv_ref, kr_ref, *outs, prompt):
    h = _norm_mod(x_ref, mod_ref, g_ref, 0, 1).astype(BF16)
    z = _dot(h, w_ref[...])
    cq_ref[...] = z[:, 0:512]
    ck_ref[...] = z[:, 512:1024]
    cv_ref[...] = z[:, 1024:1536]
    co_ref[...] = z[:, 1536:2048]
    gt_ref[...] = z[:, 2560:2688]
    cos = cos_ref[...]
    sin = sin_ref[...]
    ckv = _rms(z[:, 2304:2432]) * gkva_ref[...]
    ckv_ref[...] = ckv
    xr = z[:, 2432:2496]
    xr_rot = z[:, 2496:2560]
    rr = lax.rsqrt(jnp.mean(xr * xr, axis=-1, keepdims=True) + EPS)
    kr = rr * (xr * gkr_ref[...] * cos[:, 0:D_ROPE] + xr_rot * gkrs_ref[...] * sin[:, 0:D_ROPE])
    kr_ref[...] = kr
    qa = (_rms(z[:, 2048:2304]) * gqa_ref[...]).astype(BF16)
    qd = _dot(qa, wqb_ref[...])
    nw = D_HEADS * D_NOPE
    rw = D_HEADS * D_ROPE
    qn = [_rms(qd[:, hh * D_NOPE:(hh + 1) * D_NOPE]) * gqn_ref[...] * MLA_SCALE for hh in range(D_HEADS)]
    xq = qd[:, nw:nw + rw]
    xq_rot = qd[:, nw + rw:nw + 2 * rw]
    rq = lax.rsqrt(_dot_sel(xq * xq, seg_ref[...]) * (1.0 / D_ROPE) + EPS)
    qr = rq * (xq * gqr_ref[...] * cos + xq_rot * gqrs_ref[...] * sin) * MLA_SCALE
    ckv_b = ckv.astype(BF16)
    if prompt:
        qcat_ref, kcat_ref, v_ref = outs
        kn = _dot(ckv_b, wuk_ref[...])
        zpad = jnp.zeros((z.shape[0], 256 - D_NOPE - D_ROPE), F32)
        qcat_ref[...] = jnp.concatenate(
            [piece for hh in range(D_HEADS) for piece in (qn[hh], qr[:, hh * D_ROPE:(hh + 1) * D_ROPE], zpad)],
            axis=1).astype(BF16)
        kcat_ref[...] = jnp.concatenate(
            [piece for hh in range(D_HEADS)
             for piece in (_rms(kn[:, hh * D_NOPE:(hh + 1) * D_NOPE]) * gkn_ref[...], kr, zpad)],
            axis=1).astype(BF16)
        v_ref[...] = _dot(ckv_b, wuv_ref[...]).astype(BF16)
    else:
        u_ref, qr_ref = outs
        u_ref[...] = jnp.concatenate(
            [_dot_nt((qn[hh] * gkn_ref[...]).astype(BF16), wuk_ref[:, hh * D_NOPE:(hh + 1) * D_NOPE])
             for hh in range(D_HEADS)], axis=1).astype(BF16)
        qr_ref[...] = qr.astype(BF16)


def _odd_in(x, mod, g, w, consts, cos, sin, prompt):
    n, t, d = x.shape
    rt = _Rows(n, t)
    m = n * t
    widths = [(512, F32)] * 4 + [(128, F32), (128, F32), (D_ROPE, F32)]
    if prompt:
        widths += [(1024, BF16), (1024, BF16), (512, BF16)]
    else:
        widths += [(512, BF16), (256, BF16)]
    return pl.pallas_call(
        functools.partial(_odd_in_kernel, prompt=prompt),
        grid=(rt.steps,),
        in_specs=[rt.seq_spec(t, d), rt.seq_spec(6, d), _const_spec((1, d)), _const_spec(w.shape)]
                 + [_const_spec(c.shape) for c in consts[:9]]
                 + [rt.pos_spec(cos.shape[1]), rt.pos_spec(sin.shape[1])]
                 + [_const_spec(c.shape) for c in consts[9:]],
        out_specs=[rt.row_spec(c) for c, _ in widths],
        out_shape=[jax.ShapeDtypeStruct((m, c), dt) for c, dt in widths],
        compiler_params=_params("arbitrary"),
        name="odd_in_latent_prep",
    )(x, mod, g, w, *consts[:9], cos, sin, *consts[9:])


def _mlstm_kernel(q_ref, k_ref, v_ref, og_ref, gt_ref, gb_ref, ng_ref, c0_ref, n0_ref, m0_ref, tri_ref, eye_ref,
                  o_ref, co_ref, no_ref, mo_ref, c_scr, n_scr, m_scr, *, chunk):
    t = pl.program_id(1)

    @pl.when(t == 0)
    def _():
        c_scr[...] = c0_ref[...]
        n_scr[...] = n0_ref[...]
        m_scr[...] = m0_ref[...]

    for sq in range(q_ref.shape[0]):
        _mlstm_chunk(q_ref.at[sq], k_ref.at[sq], v_ref.at[sq], og_ref.at[sq], gt_ref.at[sq], gb_ref, ng_ref,
                     tri_ref, eye_ref, o_ref.at[sq], c_scr.at[sq], n_scr.at[sq], m_scr.at[sq], chunk)

    @pl.when(t == pl.num_programs(1) - 1)
    def _():
        co_ref[...] = c_scr[...]
        no_ref[...] = n_scr[...]
        mo_ref[...] = m_scr[...]


def _mlstm_chunk(q_ref, k_ref, v_ref, og_ref, gt_ref, gb_ref, ng_ref, tri_ref, eye_ref,
                 o_ref, c_scr, n_scr, m_scr, chunk):
    dh = q_ref.shape[1] // C_HEADS
    pre = gt_ref[...] + gb_ref[...]
    lf = _log_sigmoid(pre)
    if chunk >= 16:
        l_hi, l_mid, l_lo = _split3(lf)
        f_cum = _dot(tri_ref[...], l_hi) + _dot(tri_ref[...], l_mid) + _dot(tri_ref[...], l_lo)
    else:
        rowg = lax.broadcasted_iota(jnp.int32, lf.shape, 0)
        f_cum = jnp.zeros_like(lf)
        for s in range(chunk):
            f_cum = f_cum + jnp.where(rowg >= s, lf[s:s + 1, :], 0.0)
    ri = lax.broadcasted_iota(jnp.int32, (chunk, chunk), 0)
    ci = lax.broadcasted_iota(jnp.int32, (chunk, chunk), 1)
    pad = 16 - chunk if chunk < 16 else 0

    def pad_rows(a):
        if pad:
            return jnp.concatenate([a, jnp.zeros((pad, a.shape[1]), a.dtype)], axis=0)
        return a

    for h in range(C_HEADS):
        sl = slice(h * dh, (h + 1) * dh)
        q = q_ref[:, sl]
        k = k_ref[:, sl] * (dh ** -0.5)
        v = v_ref[:, sl]
        qb, kb, vb = q.astype(BF16), k.astype(BF16), v.astype(BF16)
        f_col = f_cum[:, C_HEADS + h:C_HEADS + h + 1]
        i_col = pre[:, h:h + 1]
        m_prev = m_scr[h:h + 1, 0:1]
        a_row = jnp.sum(jnp.where(ri == ci, i_col - f_col, 0.0), axis=0, keepdims=True)
        log_d = jnp.where(ci <= ri, f_col + a_row, NEG)
        inter = f_col + m_prev
        m_t = jnp.maximum(inter, jnp.max(log_d, axis=-1, keepdims=True))
        w_inter = jnp.exp(inter - m_t)
        qk = _dot_nt(qb, kb) * jnp.exp(log_d - m_t)
        c_old = c_scr[h]
        n_old = n_scr[h:h + 1, :]
        num = _dot(qk.astype(BF16), vb) + w_inter * _dot_nt(qb, c_old.astype(BF16))
        den = jnp.sum(qk, axis=-1, keepdims=True) + w_inter * jnp.sum(q * n_old, axis=-1, keepdims=True)
        hh = num / jnp.maximum(jnp.abs(den), jnp.exp(-m_t))
        o_ref[:, sl] = _rms(hh) * ng_ref[...] * _sigmoid(og_ref[:, sl])
        f_last = f_col[chunk - 1:chunk, :]
        a_end = f_last - f_col + i_col
        m_new = jnp.maximum(f_last + m_prev, jnp.max(a_end, axis=0, keepdims=True))
        w = jnp.exp(a_end - m_new)
        dec = jnp.exp(f_last + m_prev - m_new)
        wv_t = _dot_nt(eye_ref[...], pad_rows((w * v).astype(BF16))).astype(BF16)
        c_scr[h] = dec * c_old + _dot(wv_t, pad_rows(kb))
        n_scr[h:h + 1, :] = dec * n_old + jnp.sum(w * k, axis=0, keepdims=True)
        m_scr[h:h + 1, :] = jnp.broadcast_to(m_new, (1, m_scr.shape[1]))


MLSTM_SEQS = 2


def _mlstm(n, t, q, k, v, og, gt, gb, ng, c0, n0, m0, tri, eye):
    chunk = CHUNK_C if t % CHUNK_C == 0 else t
    nt = t // chunk
    w = q.shape[1]
    dh = w // C_HEADS
    seqs = MLSTM_SEQS
    assert n % seqs == 0

    def rows(c):
        return pl.BlockSpec((seqs, chunk, c), lambda i, j: (i, j, 0))

    def per_seq(a):
        return a.reshape(n, t, a.shape[1])

    cst = pl.BlockSpec((seqs, C_HEADS, dh, dh), lambda i, j: (i, 0, 0, 0))
    nst = pl.BlockSpec((seqs, C_HEADS, dh), lambda i, j: (i, 0, 0))
    o, c_new, n_new, m_new = pl.pallas_call(
        functools.partial(_mlstm_kernel, chunk=chunk),
        grid=(n // seqs, nt),
        in_specs=[rows(w), rows(w), rows(w), rows(w), rows(128),
                  pl.BlockSpec(gb.shape, lambda i, j: (0, 0)), pl.BlockSpec(ng.shape, lambda i, j: (0, 0)),
                  cst, nst, nst,
                  pl.BlockSpec(tri.shape, lambda i, j: (0, 0)), pl.BlockSpec(eye.shape, lambda i, j: (0, 0))],
        out_specs=[rows(w), cst, nst, nst],
        out_shape=[jax.ShapeDtypeStruct((n, t, w), F32), jax.ShapeDtypeStruct(c0.shape, F32),
                   jax.ShapeDtypeStruct(n0.shape, F32), jax.ShapeDtypeStruct(m0.shape, F32)],
        scratch_shapes=[pltpu.VMEM((seqs, C_HEADS, dh, dh), F32), pltpu.VMEM((seqs, C_HEADS, dh), F32),
                        pltpu.VMEM((seqs, C_HEADS, dh), F32)],
        compiler_params=_params("arbitrary", "arbitrary"),
        name="mlstm_scan",
    )(per_seq(q), per_seq(k), per_seq(v), per_seq(og), per_seq(gt), gb, ng, c0, n0, m0, tri, eye)
    return o.reshape(n * t, w), c_new, n_new, m_new


def _flash_kernel(qi_ref, kj_ref, q_ref, k_ref, v_ref, o_ref, m_scr, l_scr, acc_scr):
    p = pl.program_id(2)
    qi = qi_ref[p]
    kj = kj_ref[p]

    @pl.when(kj == 0)
    def _():
        m_scr[...] = jnp.full_like(m_scr, NEG)
        l_scr[...] = jnp.zeros_like(l_scr)
        acc_scr[...] = jnp.zeros_like(acc_scr)

    def step(masked):
        bq = q_ref.shape[0]
        hq = bq // 2 if bq % 16 == 0 else bq
        for q0 in range(0, bq, hq):
            rows = slice(q0, q0 + hq)
            s = _dot_nt(q_ref[rows, :], k_ref[...])
            bk = s.shape[1]
            if masked:
                ri = lax.broadcasted_iota(jnp.int32, (hq, bk), 0) + q0
                ci = lax.broadcasted_iota(jnp.int32, (hq, bk), 1)
                s = jnp.where(ci <= ri, s, NEG)
            lanes = m_scr.shape[1]
            m_old = m_scr[rows, :]
            m_new = jnp.maximum(m_old, jnp.max(s, axis=-1, keepdims=True))
            alpha = jnp.exp(m_old - m_new)
            pr = jnp.exp(s - jnp.concatenate([m_new] * (bk // lanes), axis=1))
            l_scr[rows, :] = alpha * l_scr[rows, :] + jnp.sum(pr, axis=-1, keepdims=True)
            acc_scr[rows, :] = alpha * acc_scr[rows, :] + _dot(pr.astype(BF16), v_ref[...])
            m_scr[rows, :] = m_new

    @pl.when(kj < qi)
    def _():
        step(False)

    @pl.when(kj == qi)
    def _():
        step(True)
        o_ref[...] = acc_scr[...] / l_scr[...]


def _flash(n, t, qcat, kcat, v):
    blk = FLASH_BLOCK if t % FLASH_BLOCK == 0 else t
    nq = t // blk
    pairs = [(i, j) for i in range(nq) for j in range(i + 1)]
    qi = jnp.asarray([a for a, _ in pairs], jnp.int32)
    kj = jnp.asarray([b for _, b in pairs], jnp.int32)
    dv = v.shape[1] // D_HEADS
    grid_spec = pltpu.PrefetchScalarGridSpec(
        num_scalar_prefetch=2,
        grid=(n, D_HEADS, len(pairs)),
        in_specs=[pl.BlockSpec((blk, 256), lambda b, h, p, qi, kj: (b * nq + qi[p], h)),
                  pl.BlockSpec((blk, 256), lambda b, h, p, qi, kj: (b * nq + kj[p], h)),
                  pl.BlockSpec((blk, dv), lambda b, h, p, qi, kj: (b * nq + kj[p], h))],
        out_specs=pl.BlockSpec((blk, dv), lambda b, h, p, qi, kj: (b * nq + qi[p], h)),
        scratch_shapes=[pltpu.VMEM((blk, dv), F32), pltpu.VMEM((blk, dv), F32), pltpu.VMEM((blk, dv), F32)],
    )
    assert blk % dv == 0
    return pl.pallas_call(
        _flash_kernel,
        grid_spec=grid_spec,
        out_shape=jax.ShapeDtypeStruct((n * t, v.shape[1]), F32),
        compiler_params=_params("arbitrary", "arbitrary", "arbitrary"),
        name="mla_prompt_flash",
    )(qi, kj, qcat, kcat, v)


def _paged_kernel(pt_ref, u_ref, qr_ref, cn_ref, kn_ref, wukt_ref, wuv_ref, ckv_hbm, krt_hbm,
                  o_ref, ck_buf, kr_buf, sem, m_scr, l_scr, acc_scr, *, pages, li, n_groups, n_steps):
    b = pl.program_id(0)
    g = pl.program_id(1)
    hq = u_ref.shape[1]
    tq = hq // D_HEADS
    step = b * n_groups + g
    slot = lax.rem(step, 2)

    def page_copies(seq, group, sl, lookup):
        copies = []
        for i in range(pages):
            page = pt_ref[seq, group * pages + i] if lookup else 0
            rows = pl.ds(i * PAGE_SIZE, PAGE_SIZE)
            copies.append(pltpu.make_async_copy(ckv_hbm.at[page, li], ck_buf.at[sl, rows, :], sem.at[sl]))
            copies.append(pltpu.make_async_copy(krt_hbm.at[page, li], kr_buf.at[sl, i], sem.at[sl]))
        return copies

    @pl.when(step == 0)
    def _():
        for cp in page_copies(0, 0, 0, True):
            cp.start()

    @pl.when(step + 1 < n_steps)
    def _():
        nxt = step + 1
        for cp in page_copies(lax.div(nxt, n_groups), lax.rem(nxt, n_groups), 1 - slot, True):
            cp.start()

    for cp in page_copies(0, 0, slot, False):
        cp.wait()

    @pl.when(g == 0)
    def _():
        m_scr[...] = jnp.full_like(m_scr, NEG)
        l_scr[...] = jnp.zeros_like(l_scr)
        acc_scr[...] = jnp.zeros_like(acc_scr)

    lhs = jnp.concatenate([wukt_ref[...], u_ref[0]], axis=0)
    qr = qr_ref[0]
    nk = D_HEADS * D_NOPE

    def scores(ck_b, kr_b, kr_keys_minor=True):
        big = _dot_nt(lhs, ck_b)
        rows = []
        for h in range(D_HEADS):
            kn_t = big[h * D_NOPE:(h + 1) * D_NOPE, :]
            rinv = lax.rsqrt(jnp.sum(kn_t * kn_t, axis=0, keepdims=True) * (1.0 / D_NOPE) + EPS)
            rows.append(big[nk + h * tq:nk + (h + 1) * tq, :] * rinv)
        rope = _dot(qr, kr_b) if kr_keys_minor else _dot_nt(qr, kr_b)
        return jnp.concatenate(rows, axis=0) + rope

    def update(s, ck_b):
        m_old = m_scr[...]
        m_new = jnp.maximum(m_old, jnp.max(s, axis=-1, keepdims=True))
        alpha = jnp.exp(m_old - m_new)
        pr = jnp.exp(s - m_new)
        l_scr[...] = alpha * l_scr[...] + jnp.sum(pr, axis=-1, keepdims=True)
        acc_scr[...] = alpha * acc_scr[...] + _dot(pr.astype(BF16), ck_b)
        m_scr[...] = m_new

    span = 2 * PAGE_SIZE
    cks = [ck_buf[slot, pl.ds(i * span, span), :].astype(BF16) for i in range(pages // 2)]
    krs = [jnp.concatenate([kr_buf[slot, 2 * i], kr_buf[slot, 2 * i + 1]], axis=1).astype(BF16)
           for i in range(pages // 2)]
    s_all = jnp.concatenate([scores(cks[i], krs[i]) for i in range(pages // 2)], axis=1)
    update(s_all, jnp.concatenate(cks, axis=0))

    @pl.when(g == pl.num_programs(1) - 1)
    def _():
        fill = PAGE_SIZE - tq
        ck_b = jnp.concatenate([cn_ref[...], jnp.zeros((fill, cn_ref.shape[1]), F32)], axis=0).astype(BF16)
        kr_b = jnp.concatenate([kn_ref[...], jnp.zeros((fill, kn_ref.shape[1]), F32)], axis=0).astype(BF16)
        s = scores(ck_b, kr_b, kr_keys_minor=False)
        ri = lax.broadcasted_iota(jnp.int32, s.shape, 0)
        ci = lax.broadcasted_iota(jnp.int32, s.shape, 1)
        update(jnp.where(ci <= ri % tq, s, NEG), ck_b)
        lat = (acc_scr[...] / l_scr[...]).astype(BF16)
        full = _dot(lat, wuv_ref[...])
        dv = wuv_ref.shape[1] // D_HEADS
        o_ref[...] = jnp.concatenate(
            [full[h * tq:(h + 1) * tq, h * dv:(h + 1) * dv] for h in range(D_HEADS)], axis=1)


def _paged(page_table, li, u3, qr3, ckv_new, kr_new, wuk_t, wuv, cache_ckv, cache_kr):
    n, hq, lat = u3.shape
    tq = hq // D_HEADS
    n_pages = page_table.shape[1]
    pages = min(PAGES_PER_STEP, n_pages)
    assert n_pages % pages == 0 and pages % 2 == 0
    ng = n_pages // pages
    cache_kr_t = jnp.swapaxes(cache_kr, 2, 3)
    keys = pages * PAGE_SIZE

    grid_spec = pltpu.PrefetchScalarGridSpec(
        num_scalar_prefetch=1,
        grid=(n, ng),
        in_specs=[pl.BlockSpec((1, hq, lat), lambda b, g, pt: (b, 0, 0)),
                  pl.BlockSpec((1, hq, D_ROPE), lambda b, g, pt: (b, 0, 0)),
                  pl.BlockSpec((tq, lat), lambda b, g, pt: (b, 0)),
                  pl.BlockSpec((tq, D_ROPE), lambda b, g, pt: (b, 0)),
                  pl.BlockSpec(wuk_t.shape, lambda b, g, pt: (0, 0)),
                  pl.BlockSpec(wuv.shape, lambda b, g, pt: (0, 0)),
                  pl.BlockSpec(memory_space=pl.ANY), pl.BlockSpec(memory_space=pl.ANY)],
        out_specs=pl.BlockSpec((tq, wuv.shape[1]), lambda b, g, pt: (b, 0)),
        scratch_shapes=[pltpu.VMEM((2, keys, lat), F32), pltpu.VMEM((2, pages, D_ROPE, PAGE_SIZE), F32),
                        pltpu.SemaphoreType.DMA((2,)),
                        pltpu.VMEM((hq, 1), F32), pltpu.VMEM((hq, 1), F32), pltpu.VMEM((hq, lat), F32)],
    )
    return pl.pallas_call(
        functools.partial(_paged_kernel, pages=pages, li=li, n_groups=ng, n_steps=n * ng),
        grid_spec=grid_spec,
        out_shape=jax.ShapeDtypeStruct((n * tq, wuv.shape[1]), F32),
        compiler_params=_params("arbitrary", "arbitrary"),
        name="mla_sample_paged",
    )(page_table, u3, qr3, ckv_new, kr_new, wuk_t, wuv, cache_ckv, cache_kr_t)


def _np_seg(n_seg, seg_in, seg_out):
    mat = np.zeros((n_seg * seg_in, n_seg * seg_out), np.float32)
    for s in range(n_seg):
        mat[s * seg_in:(s + 1) * seg_in, s * seg_out:(s + 1) * seg_out] = 1.0
    return mat


def _rot_cols(w):
    half = w.shape[-1] // 2
    return jnp.concatenate([-w[..., half:], w[..., :half]], axis=-1)


def _swap_halves(g):
    half = g.shape[-1] // 2
    return jnp.concatenate([g[..., half:], g[..., :half]], axis=-1)


def _rope_tables(pos, reps):
    half = D_ROPE // 2
    inv = ROPE_BASE ** (-jnp.arange(half, dtype=F32) / half)
    ang = pos.astype(F32)[:, None] * inv
    cos = jnp.concatenate([jnp.cos(ang), jnp.cos(ang)], axis=-1)
    sin = jnp.concatenate([jnp.sin(ang), jnp.sin(ang)], axis=-1)
    return jnp.tile(cos, (1, reps)), jnp.tile(sin, (1, reps))


def _trunk(x, c_mod, pos0, gla0, mc0, mn0, mm0, p, sample_ctx):
    n, t, d = x.shape
    m = n * t
    rt = _Rows(n, t)
    eye_r = jnp.eye(rt.rows, dtype=BF16)
    triu_r = jnp.asarray(np.triu(np.ones((rt.rows, rt.rows), np.float32)), BF16)
    dh = mc0.shape[3]
    eye_hk = jnp.eye(B_HEADS * gla0.shape[3], dtype=BF16)
    eye_dh = jnp.eye(dh, dtype=BF16)
    results = {}

    layer, li = 0, 0
    mod = c_mod[layer]
    w_in = p['w_in_even'][li]
    w_in = jnp.concatenate([w_in, jnp.zeros((d, 128 - B_GATE_RANK), F32)], axis=1).astype(BF16)
    a_ws = jnp.tril(p['a_ws'][li])
    a_bs = p['a_bs'][li]
    if t % CHUNK_A == 0:
        ws = a_ws
        bs = jnp.repeat(a_bs.T, CHUNK_A, axis=1)
    else:
        ws = jnp.stack([jnp.kron(jnp.eye(CHUNK_A // t, dtype=F32), a_ws[g, :t, :t]) for g in range(A_GROUPS)])
        bs = jnp.repeat(jnp.tile(a_bs[:, :t], (1, CHUNK_A // t)).T, CHUNK_A, axis=1)
    out_a, v_rows, bq, bk, bv, br, bg = _even_in(
        x, mod, p['norm_mix_g'][layer].reshape(1, d), w_in, p['a_norm_g'][li].reshape(1, -1),
        ws.astype(BF16), bs)
    hk = bq.shape[1]
    dv = bv.shape[1] // B_HEADS
    wg2 = jnp.concatenate([p['b_w_gate2'][li], jnp.zeros((128 - B_GATE_RANK, hk), F32)], axis=0)
    seg = jnp.asarray(_np_seg(B_HEADS, hk // B_HEADS, dv), BF16)
    out_b, s_new = _gla(n, t, bq, bk, bv, br, bg, wg2, p['b_gate_bias'][li].reshape(1, hk),
                        jnp.tile(p['b_norm_g'][li], B_HEADS).reshape(1, -1), gla0[li], seg, eye_hk)
    results['gla'] = s_new
    results['v_rows'] = v_rows
    routed = _out_route(
        x, mod, out_a, out_b, p['w_out'][layer].astype(BF16), p['norm_ffn_g'][layer].reshape(1, d),
        p['w_router'][layer].T, p['b_router'][layer].reshape(-1, 1), eye_r, triu_r)
    x = yield mod, routed

    layer, li = 1, 0
    mod = c_mod[layer]
    w = p['w_in_odd'][li]
    hw = C_HEADS * dh
    o_g = 3 * hw
    o_o = o_g + 2 * C_HEADS
    o_qa = o_o + hw
    o_kva = o_qa + p['d_g_qa'].shape[1]
    o_kr = o_kva + p['d_g_kva'].shape[1]
    w_kr = w[:, o_kr:o_kr + D_ROPE]
    w_odd = jnp.concatenate(
        [w[:, :o_g], w[:, o_o:o_qa], w[:, o_qa:o_kva], w[:, o_kva:o_kr], w_kr, _rot_cols(w_kr),
         w[:, o_g:o_o], jnp.zeros((d, 128 - 2 * C_HEADS), F32)], axis=1).astype(BF16)
    wqb = p['d_w_qb'][li].reshape(-1, D_HEADS, D_NOPE + D_ROPE)
    wqb_r = wqb[:, :, D_NOPE:]
    wqb2 = jnp.concatenate([wqb[:, :, :D_NOPE].reshape(-1, D_HEADS * D_NOPE),
                            wqb_r.reshape(-1, D_HEADS * D_ROPE),
                            _rot_cols(wqb_r).reshape(-1, D_HEADS * D_ROPE)], axis=1).astype(BF16)
    w_uk = p['d_w_uk'][li]
    lat = w_uk.shape[0]
    g_qr = p['d_g_qr'][li]
    g_kr = p['d_g_kr'][li]
    cos, sin = _rope_tables(pos0 + jnp.arange(t), D_HEADS)
    prompt = sample_ctx is None
    if not prompt:
        cos, sin = jnp.tile(cos, (rt.s, 1)), jnp.tile(sin, (rt.s, 1))
    consts = [p['d_g_qa'][li].reshape(1, -1), wqb2, p['d_g_qn'][li].reshape(1, -1),
              jnp.tile(g_qr, D_HEADS).reshape(1, -1), jnp.tile(_swap_halves(g_qr), D_HEADS).reshape(1, -1),
              p['d_g_kva'][li].reshape(1, -1), g_kr.reshape(1, -1), _swap_halves(g_kr).reshape(1, -1),
              p['d_g_kn'][li].reshape(1, -1),
              jnp.asarray(_np_seg(D_HEADS, D_ROPE, D_ROPE), BF16),
              w_uk.reshape(lat, -1).astype(BF16), p['d_w_uv'][li].reshape(lat, -1).astype(BF16)]
    outs = _odd_in(x, mod, p['norm_mix_g'][layer].reshape(1, d), w_odd, consts, cos, sin, prompt)
    cq, ck, cv, co, gates, ckv, kr = outs[:7]
    gb = jnp.concatenate([p['c_ig_bias'][li], p['c_fg_bias'][li],
                          jnp.zeros((128 - 2 * C_HEADS,), F32)]).reshape(1, 128)
    chunk = CHUNK_C if t % CHUNK_C == 0 else t
    tri = jnp.asarray(np.tril(np.ones((chunk, chunk), np.float32)), BF16)
    m0b =jnp.broadcast_to(mm0[li][:, :, None], (n, C_HEADS, dh))
    out_c, c_new, n_new, m_new = _mlstm(n, t, cq, ck, cv, co, gates, gb,
                                        p['c_norm_g'][li].reshape(1, -1), mc0[li], mn0[li], m0b, tri, eye_dh)
    results['mlstm'] = (c_new, n_new, m_new[:, :, 0])
    results['ckv'] = ckv.reshape(n, t, -1)
    results['kr'] = kr.reshape(n, t, -1)
    if prompt:
        qcat, kcat, vv = outs[7:]
        out_d = _flash(n, t, qcat, kcat, vv)
    else:
        u, qr = outs[7:]
        cache_ckv, cache_kr, page_table = sample_ctx
        u3 = u.reshape(n, t, D_HEADS, lat).transpose(0, 2, 1, 3).reshape(n, D_HEADS * t, lat)
        qr3 = qr.reshape(n, t, D_HEADS, D_ROPE).transpose(0, 2, 1, 3).reshape(n, D_HEADS * t, D_ROPE)
        wuk_t = w_uk.transpose(1, 2, 0).reshape(-1, lat).astype(BF16)
        out_d = _paged(page_table, li, u3, qr3, ckv, kr, wuk_t, consts[11], cache_ckv, cache_kr)
    routed = _out_route(
        x, mod, out_c, out_d, p['w_out'][layer].astype(BF16), p['norm_ffn_g'][layer].reshape(1, d),
        p['w_router'][layer].T, p['b_router'][layer].reshape(-1, 1), eye_r, triu_r)
    x = yield mod, routed
    return x, results


def kernel(x_prompt, x_sample, state_gla, state_mlstm_c, state_mlstm_n, state_mlstm_m,
           cache_ckv, cache_krope, page_table, c_prompt, c_sample,
           norm_mix_g, norm_ffn_g, w_ada, b_ada, w_out,
           w_in_even, a_norm_g, a_ws, a_bs, b_w_gate2, b_gate_bias, b_norm_g,
           w_in_odd, c_ig_bias, c_fg_bias, c_norm_g,
           d_g_qa, d_w_qb, d_g_kva, d_g_qn, d_g_qr, d_g_kr, d_g_kn, d_w_uk, d_w_uv,
           w_router, b_router, w_gate_e, w_up_e, w_down_e, w_gate_s, w_up_s, w_down_s):
    p = dict(norm_mix_g=norm_mix_g, norm_ffn_g=norm_ffn_g, w_out=w_out,
             w_in_even=w_in_even, a_norm_g=a_norm_g, a_ws=a_ws, a_bs=a_bs, b_w_gate2=b_w_gate2,
             b_gate_bias=b_gate_bias, b_norm_g=b_norm_g, w_in_odd=w_in_odd, c_ig_bias=c_ig_bias,
             c_fg_bias=c_fg_bias, c_norm_g=c_norm_g, d_g_qa=d_g_qa, d_w_qb=d_w_qb, d_g_kva=d_g_kva,
             d_g_qn=d_g_qn, d_g_qr=d_g_qr, d_g_kr=d_g_kr, d_g_kn=d_g_kn, d_w_uk=d_w_uk, d_w_uv=d_w_uv,
             w_router=w_router, b_router=b_router, w_gate_e=w_gate_e, w_up_e=w_up_e, w_down_e=w_down_e,
             w_gate_s=w_gate_s, w_up_s=w_up_s, w_down_s=w_down_s)
    n_p, t_p, d = x_prompt.shape
    n_s, t_s, _ = x_sample.shape
    depth = w_ada.shape[0]
    pad_p = (-n_p) % 8
    c_all = jnp.concatenate([c_prompt, jnp.zeros((pad_p, d), F32), c_sample], axis=0)
    mod_all = _ada(c_all, w_ada, b_ada).reshape(depth, c_all.shape[0], 6, d)
    mod_p = mod_all[:, :n_p]
    mod_s = mod_all[:, n_p + pad_p:]

    n_even, _, bh, bdk, bdv = state_gla.shape
    n_odd, _, chh, cdh, _ = state_mlstm_c.shape
    gla0_p = jnp.zeros((n_even, n_p, bh, bdk, bdv), F32)
    mc0_p = jnp.zeros((n_odd, n_p, chh, cdh, cdh), F32)
    mn0_p = jnp.zeros((n_odd, n_p, chh, cdh), F32)
    mm0_p = jnp.full((n_odd, n_p, chh), NEG, F32)
    past_len = page_table.shape[1] * PAGE_SIZE

    trunks = [_trunk(x_prompt, mod_p, 0, gla0_p, mc0_p, mn0_p, mm0_p, p, None),
              _trunk(x_sample, mod_s, past_len, state_gla, state_mlstm_c, state_mlstm_n, state_mlstm_m, p,
                     (cache_ckv, cache_krope, page_table))]
    pending = [next(tr) for tr in trunks]
    for layer in range(depth):
        mixed = _moe(layer, pending, p)
        pending = []
        for tr, x_new in zip(trunks, mixed):
            try:
                pending.append(tr.send(x_new))
            except StopIteration as done:
                pending.append(done.value)
    (y_p, rp), (y_s, rs) = pending
    aw = rs['v_rows'].shape[1]
    return (y_p, y_s, rp['gla'][None], rs['gla'][None], rs['v_rows'].reshape(1, n_s, t_s, aw),
            rp['mlstm'][0][None], rs['mlstm'][0][None], rp['mlstm'][1][None], rs['mlstm'][1][None],
            rp['mlstm'][2][None], rs['mlstm'][2][None],
            rp['ckv'][:, None], rs['ckv'][:, None], rp['kr'][:, None], rs['kr'][:, None])
```

```python
import functools

import numpy as np
import jax
import jax.numpy as jnp
from jax import lax
from jax.experimental import pallas as pl
from jax.experimental.pallas import tpu as pltpu

F32 = jnp.float32
BF16 = jnp.bfloat16

EPS = 1e-6
NEG = -1e30

A_GROUPS = 4
CHUNK_A = 128
B_HEADS = 4
B_GATE_RANK = 16
B_GATE_TAU = 16.0
CHUNK_B = 8
C_HEADS = 4
CHUNK_C = 128
D_HEADS = 4
D_NOPE = 128
D_ROPE = 64
ROPE_BASE = 10000.0
MLA_SCALE = (D_NOPE + D_ROPE) ** -0.5
PAGE_SIZE = 128
N_EXPERT_GROUPS = 8
TOPK_GROUPS = 4
TOP_K = 8
ROUTED_SCALE = 2.5

ROW_TILE = 256
SAMPLE_SEQS = 32
FLASH_BLOCK = 1024
PAGES_PER_STEP = 32
VMEM_LIMIT = 56 * 1024 * 1024


def _dot(a, b):
    return jnp.dot(a, b, preferred_element_type=F32)


def _dot_nt(a, b):
    return lax.dot_general(a, b, (((1,), (1,)), ((), ())), preferred_element_type=F32)


def _split2(x):
    hi = x.astype(BF16)
    lo = (x - hi.astype(F32)).astype(BF16)
    return hi, lo


def _split3(x):
    hi = x.astype(BF16)
    r = x - hi.astype(F32)
    mid = r.astype(BF16)
    lo = (r - mid.astype(F32)).astype(BF16)
    return hi, mid, lo


def _dot3(a, b):
    ah, al = _split2(a)
    bh, bl = _split2(b)
    return _dot(ah, bh) + _dot(ah, bl) + _dot(al, bh)


def _dot3_nt(a, b):
    ah, al = _split2(a)
    bh, bl = _split2(b)
    return _dot_nt(ah, bh) + _dot_nt(ah, bl) + _dot_nt(al, bh)


def _dot_sel(x, m01, parts=2):
    ps = _split2(x) if parts == 2 else _split3(x)
    acc = _dot(ps[0], m01)
    for p in ps[1:]:
        acc = acc + _dot(p, m01)
    return acc


def _sigmoid(x):
    return 1.0 / (1.0 + jnp.exp(-x))


def _silu(x):
    return x * _sigmoid(x)


def _log_sigmoid(x):
    return jnp.minimum(x, 0.0) - jnp.log(1.0 + jnp.exp(-jnp.abs(x)))


def _gelu(x):
    return 0.5 * x * (1.0 + jnp.tanh(0.7978845608028654 * (x + 0.044715 * x * x * x)))


def _rms(x, eps=EPS):
    return x * lax.rsqrt(jnp.mean(x * x, axis=-1, keepdims=True) + eps)


def _params(*sem):
    return pltpu.CompilerParams(dimension_semantics=sem, vmem_limit_bytes=VMEM_LIMIT)


def _const_spec(shape):
    nd = len(shape)
    return pl.BlockSpec(shape, lambda *_: (0,) * nd)


class _Rows:
    def __init__(self, n, t):
        self.n, self.t = n, t
        if t % ROW_TILE == 0:
            self.s, self.r = 1, ROW_TILE
            self.tpb = t // ROW_TILE
        else:
            assert t == 8 and n % SAMPLE_SEQS == 0, (n, t)
            self.s, self.r = SAMPLE_SEQS, t
            self.tpb = 1
        self.rows = self.s * self.r
        self.steps = n * t // self.rows

    def seq_spec(self, mid, d):
        s, tpb = self.s, self.tpb
        if mid == self.t:
            return pl.BlockSpec((s, self.r, d), lambda i: (i // tpb, i % tpb, 0))
        return pl.BlockSpec((s, mid, d), lambda i: (i // tpb, 0, 0))

    def row_spec(self, c):
        return pl.BlockSpec((self.rows, c), lambda i: (i, 0))

    def tile_spec(self, sub):
        return pl.BlockSpec((self.rows * sub, 128), lambda i: (i, 0))

    def pos_spec(self, c):
        tpb = self.tpb
        return pl.BlockSpec((self.rows, c), lambda i: (i % tpb, 0))


def _ada_kernel(c_ref, w_ref, b_ref, o_ref):
    o_ref[0] = _dot3(_silu(c_ref[...]), w_ref[0]) + b_ref[0]


def _ada(c_all, w_ada, b_ada):
    depth, d, d6 = w_ada.shape
    nc = c_all.shape[0]
    tn = 1536
    return pl.pallas_call(
        _ada_kernel,
        grid=(depth, d6 // tn),
        in_specs=[pl.BlockSpec((nc, d), lambda l, j: (0, 0)),
                  pl.BlockSpec((1, d, tn), lambda l, j: (l, 0, j)),
                  pl.BlockSpec((1, 1, tn), lambda l, j: (l, 0, j))],
        out_specs=pl.BlockSpec((1, nc, tn), lambda l, j: (l, 0, j)),
        out_shape=jax.ShapeDtypeStruct((depth, nc, d6), F32),
        compiler_params=_params("arbitrary", "arbitrary"),
        name="ada_modulation",
    )(c_all, w_ada, b_ada.reshape(depth, 1, d6))


def _norm_mod(x_ref, mod_ref, g_ref, shift_row, scale_row):
    x = x_ref[...]
    h = _rms(x) * g_ref[...]
    h = h * (1.0 + mod_ref[:, scale_row:scale_row + 1, :]) + mod_ref[:, shift_row:shift_row + 1, :]
    s, r, d = x.shape
    return h.reshape(s * r, d)


def _even_in_kernel(x_ref, mod_ref, g_ref, w_ref, an_ref, ws_ref, bs_ref,
                    oa_ref, v_ref, q_ref, k_ref, bv_ref, r_ref, bg_ref):
    h = _norm_mod(x_ref, mod_ref, g_ref, 0, 1).astype(BF16)
    z = _dot(h, w_ref[...])
    aw = A_GROUPS * CHUNK_A
    rows = z.shape[0]
    u = _gelu(z[:, 0:aw])
    va = _gelu(z[:, aw:2 * aw])
    vn = jnp.concatenate(
        [_rms(va[:, g * CHUNK_A:(g + 1) * CHUNK_A]) for g in range(A_GROUPS)], axis=1) * an_ref[...]
    v_ref[...] = vn
    vb = vn.astype(BF16)
    for c in range(rows // CHUNK_A):
        r0 = c * CHUNK_A
        mixed = jnp.concatenate(
            [_dot(ws_ref[g], vb[r0:r0 + CHUNK_A, g * CHUNK_A:(g + 1) * CHUNK_A]) for g in range(A_GROUPS)],
            axis=1) + bs_ref[...]
        oa_ref[r0:r0 + CHUNK_A, :] = u[r0:r0 + CHUNK_A, :] * mixed
    o = 2 * aw
    q_ref[...] = z[:, o:o + 256]
    k_ref[...] = z[:, o + 256:o + 512]
    bv_ref[...] = z[:, o + 512:o + 1024]
    r_ref[...] = z[:, o + 1024:o + 1536]
    bg_ref[...] = z[:, o + 1536:o + 1664]


def _even_in(x, mod, g, w, an, ws, bs):
    n, t, d = x.shape
    rt = _Rows(n, t)
    m = n * t
    widths = (512, 512, 256, 256, 512, 512, 128)
    return pl.pallas_call(
        _even_in_kernel,
        grid=(rt.steps,),
        in_specs=[rt.seq_spec(t, d), rt.seq_spec(6, d), _const_spec((1, d)), _const_spec(w.shape),
                  _const_spec(an.shape), _const_spec(ws.shape), _const_spec(bs.shape)],
        out_specs=[rt.row_spec(c) for c in widths],
        out_shape=[jax.ShapeDtypeStruct((m, c), F32) for c in widths],
        compiler_params=_params("arbitrary"),
        name="even_in_chunk_gate",
    )(x, mod, g, w, an, ws, bs)


def _gla_kernel(q_ref, k_ref, v_ref, r_ref, g_ref, wg_ref, gb_ref, ng_ref, s0_ref, seg_ref, eye_ref,
                o_ref, so_ref, s_scr, lg_scr, o_scr, *, chunk, n_chunks):
    seqs, _, hk = q_ref.shape
    dk = hk // B_HEADS
    dv = v_ref.shape[2] // B_HEADS
    t = pl.program_id(1)

    @pl.when(t == 0)
    def _():
        for sq in range(seqs):
            s_scr[sq] = s0_ref[sq].reshape(hk, dv)

    for sq in range(seqs):
        lg_scr[sq] = _log_sigmoid(_dot3(g_ref[sq], wg_ref[...]) + gb_ref[...]) * (1.0 / B_GATE_TAU)
    row = lax.broadcasted_iota(jnp.int32, (chunk, hk), 0)
    lane = lax.broadcasted_iota(jnp.int32, (chunk, hk), 1)
    pad = 16 - chunk if chunk < 16 else 0

    def pad_rows(a):
        if pad:
            return jnp.concatenate([a, jnp.zeros((pad, a.shape[1]), a.dtype)], axis=0)
        return a

    def one_chunk(sq, r0):
        lg = lg_scr[sq, pl.ds(r0, chunk), :]
        q = q_ref[sq, pl.ds(r0, chunk), :] * (dk ** -0.5)
        k = k_ref[sq, pl.ds(r0, chunk), :]
        v = v_ref[sq, pl.ds(r0, chunk), :]
        b = jnp.zeros_like(lg)
        for s in range(chunk):
            b = b + jnp.where(row >= s, lg[s:s + 1, :], 0.0)
        prods = []
        for l in range(chunk):
            e = jnp.exp(jnp.where(row <= l, b[l:l + 1, :] - b, NEG))
            prods.append(q[l:l + 1, :] * k * e)
        att = _dot(jnp.concatenate(prods, axis=0).astype(BF16), seg_ref[...])
        o_intra = jnp.concatenate(
            [jnp.sum(att[l * chunk:(l + 1) * chunk, :] * v, axis=0, keepdims=True) for l in range(chunk)], axis=0)
        s_old = s_scr[sq]
        s_b = s_old.astype(BF16)
        qe = q * jnp.exp(b)
        o_inter = jnp.concatenate(
            [_dot(jnp.where(lane // dk == h, qe, 0.0).astype(BF16), s_b) for h in range(B_HEADS)], axis=1)
        o_scr[sq, pl.ds(r0, chunk), :] = o_intra + o_inter
        b_last = b[chunk - 1:chunk, :]
        k_end = pad_rows((k * jnp.exp(b_last - b)).astype(BF16))
        k_t = _dot_nt(eye_ref[...], k_end).astype(BF16)
        dh, dm, dl = _split3(jnp.exp(b_last))
        dec3 = _dot_nt(eye_ref[...], jnp.concatenate([dh, dm, dl, jnp.zeros((13, hk), BF16)], axis=0))
        dec = dec3[:, 0:1] + dec3[:, 1:2] + dec3[:, 2:3]
        v_b = pad_rows(v.astype(BF16))
        kv = jnp.concatenate(
            [_dot(k_t[h * dk:(h + 1) * dk, :], v_b[:, h * dv:(h + 1) * dv]) for h in range(B_HEADS)], axis=0)
        s_scr[sq] = dec * s_old + kv

    def body(c, carry):
        r0 = pl.multiple_of(c * chunk, chunk)
        for sq in range(seqs):
            one_chunk(sq, r0)
        return carry

    lax.fori_loop(0, n_chunks, body, 0)

    for sq in range(seqs):
        o = o_scr[sq]
        on = jnp.concatenate([_rms(o[:, h * dv:(h + 1) * dv]) for h in range(B_HEADS)], axis=1) * ng_ref[...]
        o_ref[sq] = on * _silu(r_ref[sq])

    @pl.when(t == pl.num_programs(1) - 1)
    def _():
        for sq in range(seqs):
            so_ref[sq] = s_scr[sq].reshape(B_HEADS, dk, dv)


GLA_SEQS = 2


def _gla(n, t, q, k, v, r, g, wg, gb, ng, s0, seg, eye):
    tt = ROW_TILE if t % ROW_TILE == 0 else t
    chunk = CHUNK_B if tt % CHUNK_B == 0 else tt
    nt = t // tt
    hk, hv = q.shape[1], v.shape[1]
    dk, dv = hk // B_HEADS, hv // B_HEADS
    seqs = GLA_SEQS
    assert n % seqs == 0

    def rows(c):
        return pl.BlockSpec((seqs, tt, c), lambda i, j: (i, j, 0))

    def per_seq(a):
        return a.reshape(n, t, a.shape[1])

    st = pl.BlockSpec((seqs, B_HEADS, dk, dv), lambda i, j: (i, 0, 0, 0))
    o, s_new = pl.pallas_call(
        functools.partial(_gla_kernel, chunk=chunk, n_chunks=tt // chunk),
        grid=(n // seqs, nt),
        in_specs=[rows(hk), rows(hk), rows(hv), rows(hv), rows(g.shape[1]),
                  pl.BlockSpec(wg.shape, lambda i, j: (0, 0)), pl.BlockSpec(gb.shape, lambda i, j: (0, 0)),
                  pl.BlockSpec(ng.shape, lambda i, j: (0, 0)), st,
                  pl.BlockSpec(seg.shape, lambda i, j: (0, 0)), pl.BlockSpec(eye.shape, lambda i, j: (0, 0))],
        out_specs=[rows(hv), st],
        out_shape=[jax.ShapeDtypeStruct((n, t, hv), F32), jax.ShapeDtypeStruct(s0.shape, F32)],
        scratch_shapes=[pltpu.VMEM((seqs, hk, dv), F32), pltpu.VMEM((seqs, tt, hk), F32),
                        pltpu.VMEM((seqs, tt, hv), F32)],
        compiler_params=_params("arbitrary", "arbitrary"),
        name="gla_scan",
    )(per_seq(q), per_seq(k), per_seq(v), per_seq(r), per_seq(g), wg, gb, ng, s0, seg, eye)
    return o.reshape(n * t, hv), s_new


def _route(logits_t, bias_col):
    n_e, r = logits_t.shape
    per = n_e // N_EXPERT_GROUPS
    scores = _sigmoid(logits_t)
    biased = scores + bias_col
    sub = lax.broadcasted_iota(jnp.int32, (per, r), 0).astype(F32)
    ninf = -jnp.inf
    gs = []
    for g in range(N_EXPERT_GROUPS):
        blk = biased[g * per:(g + 1) * per, :]
        m1 = jnp.max(blk, axis=0, keepdims=True)
        i1 = jnp.min(jnp.where(blk == m1, sub, float(per)), axis=0, keepdims=True)
        m2 = jnp.max(jnp.where(sub == i1, ninf, blk), axis=0, keepdims=True)
        gs.append(m1 + m2)
    cur = jnp.concatenate(gs, axis=0)
    gsub = lax.broadcasted_iota(jnp.int32, (N_EXPERT_GROUPS, r), 0).astype(F32)
    gsel = jnp.zeros((N_EXPERT_GROUPS, r), F32)
    for _ in range(TOPK_GROUPS):
        m = jnp.max(cur, axis=0, keepdims=True)
        i = jnp.min(jnp.where(cur == m, gsub, float(N_EXPERT_GROUPS)), axis=0, keepdims=True)
        hit = gsub == i
        gsel = jnp.where(hit, 1.0, gsel)
        cur = jnp.where(hit, ninf, cur)
    cur = jnp.concatenate(
        [jnp.where(gsel[g:g + 1, :] > 0.5, biased[g * per:(g + 1) * per, :], ninf) for g in range(N_EXPERT_GROUPS)],
        axis=0)
    esub = lax.broadcasted_iota(jnp.int32, (n_e, r), 0).astype(F32)
    idx, wts, hits = [], [], []
    for _ in range(TOP_K):
        m = jnp.max(cur, axis=0, keepdims=True)
        i = jnp.min(jnp.where(cur == m, esub, float(n_e)), axis=0, keepdims=True)
        hit = esub == i
        idx.append(i)
        hits.append(hit)
        wts.append(jnp.sum(jnp.where(hit, scores, 0.0), axis=0, keepdims=True))
        cur = jnp.where(hit, ninf, cur)
    w = jnp.concatenate(wts, axis=0)
    w = w / jnp.sum(w, axis=0, keepdims=True) * ROUTED_SCALE
    return jnp.concatenate(idx, axis=0), w, hits


def _pack_pairs(x):
    w = x.shape[1] // 2
    hi = lax.bitcast_convert_type(x[:, :w].astype(BF16).astype(F32), jnp.uint32)
    lo = lax.bitcast_convert_type(x[:, w:].astype(BF16).astype(F32), jnp.uint32)
    return hi | (lo >> 16)


def _unpack_pairs(pk):
    a = lax.bitcast_convert_type(pk & jnp.uint32(0xFFFF0000), F32)
    b = lax.bitcast_convert_type(pk << 16, F32)
    return jnp.concatenate([a, b], axis=1)


def _store_row_tiles(ref, x):
    rows = x.shape[0]
    pk = _pack_pairs(x)
    sub = pk.shape[1] // 128
    for c in range(sub):
        ref[pl.ds(c, rows, stride=sub), :] = pk[:, c * 128:(c + 1) * 128]


def _load_row_tiles(ref, sub):
    rows = ref.shape[0] // sub
    return _unpack_pairs(jnp.concatenate([ref[pl.ds(c, rows, stride=sub), :] for c in range(sub)], axis=1))


def _out_kernel(x_ref, mod_ref, ma_ref, mb_ref, wo_ref, g_ref, wr_ref, br_ref, eye_ref, triu_ref,
                x1_ref, hp_ref, idx_ref, rank_ref, w3_ref, cnt_ref, run_scr):
    step = pl.program_id(0)

    @pl.when(step == 0)
    def _():
        run_scr[...] = jnp.zeros_like(run_scr)

    half = ma_ref.shape[1]
    y = _dot(ma_ref[...].astype(BF16), wo_ref[0:half, :]) + _dot(mb_ref[...].astype(BF16), wo_ref[half:, :])
    x = x_ref[...]
    s, r, d = x.shape
    x1 = x + mod_ref[:, 2:3, :] * y.reshape(s, r, d)
    x1_ref[...] = x1
    h2 = (_rms(x1) * g_ref[...] * (1.0 + mod_ref[:, 4:5, :]) + mod_ref[:, 3:4, :]).reshape(s * r, d)
    _store_row_tiles(hp_ref, h2)
    idx, w, hits = _route(_dot3_nt(wr_ref[...], h2), br_ref[...])
    idx_ref[0] = idx.astype(jnp.int32)
    sel = jnp.zeros(hits[0].shape, F32)
    for hit in hits:
        sel = jnp.where(hit, 1.0, sel)
    before = run_scr[:, 0:1] + _dot(sel.astype(BF16), triu_ref[...]) - sel
    rank_ref[0] = jnp.concatenate(
        [jnp.sum(jnp.where(hit, before, 0.0), axis=0, keepdims=True) for hit in hits], axis=0).astype(jnp.int32)
    run_scr[...] = run_scr[...] + jnp.sum(sel, axis=1, keepdims=True)
    cnt_ref[...] = run_scr[...]
    wh, wm, wl = _split3(w)
    stack = jnp.concatenate([wh, wm, wl, jnp.zeros((128 - 3 * TOP_K, s * r), BF16)], axis=0)
    w3_ref[...] = _dot_nt(eye_ref[...], stack)


def _out_route(x, mod, mix_a, mix_b, wo, g, wr_t, br, eye, triu):
    n, t, d = x.shape
    rt = _Rows(n, t)
    m = n * t
    n_e = wr_t.shape[0]
    slot = pl.BlockSpec((1, TOP_K, rt.rows), lambda i: (i, 0, 0))
    return pl.pallas_call(
        _out_kernel,
        grid=(rt.steps,),
        in_specs=[rt.seq_spec(t, d), rt.seq_spec(6, d), rt.row_spec(mix_a.shape[1]), rt.row_spec(mix_b.shape[1]),
                  _const_spec(wo.shape), _const_spec((1, d)), _const_spec(wr_t.shape), _const_spec(br.shape),
                  _const_spec(eye.shape), _const_spec(triu.shape)],
        out_specs=[rt.seq_spec(t, d), rt.tile_spec(d // 256), slot, slot, rt.row_spec(128), _const_spec((n_e, 128))],
        out_shape=[jax.ShapeDtypeStruct((n, t, d), F32), jax.ShapeDtypeStruct((m * (d // 256), 128), jnp.uint32),
                   jax.ShapeDtypeStruct((rt.steps, TOP_K, rt.rows), jnp.int32),
                   jax.ShapeDtypeStruct((rt.steps, TOP_K, rt.rows), jnp.int32),
                   jax.ShapeDtypeStruct((m, 128), F32), jax.ShapeDtypeStruct((n_e, 128), F32)],
        scratch_shapes=[pltpu.VMEM((n_e, 128), F32)],
        compiler_params=_params("arbitrary"),
        name="out_proj_route",
    )(x, mod, mix_a, mix_b, wo, g, wr_t, br, eye, triu)


ROW_SUB = 4


def _row_copy(src_ref, src_row, dst_ref, dst_row, sem):
    src = src_ref.at[pl.ds(pl.multiple_of(src_row * ROW_SUB, ROW_SUB), ROW_SUB), :]
    dst = dst_ref.at[pl.ds(pl.multiple_of(dst_row * ROW_SUB, ROW_SUB), ROW_SUB), :]
    return pltpu.make_async_copy(src, dst, sem)


def _dispatch_kernel(dest_ref, hp_ref, *rest):
    xs_ref, sem = rest[-2:]
    rows = hp_ref.shape[0] // ROW_SUB

    def issue(tok, carry):
        for k in range(TOP_K):
            _row_copy(hp_ref, tok, xs_ref, dest_ref[0, k, tok], sem).start(priority=k % 2)
        return carry

    lax.fori_loop(0, rows, issue, 0, unroll=2)

    def drain(j, carry):
        _row_copy(hp_ref, 0, xs_ref, 0, sem).wait()
        return carry

    lax.fori_loop(0, TOP_K * rows, drain, 0, unroll=16)


def _dispatch(dest3, hp, n_rows, xs_prev=None):
    steps, _, rows = dest3.shape
    assert rows & (rows - 1) == 0
    chained = xs_prev is not None
    return pl.pallas_call(
        _dispatch_kernel,
        grid=(steps,),
        in_specs=[pl.BlockSpec((1, TOP_K, rows), lambda i: (i, 0, 0), memory_space=pltpu.SMEM),
                  pl.BlockSpec((rows * ROW_SUB, 128), lambda i: (i, 0))]
                 + ([pl.BlockSpec(memory_space=pl.ANY)] if chained else []),
        out_specs=pl.BlockSpec(memory_space=pl.ANY),
        out_shape=jax.ShapeDtypeStruct((n_rows * ROW_SUB, 128), jnp.uint32),
        scratch_shapes=[pltpu.SemaphoreType.DMA(())],
        input_output_aliases={2: 0} if chained else {},
        compiler_params=_params("arbitrary"),
        name="moe_dispatch_rows",
    )(dest3, hp, *([xs_prev] if chained else []))


def _gmm_kernel(be_ref, va_ref, nu_ref, x_ref, wg_ref, wu_ref, wd_ref, y_ref, wg_b, wu_b, wd_b):
    i = pl.program_id(0)

    @pl.when(jnp.logical_or(i == 0, be_ref[i] != be_ref[jnp.maximum(i - 1, 0)]))
    def _():
        wg_b[...] = wg_ref[...].astype(BF16)
        wu_b[...] = wu_ref[...].astype(BF16)
        wd_b[...] = wd_ref[...].astype(BF16)

    @pl.when(i < nu_ref[0])
    def _():
        x = _load_row_tiles(x_ref, ROW_SUB)
        live = lax.broadcasted_iota(jnp.int32, x.shape, 0) < va_ref[i]
        x = jnp.where(live, x, 0.0).astype(BF16)
        a = _silu(_dot(x, wg_b[...])) * _dot(x, wu_b[...])
        _store_row_tiles(y_ref, _dot(a.astype(BF16), wd_b[...]))

    @pl.when(i >= nu_ref[0])
    def _():
        y_ref[...] = jnp.zeros_like(y_ref)


def _gmm(layer, block_e, valid, n_used, xs, wg, wu, wd, tm):
    r = xs.shape[0] // ROW_SUB
    d, de = wg.shape[-2:]
    assert d == ROW_SUB * 256
    nb = r // tm
    grid_spec = pltpu.PrefetchScalarGridSpec(
        num_scalar_prefetch=3,
        grid=(nb,),
        in_specs=[pl.BlockSpec((tm * ROW_SUB, 128), lambda i, be, va, nu: (i, 0)),
                  pl.BlockSpec((None, None, d, de), lambda i, be, va, nu: (layer, be[i], 0, 0)),
                  pl.BlockSpec((None, None, d, de), lambda i, be, va, nu: (layer, be[i], 0, 0)),
                  pl.BlockSpec((None, None, de, d), lambda i, be, va, nu: (layer, be[i], 0, 0))],
        out_specs=pl.BlockSpec((tm * ROW_SUB, 128), lambda i, be, va, nu: (i, 0)),
        scratch_shapes=[pltpu.VMEM((d, de), BF16), pltpu.VMEM((d, de), BF16), pltpu.VMEM((de, d), BF16)],
    )
    return pl.pallas_call(
        _gmm_kernel,
        grid_spec=grid_spec,
        out_shape=jax.ShapeDtypeStruct(xs.shape, jnp.uint32),
        compiler_params=_params("arbitrary"),
        name="moe_grouped_swiglu",
    )(block_e, valid, n_used, xs, wg, wu, wd)


def _combine_kernel(dest_ref, x_ref, mod_ref, hp_ref, w3_ref, wg_ref, wu_ref, wd_ref, ys_ref, o_ref, yg_scr, sem):
    rows = hp_ref.shape[0] // ROW_SUB

    def issue(tok, carry):
        for k in range(TOP_K):
            _row_copy(ys_ref, dest_ref[0, k, tok], yg_scr.at[k], tok, sem).start(priority=k % 2)
        return carry

    lax.fori_loop(0, rows, issue, 0, unroll=2)
    h = _load_row_tiles(hp_ref, ROW_SUB).astype(BF16)
    a = _silu(_dot(h, wg_ref[...])) * _dot(h, wu_ref[...])
    acc = _dot(a.astype(BF16), wd_ref[...])

    def drain(j, carry):
        _row_copy(ys_ref, 0, yg_scr.at[0], 0, sem).wait()
        return carry

    lax.fori_loop(0, TOP_K * rows, drain, 0, unroll=16)
    w3 = w3_ref[...]
    for k in range(TOP_K):
        wk = w3[:, k:k + 1] + w3[:, TOP_K + k:TOP_K + k + 1] + w3[:, 2 * TOP_K + k:2 * TOP_K + k + 1]
        acc = acc + wk * _load_row_tiles(yg_scr.at[k], ROW_SUB)
    x = x_ref[...]
    o_ref[...] = x + mod_ref[:, 5:6, :] * acc.reshape(x.shape)


def _combine(dest3, x1, mod, hp, w3, wg, wu, wd, ys):
    n, t, d = x1.shape
    rt = _Rows(n, t)
    assert rt.rows & (rt.rows - 1) == 0
    return pl.pallas_call(
        _combine_kernel,
        grid=(rt.steps,),
        in_specs=[pl.BlockSpec((1, TOP_K, rt.rows), lambda i: (i, 0, 0), memory_space=pltpu.SMEM),
                  rt.seq_spec(t, d), rt.seq_spec(6, d), rt.tile_spec(ROW_SUB), rt.row_spec(128),
                  _const_spec(wg.shape), _const_spec(wu.shape), _const_spec(wd.shape),
                  pl.BlockSpec(memory_space=pl.ANY)],
        out_specs=rt.seq_spec(t, d),
        out_shape=jax.ShapeDtypeStruct((n, t, d), F32),
        scratch_shapes=[pltpu.VMEM((TOP_K, rt.rows * ROW_SUB, 128), jnp.uint32), pltpu.SemaphoreType.DMA(())],
        compiler_params=_params("arbitrary"),
        name="moe_combine_shared",
    )(dest3, x1, mod, hp, w3, wg, wu, wd, ys)


def _moe(layer, groups, p):
    n_e = p['w_gate_e'].shape[1]
    a = sum(g[1][2].size for g in groups)
    tm = 512 if a // n_e >= 1024 else 128
    counts = [g[1][5][:, 0].astype(jnp.int32) for g in groups]
    total = sum(counts)
    padded = (total + tm - 1) // tm * tm
    pend = jnp.cumsum(padded)
    pstart = pend - padded
    experts = jnp.arange(n_e, dtype=jnp.int32)
    dests = []
    base = pstart
    for (_, routed), cnt in zip(groups, counts):
        onehot = routed[2][None] == experts[:, None, None, None]
        dests.append(routed[3] + jnp.sum(jnp.where(onehot, base[:, None, None, None], 0), axis=0))
        base = base + cnt
    n_blocks = -(-a // tm) + n_e
    first = jnp.arange(n_blocks, dtype=jnp.int32) * tm
    block_e = jnp.minimum(jnp.sum((pend[None, :] <= first[:, None]).astype(jnp.int32), axis=1), n_e - 1)
    mine = block_e[:, None] == experts[None, :]
    last = jnp.sum(jnp.where(mine, (pstart + total)[None, :], 0), axis=1)
    valid = jnp.clip(last - first, 0, tm).astype(jnp.int32)
    n_used = (pend[-1] // tm).astype(jnp.int32).reshape(1)
    xs = None
    for (_, routed), dest3 in zip(groups, dests):
        xs = _dispatch(dest3, routed[1], n_blocks * tm, xs)
    ys = _gmm(layer, block_e, valid, n_used, xs, p['w_gate_e'], p['w_up_e'], p['w_down_e'], tm)
    shared = [p[name][layer].astype(BF16) for name in ('w_gate_s', 'w_up_s', 'w_down_s')]
    return [_combine(dest3, routed[0], mod, routed[1], routed[4], *shared, ys)
            for (mod, routed), dest3 in zip(groups, dests)]


def _odd_in_kernel(x_ref, mod_ref, g_ref, w_ref, gqa_ref, wqb_ref, gqn_ref, gqr_ref, gqrs_ref, gkva_ref,
                   gkr_ref, gkrs_ref, gkn_ref, cos_ref, sin_ref, seg_ref, wuk_ref, wuv_ref,
                   cq_ref, ck_ref, cv_ref, co_ref, gt_ref, ckv_ref, kr_ref, *outs, prompt):
    h = _norm_mod(x_ref, mod_ref, g_ref, 0, 1).astype(BF16)
    z = _dot(h, w_ref[...])
    cq_ref[...] = z[:, 0:512]
    ck_ref[...] = z[:, 512:1024]
    cv_ref[...] = z[:, 1024:1536]
    co_ref[...] = z[:, 1536:2048]
    gt_ref[...] = z[:, 2560:2688]
    cos = cos_ref[...]
    sin = sin_ref[...]
    ckv = _rms(z[:, 2304:2432]) * gkva_ref[...]
    ckv_ref[...] = ckv
    xr = z[:, 2432:2496]
    xr_rot = z[:, 2496:2560]
    rr = lax.rsqrt(jnp.mean(xr * xr, axis=-1, keepdims=True) + EPS)
    kr = rr * (xr * gkr_ref[...] * cos[:, 0:D_ROPE] + xr_rot * gkrs_ref[...] * sin[:, 0:D_ROPE])
    kr_ref[...] = kr
    qa = (_rms(z[:, 2048:2304]) * gqa_ref[...]).astype(BF16)
    qd = _dot(qa, wqb_ref[...])
    nw = D_HEADS * D_NOPE
    rw = D_HEADS * D_ROPE
    qn = [_rms(qd[:, hh * D_NOPE:(hh + 1) * D_NOPE]) * gqn_ref[...] * MLA_SCALE for hh in range(D_HEADS)]
    xq = qd[:, nw:nw + rw]
    xq_rot = qd[:, nw + rw:nw + 2 * rw]
    rq = lax.rsqrt(_dot_sel(xq * xq, seg_ref[...]) * (1.0 / D_ROPE) + EPS)
    qr = rq * (xq * gqr_ref[...] * cos + xq_rot * gqrs_ref[...] * sin) * MLA_SCALE
    ckv_b = ckv.astype(BF16)
    if prompt:
        qcat_ref, kcat_ref, v_ref = outs
        kn = _dot(ckv_b, wuk_ref[...])
        zpad = jnp.zeros((z.shape[0], 256 - D_NOPE - D_ROPE), F32)
        qcat_ref[...] = jnp.concatenate(
            [piece for hh in range(D_HEADS) for piece in (qn[hh], qr[:, hh * D_ROPE:(hh + 1) * D_ROPE], zpad)],
            axis=1).astype(BF16)
        kcat_ref[...] = jnp.concatenate(
            [piece for hh in range(D_HEADS)
             for piece in (_rms(kn[:, hh * D_NOPE:(hh + 1) * D_NOPE]) * gkn_ref[...], kr, zpad)],
            axis=1).astype(BF16)
        v_ref[...] = _dot(ckv_b, wuv_ref[...]).astype(BF16)
    else:
        u_ref, qr_ref = outs
        u_ref[...] = jnp.concatenate(
            [_dot_nt((qn[hh] * gkn_ref[...]).astype(BF16), wuk_ref[:, hh * D_NOPE:(hh + 1) * D_NOPE])
             for hh in range(D_HEADS)], axis=1).astype(BF16)
        qr_ref[...] = qr.astype(BF16)


def _odd_in(x, mod, g, w, consts, cos, sin, prompt):
    n, t, d = x.shape
    rt = _Rows(n, t)
    m = n * t
    widths = [(512, F32)] * 4 + [(128, F32), (128, F32), (D_ROPE, F32)]
    if prompt:
        widths += [(1024, BF16), (1024, BF16), (512, BF16)]
    else:
        widths += [(512, BF16), (256, BF16)]
    return pl.pallas_call(
        functools.partial(_odd_in_kernel, prompt=prompt),
        grid=(rt.steps,),
        in_specs=[rt.seq_spec(t, d), rt.seq_spec(6, d), _const_spec((1, d)), _const_spec(w.shape)]
                 + [_const_spec(c.shape) for c in consts[:9]]
                 + [rt.pos_spec(cos.shape[1]), rt.pos_spec(sin.shape[1])]
                 + [_const_spec(c.shape) for c in consts[9:]],
        out_specs=[rt.row_spec(c) for c, _ in widths],
        out_shape=[jax.ShapeDtypeStruct((m, c), dt) for c, dt in widths],
        compiler_params=_params("arbitrary"),
        name="odd_in_latent_prep",
    )(x, mod, g, w, *consts[:9], cos, sin, *consts[9:])


def _mlstm_kernel(q_ref, k_ref, v_ref, og_ref, gt_ref, gb_ref, ng_ref, c0_ref, n0_ref, m0_ref, tri_ref, eye_ref,
                  o_ref, co_ref, no_ref, mo_ref, c_scr, n_scr, m_scr, *, chunk):
    t = pl.program_id(1)

    @pl.when(t == 0)
    def _():
        c_scr[...] = c0_ref[...]
        n_scr[...] = n0_ref[...]
        m_scr[...] = m0_ref[...]

    for sq in range(q_ref.shape[0]):
        _mlstm_chunk(q_ref.at[sq], k_ref.at[sq], v_ref.at[sq], og_ref.at[sq], gt_ref.at[sq], gb_ref, ng_ref,
                     tri_ref, eye_ref, o_ref.at[sq], c_scr.at[sq], n_scr.at[sq], m_scr.at[sq], chunk)

    @pl.when(t == pl.num_programs(1) - 1)
    def _():
        co_ref[...] = c_scr[...]
        no_ref[...] = n_scr[...]
        mo_ref[...] = m_scr[...]


def _mlstm_chunk(q_ref, k_ref, v_ref, og_ref, gt_ref, gb_ref, ng_ref, tri_ref, eye_ref,
                 o_ref, c_scr, n_scr, m_scr, chunk):
    dh = q_ref.shape[1] // C_HEADS
    pre = gt_ref[...] + gb_ref[...]
    lf = _log_sigmoid(pre)
    if chunk >= 16:
        l_hi, l_mid, l_lo = _split3(lf)
        f_cum = _dot(tri_ref[...], l_hi) + _dot(tri_ref[...], l_mid) + _dot(tri_ref[...], l_lo)
    else:
        rowg = lax.broadcasted_iota(jnp.int32, lf.shape, 0)
        f_cum = jnp.zeros_like(lf)
        for s in range(chunk):
            f_cum = f_cum + jnp.where(rowg >= s, lf[s:s + 1, :], 0.0)
    ri = lax.broadcasted_iota(jnp.int32, (chunk, chunk), 0)
    ci = lax.broadcasted_iota(jnp.int32, (chunk, chunk), 1)
    pad = 16 - chunk if chunk < 16 else 0

    def pad_rows(a):
        if pad:
            return jnp.concatenate([a, jnp.zeros((pad, a.shape[1]), a.dtype)], axis=0)
        return a

    for h in range(C_HEADS):
        sl = slice(h * dh, (h + 1) * dh)
        q = q_ref[:, sl]
        k = k_ref[:, sl] * (dh ** -0.5)
        v = v_ref[:, sl]
        qb, kb, vb = q.astype(BF16), k.astype(BF16), v.astype(BF16)
        f_col = f_cum[:, C_HEADS + h:C_HEADS + h + 1]
        i_col = pre[:, h:h + 1]
        m_prev = m_scr[h:h + 1, 0:1]
        a_row = jnp.sum(jnp.where(ri == ci, i_col - f_col, 0.0), axis=0, keepdims=True)
        log_d = jnp.where(ci <= ri, f_col + a_row, NEG)
        inter = f_col + m_prev
        m_t = jnp.maximum(inter, jnp.max(log_d, axis=-1, keepdims=True))
        w_inter = jnp.exp(inter - m_t)
        qk = _dot_nt(qb, kb) * jnp.exp(log_d - m_t)
        c_old = c_scr[h]
        n_old = n_scr[h:h + 1, :]
        num = _dot(qk.astype(BF16), vb) + w_inter * _dot_nt(qb, c_old.astype(BF16))
        den = jnp.sum(qk, axis=-1, keepdims=True) + w_inter * jnp.sum(q * n_old, axis=-1, keepdims=True)
        hh = num / jnp.maximum(jnp.abs(den), jnp.exp(-m_t))
        o_ref[:, sl] = _rms(hh) * ng_ref[...] * _sigmoid(og_ref[:, sl])
        f_last = f_col[chunk - 1:chunk, :]
        a_end = f_last - f_col + i_col
        m_new = jnp.maximum(f_last + m_prev, jnp.max(a_end, axis=0, keepdims=True))
        w = jnp.exp(a_end - m_new)
        dec = jnp.exp(f_last + m_prev - m_new)
        wv_t = _dot_nt(eye_ref[...], pad_rows((w * v).astype(BF16))).astype(BF16)
        c_scr[h] = dec * c_old + _dot(wv_t, pad_rows(kb))
        n_scr[h:h + 1, :] = dec * n_old + jnp.sum(w * k, axis=0, keepdims=True)
        m_scr[h:h + 1, :] = jnp.broadcast_to(m_new, (1, m_scr.shape[1]))


MLSTM_SEQS = 2


def _mlstm(n, t, q, k, v, og, gt, gb, ng, c0, n0, m0, tri, eye):
    chunk = CHUNK_C if t % CHUNK_C == 0 else t
    nt = t // chunk
    w = q.shape[1]
    dh = w // C_HEADS
    seqs = MLSTM_SEQS
    assert n % seqs == 0

    def rows(c):
        return pl.BlockSpec((seqs, chunk, c), lambda i, j: (i, j, 0))

    def per_seq(a):
        return a.reshape(n, t, a.shape[1])

    cst = pl.BlockSpec((seqs, C_HEADS, dh, dh), lambda i, j: (i, 0, 0, 0))
    nst = pl.BlockSpec((seqs, C_HEADS, dh), lambda i, j: (i, 0, 0))
    o, c_new, n_new, m_new = pl.pallas_call(
        functools.partial(_mlstm_kernel, chunk=chunk),
        grid=(n // seqs, nt),
        in_specs=[rows(w), rows(w), rows(w), rows(w), rows(128),
                  pl.BlockSpec(gb.shape, lambda i, j: (0, 0)), pl.BlockSpec(ng.shape, lambda i, j: (0, 0)),
                  cst, nst, nst,
                  pl.BlockSpec(tri.shape, lambda i, j: (0, 0)), pl.BlockSpec(eye.shape, lambda i, j: (0, 0))],
        out_specs=[rows(w), cst, nst, nst],
        out_shape=[jax.ShapeDtypeStruct((n, t, w), F32), jax.ShapeDtypeStruct(c0.shape, F32),
                   jax.ShapeDtypeStruct(n0.shape, F32), jax.ShapeDtypeStruct(m0.shape, F32)],
        scratch_shapes=[pltpu.VMEM((seqs, C_HEADS, dh, dh), F32), pltpu.VMEM((seqs, C_HEADS, dh), F32),
                        pltpu.VMEM((seqs, C_HEADS, dh), F32)],
        compiler_params=_params("arbitrary", "arbitrary"),
        name="mlstm_scan",
    )(per_seq(q), per_seq(k), per_seq(v), per_seq(og), per_seq(gt), gb, ng, c0, n0, m0, tri, eye)
    return o.reshape(n * t, w), c_new, n_new, m_new


def _flash_kernel(qi_ref, kj_ref, q_ref, k_ref, v_ref, o_ref, m_scr, l_scr, acc_scr):
    p = pl.program_id(2)
    qi = qi_ref[p]
    kj = kj_ref[p]

    @pl.when(kj == 0)
    def _():
        m_scr[...] = jnp.full_like(m_scr, NEG)
        l_scr[...] = jnp.zeros_like(l_scr)
        acc_scr[...] = jnp.zeros_like(acc_scr)

    def step(masked):
        bq = q_ref.shape[0]
        hq = bq // 2 if bq % 16 == 0 else bq
        for q0 in range(0, bq, hq):
            rows = slice(q0, q0 + hq)
            s = _dot_nt(q_ref[rows, :], k_ref[...])
            bk = s.shape[1]
            if masked:
                ri = lax.broadcasted_iota(jnp.int32, (hq, bk), 0) + q0
                ci = lax.broadcasted_iota(jnp.int32, (hq, bk), 1)
                s = jnp.where(ci <= ri, s, NEG)
            lanes = m_scr.shape[1]
            m_old = m_scr[rows, :]
            m_new = jnp.maximum(m_old, jnp.max(s, axis=-1, keepdims=True))
            alpha = jnp.exp(m_old - m_new)
            pr = jnp.exp(s - jnp.concatenate([m_new] * (bk // lanes), axis=1))
            l_scr[rows, :] = alpha * l_scr[rows, :] + jnp.sum(pr, axis=-1, keepdims=True)
            acc_scr[rows, :] = alpha * acc_scr[rows, :] + _dot(pr.astype(BF16), v_ref[...])
            m_scr[rows, :] = m_new

    @pl.when(kj < qi)
    def _():
        step(False)

    @pl.when(kj == qi)
    def _():
        step(True)
        o_ref[...] = acc_scr[...] / l_scr[...]


def _flash(n, t, qcat, kcat, v):
    blk = FLASH_BLOCK if t % FLASH_BLOCK == 0 else t
    nq = t // blk
    pairs = [(i, j) for i in range(nq) for j in range(i + 1)]
    qi = jnp.asarray([a for a, _ in pairs], jnp.int32)
    kj = jnp.asarray([b for _, b in pairs], jnp.int32)
    dv = v.shape[1] // D_HEADS
    grid_spec = pltpu.PrefetchScalarGridSpec(
        num_scalar_prefetch=2,
        grid=(n, D_HEADS, len(pairs)),
        in_specs=[pl.BlockSpec((blk, 256), lambda b, h, p, qi, kj: (b * nq + qi[p], h)),
                  pl.BlockSpec((blk, 256), lambda b, h, p, qi, kj: (b * nq + kj[p], h)),
                  pl.BlockSpec((blk, dv), lambda b, h, p, qi, kj: (b * nq + kj[p], h))],
        out_specs=pl.BlockSpec((blk, dv), lambda b, h, p, qi, kj: (b * nq + qi[p], h)),
        scratch_shapes=[pltpu.VMEM((blk, dv), F32), pltpu.VMEM((blk, dv), F32), pltpu.VMEM((blk, dv), F32)],
    )
    assert blk % dv == 0
    return pl.pallas_call(
        _flash_kernel,
        grid_spec=grid_spec,
        out_shape=jax.ShapeDtypeStruct((n * t, v.shape[1]), F32),
        compiler_params=_params("arbitrary", "arbitrary", "arbitrary"),
        name="mla_prompt_flash",
    )(qi, kj, qcat, kcat, v)


def _paged_kernel(pt_ref, u_ref, qr_ref, cn_ref, kn_ref, wukt_ref, wuv_ref, ckv_hbm, krt_hbm,
                  o_ref, ck_buf, kr_buf, sem, m_scr, l_scr, acc_scr, *, pages, li, n_groups, n_steps):
    b = pl.program_id(0)
    g = pl.program_id(1)
    hq = u_ref.shape[1]
    tq = hq // D_HEADS
    step = b * n_groups + g
    slot = lax.rem(step, 2)

    def page_copies(seq, group, sl, lookup):
        copies = []
        for i in range(pages):
            page = pt_ref[seq, group * pages + i] if lookup else 0
            rows = pl.ds(i * PAGE_SIZE, PAGE_SIZE)
            copies.append(pltpu.make_async_copy(ckv_hbm.at[page, li], ck_buf.at[sl, rows, :], sem.at[sl]))
            copies.append(pltpu.make_async_copy(krt_hbm.at[page, li], kr_buf.at[sl, i], sem.at[sl]))
        return copies

    @pl.when(step == 0)
    def _():
        for cp in page_copies(0, 0, 0, True):
            cp.start()

    @pl.when(step + 1 < n_steps)
    def _():
        nxt = step + 1
        for cp in page_copies(lax.div(nxt, n_groups), lax.rem(nxt, n_groups), 1 - slot, True):
            cp.start()

    for cp in page_copies(0, 0, slot, False):
        cp.wait()

    @pl.when(g == 0)
    def _():
        m_scr[...] = jnp.full_like(m_scr, NEG)
        l_scr[...] = jnp.zeros_like(l_scr)
        acc_scr[...] = jnp.zeros_like(acc_scr)

    lhs = jnp.concatenate([wukt_ref[...], u_ref[0]], axis=0)
    qr = qr_ref[0]
    nk = D_HEADS * D_NOPE

    def scores(ck_b, kr_b, kr_keys_minor=True):
        big = _dot_nt(lhs, ck_b)
        rows = []
        for h in range(D_HEADS):
            kn_t = big[h * D_NOPE:(h + 1) * D_NOPE, :]
            rinv = lax.rsqrt(jnp.sum(kn_t * kn_t, axis=0, keepdims=True) * (1.0 / D_NOPE) + EPS)
            rows.append(big[nk + h * tq:nk + (h + 1) * tq, :] * rinv)
        rope = _dot(qr, kr_b) if kr_keys_minor else _dot_nt(qr, kr_b)
        return jnp.concatenate(rows, axis=0) + rope

    def update(s, ck_b):
        m_old = m_scr[...]
        m_new = jnp.maximum(m_old, jnp.max(s, axis=-1, keepdims=True))
        alpha = jnp.exp(m_old - m_new)
        pr = jnp.exp(s - m_new)
        l_scr[...] = alpha * l_scr[...] + jnp.sum(pr, axis=-1, keepdims=True)
        acc_scr[...] = alpha * acc_scr[...] + _dot(pr.astype(BF16), ck_b)
        m_scr[...] = m_new

    span = 2 * PAGE_SIZE
    cks = [ck_buf[slot, pl.ds(i * span, span), :].astype(BF16) for i in range(pages // 2)]
    krs = [jnp.concatenate([kr_buf[slot, 2 * i], kr_buf[slot, 2 * i + 1]], axis=1).astype(BF16)
           for i in range(pages // 2)]
    s_all = jnp.concatenate([scores(cks[i], krs[i]) for i in range(pages // 2)], axis=1)
    update(s_all, jnp.concatenate(cks, axis=0))

    @pl.when(g == pl.num_programs(1) - 1)
    def _():
        fill = PAGE_SIZE - tq
        ck_b = jnp.concatenate([cn_ref[...], jnp.zeros((fill, cn_ref.shape[1]), F32)], axis=0).astype(BF16)
        kr_b = jnp.concatenate([kn_ref[...], jnp.zeros((fill, kn_ref.shape[1]), F32)], axis=0).astype(BF16)
        s = scores(ck_b, kr_b, kr_keys_minor=False)
        ri = lax.broadcasted_iota(jnp.int32, s.shape, 0)
        ci = lax.broadcasted_iota(jnp.int32, s.shape, 1)
        update(jnp.where(ci <= ri % tq, s, NEG), ck_b)
        lat = (acc_scr[...] / l_scr[...]).astype(BF16)
        full = _dot(lat, wuv_ref[...])
        dv = wuv_ref.shape[1] // D_HEADS
        o_ref[...] = jnp.concatenate(
            [full[h * tq:(h + 1) * tq, h * dv:(h + 1) * dv] for h in range(D_HEADS)], axis=1)


def _paged(page_table, li, u3, qr3, ckv_new, kr_new, wuk_t, wuv, cache_ckv, cache_kr):
    n, hq, lat = u3.shape
    tq = hq // D_HEADS
    n_pages = page_table.shape[1]
    pages = min(PAGES_PER_STEP, n_pages)
    assert n_pages % pages == 0 and pages % 2 == 0
    ng = n_pages // pages
    cache_kr_t = jnp.swapaxes(cache_kr, 2, 3)
    keys = pages * PAGE_SIZE

    grid_spec = pltpu.PrefetchScalarGridSpec(
        num_scalar_prefetch=1,
        grid=(n, ng),
        in_specs=[pl.BlockSpec((1, hq, lat), lambda b, g, pt: (b, 0, 0)),
                  pl.BlockSpec((1, hq, D_ROPE), lambda b, g, pt: (b, 0, 0)),
                  pl.BlockSpec((tq, lat), lambda b, g, pt: (b, 0)),
                  pl.BlockSpec((tq, D_ROPE), lambda b, g, pt: (b, 0)),
                  pl.BlockSpec(wuk_t.shape, lambda b, g, pt: (0, 0)),
                  pl.BlockSpec(wuv.shape, lambda b, g, pt: (0, 0)),
                  pl.BlockSpec(memory_space=pl.ANY), pl.BlockSpec(memory_space=pl.ANY)],
        out_specs=pl.BlockSpec((tq, wuv.shape[1]), lambda b, g, pt: (b, 0)),
        scratch_shapes=[pltpu.VMEM((2, keys, lat), F32), pltpu.VMEM((2, pages, D_ROPE, PAGE_SIZE), F32),
                        pltpu.SemaphoreType.DMA((2,)),
                        pltpu.VMEM((hq, 1), F32), pltpu.VMEM((hq, 1), F32), pltpu.VMEM((hq, lat), F32)],
    )
    return pl.pallas_call(
        functools.partial(_paged_kernel, pages=pages, li=li, n_groups=ng, n_steps=n * ng),
        grid_spec=grid_spec,
        out_shape=jax.ShapeDtypeStruct((n * tq, wuv.shape[1]), F32),
        compiler_params=_params("arbitrary", "arbitrary"),
        name="mla_sample_paged",
    )(page_table, u3, qr3, ckv_new, kr_new, wuk_t, wuv, cache_ckv, cache_kr_t)


def _np_seg(n_seg, seg_in, seg_out):
    mat = np.zeros((n_seg * seg_in, n_seg * seg_out), np.float32)
    for s in range(n_seg):
        mat[s * seg_in:(s + 1) * seg_in, s * seg_out:(s + 1) * seg_out] = 1.0
    return mat


def _rot_cols(w):
    half = w.shape[-1] // 2
    return jnp.concatenate([-w[..., half:], w[..., :half]], axis=-1)


def _swap_halves(g):
    half = g.shape[-1] // 2
    return jnp.concatenate([g[..., half:], g[..., :half]], axis=-1)


def _rope_tables(pos, reps):
    half = D_ROPE // 2
    inv = ROPE_BASE ** (-jnp.arange(half, dtype=F32) / half)
    ang = pos.astype(F32)[:, None] * inv
    cos = jnp.concatenate([jnp.cos(ang), jnp.cos(ang)], axis=-1)
    sin = jnp.concatenate([jnp.sin(ang), jnp.sin(ang)], axis=-1)
    return jnp.tile(cos, (1, reps)), jnp.tile(sin, (1, reps))


def _trunk(x, c_mod, pos0, gla0, mc0, mn0, mm0, p, sample_ctx):
    n, t, d = x.shape
    m = n * t
    rt = _Rows(n, t)
    eye_r = jnp.eye(rt.rows, dtype=BF16)
    triu_r = jnp.asarray(np.triu(np.ones((rt.rows, rt.rows), np.float32)), BF16)
    dh = mc0.shape[3]
    eye_hk = jnp.eye(B_HEADS * gla0.shape[3], dtype=BF16)
    eye_dh = jnp.eye(dh, dtype=BF16)
    results = {}

    layer, li = 0, 0
    mod = c_mod[layer]
    w_in = p['w_in_even'][li]
    w_in = jnp.concatenate([w_in, jnp.zeros((d, 128 - B_GATE_RANK), F32)], axis=1).astype(BF16)
    a_ws = jnp.tril(p['a_ws'][li])
    a_bs = p['a_bs'][li]
    if t % CHUNK_A == 0:
        ws = a_ws
        bs = jnp.repeat(a_bs.T, CHUNK_A, axis=1)
    else:
        ws = jnp.stack([jnp.kron(jnp.eye(CHUNK_A // t, dtype=F32), a_ws[g, :t, :t]) for g in range(A_GROUPS)])
        bs = jnp.repeat(jnp.tile(a_bs[:, :t], (1, CHUNK_A // t)).T, CHUNK_A, axis=1)
    out_a, v_rows, bq, bk, bv, br, bg = _even_in(
        x, mod, p['norm_mix_g'][layer].reshape(1, d), w_in, p['a_norm_g'][li].reshape(1, -1),
        ws.astype(BF16), bs)
    hk = bq.shape[1]
    dv = bv.shape[1] // B_HEADS
    wg2 = jnp.concatenate([p['b_w_gate2'][li], jnp.zeros((128 - B_GATE_RANK, hk), F32)], axis=0)
    seg = jnp.asarray(_np_seg(B_HEADS, hk // B_HEADS, dv), BF16)
    out_b, s_new = _gla(n, t, bq, bk, bv, br, bg, wg2, p['b_gate_bias'][li].reshape(1, hk),
                        jnp.tile(p['b_norm_g'][li], B_HEADS).reshape(1, -1), gla0[li], seg, eye_hk)
    results['gla'] = s_new
    results['v_rows'] = v_rows
    routed = _out_route(
        x, mod, out_a, out_b, p['w_out'][layer].astype(BF16), p['norm_ffn_g'][layer].reshape(1, d),
        p['w_router'][layer].T, p['b_router'][layer].reshape(-1, 1), eye_r, triu_r)
    x = yield mod, routed

    layer, li = 1, 0
    mod = c_mod[layer]
    w = p['w_in_odd'][li]
    hw = C_HEADS * dh
    o_g = 3 * hw
    o_o = o_g + 2 * C_HEADS
    o_qa = o_o + hw
    o_kva = o_qa + p['d_g_qa'].shape[1]
    o_kr = o_kva + p['d_g_kva'].shape[1]
    w_kr = w[:, o_kr:o_kr + D_ROPE]
    w_odd = jnp.concatenate(
        [w[:, :o_g], w[:, o_o:o_qa], w[:, o_qa:o_kva], w[:, o_kva:o_kr], w_kr, _rot_cols(w_kr),
         w[:, o_g:o_o], jnp.zeros((d, 128 - 2 * C_HEADS), F32)], axis=1).astype(BF16)
    wqb = p['d_w_qb'][li].reshape(-1, D_HEADS, D_NOPE + D_ROPE)
    wqb_r = wqb[:, :, D_NOPE:]
    wqb2 = jnp.concatenate([wqb[:, :, :D_NOPE].reshape(-1, D_HEADS * D_NOPE),
                            wqb_r.reshape(-1, D_HEADS * D_ROPE),
                            _rot_cols(wqb_r).reshape(-1, D_HEADS * D_ROPE)], axis=1).astype(BF16)
    w_uk = p['d_w_uk'][li]
    lat = w_uk.shape[0]
    g_qr = p['d_g_qr'][li]
    g_kr = p['d_g_kr'][li]
    cos, sin = _rope_tables(pos0 + jnp.arange(t), D_HEADS)
    prompt = sample_ctx is None
    if not prompt:
        cos, sin = jnp.tile(cos, (rt.s, 1)), jnp.tile(sin, (rt.s, 1))
    consts = [p['d_g_qa'][li].reshape(1, -1), wqb2, p['d_g_qn'][li].reshape(1, -1),
              jnp.tile(g_qr, D_HEADS).reshape(1, -1), jnp.tile(_swap_halves(g_qr), D_HEADS).reshape(1, -1),
              p['d_g_kva'][li].reshape(1, -1), g_kr.reshape(1, -1), _swap_halves(g_kr).reshape(1, -1),
              p['d_g_kn'][li].reshape(1, -1),
              jnp.asarray(_np_seg(D_HEADS, D_ROPE, D_ROPE), BF16),
              w_uk.reshape(lat, -1).astype(BF16), p['d_w_uv'][li].reshape(lat, -1).astype(BF16)]
    outs = _odd_in(x, mod, p['norm_mix_g'][layer].reshape(1, d), w_odd, consts, cos, sin, prompt)
    cq, ck, cv, co, gates, ckv, kr = outs[:7]
    gb = jnp.concatenate([p['c_ig_bias'][li], p['c_fg_bias'][li],
                          jnp.zeros((128 - 2 * C_HEADS,), F32)]).reshape(1, 128)
    chunk = CHUNK_C if t % CHUNK_C == 0 else t
    tri = jnp.asarray(np.tril(np.ones((chunk, chunk), np.float32)), BF16)
    m0b =jnp.broadcast_to(mm0[li][:, :, None], (n, C_HEADS, dh))
    out_c, c_new, n_new, m_new = _mlstm(n, t, cq, ck, cv, co, gates, gb,
                                        p['c_norm_g'][li].reshape(1, -1), mc0[li], mn0[li], m0b, tri, eye_dh)
    results['mlstm'] = (c_new, n_new, m_new[:, :, 0])
    results['ckv'] = ckv.reshape(n, t, -1)
    results['kr'] = kr.reshape(n, t, -1)
    if prompt:
        qcat, kcat, vv = outs[7:]
        out_d = _flash(n, t, qcat, kcat, vv)
    else:
        u, qr = outs[7:]
        cache_ckv, cache_kr, page_table = sample_ctx
        u3 = u.reshape(n, t, D_HEADS, lat).transpose(0, 2, 1, 3).reshape(n, D_HEADS * t, lat)
        qr3 = qr.reshape(n, t, D_HEADS, D_ROPE).transpose(0, 2, 1, 3).reshape(n, D_HEADS * t, D_ROPE)
        wuk_t = w_uk.transpose(1, 2, 0).reshape(-1, lat).astype(BF16)
        out_d = _paged(page_table, li, u3, qr3, ckv, kr, wuk_t, consts[11], cache_ckv, cache_kr)
    routed = _out_route(
        x, mod, out_c, out_d, p['w_out'][layer].astype(BF16), p['norm_ffn_g'][layer].reshape(1, d),
        p['w_router'][layer].T, p['b_router'][layer].reshape(-1, 1), eye_r, triu_r)
    x = yield mod, routed
    return x, results


def kernel(x_prompt, x_sample, state_gla, state_mlstm_c, state_mlstm_n, state_mlstm_m,
           cache_ckv, cache_krope, page_table, c_prompt, c_sample,
           norm_mix_g, norm_ffn_g, w_ada, b_ada, w_out,
           w_in_even, a_norm_g, a_ws, a_bs, b_w_gate2, b_gate_bias, b_norm_g,
           w_in_odd, c_ig_bias, c_fg_bias, c_norm_g,
           d_g_qa, d_w_qb, d_g_kva, d_g_qn, d_g_qr, d_g_kr, d_g_kn, d_w_uk, d_w_uv,
           w_router, b_router, w_gate_e, w_up_e, w_down_e, w_gate_s, w_up_s, w_down_s):
    p = dict(norm_mix_g=norm_mix_g, norm_ffn_g=norm_ffn_g, w_out=w_out,
             w_in_even=w_in_even, a_norm_g=a_norm_g, a_ws=a_ws, a_bs=a_bs, b_w_gate2=b_w_gate2,
             b_gate_bias=b_gate_bias, b_norm_g=b_norm_g, w_in_odd=w_in_odd, c_ig_bias=c_ig_bias,
             c_fg_bias=c_fg_bias, c_norm_g=c_norm_g, d_g_qa=d_g_qa, d_w_qb=d_w_qb, d_g_kva=d_g_kva,
             d_g_qn=d_g_qn, d_g_qr=d_g_qr, d_g_kr=d_g_kr, d_g_kn=d_g_kn, d_w_uk=d_w_uk, d_w_uv=d_w_uv,
             w_router=w_router, b_router=b_router, w_gate_e=w_gate_e, w_up_e=w_up_e, w_down_e=w_down_e,
             w_gate_s=w_gate_s, w_up_s=w_up_s, w_down_s=w_down_s)
    n_p, t_p, d = x_prompt.shape
    n_s, t_s, _ = x_sample.shape
    depth = w_ada.shape[0]
    pad_p = (-n_p) % 8
    c_all = jnp.concatenate([c_prompt, jnp.zeros((pad_p, d), F32), c_sample], axis=0)
    mod_all = _ada(c_all, w_ada, b_ada).reshape(depth, c_all.shape[0], 6, d)
    mod_p = mod_all[:, :n_p]
    mod_s = mod_all[:, n_p + pad_p:]

    n_even, _, bh, bdk, bdv = state_gla.shape
    n_odd, _, chh, cdh, _ = state_mlstm_c.shape
    gla0_p = jnp.zeros((n_even, n_p, bh, bdk, bdv), F32)
    mc0_p = jnp.zeros((n_odd, n_p, chh, cdh, cdh), F32)
    mn0_p = jnp.zeros((n_odd, n_p, chh, cdh), F32)
    mm0_p = jnp.full((n_odd, n_p, chh), NEG, F32)
    past_len = page_table.shape[1] * PAGE_SIZE

    trunks = [_trunk(x_prompt, mod_p, 0, gla0_p, mc0_p, mn0_p, mm0_p, p, None),
              _trunk(x_sample, mod_s, past_len, state_gla, state_mlstm_c, state_mlstm_n, state_mlstm_m, p,
                     (cache_ckv, cache_krope, page_table))]
    pending = [next(tr) for tr in trunks]
    for layer in range(depth):
        mixed = _moe(layer, pending, p)
        pending = []
        for tr, x_new in zip(trunks, mixed):
            try:
                pending.append(tr.send(x_new))
            except StopIteration as done:
                pending.append(done.value)
    (y_p, rp), (y_s, rs) = pending
    aw = rs['v_rows'].shape[1]
    return (y_p, y_s, rp['gla'][None], rs['gla'][None], rs['v_rows'].reshape(1, n_s, t_s, aw),
            rp['mlstm'][0][None], rs['mlstm'][0][None], rp['mlstm'][1][None], rs['mlstm'][1][None],
            rp['mlstm'][2][None], rs['mlstm'][2][None],
            rp['ckv'][:, None], rs['ckv'][:, None], rp['kr'][:, None], rs['kr'][:, None])
```

```python
import functools

import numpy as np
import jax
import jax.numpy as jnp
from jax import lax
from jax.experimental import pallas as pl
from jax.experimental.pallas import tpu as pltpu

F32 = jnp.float32
BF16 = jnp.bfloat16

EPS = 1e-6
NEG = -1e30

A_GROUPS = 4
CHUNK_A = 128
B_HEADS = 4
B_GATE_RANK = 16
B_GATE_TAU = 16.0
CHUNK_B = 16
C_HEADS = 4
CHUNK_C = 128
D_HEADS = 4
D_NOPE = 128
D_ROPE = 64
ROPE_BASE = 10000.0
MLA_SCALE = (D_NOPE + D_ROPE) ** -0.5
PAGE_SIZE = 128
N_EXPERT_GROUPS = 8
TOPK_GROUPS = 4
TOP_K = 8
ROUTED_SCALE = 2.5

ROW_TILE = 256
SAMPLE_SEQS = 32
FLASH_BLOCK = 1024
PAGES_PER_STEP = 32
VMEM_LIMIT = 56 * 1024 * 1024


def _dot(a, b):
    return jnp.dot(a, b, preferred_element_type=F32)


def _dot_nt(a, b):
    return lax.dot_general(a, b, (((1,), (1,)), ((), ())), preferred_element_type=F32)


def _split2(x):
    hi = x.astype(BF16)
    lo = (x - hi.astype(F32)).astype(BF16)
    return hi, lo


def _split3(x):
    hi = x.astype(BF16)
    r = x - hi.astype(F32)
    mid = r.astype(BF16)
    lo = (r - mid.astype(F32)).astype(BF16)
    return hi, mid, lo


def _dot3(a, b):
    ah, al = _split2(a)
    bh, bl = _split2(b)
    return _dot(ah, bh) + _dot(ah, bl) + _dot(al, bh)


def _dot3_nt(a, b):
    ah, al = _split2(a)
    bh, bl = _split2(b)
    return _dot_nt(ah, bh) + _dot_nt(ah, bl) + _dot_nt(al, bh)


def _dot_sel(x, m01, parts=2):
    ps = _split2(x) if parts == 2 else _split3(x)
    acc = _dot(ps[0], m01)
    for p in ps[1:]:
        acc = acc + _dot(p, m01)
    return acc


def _sigmoid(x):
    return 1.0 / (1.0 + jnp.exp(-x))


def _silu(x):
    return x * _sigmoid(x)


def _log_sigmoid(x):
    return jnp.minimum(x, 0.0) - jnp.log(1.0 + jnp.exp(-jnp.abs(x)))


def _gelu(x):
    return 0.5 * x * (1.0 + jnp.tanh(0.7978845608028654 * (x + 0.044715 * x * x * x)))


def _rms(x, eps=EPS):
    return x * lax.rsqrt(jnp.mean(x * x, axis=-1, keepdims=True) + eps)


def _params(*sem):
    return pltpu.CompilerParams(dimension_semantics=sem, vmem_limit_bytes=VMEM_LIMIT)


def _const_spec(shape):
    nd = len(shape)
    return pl.BlockSpec(shape, lambda *_: (0,) * nd)


class _Rows:
    def __init__(self, n, t):
        self.n, self.t = n, t
        if t % ROW_TILE == 0:
            self.s, self.r = 1, ROW_TILE
            self.tpb = t // ROW_TILE
        else:
            assert t == 8 and n % SAMPLE_SEQS == 0, (n, t)
            self.s, self.r = SAMPLE_SEQS, t
            self.tpb = 1
        self.rows = self.s * self.r
        self.steps = n * t // self.rows

    def seq_spec(self, mid, d):
        s, tpb = self.s, self.tpb
        if mid == self.t:
            return pl.BlockSpec((s, self.r, d), lambda i: (i // tpb, i % tpb, 0))
        return pl.BlockSpec((s, mid, d), lambda i: (i // tpb, 0, 0))

    def row_spec(self, c):
        return pl.BlockSpec((self.rows, c), lambda i: (i, 0))

    def tile_spec(self, sub):
        return pl.BlockSpec((self.rows * sub, 128), lambda i: (i, 0))

    def pos_spec(self, c):
        tpb = self.tpb
        return pl.BlockSpec((self.rows, c), lambda i: (i % tpb, 0))


def _ada_kernel(c_ref, w_ref, b_ref, o_ref):
    o_ref[0] = _dot3(_silu(c_ref[...]), w_ref[0]) + b_ref[0]


def _ada(c_all, w_ada, b_ada):
    depth, d, d6 = w_ada.shape
    nc = c_all.shape[0]
    tn = 1536
    return pl.pallas_call(
        _ada_kernel,
        grid=(depth, d6 // tn),
        in_specs=[pl.BlockSpec((nc, d), lambda l, j: (0, 0)),
                  pl.BlockSpec((1, d, tn), lambda l, j: (l, 0, j)),
                  pl.BlockSpec((1, 1, tn), lambda l, j: (l, 0, j))],
        out_specs=pl.BlockSpec((1, nc, tn), lambda l, j: (l, 0, j)),
        out_shape=jax.ShapeDtypeStruct((depth, nc, d6), F32),
        compiler_params=_params("arbitrary", "arbitrary"),
        name="ada_modulation",
    )(c_all, w_ada, b_ada.reshape(depth, 1, d6))


def _norm_mod(x_ref, mod_ref, g_ref, shift_row, scale_row):
    x = x_ref[...]
    h = _rms(x) * g_ref[...]
    h = h * (1.0 + mod_ref[:, scale_row:scale_row + 1, :]) + mod_ref[:, shift_row:shift_row + 1, :]
    s, r, d = x.shape
    return h.reshape(s * r, d)


def _even_in_kernel(x_ref, mod_ref, g_ref, w_ref, an_ref, ws_ref, bs_ref,
                    oa_ref, v_ref, q_ref, k_ref, bv_ref, r_ref, bg_ref):
    h = _norm_mod(x_ref, mod_ref, g_ref, 0, 1).astype(BF16)
    z = _dot(h, w_ref[...])
    aw = A_GROUPS * CHUNK_A
    rows = z.shape[0]
    u = _gelu(z[:, 0:aw])
    va = _gelu(z[:, aw:2 * aw])
    vn = jnp.concatenate(
        [_rms(va[:, g * CHUNK_A:(g + 1) * CHUNK_A]) for g in range(A_GROUPS)], axis=1) * an_ref[...]
    v_ref[...] = vn
    vb = vn.astype(BF16)
    for c in range(rows // CHUNK_A):
        r0 = c * CHUNK_A
        mixed = jnp.concatenate(
            [_dot(ws_ref[g], vb[r0:r0 + CHUNK_A, g * CHUNK_A:(g + 1) * CHUNK_A]) for g in range(A_GROUPS)],
            axis=1) + bs_ref[...]
        oa_ref[r0:r0 + CHUNK_A, :] = u[r0:r0 + CHUNK_A, :] * mixed
    o = 2 * aw
    q_ref[...] = z[:, o:o + 256]
    k_ref[...] = z[:, o + 256:o + 512]
    bv_ref[...] = z[:, o + 512:o + 1024]
    r_ref[...] = z[:, o + 1024:o + 1536]
    bg_ref[...] = z[:, o + 1536:o + 1664]


def _even_in(x, mod, g, w, an, ws, bs):
    n, t, d = x.shape
    rt = _Rows(n, t)
    m = n * t
    widths = (512, 512, 256, 256, 512, 512, 128)
    return pl.pallas_call(
        _even_in_kernel,
        grid=(rt.steps,),
        in_specs=[rt.seq_spec(t, d), rt.seq_spec(6, d), _const_spec((1, d)), _const_spec(w.shape),
                  _const_spec(an.shape), _const_spec(ws.shape), _const_spec(bs.shape)],
        out_specs=[rt.row_spec(c) for c in widths],
        out_shape=[jax.ShapeDtypeStruct((m, c), F32) for c in widths],
        compiler_params=_params("arbitrary"),
        name="even_in_chunk_gate",
    )(x, mod, g, w, an, ws, bs)


def _gla_kernel(q_ref, k_ref, v_ref, r_ref, g_ref, wg_ref, gb_ref, ng_ref, s0_ref, seg_ref, eye_ref,
                o_ref, so_ref, s_scr, lg_scr, o_scr, *, chunk, n_chunks):
    seqs, _, hk = q_ref.shape
    dk = hk // B_HEADS
    dv = v_ref.shape[2] // B_HEADS
    t = pl.program_id(1)

    @pl.when(t == 0)
    def _():
        for sq in range(seqs):
            s_scr[sq] = s0_ref[sq].reshape(hk, dv)

    for sq in range(seqs):
        lg_scr[sq] = _log_sigmoid(_dot3(g_ref[sq], wg_ref[...]) + gb_ref[...]) * (1.0 / B_GATE_TAU)
    row = lax.broadcasted_iota(jnp.int32, (chunk, hk), 0)
    lane = lax.broadcasted_iota(jnp.int32, (chunk, hk), 1)
    pad = 16 - chunk if chunk < 16 else 0

    def pad_rows(a):
        if pad:
            return jnp.concatenate([a, jnp.zeros((pad, a.shape[1]), a.dtype)], axis=0)
        return a

    def one_chunk(sq, r0):
        lg = lg_scr[sq, pl.ds(r0, chunk), :]
        q = q_ref[sq, pl.ds(r0, chunk), :] * (dk ** -0.5)
        k = k_ref[sq, pl.ds(r0, chunk), :]
        v = v_ref[sq, pl.ds(r0, chunk), :]
        b = jnp.zeros_like(lg)
        for s in range(chunk):
            b = b + jnp.where(row >= s, lg[s:s + 1, :], 0.0)
        prods = []
        for l in range(chunk):
            e = jnp.exp(jnp.where(row <= l, b[l:l + 1, :] - b, NEG))
            prods.append(q[l:l + 1, :] * k * e)
        att = _dot(jnp.concatenate(prods, axis=0).astype(BF16), seg_ref[...])
        o_intra = jnp.concatenate(
            [jnp.sum(att[l * chunk:(l + 1) * chunk, :] * v, axis=0, keepdims=True) for l in range(chunk)], axis=0)
        s_old = s_scr[sq]
        s_b = s_old.astype(BF16)
        qe = q * jnp.exp(b)
        o_inter = jnp.concatenate(
            [_dot(jnp.where(lane // dk == h, qe, 0.0).astype(BF16), s_b) for h in range(B_HEADS)], axis=1)
        o_scr[sq, pl.ds(r0, chunk), :] = o_intra + o_inter
        b_last = b[chunk - 1:chunk, :]
        k_end = pad_rows((k * jnp.exp(b_last - b)).astype(BF16))
        k_t = _dot_nt(eye_ref[...], k_end).astype(BF16)
        dh, dm, dl = _split3(jnp.exp(b_last))
        dec3 = _dot_nt(eye_ref[...], jnp.concatenate([dh, dm, dl, jnp.zeros((13, hk), BF16)], axis=0))
        dec = dec3[:, 0:1] + dec3[:, 1:2] + dec3[:, 2:3]
        v_b = pad_rows(v.astype(BF16))
        kv = jnp.concatenate(
            [_dot(k_t[h * dk:(h + 1) * dk, :], v_b[:, h * dv:(h + 1) * dv]) for h in range(B_HEADS)], axis=0)
        s_scr[sq] = dec * s_old + kv

    def body(c, carry):
        r0 = pl.multiple_of(c * chunk, chunk)
        for sq in range(seqs):
            one_chunk(sq, r0)
        return carry

    lax.fori_loop(0, n_chunks, body, 0)

    for sq in range(seqs):
        o = o_scr[sq]
        on = jnp.concatenate([_rms(o[:, h * dv:(h + 1) * dv]) for h in range(B_HEADS)], axis=1) * ng_ref[...]
        o_ref[sq] = on * _silu(r_ref[sq])

    @pl.when(t == pl.num_programs(1) - 1)
    def _():
        for sq in range(seqs):
            so_ref[sq] = s_scr[sq].reshape(B_HEADS, dk, dv)


GLA_SEQS = 2


def _gla(n, t, q, k, v, r, g, wg, gb, ng, s0, seg, eye):
    tt = ROW_TILE if t % ROW_TILE == 0 else t
    chunk = CHUNK_B if tt % CHUNK_B == 0 else tt
    nt = t // tt
    hk, hv = q.shape[1], v.shape[1]
    dk, dv = hk // B_HEADS, hv // B_HEADS
    seqs = GLA_SEQS
    assert n % seqs == 0

    def rows(c):
        return pl.BlockSpec((seqs, tt, c), lambda i, j: (i, j, 0))

    def per_seq(a):
        return a.reshape(n, t, a.shape[1])

    st = pl.BlockSpec((seqs, B_HEADS, dk, dv), lambda i, j: (i, 0, 0, 0))
    o, s_new = pl.pallas_call(
        functools.partial(_gla_kernel, chunk=chunk, n_chunks=tt // chunk),
        grid=(n // seqs, nt),
        in_specs=[rows(hk), rows(hk), rows(hv), rows(hv), rows(g.shape[1]),
                  pl.BlockSpec(wg.shape, lambda i, j: (0, 0)), pl.BlockSpec(gb.shape, lambda i, j: (0, 0)),
                  pl.BlockSpec(ng.shape, lambda i, j: (0, 0)), st,
                  pl.BlockSpec(seg.shape, lambda i, j: (0, 0)), pl.BlockSpec(eye.shape, lambda i, j: (0, 0))],
        out_specs=[rows(hv), st],
        out_shape=[jax.ShapeDtypeStruct((n, t, hv), F32), jax.ShapeDtypeStruct(s0.shape, F32)],
        scratch_shapes=[pltpu.VMEM((seqs, hk, dv), F32), pltpu.VMEM((seqs, tt, hk), F32),
                        pltpu.VMEM((seqs, tt, hv), F32)],
        compiler_params=_params("arbitrary", "arbitrary"),
        name="gla_scan",
    )(per_seq(q), per_seq(k), per_seq(v), per_seq(r), per_seq(g), wg, gb, ng, s0, seg, eye)
    return o.reshape(n * t, hv), s_new


def _route(logits_t, bias_col):
    n_e, r = logits_t.shape
    per = n_e // N_EXPERT_GROUPS
    scores = _sigmoid(logits_t)
    biased = scores + bias_col
    sub = lax.broadcasted_iota(jnp.int32, (per, r), 0).astype(F32)
    ninf = -jnp.inf
    gs = []
    for g in range(N_EXPERT_GROUPS):
        blk = biased[g * per:(g + 1) * per, :]
        m1 = jnp.max(blk, axis=0, keepdims=True)
        i1 = jnp.min(jnp.where(blk == m1, sub, float(per)), axis=0, keepdims=True)
        m2 = jnp.max(jnp.where(sub == i1, ninf, blk), axis=0, keepdims=True)
        gs.append(m1 + m2)
    cur = jnp.concatenate(gs, axis=0)
    gsub = lax.broadcasted_iota(jnp.int32, (N_EXPERT_GROUPS, r), 0).astype(F32)
    gsel = jnp.zeros((N_EXPERT_GROUPS, r), F32)
    for _ in range(TOPK_GROUPS):
        m = jnp.max(cur, axis=0, keepdims=True)
        i = jnp.min(jnp.where(cur == m, gsub, float(N_EXPERT_GROUPS)), axis=0, keepdims=True)
        hit = gsub == i
        gsel = jnp.where(hit, 1.0, gsel)
        cur = jnp.where(hit, ninf, cur)
    cur = jnp.concatenate(
        [jnp.where(gsel[g:g + 1, :] > 0.5, biased[g * per:(g + 1) * per, :], ninf) for g in range(N_EXPERT_GROUPS)],
        axis=0)
    esub = lax.broadcasted_iota(jnp.int32, (n_e, r), 0).astype(F32)
    idx, wts, hits = [], [], []
    for _ in range(TOP_K):
        m = jnp.max(cur, axis=0, keepdims=True)
        i = jnp.min(jnp.where(cur == m, esub, float(n_e)), axis=0, keepdims=True)
        hit = esub == i
        idx.append(i)
        hits.append(hit)
        wts.append(jnp.sum(jnp.where(hit, scores, 0.0), axis=0, keepdims=True))
        cur = jnp.where(hit, ninf, cur)
    w = jnp.concatenate(wts, axis=0)
    w = w / jnp.sum(w, axis=0, keepdims=True) * ROUTED_SCALE
    return jnp.concatenate(idx, axis=0), w, hits


def _pack_pairs(x):
    w = x.shape[1] // 2
    hi = lax.bitcast_convert_type(x[:, :w].astype(BF16).astype(F32), jnp.uint32)
    lo = lax.bitcast_convert_type(x[:, w:].astype(BF16).astype(F32), jnp.uint32)
    return hi | (lo >> 16)


def _unpack_pairs(pk):
    a = lax.bitcast_convert_type(pk & jnp.uint32(0xFFFF0000), F32)
    b = lax.bitcast_convert_type(pk << 16, F32)
    return jnp.concatenate([a, b], axis=1)


def _store_row_tiles(ref, x):
    rows = x.shape[0]
    pk = _pack_pairs(x)
    sub = pk.shape[1] // 128
    for c in range(sub):
        ref[pl.ds(c, rows, stride=sub), :] = pk[:, c * 128:(c + 1) * 128]


def _load_row_tiles(ref, sub):
    rows = ref.shape[0] // sub
    return _unpack_pairs(jnp.concatenate([ref[pl.ds(c, rows, stride=sub), :] for c in range(sub)], axis=1))


def _out_kernel(x_ref, mod_ref, ma_ref, mb_ref, wo_ref, g_ref, wr_ref, br_ref, eye_ref, triu_ref,
                x1_ref, hp_ref, idx_ref, rank_ref, w3_ref, cnt_ref, run_scr):
    step = pl.program_id(0)

    @pl.when(step == 0)
    def _():
        run_scr[...] = jnp.zeros_like(run_scr)

    half = ma_ref.shape[1]
    y = _dot(ma_ref[...].astype(BF16), wo_ref[0:half, :]) + _dot(mb_ref[...].astype(BF16), wo_ref[half:, :])
    x = x_ref[...]
    s, r, d = x.shape
    x1 = x + mod_ref[:, 2:3, :] * y.reshape(s, r, d)
    x1_ref[...] = x1
    h2 = (_rms(x1) * g_ref[...] * (1.0 + mod_ref[:, 4:5, :]) + mod_ref[:, 3:4, :]).reshape(s * r, d)
    _store_row_tiles(hp_ref, h2)
    idx, w, hits = _route(_dot3_nt(wr_ref[...], h2), br_ref[...])
    idx_ref[0] = idx.astype(jnp.int32)
    sel = jnp.zeros(hits[0].shape, F32)
    for hit in hits:
        sel = jnp.where(hit, 1.0, sel)
    before = run_scr[:, 0:1] + _dot(sel.astype(BF16), triu_ref[...]) - sel
    rank_ref[0] = jnp.concatenate(
        [jnp.sum(jnp.where(hit, before, 0.0), axis=0, keepdims=True) for hit in hits], axis=0).astype(jnp.int32)
    run_scr[...] = run_scr[...] + jnp.sum(sel, axis=1, keepdims=True)
    cnt_ref[...] = run_scr[...]
    wh, wm, wl = _split3(w)
    stack = jnp.concatenate([wh, wm, wl, jnp.zeros((128 - 3 * TOP_K, s * r), BF16)], axis=0)
    w3_ref[...] = _dot_nt(eye_ref[...], stack)


def _out_route(x, mod, mix_a, mix_b, wo, g, wr_t, br, eye, triu):
    n, t, d = x.shape
    rt = _Rows(n, t)
    m = n * t
    n_e = wr_t.shape[0]
    slot = pl.BlockSpec((1, TOP_K, rt.rows), lambda i: (i, 0, 0))
    return pl.pallas_call(
        _out_kernel,
        grid=(rt.steps,),
        in_specs=[rt.seq_spec(t, d), rt.seq_spec(6, d), rt.row_spec(mix_a.shape[1]), rt.row_spec(mix_b.shape[1]),
                  _const_spec(wo.shape), _const_spec((1, d)), _const_spec(wr_t.shape), _const_spec(br.shape),
                  _const_spec(eye.shape), _const_spec(triu.shape)],
        out_specs=[rt.seq_spec(t, d), rt.tile_spec(d // 256), slot, slot, rt.row_spec(128), _const_spec((n_e, 128))],
        out_shape=[jax.ShapeDtypeStruct((n, t, d), F32), jax.ShapeDtypeStruct((m * (d // 256), 128), jnp.uint32),
                   jax.ShapeDtypeStruct((rt.steps, TOP_K, rt.rows), jnp.int32),
                   jax.ShapeDtypeStruct((rt.steps, TOP_K, rt.rows), jnp.int32),
                   jax.ShapeDtypeStruct((m, 128), F32), jax.ShapeDtypeStruct((n_e, 128), F32)],
        scratch_shapes=[pltpu.VMEM((n_e, 128), F32)],
        compiler_params=_params("arbitrary"),
        name="out_proj_route",
    )(x, mod, mix_a, mix_b, wo, g, wr_t, br, eye, triu)


ROW_SUB = 4


def _row_copy(src_ref, src_row, dst_ref, dst_row, sem):
    src = src_ref.at[pl.ds(pl.multiple_of(src_row * ROW_SUB, ROW_SUB), ROW_SUB), :]
    dst = dst_ref.at[pl.ds(pl.multiple_of(dst_row * ROW_SUB, ROW_SUB), ROW_SUB), :]
    return pltpu.make_async_copy(src, dst, sem)


def _dispatch_kernel(dest_ref, hp_ref, *rest):
    xs_ref, sem = rest[-2:]
    rows = hp_ref.shape[0] // ROW_SUB

    def issue(tok, carry):
        for k in range(TOP_K):
            _row_copy(hp_ref, tok, xs_ref, dest_ref[0, k, tok], sem).start(priority=k % 2)
        return carry

    lax.fori_loop(0, rows, issue, 0, unroll=2)

    def drain(j, carry):
        _row_copy(hp_ref, 0, xs_ref, 0, sem).wait()
        return carry

    lax.fori_loop(0, TOP_K * rows, drain, 0, unroll=16)


def _dispatch(dest3, hp, n_rows, xs_prev=None):
    steps, _, rows = dest3.shape
    assert rows & (rows - 1) == 0
    chained = xs_prev is not None
    return pl.pallas_call(
        _dispatch_kernel,
        grid=(steps,),
        in_specs=[pl.BlockSpec((1, TOP_K, rows), lambda i: (i, 0, 0), memory_space=pltpu.SMEM),
                  pl.BlockSpec((rows * ROW_SUB, 128), lambda i: (i, 0))]
                 + ([pl.BlockSpec(memory_space=pl.ANY)] if chained else []),
        out_specs=pl.BlockSpec(memory_space=pl.ANY),
        out_shape=jax.ShapeDtypeStruct((n_rows * ROW_SUB, 128), jnp.uint32),
        scratch_shapes=[pltpu.SemaphoreType.DMA(())],
        input_output_aliases={2: 0} if chained else {},
        compiler_params=_params("arbitrary"),
        name="moe_dispatch_rows",
    )(dest3, hp, *([xs_prev] if chained else []))


def _gmm_kernel(be_ref, va_ref, nu_ref, x_ref, wg_ref, wu_ref, wd_ref, y_ref):
    i = pl.program_id(0)

    @pl.when(i < nu_ref[0])
    def _():
        x = _load_row_tiles(x_ref, ROW_SUB)
        live = lax.broadcasted_iota(jnp.int32, x.shape, 0) < va_ref[i]
        x = jnp.where(live, x, 0.0).astype(BF16)
        a = _silu(_dot(x, wg_ref[...].astype(BF16))) * _dot(x, wu_ref[...].astype(BF16))
        _store_row_tiles(y_ref, _dot(a.astype(BF16), wd_ref[...].astype(BF16)))

    @pl.when(i >= nu_ref[0])
    def _():
        y_ref[...] = jnp.zeros_like(y_ref)


def _gmm(layer, block_e, valid, n_used, xs, wg, wu, wd, tm):
    r = xs.shape[0] // ROW_SUB
    d, de = wg.shape[-2:]
    assert d == ROW_SUB * 256
    nb = r // tm
    grid_spec = pltpu.PrefetchScalarGridSpec(
        num_scalar_prefetch=3,
        grid=(nb,),
        in_specs=[pl.BlockSpec((tm * ROW_SUB, 128), lambda i, be, va, nu: (i, 0)),
                  pl.BlockSpec((None, None, d, de), lambda i, be, va, nu: (layer, be[i], 0, 0)),
                  pl.BlockSpec((None, None, d, de), lambda i, be, va, nu: (layer, be[i], 0, 0)),
                  pl.BlockSpec((None, None, de, d), lambda i, be, va, nu: (layer, be[i], 0, 0))],
        out_specs=pl.BlockSpec((tm * ROW_SUB, 128), lambda i, be, va, nu: (i, 0)),
    )
    return pl.pallas_call(
        _gmm_kernel,
        grid_spec=grid_spec,
        out_shape=jax.ShapeDtypeStruct(xs.shape, jnp.uint32),
        compiler_params=_params("arbitrary"),
        name="moe_grouped_swiglu",
    )(block_e, valid, n_used, xs, wg, wu, wd)


def _combine_kernel(dest_ref, x_ref, mod_ref, hp_ref, w3_ref, wg_ref, wu_ref, wd_ref, ys_ref, o_ref, yg_scr, sem):
    rows = hp_ref.shape[0] // ROW_SUB

    def issue(tok, carry):
        for k in range(TOP_K):
            _row_copy(ys_ref, dest_ref[0, k, tok], yg_scr.at[k], tok, sem).start(priority=k % 2)
        return carry

    lax.fori_loop(0, rows, issue, 0, unroll=2)
    h = _load_row_tiles(hp_ref, ROW_SUB).astype(BF16)
    a = _silu(_dot(h, wg_ref[...])) * _dot(h, wu_ref[...])
    acc = _dot(a.astype(BF16), wd_ref[...])

    def drain(j, carry):
        _row_copy(ys_ref, 0, yg_scr.at[0], 0, sem).wait()
        return carry

    lax.fori_loop(0, TOP_K * rows, drain, 0, unroll=16)
    w3 = w3_ref[...]
    for k in range(TOP_K):
        wk = w3[:, k:k + 1] + w3[:, TOP_K + k:TOP_K + k + 1] + w3[:, 2 * TOP_K + k:2 * TOP_K + k + 1]
        acc = acc + wk * _load_row_tiles(yg_scr.at[k], ROW_SUB)
    x = x_ref[...]
    o_ref[...] = x + mod_ref[:, 5:6, :] * acc.reshape(x.shape)


def _combine(dest3, x1, mod, hp, w3, wg, wu, wd, ys):
    n, t, d = x1.shape
    rt = _Rows(n, t)
    assert rt.rows & (rt.rows - 1) == 0
    return pl.pallas_call(
        _combine_kernel,
        grid=(rt.steps,),
        in_specs=[pl.BlockSpec((1, TOP_K, rt.rows), lambda i: (i, 0, 0), memory_space=pltpu.SMEM),
                  rt.seq_spec(t, d), rt.seq_spec(6, d), rt.tile_spec(ROW_SUB), rt.row_spec(128),
                  _const_spec(wg.shape), _const_spec(wu.shape), _const_spec(wd.shape),
                  pl.BlockSpec(memory_space=pl.ANY)],
        out_specs=rt.seq_spec(t, d),
        out_shape=jax.ShapeDtypeStruct((n, t, d), F32),
        scratch_shapes=[pltpu.VMEM((TOP_K, rt.rows * ROW_SUB, 128), jnp.uint32), pltpu.SemaphoreType.DMA(())],
        compiler_params=_params("arbitrary"),
        name="moe_combine_shared",
    )(dest3, x1, mod, hp, w3, wg, wu, wd, ys)


def _moe(layer, groups, p):
    n_e = p['w_gate_e'].shape[1]
    a = sum(g[1][2].size for g in groups)
    tm = 1024 if a // n_e >= 2048 else 128
    counts = [g[1][5][:, 0].astype(jnp.int32) for g in groups]
    total = sum(counts)
    padded = (total + tm - 1) // tm * tm
    pend = jnp.cumsum(padded)
    pstart = pend - padded
    experts = jnp.arange(n_e, dtype=jnp.int32)
    dests = []
    base = pstart
    for (_, routed), cnt in zip(groups, counts):
        onehot = routed[2][None] == experts[:, None, None, None]
        dests.append(routed[3] + jnp.sum(jnp.where(onehot, base[:, None, None, None], 0), axis=0))
        base = base + cnt
    n_blocks = -(-a // tm) + n_e
    first = jnp.arange(n_blocks, dtype=jnp.int32) * tm
    block_e = jnp.minimum(jnp.sum((pend[None, :] <= first[:, None]).astype(jnp.int32), axis=1), n_e - 1)
    mine = block_e[:, None] == experts[None, :]
    last = jnp.sum(jnp.where(mine, (pstart + total)[None, :], 0), axis=1)
    valid = jnp.clip(last - first, 0, tm).astype(jnp.int32)
    n_used = (pend[-1] // tm).astype(jnp.int32).reshape(1)
    xs = None
    for (_, routed), dest3 in zip(groups, dests):
        xs = _dispatch(dest3, routed[1], n_blocks * tm, xs)
    ys = _gmm(layer, block_e, valid, n_used, xs, p['w_gate_e'], p['w_up_e'], p['w_down_e'], tm)
    shared = [p[name][layer].astype(BF16) for name in ('w_gate_s', 'w_up_s', 'w_down_s')]
    return [_combine(dest3, routed[0], mod, routed[1], routed[4], *shared, ys)
            for (mod, routed), dest3 in zip(groups, dests)]


def _odd_in_kernel(x_ref, mod_ref, g_ref, w_ref, gqa_ref, wqb_ref, gqn_ref, gqr_ref, gqrs_ref, gkva_ref,
                   gkr_ref, gkrs_ref, gkn_ref, cos_ref, sin_ref, seg_ref, wuk_ref, wuv_ref,
                   cq_ref, ck_ref, cv_ref, co_ref, gt_ref, ckv_ref, kr_ref, *outs, prompt):
    h = _norm_mod(x_ref, mod_ref, g_ref, 0, 1).astype(BF16)
    z = _dot(h, w_ref[...])
    cq_ref[...] = z[:, 0:512]
    ck_ref[...] = z[:, 512:1024]
    cv_ref[...] = z[:, 1024:1536]
    co_ref[...] = z[:, 1536:2048]
    gt_ref[...] = z[:, 2560:2688]
    cos = cos_ref[...]
    sin = sin_ref[...]
    ckv = _rms(z[:, 2304:2432]) * gkva_ref[...]
    ckv_ref[...] = ckv
    xr = z[:, 2432:2496]
    xr_rot = z[:, 2496:2560]
    rr = lax.rsqrt(jnp.mean(xr * xr, axis=-1, keepdims=True) + EPS)
    kr = rr * (xr * gkr_ref[...] * cos[:, 0:D_ROPE] + xr_rot * gkrs_ref[...] * sin[:, 0:D_ROPE])
    kr_ref[...] = kr
    qa = (_rms(z[:, 2048:2304]) * gqa_ref[...]).astype(BF16)
    qd = _dot(qa, wqb_ref[...])
    nw = D_HEADS * D_NOPE
    rw = D_HEADS * D_ROPE
    qn = [_rms(qd[:, hh * D_NOPE:(hh + 1) * D_NOPE]) * gqn_ref[...] * MLA_SCALE for hh in range(D_HEADS)]
    xq = qd[:, nw:nw + rw]
    xq_rot = qd[:, nw + rw:nw + 2 * rw]
    rq = lax.rsqrt(_dot_sel(xq * xq, seg_ref[...]) * (1.0 / D_ROPE) + EPS)
    qr = rq * (xq * gqr_ref[...] * cos + xq_rot * gqrs_ref[...] * sin) * MLA_SCALE
    ckv_b = ckv.astype(BF16)
    if prompt:
        qcat_ref, kcat_ref, v_ref = outs
        kn = _dot(ckv_b, wuk_ref[...])
        zpad = jnp.zeros((z.shape[0], 256 - D_NOPE - D_ROPE), F32)
        qcat_ref[...] = jnp.concatenate(
            [piece for hh in range(D_HEADS) for piece in (qn[hh], qr[:, hh * D_ROPE:(hh + 1) * D_ROPE], zpad)],
            axis=1).astype(BF16)
        kcat_ref[...] = jnp.concatenate(
            [piece for hh in range(D_HEADS)
             for piece in (_rms(kn[:, hh * D_NOPE:(hh + 1) * D_NOPE]) * gkn_ref[...], kr, zpad)],
            axis=1).astype(BF16)
        v_ref[...] = _dot(ckv_b, wuv_ref[...]).astype(BF16)
    else:
        u_ref, qr_ref = outs
        u_ref[...] = jnp.concatenate(
            [_dot_nt((qn[hh] * gkn_ref[...]).astype(BF16), wuk_ref[:, hh * D_NOPE:(hh + 1) * D_NOPE])
             for hh in range(D_HEADS)], axis=1).astype(BF16)
        qr_ref[...] = qr.astype(BF16)


def _odd_in(x, mod, g, w, consts, cos, sin, prompt):
    n, t, d = x.shape
    rt = _Rows(n, t)
    m = n * t
    widths = [(512, F32)] * 4 + [(128, F32), (128, F32), (D_ROPE, F32)]
    if prompt:
        widths += [(1024, BF16), (1024, BF16), (512, BF16)]
    else:
        widths += [(512, BF16), (256, BF16)]
    return pl.pallas_call(
        functools.partial(_odd_in_kernel, prompt=prompt),
        grid=(rt.steps,),
        in_specs=[rt.seq_spec(t, d), rt.seq_spec(6, d), _const_spec((1, d)), _const_spec(w.shape)]
                 + [_const_spec(c.shape) for c in consts[:9]]
                 + [rt.pos_spec(cos.shape[1]), rt.pos_spec(sin.shape[1])]
                 + [_const_spec(c.shape) for c in consts[9:]],
        out_specs=[rt.row_spec(c) for c, _ in widths],
        out_shape=[jax.ShapeDtypeStruct((m, c), dt) for c, dt in widths],
        compiler_params=_params("arbitrary"),
        name="odd_in_latent_prep",
    )(x, mod, g, w, *consts[:9], cos, sin, *consts[9:])


def _mlstm_kernel(q_ref, k_ref, v_ref, og_ref, gt_ref, gb_ref, ng_ref, c0_ref, n0_ref, m0_ref, tri_ref, eye_ref,
                  o_ref, co_ref, no_ref, mo_ref, c_scr, n_scr, m_scr, *, chunk):
    t = pl.program_id(1)

    @pl.when(t == 0)
    def _():
        c_scr[...] = c0_ref[...]
        n_scr[...] = n0_ref[...]
        m_scr[...] = m0_ref[...]

    for sq in range(q_ref.shape[0]):
        _mlstm_chunk(q_ref.at[sq], k_ref.at[sq], v_ref.at[sq], og_ref.at[sq], gt_ref.at[sq], gb_ref, ng_ref,
                     tri_ref, eye_ref, o_ref.at[sq], c_scr.at[sq], n_scr.at[sq], m_scr.at[sq], chunk)

    @pl.when(t == pl.num_programs(1) - 1)
    def _():
        co_ref[...] = c_scr[...]
        no_ref[...] = n_scr[...]
        mo_ref[...] = m_scr[...]


def _mlstm_chunk(q_ref, k_ref, v_ref, og_ref, gt_ref, gb_ref, ng_ref, tri_ref, eye_ref,
                 o_ref, c_scr, n_scr, m_scr, chunk):
    dh = q_ref.shape[1] // C_HEADS
    pre = gt_ref[...] + gb_ref[...]
    lf = _log_sigmoid(pre)
    if chunk >= 16:
        l_hi, l_mid, l_lo = _split3(lf)
        f_cum = _dot(tri_ref[...], l_hi) + _dot(tri_ref[...], l_mid) + _dot(tri_ref[...], l_lo)
    else:
        rowg = lax.broadcasted_iota(jnp.int32, lf.shape, 0)
        f_cum = jnp.zeros_like(lf)
        for s in range(chunk):
            f_cum = f_cum + jnp.where(rowg >= s, lf[s:s + 1, :], 0.0)
    ri = lax.broadcasted_iota(jnp.int32, (chunk, chunk), 0)
    ci = lax.broadcasted_iota(jnp.int32, (chunk, chunk), 1)
    pad = 16 - chunk if chunk < 16 else 0

    def pad_rows(a):
        if pad:
            return jnp.concatenate([a, jnp.zeros((pad, a.shape[1]), a.dtype)], axis=0)
        return a

    for h in range(C_HEADS):
        sl = slice(h * dh, (h + 1) * dh)
        q = q_ref[:, sl]
        k = k_ref[:, sl] * (dh ** -0.5)
        v = v_ref[:, sl]
        qb, kb, vb = q.astype(BF16), k.astype(BF16), v.astype(BF16)
        f_col = f_cum[:, C_HEADS + h:C_HEADS + h + 1]
        i_col = pre[:, h:h + 1]
        m_prev = m_scr[h:h + 1, 0:1]
        a_row = jnp.sum(jnp.where(ri == ci, i_col - f_col, 0.0), axis=0, keepdims=True)
        log_d = jnp.where(ci <= ri, f_col + a_row, NEG)
        inter = f_col + m_prev
        m_t = jnp.maximum(inter, jnp.max(log_d, axis=-1, keepdims=True))
        w_inter = jnp.exp(inter - m_t)
        qk = _dot_nt(qb, kb) * jnp.exp(log_d - m_t)
        c_old = c_scr[h]
        n_old = n_scr[h:h + 1, :]
        num = _dot(qk.astype(BF16), vb) + w_inter * _dot_nt(qb, c_old.astype(BF16))
        den = jnp.sum(qk, axis=-1, keepdims=True) + w_inter * jnp.sum(q * n_old, axis=-1, keepdims=True)
        hh = num / jnp.maximum(jnp.abs(den), jnp.exp(-m_t))
        o_ref[:, sl] = _rms(hh) * ng_ref[...] * _sigmoid(og_ref[:, sl])
        f_last = f_col[chunk - 1:chunk, :]
        a_end = f_last - f_col + i_col
        m_new = jnp.maximum(f_last + m_prev, jnp.max(a_end, axis=0, keepdims=True))
        w = jnp.exp(a_end - m_new)
        dec = jnp.exp(f_last + m_prev - m_new)
        wv_t = _dot_nt(eye_ref[...], pad_rows((w * v).astype(BF16))).astype(BF16)
        c_scr[h] = dec * c_old + _dot(wv_t, pad_rows(kb))
        n_scr[h:h + 1, :] = dec * n_old + jnp.sum(w * k, axis=0, keepdims=True)
        m_scr[h:h + 1, :] = jnp.broadcast_to(m_new, (1, m_scr.shape[1]))


MLSTM_SEQS = 2


def _mlstm(n, t, q, k, v, og, gt, gb, ng, c0, n0, m0, tri, eye):
    chunk = CHUNK_C if t % CHUNK_C == 0 else t
    nt = t // chunk
    w = q.shape[1]
    dh = w // C_HEADS
    seqs = MLSTM_SEQS
    assert n % seqs == 0

    def rows(c):
        return pl.BlockSpec((seqs, chunk, c), lambda i, j: (i, j, 0))

    def per_seq(a):
        return a.reshape(n, t, a.shape[1])

    cst = pl.BlockSpec((seqs, C_HEADS, dh, dh), lambda i, j: (i, 0, 0, 0))
    nst = pl.BlockSpec((seqs, C_HEADS, dh), lambda i, j: (i, 0, 0))
    o, c_new, n_new, m_new = pl.pallas_call(
        functools.partial(_mlstm_kernel, chunk=chunk),
        grid=(n // seqs, nt),
        in_specs=[rows(w), rows(w), rows(w), rows(w), rows(128),
                  pl.BlockSpec(gb.shape, lambda i, j: (0, 0)), pl.BlockSpec(ng.shape, lambda i, j: (0, 0)),
                  cst, nst, nst,
                  pl.BlockSpec(tri.shape, lambda i, j: (0, 0)), pl.BlockSpec(eye.shape, lambda i, j: (0, 0))],
        out_specs=[rows(w), cst, nst, nst],
        out_shape=[jax.ShapeDtypeStruct((n, t, w), F32), jax.ShapeDtypeStruct(c0.shape, F32),
                   jax.ShapeDtypeStruct(n0.shape, F32), jax.ShapeDtypeStruct(m0.shape, F32)],
        scratch_shapes=[pltpu.VMEM((seqs, C_HEADS, dh, dh), F32), pltpu.VMEM((seqs, C_HEADS, dh), F32),
                        pltpu.VMEM((seqs, C_HEADS, dh), F32)],
        compiler_params=_params("arbitrary", "arbitrary"),
        name="mlstm_scan",
    )(per_seq(q), per_seq(k), per_seq(v), per_seq(og), per_seq(gt), gb, ng, c0, n0, m0, tri, eye)
    return o.reshape(n * t, w), c_new, n_new, m_new


def _flash_kernel(qi_ref, kj_ref, q_ref, k_ref, v_ref, o_ref, m_scr, l_scr, acc_scr):
    p = pl.program_id(2)
    qi = qi_ref[p]
    kj = kj_ref[p]

    @pl.when(kj == 0)
    def _():
        m_scr[...] = jnp.full_like(m_scr, NEG)
        l_scr[...] = jnp.zeros_like(l_scr)
        acc_scr[...] = jnp.zeros_like(acc_scr)

    def step(masked):
        bq = q_ref.shape[0]
        hq = bq // 2 if bq % 16 == 0 else bq
        for q0 in range(0, bq, hq):
            rows = slice(q0, q0 + hq)
            s = _dot_nt(q_ref[rows, :], k_ref[...])
            bk = s.shape[1]
            if masked:
                ri = lax.broadcasted_iota(jnp.int32, (hq, bk), 0) + q0
                ci = lax.broadcasted_iota(jnp.int32, (hq, bk), 1)
                s = jnp.where(ci <= ri, s, NEG)
            lanes = m_scr.shape[1]
            m_old = m_scr[rows, :]
            m_new = jnp.maximum(m_old, jnp.max(s, axis=-1, keepdims=True))
            alpha = jnp.exp(m_old - m_new)
            pr = jnp.exp(s - jnp.concatenate([m_new] * (bk // lanes), axis=1))
            l_scr[rows, :] = alpha * l_scr[rows, :] + jnp.sum(pr, axis=-1, keepdims=True)
            acc_scr[rows, :] = alpha * acc_scr[rows, :] + _dot(pr.astype(BF16), v_ref[...])
            m_scr[rows, :] = m_new

    @pl.when(kj < qi)
    def _():
        step(False)

    @pl.when(kj == qi)
    def _():
        step(True)
        o_ref[...] = acc_scr[...] / l_scr[...]


def _flash(n, t, qcat, kcat, v):
    blk = FLASH_BLOCK if t % FLASH_BLOCK == 0 else t
    nq = t // blk
    pairs = [(i, j) for i in range(nq) for j in range(i + 1)]
    qi = jnp.asarray([a for a, _ in pairs], jnp.int32)
    kj = jnp.asarray([b for _, b in pairs], jnp.int32)
    dv = v.shape[1] // D_HEADS
    grid_spec = pltpu.PrefetchScalarGridSpec(
        num_scalar_prefetch=2,
        grid=(n, D_HEADS, len(pairs)),
        in_specs=[pl.BlockSpec((blk, 256), lambda b, h, p, qi, kj: (b * nq + qi[p], h)),
                  pl.BlockSpec((blk, 256), lambda b, h, p, qi, kj: (b * nq + kj[p], h)),
                  pl.BlockSpec((blk, dv), lambda b, h, p, qi, kj: (b * nq + kj[p], h))],
        out_specs=pl.BlockSpec((blk, dv), lambda b, h, p, qi, kj: (b * nq + qi[p], h)),
        scratch_shapes=[pltpu.VMEM((blk, dv), F32), pltpu.VMEM((blk, dv), F32), pltpu.VMEM((blk, dv), F32)],
    )
    assert blk % dv == 0
    return pl.pallas_call(
        _flash_kernel,
        grid_spec=grid_spec,
        out_shape=jax.ShapeDtypeStruct((n * t, v.shape[1]), F32),
        compiler_params=_params("arbitrary", "arbitrary", "arbitrary"),
        name="mla_prompt_flash",
    )(qi, kj, qcat, kcat, v)


def _paged_kernel(pt_ref, u_ref, qr_ref, cn_ref, kn_ref, wukt_ref, wuv_ref, ckv_hbm, krt_hbm,
                  o_ref, ck_buf, kr_buf, sem, m_scr, l_scr, acc_scr, *, pages, li, n_groups, n_steps):
    b = pl.program_id(0)
    g = pl.program_id(1)
    hq = u_ref.shape[1]
    tq = hq // D_HEADS
    step = b * n_groups + g
    slot = lax.rem(step, 2)

    def page_copies(seq, group, sl, lookup):
        copies = []
        for i in range(pages):
            page = pt_ref[seq, group * pages + i] if lookup else 0
            rows = pl.ds(i * PAGE_SIZE, PAGE_SIZE)
            copies.append(pltpu.make_async_copy(ckv_hbm.at[page, li], ck_buf.at[sl, rows, :], sem.at[sl]))
            copies.append(pltpu.make_async_copy(krt_hbm.at[page, li], kr_buf.at[sl, i], sem.at[sl]))
        return copies

    @pl.when(step == 0)
    def _():
        for cp in page_copies(0, 0, 0, True):
            cp.start()

    @pl.when(step + 1 < n_steps)
    def _():
        nxt = step + 1
        for cp in page_copies(lax.div(nxt, n_groups), lax.rem(nxt, n_groups), 1 - slot, True):
            cp.start()

    for cp in page_copies(0, 0, slot, False):
        cp.wait()

    @pl.when(g == 0)
    def _():
        m_scr[...] = jnp.full_like(m_scr, NEG)
        l_scr[...] = jnp.zeros_like(l_scr)
        acc_scr[...] = jnp.zeros_like(acc_scr)

    lhs = jnp.concatenate([wukt_ref[...], u_ref[0]], axis=0)
    qr = qr_ref[0]
    nk = D_HEADS * D_NOPE

    def scores(ck_b, kr_b, kr_keys_minor=True):
        big = _dot_nt(lhs, ck_b)
        rows = []
        for h in range(D_HEADS):
            kn_t = big[h * D_NOPE:(h + 1) * D_NOPE, :]
            rinv = lax.rsqrt(jnp.sum(kn_t * kn_t, axis=0, keepdims=True) * (1.0 / D_NOPE) + EPS)
            rows.append(big[nk + h * tq:nk + (h + 1) * tq, :] * rinv)
        rope = _dot(qr, kr_b) if kr_keys_minor else _dot_nt(qr, kr_b)
        return jnp.concatenate(rows, axis=0) + rope

    def update(s, ck_b):
        m_old = m_scr[...]
        m_new = jnp.maximum(m_old, jnp.max(s, axis=-1, keepdims=True))
        alpha = jnp.exp(m_old - m_new)
        pr = jnp.exp(s - m_new)
        l_scr[...] = alpha * l_scr[...] + jnp.sum(pr, axis=-1, keepdims=True)
        acc_scr[...] = alpha * acc_scr[...] + _dot(pr.astype(BF16), ck_b)
        m_scr[...] = m_new

    span = 2 * PAGE_SIZE
    cks = [ck_buf[slot, pl.ds(i * span, span), :].astype(BF16) for i in range(pages // 2)]
    krs = [jnp.concatenate([kr_buf[slot, 2 * i], kr_buf[slot, 2 * i + 1]], axis=1).astype(BF16)
           for i in range(pages // 2)]
    s_all = jnp.concatenate([scores(cks[i], krs[i]) for i in range(pages // 2)], axis=1)
    update(s_all, jnp.concatenate(cks, axis=0))

    @pl.when(g == pl.num_programs(1) - 1)
    def _():
        fill = PAGE_SIZE - tq
        ck_b = jnp.concatenate([cn_ref[...], jnp.zeros((fill, cn_ref.shape[1]), F32)], axis=0).astype(BF16)
        kr_b = jnp.concatenate([kn_ref[...], jnp.zeros((fill, kn_ref.shape[1]), F32)], axis=0).astype(BF16)
        s = scores(ck_b, kr_b, kr_keys_minor=False)
        ri = lax.broadcasted_iota(jnp.int32, s.shape, 0)
        ci = lax.broadcasted_iota(jnp.int32, s.shape, 1)
        update(jnp.where(ci <= ri % tq, s, NEG), ck_b)
        lat = (acc_scr[...] / l_scr[...]).astype(BF16)
        full = _dot(lat, wuv_ref[...])
        dv = wuv_ref.shape[1] // D_HEADS
        o_ref[...] = jnp.concatenate(
            [full[h * tq:(h + 1) * tq, h * dv:(h + 1) * dv] for h in range(D_HEADS)], axis=1)


def _paged(page_table, li, u3, qr3, ckv_new, kr_new, wuk_t, wuv, cache_ckv, cache_kr):
    n, hq, lat = u3.shape
    tq = hq // D_HEADS
    n_pages = page_table.shape[1]
    pages = min(PAGES_PER_STEP, n_pages)
    assert n_pages % pages == 0 and pages % 2 == 0
    ng = n_pages // pages
    cache_kr_t = jnp.swapaxes(cache_kr, 2, 3)
    keys = pages * PAGE_SIZE

    grid_spec = pltpu.PrefetchScalarGridSpec(
        num_scalar_prefetch=1,
        grid=(n, ng),
        in_specs=[pl.BlockSpec((1, hq, lat), lambda b, g, pt: (b, 0, 0)),
                  pl.BlockSpec((1, hq, D_ROPE), lambda b, g, pt: (b, 0, 0)),
                  pl.BlockSpec((tq, lat), lambda b, g, pt: (b, 0)),
                  pl.BlockSpec((tq, D_ROPE), lambda b, g, pt: (b, 0)),
                  pl.BlockSpec(wuk_t.shape, lambda b, g, pt: (0, 0)),
                  pl.BlockSpec(wuv.shape, lambda b, g, pt: (0, 0)),
                  pl.BlockSpec(memory_space=pl.ANY), pl.BlockSpec(memory_space=pl.ANY)],
        out_specs=pl.BlockSpec((tq, wuv.shape[1]), lambda b, g, pt: (b, 0)),
        scratch_shapes=[pltpu.VMEM((2, keys, lat), F32), pltpu.VMEM((2, pages, D_ROPE, PAGE_SIZE), F32),
                        pltpu.SemaphoreType.DMA((2,)),
                        pltpu.VMEM((hq, 1), F32), pltpu.VMEM((hq, 1), F32), pltpu.VMEM((hq, lat), F32)],
    )
    return pl.pallas_call(
        functools.partial(_paged_kernel, pages=pages, li=li, n_groups=ng, n_steps=n * ng),
        grid_spec=grid_spec,
        out_shape=jax.ShapeDtypeStruct((n * tq, wuv.shape[1]), F32),
        compiler_params=_params("arbitrary", "arbitrary"),
        name="mla_sample_paged",
    )(page_table, u3, qr3, ckv_new, kr_new, wuk_t, wuv, cache_ckv, cache_kr_t)


def _np_seg(n_seg, seg_in, seg_out):
    mat = np.zeros((n_seg * seg_in, n_seg * seg_out), np.float32)
    for s in range(n_seg):
        mat[s * seg_in:(s + 1) * seg_in, s * seg_out:(s + 1) * seg_out] = 1.0
    return mat


def _rot_cols(w):
    half = w.shape[-1] // 2
    return jnp.concatenate([-w[..., half:], w[..., :half]], axis=-1)


def _swap_halves(g):
    half = g.shape[-1] // 2
    return jnp.concatenate([g[..., half:], g[..., :half]], axis=-1)


def _rope_tables(pos, reps):
    half = D_ROPE // 2
    inv = ROPE_BASE ** (-jnp.arange(half, dtype=F32) / half)
    ang = pos.astype(F32)[:, None] * inv
    cos = jnp.concatenate([jnp.cos(ang), jnp.cos(ang)], axis=-1)
    sin = jnp.concatenate([jnp.sin(ang), jnp.sin(ang)], axis=-1)
    return jnp.tile(cos, (1, reps)), jnp.tile(sin, (1, reps))


def _trunk(x, c_mod, pos0, gla0, mc0, mn0, mm0, p, sample_ctx):
    n, t, d = x.shape
    m = n * t
    rt = _Rows(n, t)
    eye_r = jnp.eye(rt.rows, dtype=BF16)
    triu_r = jnp.asarray(np.triu(np.ones((rt.rows, rt.rows), np.float32)), BF16)
    dh = mc0.shape[3]
    eye_hk = jnp.eye(B_HEADS * gla0.shape[3], dtype=BF16)
    eye_dh = jnp.eye(dh, dtype=BF16)
    results = {}

    layer, li = 0, 0
    mod = c_mod[layer]
    w_in = p['w_in_even'][li]
    w_in = jnp.concatenate([w_in, jnp.zeros((d, 128 - B_GATE_RANK), F32)], axis=1).astype(BF16)
    a_ws = jnp.tril(p['a_ws'][li])
    a_bs = p['a_bs'][li]
    if t % CHUNK_A == 0:
        ws = a_ws
        bs = jnp.repeat(a_bs.T, CHUNK_A, axis=1)
    else:
        ws = jnp.stack([jnp.kron(jnp.eye(CHUNK_A // t, dtype=F32), a_ws[g, :t, :t]) for g in range(A_GROUPS)])
        bs = jnp.repeat(jnp.tile(a_bs[:, :t], (1, CHUNK_A // t)).T, CHUNK_A, axis=1)
    out_a, v_rows, bq, bk, bv, br, bg = _even_in(
        x, mod, p['norm_mix_g'][layer].reshape(1, d), w_in, p['a_norm_g'][li].reshape(1, -1),
        ws.astype(BF16), bs)
    hk = bq.shape[1]
    dv = bv.shape[1] // B_HEADS
    wg2 = jnp.concatenate([p['b_w_gate2'][li], jnp.zeros((128 - B_GATE_RANK, hk), F32)], axis=0)
    seg = jnp.asarray(_np_seg(B_HEADS, hk // B_HEADS, dv), BF16)
    out_b, s_new = _gla(n, t, bq, bk, bv, br, bg, wg2, p['b_gate_bias'][li].reshape(1, hk),
                        jnp.tile(p['b_norm_g'][li], B_HEADS).reshape(1, -1), gla0[li], seg, eye_hk)
    results['gla'] = s_new
    results['v_rows'] = v_rows
    routed = _out_route(
        x, mod, out_a, out_b, p['w_out'][layer].astype(BF16), p['norm_ffn_g'][layer].reshape(1, d),
        p['w_router'][layer].T, p['b_router'][layer].reshape(-1, 1), eye_r, triu_r)
    x = yield mod, routed

    layer, li = 1, 0
    mod = c_mod[layer]
    w = p['w_in_odd'][li]
    hw = C_HEADS * dh
    o_g = 3 * hw
    o_o = o_g + 2 * C_HEADS
    o_qa = o_o + hw
    o_kva = o_qa + p['d_g_qa'].shape[1]
    o_kr = o_kva + p['d_g_kva'].shape[1]
    w_kr = w[:, o_kr:o_kr + D_ROPE]
    w_odd = jnp.concatenate(
        [w[:, :o_g], w[:, o_o:o_qa], w[:, o_qa:o_kva], w[:, o_kva:o_kr], w_kr, _rot_cols(w_kr),
         w[:, o_g:o_o], jnp.zeros((d, 128 - 2 * C_HEADS), F32)], axis=1).astype(BF16)
    wqb = p['d_w_qb'][li].reshape(-1, D_HEADS, D_NOPE + D_ROPE)
    wqb_r = wqb[:, :, D_NOPE:]
    wqb2 = jnp.concatenate([wqb[:, :, :D_NOPE].reshape(-1, D_HEADS * D_NOPE),
                            wqb_r.reshape(-1, D_HEADS * D_ROPE),
                            _rot_cols(wqb_r).reshape(-1, D_HEADS * D_ROPE)], axis=1).astype(BF16)
    w_uk = p['d_w_uk'][li]
    lat = w_uk.shape[0]
    g_qr = p['d_g_qr'][li]
    g_kr = p['d_g_kr'][li]
    cos, sin = _rope_tables(pos0 + jnp.arange(t), D_HEADS)
    prompt = sample_ctx is None
    if not prompt:
        cos, sin = jnp.tile(cos, (rt.s, 1)), jnp.tile(sin, (rt.s, 1))
    consts = [p['d_g_qa'][li].reshape(1, -1), wqb2, p['d_g_qn'][li].reshape(1, -1),
              jnp.tile(g_qr, D_HEADS).reshape(1, -1), jnp.tile(_swap_halves(g_qr), D_HEADS).reshape(1, -1),
              p['d_g_kva'][li].reshape(1, -1), g_kr.reshape(1, -1), _swap_halves(g_kr).reshape(1, -1),
              p['d_g_kn'][li].reshape(1, -1),
              jnp.asarray(_np_seg(D_HEADS, D_ROPE, D_ROPE), BF16),
              w_uk.reshape(lat, -1).astype(BF16), p['d_w_uv'][li].reshape(lat, -1).astype(BF16)]
    outs = _odd_in(x, mod, p['norm_mix_g'][layer].reshape(1, d), w_odd, consts, cos, sin, prompt)
    cq, ck, cv, co, gates, ckv, kr = outs[:7]
    gb = jnp.concatenate([p['c_ig_bias'][li], p['c_fg_bias'][li],
                          jnp.zeros((128 - 2 * C_HEADS,), F32)]).reshape(1, 128)
    chunk = CHUNK_C if t % CHUNK_C == 0 else t
    tri = jnp.asarray(np.tril(np.ones((chunk, chunk), np.float32)), BF16)
    m0b =jnp.broadcast_to(mm0[li][:, :, None], (n, C_HEADS, dh))
    out_c, c_new, n_new, m_new = _mlstm(n, t, cq, ck, cv, co, gates, gb,
                                        p['c_norm_g'][li].reshape(1, -1), mc0[li], mn0[li], m0b, tri, eye_dh)
    results['mlstm'] = (c_new, n_new, m_new[:, :, 0])
    results['ckv'] = ckv.reshape(n, t, -1)
    results['kr'] = kr.reshape(n, t, -1)
    if prompt:
        qcat, kcat, vv = outs[7:]
        out_d = _flash(n, t, qcat, kcat, vv)
    else:
        u, qr = outs[7:]
        cache_ckv, cache_kr, page_table = sample_ctx
        u3 = u.reshape(n, t, D_HEADS, lat).transpose(0, 2, 1, 3).reshape(n, D_HEADS * t, lat)
        qr3 = qr.reshape(n, t, D_HEADS, D_ROPE).transpose(0, 2, 1, 3).reshape(n, D_HEADS * t, D_ROPE)
        wuk_t = w_uk.transpose(1, 2, 0).reshape(-1, lat).astype(BF16)
        out_d = _paged(page_table, li, u3, qr3, ckv, kr, wuk_t, consts[11], cache_ckv, cache_kr)
    routed = _out_route(
        x, mod, out_c, out_d, p['w_out'][layer].astype(BF16), p['norm_ffn_g'][layer].reshape(1, d),
        p['w_router'][layer].T, p['b_router'][layer].reshape(-1, 1), eye_r, triu_r)
    x = yield mod, routed
    return x, results


def kernel(x_prompt, x_sample, state_gla, state_mlstm_c, state_mlstm_n, state_mlstm_m,
           cache_ckv, cache_krope, page_table, c_prompt, c_sample,
           norm_mix_g, norm_ffn_g, w_ada, b_ada, w_out,
           w_in_even, a_norm_g, a_ws, a_bs, b_w_gate2, b_gate_bias, b_norm_g,
           w_in_odd, c_ig_bias, c_fg_bias, c_norm_g,
           d_g_qa, d_w_qb, d_g_kva, d_g_qn, d_g_qr, d_g_kr, d_g_kn, d_w_uk, d_w_uv,
           w_router, b_router, w_gate_e, w_up_e, w_down_e, w_gate_s, w_up_s, w_down_s):
    p = dict(norm_mix_g=norm_mix_g, norm_ffn_g=norm_ffn_g, w_out=w_out,
             w_in_even=w_in_even, a_norm_g=a_norm_g, a_ws=a_ws, a_bs=a_bs, b_w_gate2=b_w_gate2,
             b_gate_bias=b_gate_bias, b_norm_g=b_norm_g, w_in_odd=w_in_odd, c_ig_bias=c_ig_bias,
             c_fg_bias=c_fg_bias, c_norm_g=c_norm_g, d_g_qa=d_g_qa, d_w_qb=d_w_qb, d_g_kva=d_g_kva,
             d_g_qn=d_g_qn, d_g_qr=d_g_qr, d_g_kr=d_g_kr, d_g_kn=d_g_kn, d_w_uk=d_w_uk, d_w_uv=d_w_uv,
             w_router=w_router, b_router=b_router, w_gate_e=w_gate_e, w_up_e=w_up_e, w_down_e=w_down_e,
             w_gate_s=w_gate_s, w_up_s=w_up_s, w_down_s=w_down_s)
    n_p, t_p, d = x_prompt.shape
    n_s, t_s, _ = x_sample.shape
    depth = w_ada.shape[0]
    pad_p = (-n_p) % 8
    c_all = jnp.concatenate([c_prompt, jnp.zeros((pad_p, d), F32), c_sample], axis=0)
    mod_all = _ada(c_all, w_ada, b_ada).reshape(depth, c_all.shape[0], 6, d)
    mod_p = mod_all[:, :n_p]
    mod_s = mod_all[:, n_p + pad_p:]

    n_even, _, bh, bdk, bdv = state_gla.shape
    n_odd, _, chh, cdh, _ = state_mlstm_c.shape
    gla0_p = jnp.zeros((n_even, n_p, bh, bdk, bdv), F32)
    mc0_p = jnp.zeros((n_odd, n_p, chh, cdh, cdh), F32)
    mn0_p = jnp.zeros((n_odd, n_p, chh, cdh), F32)
    mm0_p = jnp.full((n_odd, n_p, chh), NEG, F32)
    past_len = page_table.shape[1] * PAGE_SIZE

    trunks = [_trunk(x_prompt, mod_p, 0, gla0_p, mc0_p, mn0_p, mm0_p, p, None),
              _trunk(x_sample, mod_s, past_len, state_gla, state_mlstm_c, state_mlstm_n, state_mlstm_m, p,
                     (cache_ckv, cache_krope, page_table))]
    pending = [next(tr) for tr in trunks]
    for layer in range(depth):
        mixed = _moe(layer, pending, p)
        pending = []
        for tr, x_new in zip(trunks, mixed):
            try:
                pending.append(tr.send(x_new))
            except StopIteration as done:
                pending.append(done.value)
    (y_p, rp), (y_s, rs) = pending
    aw = rs['v_rows'].shape[1]
    return (y_p, y_s, rp['gla'][None], rs['gla'][None], rs['v_rows'].reshape(1, n_s, t_s, aw),
            rp['mlstm'][0][None], rs['mlstm'][0][None], rp['mlstm'][1][None], rs['mlstm'][1][None],
            rp['mlstm'][2][None], rs['mlstm'][2][None],
            rp['ckv'][:, None], rs['ckv'][:, None], rp['kr'][:, None], rs['kr'][:, None])
```
